```python
import jax, jax.numpy as jnp
from jax import lax
import numpy as np


D_MODEL = 1024
BATCH = 16
SEQ = 4096
DEPTH = 4

CTX_LEN = 256
GRID_W = 64
HEAD_DIM = 128
D_LRU = D_MODEL
LRU_BLOCK_W = 64
LRU_BLOCKS = D_LRU // LRU_BLOCK_W
LRU_C = 8.0
CONV_W = 4
CONV_LEFT = 2
N_HEADS_W = D_MODEL // HEAD_DIM
N_KV_W = N_HEADS_W // 4
WINDOW = 128
N_HEADS_G = D_MODEL // HEAD_DIM
N_KV_G = N_HEADS_G // 4
Q_BLOCK = 128
ROPE_THETA = 10000.0
EPS = 1e-6
NEG_INF = -1e30
N_BRANCH = 3
BRANCH_W = D_MODEL

QW_W, KVW_W = N_HEADS_W * HEAD_DIM, N_KV_W * HEAD_DIM
QW_G, KVW_G = N_HEADS_G * HEAD_DIM, N_KV_G * HEAD_DIM
IN_WIDTHS = (D_LRU, D_LRU,
             QW_W, KVW_W, KVW_W, QW_W,
             QW_G, KVW_G, KVW_G, QW_G,
             D_MODEL, D_MODEL, D_MODEL)
IN_WIDTH = sum(IN_WIDTHS)
SPLIT_POINTS = tuple(int(v) for v in np.cumsum(IN_WIDTHS)[:-1])

kernel_name = "hybrid_rglru_window_axial_prefix_block"


def rmsnorm(x, g):
    xf = x.astype(jnp.float32)
    y = xf * lax.rsqrt(jnp.mean(xf * xf, axis=-1, keepdims=True) + EPS)
    return y.astype(x.dtype) * g


def axial_rope_tables(S):
    rows = S // GRID_W
    row_id = jnp.repeat(jnp.arange(rows), GRID_W)
    col_id = jnp.tile(jnp.arange(GRID_W), rows)
    P = HEAD_DIM // 4
    inv = ROPE_THETA ** (-jnp.arange(P, dtype=jnp.float32) / P)
    ang = jnp.stack([row_id[:, None] * inv, col_id[:, None] * inv], axis=1)
    return jnp.cos(ang), jnp.sin(ang)


def rope_2d(x, cos, sin):
    B, T, H, hd = x.shape
    xs = x.reshape(B, T, H, 2, 2, hd // 4)
    x0, x1 = xs[..., 0, :], xs[..., 1, :]
    c, s = cos[None, :, None], sin[None, :, None]
    return jnp.stack([x0 * c - x1 * s, x1 * c + x0 * s], axis=-2).reshape(B, T, H, hd).astype(x.dtype)


def dwconv(u, w, b):
    T = u.shape[1]
    up = jnp.pad(u, ((0, 0), (CONV_LEFT, CONV_W - 1 - CONV_LEFT), (0, 0)))
    return sum(up[:, k:k + T] * w[k] for k in range(CONV_W)) + b


def lru_coeffs(u, wa, ba, wx, bx, lam):
    ub = u.reshape(*u.shape[:-1], LRU_BLOCKS, LRU_BLOCK_W)
    r = jax.nn.sigmoid(jnp.einsum('btnd,nde->btne', ub, wa).reshape(u.shape) + ba)
    i = jax.nn.sigmoid(jnp.einsum('btnd,nde->btne', ub, wx).reshape(u.shape) + bx)
    log_a = -LRU_C * r.astype(jnp.float32) * jax.nn.softplus(-lam.astype(jnp.float32))
    a = jnp.exp(log_a)
    b = jnp.sqrt(-jnp.expm1(2.0 * log_a)) * (i * u).astype(jnp.float32)
    return a, b


def _combine(left, right):
    a1, b1 = left
    a2, b2 = right
    return a1 * a2, a2 * b1 + b2


def linear_scan(a, b, reverse):
    return lax.associative_scan(_combine, (a, b), reverse=reverse, axis=1)[1]


def rglru_branch(u, uc, conv_w, conv_b, wa, ba, wx, bx, lam):
    u = dwconv(u, conv_w, conv_b)
    uc = dwconv(uc, conv_w, conv_b)
    y, yc = 0.0, 0.0
    for d, rev in enumerate((False, True)):
        ac, bc = lru_coeffs(uc, wa[d], ba[d], wx[d], bx[d], lam[d])
        hc = linear_scan(ac, bc, rev)
        a, b = lru_coeffs(u, wa[d], ba[d], wx[d], bx[d], lam[d])
        if rev:
            b = b.at[:, -1].add(a[:, -1] * hc[:, 0])
        else:
            b = b.at[:, 0].add(a[:, 0] * hc[:, -1])
        y = y + linear_scan(a, b, rev)
        yc = yc + hc
    return y.astype(u.dtype), yc.astype(u.dtype)


def attend(q, kv_sets, sink):
    scale = q.shape[-1] ** -0.5
    logits = []
    for k, v, mask in kv_sets:
        s = jnp.einsum('bqkgd,bjkd->bkgqj', q, k).astype(jnp.float32) * scale
        if mask is not None:
            s = jnp.where(mask, s, NEG_INF)
        logits.append(s)
    if sink is not None:
        B, Q, KVH, G, _ = q.shape
        logits.append(jnp.broadcast_to(sink.astype(jnp.float32).reshape(1, KVH, G, 1, 1), (B, KVH, G, Q, 1)))
    p = jax.nn.softmax(jnp.concatenate(logits, axis=-1), axis=-1)
    out, off = 0.0, 0
    for k, v, _ in kv_sets:
        J = k.shape[1]
        out = out + jnp.einsum('bkgqj,bjkd->bqkgd', p[..., off:off + J].astype(v.dtype), v)
        off += J
    return out


def latent_attention(q, k, v, kc, vc, sink, window):
    B, S, H, hd = q.shape
    KVH = k.shape[2]
    nblk = S // Q_BLOCK
    qb = q.reshape(B, nblk, Q_BLOCK, KVH, H // KVH, hd).swapaxes(0, 1)
    if window is None:
        out = lax.map(lambda qn: attend(qn, [(k, v, None), (kc, vc, None)], sink), qb)
    else:
        span = Q_BLOCK + 2 * window
        pad = ((0, 0), (window, window), (0, 0), (0, 0))
        kp, vp = jnp.pad(k, pad), jnp.pad(v, pad)

        def body(args):
            n, qn = args
            start = n * Q_BLOCK
            kn = lax.dynamic_slice_in_dim(kp, start, span, axis=1)
            vn = lax.dynamic_slice_in_dim(vp, start, span, axis=1)
            qpos = start + jnp.arange(Q_BLOCK)
            kpos = start - window + jnp.arange(span)
            mask = (jnp.abs(kpos[None, :] - qpos[:, None]) <= window) & (kpos >= 0)[None, :] & (kpos < S)[None, :]
            return attend(qn, [(kn, vn, mask), (kc, vc, None)], sink)

        out = lax.map(body, (jnp.arange(nblk), qb))
    return out.swapaxes(0, 1).reshape(B, S, H * hd)


def context_attention(q, k, v, sink):
    B, L, H, hd = q.shape
    KVH = k.shape[2]
    o = attend(q.reshape(B, L, KVH, H // KVH, hd), [(k, v, None)], sink)
    return o.reshape(B, L, H * hd)


def heads(t, n):
    return t.reshape(*t.shape[:-1], n, HEAD_DIM)


def merge_branches(ys, gate_paths, merge_logits, w_branch, w_out):
    out = 0.0
    for n in range(N_BRANCH):
        out = out + jax.nn.sigmoid(merge_logits[n]) * ((ys[n] * jax.nn.silu(gate_paths[n])) @ w_branch[n])
    return out @ w_out


def _fwd_setup_inputs(seed: int = 0) -> dict:
    key = jax.random.key(seed)
    ks = jax.random.split(key, 24)
    nrm = lambda k, shape, s: jax.random.normal(k, shape, jnp.float32) * s
    x = nrm(ks[0], (BATCH, SEQ, D_MODEL), 1.0)
    c = nrm(ks[1], (BATCH, D_MODEL), 1.0)
    ctx = nrm(ks[2], (BATCH, CTX_LEN, D_MODEL), 1.0)
    c_ctx = nrm(ks[3], (D_MODEL,), 1.0)
    norm_g = 1.0 + nrm(ks[4], (DEPTH, D_MODEL), 0.02)
    w_mod = nrm(ks[5], (DEPTH, D_MODEL, 3 * D_MODEL), 0.5 * D_MODEL ** -0.5)
    b_mod = nrm(ks[6], (DEPTH, 3 * D_MODEL), 0.02)
    w_in = nrm(ks[7], (DEPTH, D_MODEL, IN_WIDTH), D_MODEL ** -0.5)
    conv_w = nrm(ks[8], (DEPTH, CONV_W, D_LRU), CONV_W ** -0.5)
    conv_b = nrm(ks[9], (DEPTH, D_LRU), 0.02)
    lru_wa = nrm(ks[10], (DEPTH, 2, LRU_BLOCKS, LRU_BLOCK_W, LRU_BLOCK_W), LRU_BLOCK_W ** -0.5)
    lru_ba = nrm(ks[11], (DEPTH, 2, D_LRU), 0.1)
    lru_wx = nrm(ks[12], (DEPTH, 2, LRU_BLOCKS, LRU_BLOCK_W, LRU_BLOCK_W), LRU_BLOCK_W ** -0.5)
    lru_bx = nrm(ks[13], (DEPTH, 2, D_LRU), 0.1)
    u = jax.random.uniform(ks[14], (DEPTH, 2, D_LRU), jnp.float32, 0.9, 0.999)
    a = u ** (1.0 / LRU_C)
    lru_lambda = jnp.log(a) - jnp.log1p(-a)
    attn_sink = nrm(ks[15], (DEPTH, N_HEADS_W), 0.5)
    q_norm_g = 1.0 + nrm(ks[16], (DEPTH, HEAD_DIM), 0.02)
    k_norm_g = 1.0 + nrm(ks[17], (DEPTH, HEAD_DIM), 0.02)
    w_branch = nrm(ks[18], (DEPTH, N_BRANCH, BRANCH_W, D_MODEL), BRANCH_W ** -0.5)
    w_out = nrm(ks[19], (DEPTH, D_MODEL, D_MODEL), D_MODEL ** -0.5)
    final_g = 1.0 + nrm(ks[20], (D_MODEL,), 0.02)
    return {"x": x, "c": c, "ctx": ctx, "c_ctx": c_ctx, "norm_g": norm_g, "w_mod": w_mod, "b_mod": b_mod,
            "w_in": w_in, "conv_w": conv_w, "conv_b": conv_b, "lru_wa": lru_wa, "lru_ba": lru_ba,
            "lru_wx": lru_wx, "lru_bx": lru_bx, "lru_lambda": lru_lambda, "attn_sink": attn_sink,
            "q_norm_g": q_norm_g, "k_norm_g": k_norm_g, "w_branch": w_branch, "w_out": w_out,
            "final_g": final_g}


def _fwd_reference(x, c, ctx, c_ctx, norm_g, w_mod, b_mod, w_in, conv_w, conv_b, lru_wa, lru_ba, lru_wx, lru_bx,
              lru_lambda, attn_sink, q_norm_g, k_norm_g, w_branch, w_out, final_g):
    S = x.shape[1]
    cos, sin = axial_rope_tables(S)
    sc, scc = jax.nn.silu(c), jax.nn.silu(c_ctx)
    xc = ctx
    for l in range(DEPTH):
        last = l == DEPTH - 1
        shift, scale, gate = jnp.split(sc @ w_mod[l] + b_mod[l], 3, axis=-1)
        shift_c, scale_c, gate_c = jnp.split(scc @ w_mod[l] + b_mod[l], 3, axis=-1)
        h = rmsnorm(x, norm_g[l]) * (1.0 + scale[:, None]) + shift[:, None]
        hc = rmsnorm(xc, norm_g[l]) * (1.0 + scale_c) + shift_c
        w_parts = jnp.split(w_in[l], SPLIT_POINTS, axis=1)
        uA, gA, qB, kB, vB, gB, qC, kC, vC, gC, mA, mB, mC = [h @ w for w in w_parts]
        uAc, gAc, qBc, kBc, vBc, gBc, qCc, kCc, vCc, gCc, mAc, mBc, mCc = [hc @ w for w in w_parts]
        yA, yAc = rglru_branch(uA, uAc, conv_w[l], conv_b[l], lru_wa[l], lru_ba[l], lru_wx[l], lru_bx[l], lru_lambda[l])
        kBc_h, vBc_h = heads(kBc, N_KV_W), heads(vBc, N_KV_W)
        yB = latent_attention(rope_2d(heads(qB, N_HEADS_W), cos, sin), rope_2d(heads(kB, N_KV_W), cos, sin),
                              heads(vB, N_KV_W), kBc_h, vBc_h, attn_sink[l], WINDOW)
        kCc_h, vCc_h = rmsnorm(heads(kCc, N_KV_G), k_norm_g[l]), heads(vCc, N_KV_G)
        yC = latent_attention(rope_2d(rmsnorm(heads(qC, N_HEADS_G), q_norm_g[l]), cos, sin),
                              rope_2d(rmsnorm(heads(kC, N_KV_G), k_norm_g[l]), cos, sin),
                              heads(vC, N_KV_G), kCc_h, vCc_h, None, None)
        y = merge_branches((yA, yB, yC), (gA, gB, gC), (mA, mB, mC), w_branch[l], w_out[l])
        if not last:
            yBc = context_attention(heads(qBc, N_HEADS_W), kBc_h, vBc_h, attn_sink[l])
            yCc = context_attention(rmsnorm(heads(qCc, N_HEADS_G), q_norm_g[l]), kCc_h, vCc_h, None)
            yc = merge_branches((yAc, yBc, yCc), (gAc, gBc, gCc), (mAc, mBc, mCc), w_branch[l], w_out[l])
            xc = xc + gate_c * yc
        x = x + gate[:, None] * y
    return rmsnorm(x, final_g)


import jax as _jax
import jax.numpy as _jnp

TWIN_FORMAT = 'train_step'
FWD_PARAMS = ['x', 'c', 'ctx', 'c_ctx', 'norm_g', 'w_mod', 'b_mod', 'w_in', 'conv_w', 'conv_b', 'lru_wa', 'lru_ba', 'lru_wx', 'lru_bx', 'lru_lambda', 'attn_sink', 'q_norm_g', 'k_norm_g', 'w_branch', 'w_out', 'final_g']
TWIN_WEIGHTS = ['c_ctx', 'norm_g', 'w_mod', 'b_mod', 'w_in', 'conv_w', 'conv_b', 'lru_wa', 'lru_ba', 'lru_wx', 'lru_bx', 'lru_lambda', 'attn_sink', 'q_norm_g', 'k_norm_g', 'w_branch', 'w_out', 'final_g']
TWIN_DIFF_INPUT = 'x'
TWIN_INPUTS = ['x', 'c', 'ctx', 'c_ctx', 'norm_g', 'w_mod', 'b_mod', 'w_in', 'conv_w', 'conv_b', 'lru_wa', 'lru_ba', 'lru_wx', 'lru_bx', 'lru_lambda', 'attn_sink', 'q_norm_g', 'k_norm_g', 'w_branch', 'w_out', 'final_g', 'loss_target', 'm_c_ctx', 'm_norm_g', 'm_w_mod', 'm_b_mod', 'm_w_in', 'm_conv_w', 'm_conv_b', 'm_lru_wa', 'm_lru_ba', 'm_lru_wx', 'm_lru_bx', 'm_lru_lambda', 'm_attn_sink', 'm_q_norm_g', 'm_k_norm_g', 'm_w_branch', 'm_w_out', 'm_final_g', 'v_c_ctx', 'v_norm_g', 'v_w_mod', 'v_b_mod', 'v_w_in', 'v_conv_w', 'v_conv_b', 'v_lru_wa', 'v_lru_ba', 'v_lru_wx', 'v_lru_bx', 'v_lru_lambda', 'v_attn_sink', 'v_q_norm_g', 'v_k_norm_g', 'v_w_branch', 'v_w_out', 'v_final_g']
TWIN_OUTPUTS = ['loss', 'grad_x', 'grad_c_ctx', 'grad_norm_g', 'grad_w_mod', 'grad_b_mod', 'grad_w_in', 'grad_conv_w', 'grad_conv_b', 'grad_lru_wa', 'grad_lru_ba', 'grad_lru_wx', 'grad_lru_bx', 'grad_lru_lambda', 'grad_attn_sink', 'grad_q_norm_g', 'grad_k_norm_g', 'grad_w_branch', 'grad_w_out', 'grad_final_g', 'delta_c_ctx', 'delta_norm_g', 'delta_w_mod', 'delta_b_mod', 'delta_w_in', 'delta_conv_w', 'delta_conv_b', 'delta_lru_wa', 'delta_lru_ba', 'delta_lru_wx', 'delta_lru_bx', 'delta_lru_lambda', 'delta_attn_sink', 'delta_q_norm_g', 'delta_k_norm_g', 'delta_w_branch', 'delta_w_out', 'delta_final_g', 'new_m_c_ctx', 'new_m_norm_g', 'new_m_w_mod', 'new_m_b_mod', 'new_m_w_in', 'new_m_conv_w', 'new_m_conv_b', 'new_m_lru_wa', 'new_m_lru_ba', 'new_m_lru_wx', 'new_m_lru_bx', 'new_m_lru_lambda', 'new_m_attn_sink', 'new_m_q_norm_g', 'new_m_k_norm_g', 'new_m_w_branch', 'new_m_w_out', 'new_m_final_g', 'new_v_c_ctx', 'new_v_norm_g', 'new_v_w_mod', 'new_v_b_mod', 'new_v_w_in', 'new_v_conv_w', 'new_v_conv_b', 'new_v_lru_wa', 'new_v_lru_ba', 'new_v_lru_wx', 'new_v_lru_bx', 'new_v_lru_lambda', 'new_v_attn_sink', 'new_v_q_norm_g', 'new_v_k_norm_g', 'new_v_w_branch', 'new_v_w_out', 'new_v_final_g']
TWIN_LEAF_KINDS = {'loss': 'loss', 'grad_x': 'grad_x', 'grad_c_ctx': 'grad_w', 'grad_norm_g': 'grad_w', 'grad_w_mod': 'grad_w', 'grad_b_mod': 'grad_w', 'grad_w_in': 'grad_w', 'grad_conv_w': 'grad_w', 'grad_conv_b': 'grad_w', 'grad_lru_wa': 'grad_w', 'grad_lru_ba': 'grad_w', 'grad_lru_wx': 'grad_w', 'grad_lru_bx': 'grad_w', 'grad_lru_lambda': 'grad_w', 'grad_attn_sink': 'grad_w', 'grad_q_norm_g': 'grad_w', 'grad_k_norm_g': 'grad_w', 'grad_w_branch': 'grad_w', 'grad_w_out': 'grad_w', 'grad_final_g': 'grad_w', 'delta_c_ctx': 'delta_w', 'delta_norm_g': 'delta_w', 'delta_w_mod': 'delta_w', 'delta_b_mod': 'delta_w', 'delta_w_in': 'delta_w', 'delta_conv_w': 'delta_w', 'delta_conv_b': 'delta_w', 'delta_lru_wa': 'delta_w', 'delta_lru_ba': 'delta_w', 'delta_lru_wx': 'delta_w', 'delta_lru_bx': 'delta_w', 'delta_lru_lambda': 'delta_w', 'delta_attn_sink': 'delta_w', 'delta_q_norm_g': 'delta_w', 'delta_k_norm_g': 'delta_w', 'delta_w_branch': 'delta_w', 'delta_w_out': 'delta_w', 'delta_final_g': 'delta_w', 'new_m_c_ctx': 'new_m', 'new_m_norm_g': 'new_m', 'new_m_w_mod': 'new_m', 'new_m_b_mod': 'new_m', 'new_m_w_in': 'new_m', 'new_m_conv_w': 'new_m', 'new_m_conv_b': 'new_m', 'new_m_lru_wa': 'new_m', 'new_m_lru_ba': 'new_m', 'new_m_lru_wx': 'new_m', 'new_m_lru_bx': 'new_m', 'new_m_lru_lambda': 'new_m', 'new_m_attn_sink': 'new_m', 'new_m_q_norm_g': 'new_m', 'new_m_k_norm_g': 'new_m', 'new_m_w_branch': 'new_m', 'new_m_w_out': 'new_m', 'new_m_final_g': 'new_m', 'new_v_c_ctx': 'new_v', 'new_v_norm_g': 'new_v', 'new_v_w_mod': 'new_v', 'new_v_b_mod': 'new_v', 'new_v_w_in': 'new_v', 'new_v_conv_w': 'new_v', 'new_v_conv_b': 'new_v', 'new_v_lru_wa': 'new_v', 'new_v_lru_ba': 'new_v', 'new_v_lru_wx': 'new_v', 'new_v_lru_bx': 'new_v', 'new_v_lru_lambda': 'new_v', 'new_v_attn_sink': 'new_v', 'new_v_q_norm_g': 'new_v', 'new_v_k_norm_g': 'new_v', 'new_v_w_branch': 'new_v', 'new_v_w_out': 'new_v', 'new_v_final_g': 'new_v'}


def _forward(args):
    return _fwd_reference(*[args[k] for k in FWD_PARAMS])


def _output_shape():
    out = _jax.eval_shape(lambda: _forward(_fwd_setup_inputs(0)))
    return out.shape, out.dtype

N_MICROBATCH = 1
ADAM_LR = 0.001
ADAM_B1 = 0.9
ADAM_B2 = 0.999
ADAM_EPS = 1e-08
ADAM_WD = 0.01
ADAM_STEP = 10
PER_EXAMPLE_BATCH_AXIS = {'x': 0, 'c': 0, 'ctx': 0, 'loss_target': 0}
SHARED_INPUTS = []
_WEIGHT_DTYPES = {'c_ctx': _jnp.float32, 'norm_g': _jnp.float32, 'w_mod': _jnp.float32, 'b_mod': _jnp.float32, 'w_in': _jnp.float32, 'conv_w': _jnp.float32, 'conv_b': _jnp.float32, 'lru_wa': _jnp.float32, 'lru_ba': _jnp.float32, 'lru_wx': _jnp.float32, 'lru_bx': _jnp.float32, 'lru_lambda': _jnp.float32, 'attn_sink': _jnp.float32, 'q_norm_g': _jnp.float32, 'k_norm_g': _jnp.float32, 'w_branch': _jnp.float32, 'w_out': _jnp.float32, 'final_g': _jnp.float32}
MOMENT_SCALE = {'c_ctx': 6.825165e-02, 'norm_g': 1.665402e-01, 'w_mod': 2.635093e-01, 'b_mod': 3.824804e-01, 'w_in': 6.801801e-02, 'conv_w': 1.492863e-01, 'conv_b': 3.508683e-01, 'lru_wa': 8.439905e-03, 'lru_ba': 1.401220e-02, 'lru_wx': 1.729382e-02, 'lru_bx': 3.246294e-02, 'lru_lambda': 3.748087e-02, 'attn_sink': 1.391693e-04, 'q_norm_g': 6.403298e-03, 'k_norm_g': 6.268890e-03, 'w_branch': 8.809324e-02, 'w_out': 1.537765e-01, 'final_g': 6.470542e+01}


def _to_microbatches(a, axis):
    t = _jnp.moveaxis(a, axis, 0)
    t = t.reshape((N_MICROBATCH, t.shape[0] // N_MICROBATCH) + t.shape[1:])
    return _jnp.moveaxis(t, 1, axis + 1)


def setup_inputs(seed: int = 0) -> dict:
    inp = _fwd_setup_inputs(seed)
    key = _jax.random.fold_in(_jax.random.key(seed), 7919)
    shape, _ = _output_shape()
    out = dict(inp)
    out["loss_target"] = _jax.random.normal(_jax.random.fold_in(key, 0), shape, _jnp.float32)
    for i, name in enumerate(TWIN_WEIGHTS):
        w = inp[name].astype(_jnp.float32)
        if MOMENT_SCALE is None:
            s = _jnp.sqrt(_jnp.mean(_jnp.square(w)) + 1e-30)
        else:
            s = MOMENT_SCALE[name]
        km, kv = _jax.random.split(_jax.random.fold_in(key, i + 1))
        out[name] = w
        out["m_" + name] = s * _jax.random.normal(km, w.shape, _jnp.float32)
        out["v_" + name] = (s * s) * _jax.random.uniform(kv, w.shape, _jnp.float32, 0.5, 1.5)
    if N_MICROBATCH > 1:
        for name, axis in PER_EXAMPLE_BATCH_AXIS.items():
            out[name] = _to_microbatches(out[name], axis)
    return {'x': out['x'], 'c': out['c'], 'ctx': out['ctx'], 'c_ctx': out['c_ctx'], 'norm_g': out['norm_g'], 'w_mod': out['w_mod'], 'b_mod': out['b_mod'], 'w_in': out['w_in'], 'conv_w': out['conv_w'], 'conv_b': out['conv_b'], 'lru_wa': out['lru_wa'], 'lru_ba': out['lru_ba'], 'lru_wx': out['lru_wx'], 'lru_bx': out['lru_bx'], 'lru_lambda': out['lru_lambda'], 'attn_sink': out['attn_sink'], 'q_norm_g': out['q_norm_g'], 'k_norm_g': out['k_norm_g'], 'w_branch': out['w_branch'], 'w_out': out['w_out'], 'final_g': out['final_g'], 'loss_target': out['loss_target'], 'm_c_ctx': out['m_c_ctx'], 'm_norm_g': out['m_norm_g'], 'm_w_mod': out['m_w_mod'], 'm_b_mod': out['m_b_mod'], 'm_w_in': out['m_w_in'], 'm_conv_w': out['m_conv_w'], 'm_conv_b': out['m_conv_b'], 'm_lru_wa': out['m_lru_wa'], 'm_lru_ba': out['m_lru_ba'], 'm_lru_wx': out['m_lru_wx'], 'm_lru_bx': out['m_lru_bx'], 'm_lru_lambda': out['m_lru_lambda'], 'm_attn_sink': out['m_attn_sink'], 'm_q_norm_g': out['m_q_norm_g'], 'm_k_norm_g': out['m_k_norm_g'], 'm_w_branch': out['m_w_branch'], 'm_w_out': out['m_w_out'], 'm_final_g': out['m_final_g'], 'v_c_ctx': out['v_c_ctx'], 'v_norm_g': out['v_norm_g'], 'v_w_mod': out['v_w_mod'], 'v_b_mod': out['v_b_mod'], 'v_w_in': out['v_w_in'], 'v_conv_w': out['v_conv_w'], 'v_conv_b': out['v_conv_b'], 'v_lru_wa': out['v_lru_wa'], 'v_lru_ba': out['v_lru_ba'], 'v_lru_wx': out['v_lru_wx'], 'v_lru_bx': out['v_lru_bx'], 'v_lru_lambda': out['v_lru_lambda'], 'v_attn_sink': out['v_attn_sink'], 'v_q_norm_g': out['v_q_norm_g'], 'v_k_norm_g': out['v_k_norm_g'], 'v_w_branch': out['v_w_branch'], 'v_w_out': out['v_w_out'], 'v_final_g': out['v_final_g']}


def _loss(weights, diff, rest, loss_target):
    with _jax.named_scope("forward"):
        args = {**rest, TWIN_DIFF_INPUT: diff, **{k: w.astype(_WEIGHT_DTYPES[k]) for k, w in weights.items()}}
        y = _forward(args)
    with _jax.named_scope("loss_head"):
        err = _jnp.square(y.astype(_jnp.float32) - loss_target)
        return 0.5 * _jnp.sum(_jnp.mean(err, axis=-1)) if err.ndim else 0.5 * err


def _adamw(w, g, m, v):
    m = ADAM_B1 * m + (1.0 - ADAM_B1) * g
    v = ADAM_B2 * v + (1.0 - ADAM_B2) * _jnp.square(g)
    m_hat = m / (1.0 - ADAM_B1 ** ADAM_STEP)
    v_hat = v / (1.0 - ADAM_B2 ** ADAM_STEP)
    delta = -ADAM_LR * (m_hat / (_jnp.sqrt(v_hat) + ADAM_EPS) + ADAM_WD * w)
    return delta, m, v


def reference(x, c, ctx, c_ctx, norm_g, w_mod, b_mod, w_in, conv_w, conv_b, lru_wa, lru_ba, lru_wx, lru_bx, lru_lambda, attn_sink, q_norm_g, k_norm_g, w_branch, w_out, final_g, loss_target, m_c_ctx, m_norm_g, m_w_mod, m_b_mod, m_w_in, m_conv_w, m_conv_b, m_lru_wa, m_lru_ba, m_lru_wx, m_lru_bx, m_lru_lambda, m_attn_sink, m_q_norm_g, m_k_norm_g, m_w_branch, m_w_out, m_final_g, v_c_ctx, v_norm_g, v_w_mod, v_b_mod, v_w_in, v_conv_w, v_conv_b, v_lru_wa, v_lru_ba, v_lru_wx, v_lru_bx, v_lru_lambda, v_attn_sink, v_q_norm_g, v_k_norm_g, v_w_branch, v_w_out, v_final_g):
    given = dict(x=x, c=c, ctx=ctx, c_ctx=c_ctx, norm_g=norm_g, w_mod=w_mod, b_mod=b_mod, w_in=w_in, conv_w=conv_w, conv_b=conv_b, lru_wa=lru_wa, lru_ba=lru_ba, lru_wx=lru_wx, lru_bx=lru_bx, lru_lambda=lru_lambda, attn_sink=attn_sink, q_norm_g=q_norm_g, k_norm_g=k_norm_g, w_branch=w_branch, w_out=w_out, final_g=final_g, loss_target=loss_target, m_c_ctx=m_c_ctx, m_norm_g=m_norm_g, m_w_mod=m_w_mod, m_b_mod=m_b_mod, m_w_in=m_w_in, m_conv_w=m_conv_w, m_conv_b=m_conv_b, m_lru_wa=m_lru_wa, m_lru_ba=m_lru_ba, m_lru_wx=m_lru_wx, m_lru_bx=m_lru_bx, m_lru_lambda=m_lru_lambda, m_attn_sink=m_attn_sink, m_q_norm_g=m_q_norm_g, m_k_norm_g=m_k_norm_g, m_w_branch=m_w_branch, m_w_out=m_w_out, m_final_g=m_final_g, v_c_ctx=v_c_ctx, v_norm_g=v_norm_g, v_w_mod=v_w_mod, v_b_mod=v_b_mod, v_w_in=v_w_in, v_conv_w=v_conv_w, v_conv_b=v_conv_b, v_lru_wa=v_lru_wa, v_lru_ba=v_lru_ba, v_lru_wx=v_lru_wx, v_lru_bx=v_lru_bx, v_lru_lambda=v_lru_lambda, v_attn_sink=v_attn_sink, v_q_norm_g=v_q_norm_g, v_k_norm_g=v_k_norm_g, v_w_branch=v_w_branch, v_w_out=v_w_out, v_final_g=v_final_g)
    weights = {n: given[n] for n in TWIN_WEIGHTS}
    shared = {n: given[n] for n in SHARED_INPUTS}
    per_example = {n: given[n] for n in ['x', 'c', 'ctx']}
    grad_fn = _jax.value_and_grad(_loss, argnums=(0, 1))

    def one_microbatch(ex, loss_target):
        ex = dict(ex)
        diff = ex.pop(TWIN_DIFF_INPUT)
        return grad_fn(weights, diff, {**shared, **ex}, loss_target)

    if N_MICROBATCH == 1:
        loss, (grad_w, grad_x) = one_microbatch(per_example, given["loss_target"])
    else:
        def body(carry, xs):
            loss_sum, grad_sum = carry
            l_k, (gw_k, gx_k) = one_microbatch(xs[0], xs[1])
            with _jax.named_scope("update"):
                return (loss_sum + l_k, _jax.tree.map(_jnp.add, grad_sum, gw_k)), gx_k

        init = (_jnp.zeros((), _jnp.float32), _jax.tree.map(_jnp.zeros_like, weights))
        (loss, grad_w), grad_x = _jax.lax.scan(body, init, (per_example, given["loss_target"]))
    with _jax.named_scope("update"):
        delta_w, new_m, new_v = {}, {}, {}
        for n in TWIN_WEIGHTS:
            delta_w[n], new_m[n], new_v[n] = _adamw(weights[n], grad_w[n], given["m_" + n], given["v_" + n])
    return (loss, grad_x, *[grad_w[n] for n in TWIN_WEIGHTS], *[delta_w[n] for n in TWIN_WEIGHTS],
            *[new_m[n] for n in TWIN_WEIGHTS], *[new_v[n] for n in TWIN_WEIGHTS])
```

```python
import functools

import jax
import jax.numpy as jnp
from jax import lax
from jax.experimental import pallas as pl
from jax.experimental.pallas import tpu as pltpu

F32 = jnp.float32
BF16 = jnp.bfloat16

AXES = ("x", "y", "c")
N_DEV = 8
DEPTH = 4
HEAD_DIM = 128
GRID_W = 64
WINDOW = 128
LRU_BLOCK_W = 64
GATE_BLOCK = 256
LRU_C = 8.0
ROPE_THETA = 10000.0
EPS = 1e-6
NEG_INF = -1e30
KV_GROUP = 4
LANE = 128
SUBLANE = 8
VMEM_LIMIT = 56 * 1024 * 1024

ADAM_LR = 0.001
ADAM_B1 = 0.9
ADAM_B2 = 0.999
ADAM_EPS = 1e-08
ADAM_WD = 0.01
ADAM_STEP = 10

WEIGHTS = ("c_ctx", "norm_g", "w_mod", "b_mod", "w_in", "conv_w", "conv_b", "lru_wa", "lru_ba", "lru_wx",
           "lru_bx", "lru_lambda", "attn_sink", "q_norm_g", "k_norm_g", "w_branch", "w_out", "final_g")
BIG = ("w_mod", "w_in", "w_branch", "w_out")
SMALL_SHARDED = ("conv_w", "lru_ba", "lru_bx", "lru_lambda")
REPLICATED = ("c_ctx", "norm_g", "b_mod", "conv_b", "lru_wa", "lru_wx", "attn_sink", "q_norm_g", "k_norm_g",
              "final_g")


def _call(body, **kw):
    return pl.pallas_call(body, **kw)


def _params(dims=None, vmem=VMEM_LIMIT):
    return pltpu.CompilerParams(dimension_semantics=dims, vmem_limit_bytes=vmem)


def _pick(n, target, mult):
    for t in range(min(n, target), 0, -1):
        if n % t == 0 and t % mult == 0:
            return t
    return n


def _mm(a, b, *, name, ta=False, tb=False, out_dtype=F32, tm=512, tn=1024, tk=1024):
    M, K = (a.shape[1], a.shape[0]) if ta else a.shape
    N = b.shape[0] if tb else b.shape[1]
    assert (b.shape[1] if tb else b.shape[0]) == K
    tm = _pick(M, tm, LANE if ta else 16)
    tn = _pick(N, tn, LANE)
    tk = _pick(K, tk, LANE)
    nk = K // tk
    dn = (((0 if ta else 1,), (1 if tb else 0,)), ((), ()))

    def body(a_ref, b_ref, o_ref, *acc):
        r = lax.dot_general(a_ref[...].astype(BF16), b_ref[...].astype(BF16), dn,
                            preferred_element_type=F32)
        if nk == 1:
            o_ref[...] = r.astype(o_ref.dtype)
        else:
            k = pl.program_id(2)

            @pl.when(k == 0)
            def _():
                acc[0][...] = r

            @pl.when(k > 0)
            def _():
                acc[0][...] += r

            @pl.when(k == nk - 1)
            def _():
                o_ref[...] = acc[0][...].astype(o_ref.dtype)

    a_spec = (pl.BlockSpec((tk, tm), lambda i, j, k: (k, i)) if ta
              else pl.BlockSpec((tm, tk), lambda i, j, k: (i, k)))
    b_spec = (pl.BlockSpec((tn, tk), lambda i, j, k: (j, k)) if tb
              else pl.BlockSpec((tk, tn), lambda i, j, k: (k, j)))
    return _call(
        body, name=name, grid=(M // tm, N // tn, nk),
        in_specs=[a_spec, b_spec],
        out_specs=pl.BlockSpec((tm, tn), lambda i, j, k: (i, j)),
        out_shape=jax.ShapeDtypeStruct((M, N), out_dtype),
        scratch_shapes=[pltpu.VMEM((tm, tn), F32)] if nk > 1 else [],
        compiler_params=_params(("parallel", "parallel", "arbitrary")),
    )(a, b)


def _matmul(name):
    @jax.custom_vjp
    def f(a, w):
        return _mm(a, w, name=name + "_fwd")

    def fwd(a, w):
        return f(a, w), (a, w)

    def bwd(res, g):
        a, w = res
        da = _mm(g, w, name=name + "_da", tb=True)
        dw = _mm(a, g, name=name + "_dw", ta=True, out_dtype=w.dtype, tm=1024, tn=1024, tk=512)
        return da, dw

    f.defvjp(fwd, bwd)
    return f


def _row_specs(xs, tps, gps, rts, tm, n_lat_tiles):
    x_specs = [pl.BlockSpec((None, tm, x.shape[2]), lambda b, i: (b, i, 0)) for x in xs]
    tp_specs = [pl.BlockSpec((None, None, 1, p.shape[3]),
                             lambda b, i: (b, (i < n_lat_tiles).astype(jnp.int32), 0, 0)) for p in tps]
    gp_specs = [pl.BlockSpec(p.shape, lambda b, i: (0, 0)) for p in gps]
    rt_specs = [pl.BlockSpec((tm, t.shape[1]), lambda b, i: (i, 0)) for t in rts]
    return x_specs, tp_specs, gp_specs, rt_specs


def _row_fwd(f, name, xs, tps, gps, rts, outs, tm, n_lat):
    B, T, _ = xs[0].shape
    n_in = len(xs) + len(tps) + len(gps) + len(rts)

    def body(*refs):
        vals = f(*[r[...] for r in refs[:n_in]])
        for o, v in zip(refs[n_in:], vals):
            o[...] = v.astype(o.dtype)

    x_specs, tp_specs, gp_specs, rt_specs = _row_specs(xs, tps, gps, rts, tm, n_lat // tm)
    res = _call(
        body, name=name, grid=(B, T // tm),
        in_specs=x_specs + tp_specs + gp_specs + rt_specs,
        out_specs=[pl.BlockSpec((None, tm, w), lambda b, i: (b, i, 0)) for w, _ in outs],
        out_shape=[jax.ShapeDtypeStruct((B, T, w), dt) for w, dt in outs],
        compiler_params=_params(("parallel", "parallel")),
    )(*xs, *tps, *gps, *rts)
    return list(res)


def _row_bwd(f, name, xs, tps, gps, rts, douts, tm, n_lat):
    B, T, _ = xs[0].shape
    nx, ntp, ngp, nd = len(xs), len(tps), len(gps), len(douts)
    n_lat_tiles = n_lat // tm
    n_diff = nx + ntp + ngp
    n_in = n_diff + len(rts)

    def body(*refs):
        diff = [r[...] for r in refs[:n_diff]]
        tabs = [r[...] for r in refs[n_diff:n_in]]
        dos = [r[...] for r in refs[n_in:n_in + nd]]
        o_refs = refs[n_in + nd:]
        prim, vjp = jax.vjp(lambda *d: tuple(f(*d, *tabs)), *diff)
        grads = vjp(tuple(d.astype(p.dtype) for d, p in zip(dos, prim)))
        b, i = pl.program_id(0), pl.program_id(1)
        for k in range(nx):
            o_refs[k][...] = grads[k].astype(o_refs[k].dtype)
        first_tp = jnp.logical_or(i == 0, i == n_lat_tiles)
        first_gp = jnp.logical_and(b == 0, i == 0)
        for k in range(nx, n_diff):
            first = first_tp if k < nx + ntp else first_gp

            @pl.when(first)
            def _(k=k):
                o_refs[k][...] = grads[k]

            @pl.when(jnp.logical_not(first))
            def _(k=k):
                o_refs[k][...] += grads[k]

    x_specs, tp_specs, gp_specs, rt_specs = _row_specs(xs, tps, gps, rts, tm, n_lat_tiles)
    d_specs = [pl.BlockSpec((None, tm, d.shape[2]), lambda b, i: (b, i, 0)) for d in douts]
    res = _call(
        body, name=name, grid=(B, T // tm),
        in_specs=x_specs + tp_specs + gp_specs + rt_specs + d_specs,
        out_specs=x_specs + tp_specs + gp_specs,
        out_shape=[jax.ShapeDtypeStruct(a.shape, F32) for a in (*xs, *tps, *gps)],
        compiler_params=_params(("arbitrary", "arbitrary")),
    )(*xs, *tps, *gps, *rts, *douts)
    res = list(res)
    return res[:nx], res[nx:nx + ntp], res[nx + ntp:]


def _rowwise(f, name, outs, tm, n_lat, n_x, n_tp, n_gp):
    def split(args):
        return (args[:n_x], args[n_x:n_x + n_tp], args[n_x + n_tp:n_x + n_tp + n_gp],
                args[n_x + n_tp + n_gp:])

    @jax.custom_vjp
    def op(*args):
        xs, tps, gps, rts = split(args)
        return tuple(_row_fwd(f, name + "_fwd", xs, tps, gps, rts, outs, tm, n_lat))

    def fwd(*args):
        return op(*args), args

    def bwd(args, g):
        xs, tps, gps, rts = split(args)
        dxs, dtps, dgps = _row_bwd(f, name + "_bwd", xs, tps, gps, rts, list(g), tm, n_lat)
        return (*dxs, *dtps, *dgps, *[jnp.zeros_like(t) for t in rts])

    op.defvjp(fwd, bwd)
    return op


def _shift_impl(u, k):
    n = u.shape[0]
    r = pltpu.roll(u, k % n, axis=0)
    row = lax.broadcasted_iota(jnp.int32, u.shape, 0)
    valid = (row >= k) if k > 0 else (row < n + k)
    return jnp.where(valid, r, 0.0)


@functools.partial(jax.custom_vjp, nondiff_argnums=(1,))
def _shift(u, k):
    return _shift_impl(u, k)


_shift.defvjp(lambda u, k: (_shift_impl(u, k), None), lambda k, _, g: (_shift_impl(g, -k),))


def _swap_impl(x):
    lane = lax.broadcasted_iota(jnp.int32, x.shape, 1)
    q = HEAD_DIM // 4
    return jnp.where((lane % (2 * q)) < q, pltpu.roll(x, HEAD_DIM - q, axis=1), pltpu.roll(x, q, axis=1))


@jax.custom_vjp
def _swap(x):
    return _swap_impl(x)


_swap.defvjp(lambda x: (_swap_impl(x), None), lambda _, g: (_swap_impl(g),))


def _conv_f(ul, uc, cw, cb):
    def conv(u):
        return (_shift(u, 2) * cw[0:1] + _shift(u, 1) * cw[1:2] + u * cw[2:3] + _shift(u, -1) * cw[3:4] + cb)
    return conv(ul), conv(uc)


def _conv_specs(B, T, D):
    u_spec = pl.BlockSpec((None, T, LANE), lambda j, b: (b, 0, j))
    cw_spec = pl.BlockSpec((4, LANE), lambda j, b: (0, j))
    cb_spec = pl.BlockSpec((1, LANE), lambda j, b: (0, j))
    return u_spec, cw_spec, cb_spec


def _conv_fwd(u, cw, cb, S):
    B, T, D = u.shape

    def body(u_ref, cw_ref, cb_ref, o_ref):
        vl, vc = _conv_f(u_ref[0:S, :], u_ref[S:T, :], cw_ref[...], cb_ref[...])
        o_ref[0:S, :] = vl
        o_ref[S:T, :] = vc

    u_spec, cw_spec, cb_spec = _conv_specs(B, T, D)
    return _call(body, name="conv_fwd", grid=(D // LANE, B), in_specs=[u_spec, cw_spec, cb_spec],
                 out_specs=u_spec, out_shape=jax.ShapeDtypeStruct(u.shape, F32),
                 compiler_params=_params(("parallel", "parallel")))(u, cw, cb)


def _conv_bwd(u, cw, cb, dv, S):
    B, T, D = u.shape

    def body(u_ref, cw_ref, cb_ref, dv_ref, du_ref, dcw_ref, dcb_ref):
        _, vjp = jax.vjp(_conv_f, u_ref[0:S, :], u_ref[S:T, :], cw_ref[...], cb_ref[...])
        dul, duc, dcw, dcb = vjp((dv_ref[0:S, :], dv_ref[S:T, :]))
        du_ref[0:S, :] = dul
        du_ref[S:T, :] = duc
        first = pl.program_id(1) == 0

        @pl.when(first)
        def _():
            dcw_ref[...] = dcw
            dcb_ref[...] = dcb

        @pl.when(jnp.logical_not(first))
        def _():
            dcw_ref[...] += dcw
            dcb_ref[...] += dcb

    u_spec, cw_spec, cb_spec = _conv_specs(B, T, D)
    return _call(body, name="conv_bwd", grid=(D // LANE, B), in_specs=[u_spec, cw_spec, cb_spec, u_spec],
                 out_specs=[u_spec, cw_spec, cb_spec],
                 out_shape=[jax.ShapeDtypeStruct(u.shape, F32), jax.ShapeDtypeStruct(cw.shape, F32),
                            jax.ShapeDtypeStruct(cb.shape, F32)],
                 compiler_params=_params(("parallel", "arbitrary")))(u, cw, cb, dv)


def _conv_op(S):
    @jax.custom_vjp
    def op(u, cw, cb):
        return _conv_fwd(u, cw, cb, S)

    def fwd(u, cw, cb):
        return op(u, cw, cb), (u, cw, cb)

    def bwd(res, g):
        u, cw, cb = res
        return tuple(_conv_bwd(u, cw, cb, g, S))

    op.defvjp(fwd, bwd)
    return op


SCAN_UNROLL = 4


def _group_scan(A, Bv, asc):
    row = lax.broadcasted_iota(jnp.int32, A.shape, 0)
    for s in (1, 2, 4):
        sh = s if asc else SUBLANE - s
        valid = (row >= s) if asc else (row < SUBLANE - s)
        A_sh = pltpu.roll(A, sh, axis=0)
        B_sh = pltpu.roll(Bv, sh, axis=0)
        Bv = jnp.where(valid, A * B_sh, 0.0) + Bv
        A = jnp.where(valid, A * A_sh, A)
    return A, Bv


def _chain_step(A, Bv, carry, asc):
    row = lax.broadcasted_iota(jnp.int32, A.shape, 0)
    A2, B2 = _group_scan(A, Bv, asc)
    h = A2 * carry + B2
    if asc:
        prev = jnp.where(row == 0, carry, pltpu.roll(h, 1, axis=0))
        return h, prev, h[SUBLANE - 1:SUBLANE, :]
    prev = jnp.where(row == SUBLANE - 1, carry, pltpu.roll(h, SUBLANE - 1, axis=0))
    return h, prev, h[0:1, :]


def _chain_loop(segments, step):
    carry = jnp.zeros((1, LANE), F32)
    for lo, hi, asc in segments:
        span = SUBLANE * SCAN_UNROLL
        assert (hi - lo) % span == 0

        def it(t, carry, lo=lo, hi=hi, asc=asc, span=span):
            base = lo + t * span if asc else hi - (t + 1) * span
            order = range(SCAN_UNROLL) if asc else reversed(range(SCAN_UNROLL))
            for j in order:
                carry = step(pl.multiple_of(base + SUBLANE * j, SUBLANE), carry, asc)
            return carry

        carry = lax.fori_loop(0, (hi - lo) // span, it, carry)
    return carry


def _scan_specs(T):
    return pl.BlockSpec((None, T, LANE), lambda j, b: (b, 0, j))


def _scan_fwd(a, b, S, reverse):
    B, T, D = a.shape
    asc = not reverse
    segments = [(S, T, asc), (0, S, asc)]

    def body(a_ref, b_ref, h_ref, hp_ref):
        def step(r0, carry, asc):
            rows = pl.ds(r0, SUBLANE)
            h, prev, carry = _chain_step(a_ref[rows, :], b_ref[rows, :], carry, asc)
            h_ref[rows, :] = h
            hp_ref[rows, :] = prev
            return carry
        _chain_loop(segments, step)

    spec = _scan_specs(T)
    return _call(body, name="scan_rev_fwd" if reverse else "scan_fwd_fwd", grid=(D // LANE, B),
                 in_specs=[spec, spec], out_specs=[spec, spec],
                 out_shape=[jax.ShapeDtypeStruct(a.shape, F32)] * 2,
                 compiler_params=_params(("parallel", "parallel")))(a, b)


def _scan_bwd(a, hp, dy, S, reverse):
    B, T, D = a.shape
    asc = reverse
    segments = [(0, S, asc), (S, T, asc)]

    def body(a_ref, hp_ref, dy_ref, da_ref, db_ref):
        def step(r0, carry, asc):
            rows = pl.ds(r0, SUBLANE)
            A, dy = a_ref[rows, :], dy_ref[rows, :]
            _, s_prev, carry = _chain_step(A, A * dy, carry, asc)
            g = dy + s_prev
            db_ref[rows, :] = g
            da_ref[rows, :] = g * hp_ref[rows, :]
            return carry
        _chain_loop(segments, step)

    spec = _scan_specs(T)
    return _call(body, name="scan_rev_bwd" if reverse else "scan_fwd_bwd", grid=(D // LANE, B),
                 in_specs=[spec, spec, spec], out_specs=[spec, spec],
                 out_shape=[jax.ShapeDtypeStruct(a.shape, F32)] * 2,
                 compiler_params=_params(("parallel", "parallel")))(a, hp, dy)


def _scan_op(S, reverse):
    @jax.custom_vjp
    def op(a, b):
        return _scan_fwd(a, b, S, reverse)[0]

    def fwd(a, b):
        h, hp = _scan_fwd(a, b, S, reverse)
        return h, (a, hp)

    def bwd(res, g):
        a, hp = res
        return tuple(_scan_bwd(a, hp, g, S, reverse))

    op.defvjp(fwd, bwd)
    return op


def _band_lo(qi, tq, S):
    span = tq + 2 * WINDOW
    return pl.multiple_of(jnp.clip(qi * tq - WINDOW, 0, S - span), LANE)


def _band_mask(qi, tq, lo, span, transposed):
    shape = (span, tq) if transposed else (tq, span)
    qpos = qi * tq + lax.broadcasted_iota(jnp.int32, shape, 1 if transposed else 0)
    kpos = lo + lax.broadcasted_iota(jnp.int32, shape, 0 if transposed else 1)
    return jnp.abs(kpos - qpos) <= WINDOW


NT = (((1,), (1,)), ((), ()))


def _attn_fwd(q, k, v, sink, S, band):
    B, T, HD = q.shape
    H = HD // HEAD_DIM
    L = T - S
    tq = L
    n_lq = S // tq
    span = tq + 2 * WINDOW
    scale = HEAD_DIM ** -0.5
    has_sink = sink is not None

    def body(*refs):
        if has_sink:
            q_ref, k_ref, v_ref, s_ref, o_ref = refs
            sk = s_ref[:, 0:1]
        else:
            q_ref, k_ref, v_ref, o_ref = refs
            sk = None
        qi = pl.program_id(2)
        qv = q_ref[...]

        def scores(kk, mask=None):
            s = lax.dot_general(qv, kk, NT, preferred_element_type=F32) * scale
            return s if mask is None else jnp.where(mask, s, NEG_INF)

        def finish(parts):
            m = functools.reduce(jnp.maximum, [jnp.max(s, axis=-1, keepdims=True) for s, _ in parts])
            if has_sink:
                m = jnp.maximum(m, sk)
            ps = [jnp.exp(s - m) for s, _ in parts]
            l = functools.reduce(jnp.add, [jnp.sum(p, axis=-1, keepdims=True) for p in ps])
            if has_sink:
                l = l + jnp.exp(sk - m)
            o = functools.reduce(jnp.add, [jnp.dot(p.astype(BF16), vv, preferred_element_type=F32)
                                           for p, (_, vv) in zip(ps, parts)])
            o_ref[...] = o * (1.0 / l)

        @pl.when(qi < n_lq)
        def _():
            if band:
                lo = _band_lo(qi, tq, S)
                slab = pl.ds(lo, span)
                finish([(scores(k_ref[S:T, :]), v_ref[S:T, :]),
                        (scores(k_ref[slab, :], _band_mask(qi, tq, lo, span, False)), v_ref[slab, :])])
            else:
                finish([(scores(k_ref[...]), v_ref[...])])

        @pl.when(qi >= n_lq)
        def _():
            finish([(scores(k_ref[S:T, :]), v_ref[S:T, :])])

    q_spec = pl.BlockSpec((None, tq, HEAD_DIM), lambda b, h, i: (b, i, h))
    kv_spec = pl.BlockSpec((None, T, HEAD_DIM), lambda b, h, i: (b, 0, h // KV_GROUP))
    in_specs = [q_spec, kv_spec, kv_spec]
    args = [q, k, v]
    if has_sink:
        in_specs.append(pl.BlockSpec((None, 1, LANE), lambda b, h, i: (h, 0, 0)))
        args.append(sink)
    return _call(body, name="attn_band_fwd" if band else "attn_dense_fwd", grid=(B, H, T // tq),
                 in_specs=in_specs, out_specs=q_spec, out_shape=jax.ShapeDtypeStruct(q.shape, F32),
                 compiler_params=_params(("parallel", "parallel", "parallel")))(*args)


def _attn_bwd(q, k, v, sink, do, S, band):
    B, T, HD = q.shape
    H = HD // HEAD_DIM
    KVH = H // KV_GROUP
    L = T - S
    tq = L
    n_lq = S // tq
    span = tq + 2 * WINDOW
    scale = HEAD_DIM ** -0.5
    has_sink = sink is not None

    def body(*refs):
        if has_sink:
            q_ref, k_ref, v_ref, do_ref, s_ref, dq_ref, dk_ref, dv_ref, ds_ref, *scr = refs
            sk = s_ref[:, 0:1]
        else:
            q_ref, k_ref, v_ref, do_ref, dq_ref, dk_ref, dv_ref, *scr = refs
            sk = None
        g, qi = pl.program_id(2), pl.program_id(3)
        first = jnp.logical_and(g == 0, qi == 0)

        @pl.when(first)
        def _():
            dk_ref[...] = jnp.zeros_like(dk_ref)
            dv_ref[...] = jnp.zeros_like(dv_ref)
            if not band:
                scr[0][...] = k_ref[...].astype(F32).T.astype(BF16)

        qv = q_ref[...]
        dov = do_ref[...].astype(BF16)

        def run(segs, dense_all):
            sT = []
            for rows, mask in segs:
                s = lax.dot_general(k_ref[rows, :], qv, NT, preferred_element_type=F32) * scale
                sT.append(s if mask is None else jnp.where(mask, s, NEG_INF))
            m = functools.reduce(jnp.maximum, [jnp.max(s, axis=0, keepdims=True) for s in sT])
            if has_sink:
                m = jnp.maximum(m, sk)
            pT = [jnp.exp(s - m) for s in sT]
            l = functools.reduce(jnp.add, [jnp.sum(p, axis=0, keepdims=True) for p in pT])
            if has_sink:
                psk = jnp.exp(sk - m)
                l = l + psk
            inv = 1.0 / l
            pT = [p * inv for p in pT]
            dpT = [lax.dot_general(v_ref[rows, :], dov, NT, preferred_element_type=F32) for rows, _ in segs]
            delta = functools.reduce(jnp.add, [jnp.sum(p * dp, axis=0, keepdims=True) for p, dp in zip(pT, dpT)])
            dq = None
            for (rows, _), p, dp in zip(segs, pT, dpT):
                ds = p * (dp - delta) * scale
                dsb = ds.astype(BF16)
                dv_ref[rows, :] += jnp.dot(p.astype(BF16), dov, preferred_element_type=F32)
                dk_ref[rows, :] += jnp.dot(dsb, qv, preferred_element_type=F32)
                if dense_all:
                    part = jnp.dot(scr[0][...], dsb, preferred_element_type=F32).T
                else:
                    part = jnp.dot(ds.T.astype(BF16), k_ref[rows, :], preferred_element_type=F32)
                dq = part if dq is None else dq + part
            dq_ref[...] = dq
            if has_sink:
                dsk = -jnp.sum(psk * inv * delta, axis=1, keepdims=True)

                @pl.when(qi == 0)
                def _():
                    ds_ref[...] = jnp.broadcast_to(dsk, ds_ref.shape)

                @pl.when(qi > 0)
                def _():
                    ds_ref[...] += jnp.broadcast_to(dsk, ds_ref.shape)

        ctx_rows = pl.ds(S, L)

        @pl.when(qi < n_lq)
        def _():
            if band:
                lo = _band_lo(qi, tq, S)
                run([(ctx_rows, None), (pl.ds(lo, span), _band_mask(qi, tq, lo, span, True))], False)
            else:
                run([(pl.ds(0, T), None)], True)

        @pl.when(qi >= n_lq)
        def _():
            run([(ctx_rows, None)], False)

    q_spec = pl.BlockSpec((None, tq, HEAD_DIM), lambda b, kv, g, i: (b, i, kv * KV_GROUP + g))
    kv_spec = pl.BlockSpec((None, T, HEAD_DIM), lambda b, kv, g, i: (b, 0, kv))
    in_specs = [q_spec, kv_spec, kv_spec, q_spec]
    out_specs = [q_spec, kv_spec, kv_spec]
    out_shape = [jax.ShapeDtypeStruct(q.shape, F32), jax.ShapeDtypeStruct(k.shape, F32),
                 jax.ShapeDtypeStruct(v.shape, F32)]
    args = [q, k, v, do]
    if has_sink:
        in_specs.append(pl.BlockSpec((None, 1, LANE), lambda b, kv, g, i: (kv * KV_GROUP + g, 0, 0)))
        args.append(sink)
        out_specs.append(pl.BlockSpec((None, None, 1, LANE), lambda b, kv, g, i: (b, kv * KV_GROUP + g, 0, 0)))
        out_shape.append(jax.ShapeDtypeStruct((B, H, 1, LANE), F32))
    res = _call(body, name="attn_band_bwd" if band else "attn_dense_bwd", grid=(B, KVH, KV_GROUP, T // tq),
                in_specs=in_specs, out_specs=out_specs, out_shape=out_shape,
                scratch_shapes=[] if band else [pltpu.VMEM((HEAD_DIM, T), BF16)],
                compiler_params=_params(("parallel", "parallel", "arbitrary", "arbitrary")))(*args)
    return (res[0], res[1], res[2], res[3] if has_sink else None)


def _prep_f(norm):
    def f(q, k, v, *rest):
        if norm:
            qg, kg, cos, sin = rest
        else:
            (cos, sin), qg, kg = rest, None, None

        def heads(x, g):
            outs = []
            for h in range(x.shape[1] // HEAD_DIM):
                xh = x[:, h * HEAD_DIM:(h + 1) * HEAD_DIM]
                if g is not None:
                    xh = xh * lax.rsqrt(jnp.mean(xh * xh, axis=-1, keepdims=True) + EPS) * g
                outs.append(xh * cos + _swap(xh) * sin)
            return jnp.concatenate(outs, axis=1) if len(outs) > 1 else outs[0]

        return heads(q, qg), heads(k, kg), v

    return f


def _attn_branch(S, tm, band, norm, has_sink):
    f = _prep_f(norm)
    name = "band" if band else "dense"

    def prep(q, k, v, gains, tabs):
        outs = [(q.shape[2], BF16), (k.shape[2], BF16), (v.shape[2], BF16)]
        return _row_fwd(f, "prep_" + name + "_fwd", [q, k, v], [], list(gains), list(tabs), outs, tm, S)

    def unpack(args):
        q, k, v = args[:3]
        rest = list(args[3:])
        gains = [rest.pop(0), rest.pop(0)] if norm else []
        sink = rest.pop(0) if has_sink else None
        return q, k, v, gains, sink, rest

    def sink_lanes(sink):
        return None if sink is None else jnp.broadcast_to(sink[:, None, None], (sink.shape[0], 1, LANE))

    @jax.custom_vjp
    def op(*args):
        q, k, v, gains, sink, tabs = unpack(args)
        qp, kp, vp = prep(q, k, v, gains, tabs)
        return _attn_fwd(qp, kp, vp, sink_lanes(sink), S, band)

    def fwd(*args):
        q, k, v, gains, sink, tabs = unpack(args)
        qp, kp, vp = prep(q, k, v, gains, tabs)
        o = _attn_fwd(qp, kp, vp, sink_lanes(sink), S, band)
        return o, (args, qp, kp, vp)

    def bwd(res, do):
        args, qp, kp, vp = res
        q, k, v, gains, sink, tabs = unpack(args)
        dqp, dkp, dvp, dsk = _attn_bwd(qp, kp, vp, sink_lanes(sink), do, S, band)
        dxs, _, dgains = _row_bwd(f, "prep_" + name + "_bwd", [q, k, v], [], list(gains), list(tabs),
                                  [dqp, dkp, dvp], tm, S)
        out = list(dxs) + list(dgains)
        if has_sink:
            out.append(jnp.sum(dsk[:, :, 0, 0], axis=0))
        return (*out, *[jnp.zeros_like(t) for t in tabs])

    op.defvjp(fwd, bwd)
    return op


def _final_loss(S, tm):
    def run(X, target, g):
        B, T, D = X.shape
        n_lat_tiles = S // tm

        def lossf(x, gg, tgt):
            y = x * lax.rsqrt(jnp.mean(x * x, axis=-1, keepdims=True) + EPS) * gg
            err = y - tgt
            return 0.5 * jnp.sum(jnp.sum(err * err, axis=-1, keepdims=True), axis=0, keepdims=True) / D

        def body(x_ref, t_ref, g_ref, loss_ref, dx_ref, dg_ref):
            b, i = pl.program_id(0), pl.program_id(1)
            tgt = t_ref[...]
            val, vjp = jax.vjp(lambda x, gg: lossf(x, gg, tgt), x_ref[...], g_ref[...])
            dx, dg = vjp(jnp.ones((1, 1), F32))
            lat = (i < n_lat_tiles).astype(F32)
            dx_ref[...] = dx * lat

            @pl.when(i == 0)
            def _():
                loss_ref[...] = jnp.zeros_like(loss_ref)

            @pl.when(jnp.logical_and(b == 0, i == 0))
            def _():
                dg_ref[...] = jnp.zeros_like(dg_ref)

            loss_ref[...] += jnp.broadcast_to(val * lat, loss_ref.shape)
            dg_ref[...] += dg * lat

        x_spec = pl.BlockSpec((None, tm, D), lambda b, i: (b, i, 0))
        t_spec = pl.BlockSpec((None, tm, D), lambda b, i: (b, jnp.minimum(i, n_lat_tiles - 1), 0))
        g_spec = pl.BlockSpec((1, D), lambda b, i: (0, 0))
        loss, dx, dg = _call(
            body, name="final_loss", grid=(B, T // tm), in_specs=[x_spec, t_spec, g_spec],
            out_specs=[pl.BlockSpec((None, 1, LANE), lambda b, i: (b, 0, 0)), x_spec, g_spec],
            out_shape=[jax.ShapeDtypeStruct((B, 1, LANE), F32), jax.ShapeDtypeStruct(X.shape, F32),
                       jax.ShapeDtypeStruct(g.shape, F32)],
            compiler_params=_params(("arbitrary", "arbitrary")))(X, target, g)
        return jnp.sum(loss[:, 0, 0]), dx, dg

    @jax.custom_vjp
    def op(X, target, g):
        return run(X, target, g)[0]

    def fwd(X, target, g):
        loss, dx, dg = run(X, target, g)
        return loss, (dx, dg, target)

    def bwd(res, ct):
        dx, dg, target = res
        return ct * dx, jnp.zeros_like(target), ct * dg

    op.defvjp(fwd, bwd)
    return op


def _prenorm_f(D):
    def f(x, mp, g):
        y = x * lax.rsqrt(jnp.mean(x * x, axis=-1, keepdims=True) + EPS) * g
        return (y * (1.0 + mp[:, D:2 * D]) + mp[:, 0:D],)
    return f


def _resid_f(D):
    def f(x, y, mp):
        return (x + mp[:, 2 * D:3 * D] * y,)
    return f


def _gate2_f(hf, hr, g):
    return ((hf + hr) * (g * jax.nn.sigmoid(g)),)


def _gate1_f(y, g):
    return (y * (g * jax.nn.sigmoid(g)),)


def _merge_f(D):
    def f(m3, pa, pb, pc):
        return (jax.nn.sigmoid(m3[:, 0:D]) * pa + jax.nn.sigmoid(m3[:, D:2 * D]) * pb
                + jax.nn.sigmoid(m3[:, 2 * D:3 * D]) * pc,)
    return f


def _coef_f(D):
    nblk = D // GATE_BLOCK

    def f(v, pv, wm):
        vb = v.astype(BF16)

        def gate(k):
            cols = []
            for j in range(nblk):
                r0 = (k * nblk + j) * GATE_BLOCK
                cols.append(jnp.dot(vb[:, j * GATE_BLOCK:(j + 1) * GATE_BLOCK],
                                    wm[r0:r0 + GATE_BLOCK, :].astype(BF16), preferred_element_type=F32))
            return jnp.concatenate(cols, axis=1)

        outs = []
        for d in range(2):
            r = jax.nn.sigmoid(gate(d) + pv[d:d + 1])
            i = jax.nn.sigmoid(gate(2 + d) + pv[2 + d:3 + d])
            la = r * pv[4 + d:5 + d]
            t = jnp.tanh(la)
            outs += [jnp.exp(la), jnp.sqrt(-2.0 * t / (1.0 - t)) * (i * v)]
        return tuple(outs)

    return f


def _gate_blocks(w):
    per = GATE_BLOCK // LRU_BLOCK_W
    n = w.shape[1]
    w5 = w.reshape(2, n // per, per, LRU_BLOCK_W, LRU_BLOCK_W)
    dense = jnp.einsum("djiab,ik->djiakb", w5, jnp.eye(per, dtype=w.dtype))
    return dense.reshape(2, (n // per) * GATE_BLOCK, GATE_BLOCK)


def _exchange(x, name, gather):
    out_shape = (N_DEV,) + x.shape if gather else x.shape

    def body(x_ref, o_ref, send_sems, recv_sems, local_sem):
        mx, my, mc = lax.axis_index("x"), lax.axis_index("y"), lax.axis_index("c")
        me = 4 * mx + 2 * my + mc

        def src(p):
            return x_ref if gather else x_ref.at[p]

        local = pltpu.make_async_copy(src(me), o_ref.at[me], local_sem)
        local.start()
        sends, recvs = [], []
        for k in range(1, N_DEV):
            px = 1 - mx if k & 4 else mx
            py = 1 - my if k & 2 else my
            pc = 1 - mc if k & 1 else mc
            peer = 4 * px + 2 * py + pc
            sends.append(pltpu.make_async_remote_copy(
                src_ref=src(peer), dst_ref=o_ref.at[me], send_sem=send_sems.at[k - 1],
                recv_sem=recv_sems.at[k - 1], device_id=(px, py, pc), device_id_type=pl.DeviceIdType.MESH))
            recvs.append(pltpu.make_async_remote_copy(
                src_ref=src(peer), dst_ref=o_ref.at[peer], send_sem=send_sems.at[k - 1],
                recv_sem=recv_sems.at[k - 1], device_id=(px, py, pc), device_id_type=pl.DeviceIdType.MESH))
        for cp in sends:
            cp.start()
        for cp in recvs:
            cp.wait_recv()
        for cp in sends:
            cp.wait_send()
        local.wait()

    hbm = pl.BlockSpec(memory_space=pltpu.HBM)
    return _call(body, name=name, in_specs=[hbm], out_specs=hbm,
                 out_shape=jax.ShapeDtypeStruct(out_shape, x.dtype),
                 scratch_shapes=[pltpu.SemaphoreType.DMA((N_DEV - 1,)), pltpu.SemaphoreType.DMA((N_DEV - 1,)),
                                 pltpu.SemaphoreType.DMA(())])(x)


def _adamw(gs, w, m, v, name):
    n, R, C = gs.shape
    tr = _pick(R, 256, 16)

    def body(g_ref, w_ref, m_ref, v_ref, go_ref, d_ref, mo_ref, vo_ref):
        g = g_ref[0].astype(F32)
        for p in range(1, n):
            g = g + g_ref[p].astype(F32)
        m2 = ADAM_B1 * m_ref[...] + (1.0 - ADAM_B1) * g
        v2 = ADAM_B2 * v_ref[...] + (1.0 - ADAM_B2) * (g * g)
        m_hat = m2 / (1.0 - ADAM_B1 ** ADAM_STEP)
        v_hat = v2 / (1.0 - ADAM_B2 ** ADAM_STEP)
        go_ref[...] = g
        d_ref[...] = -ADAM_LR * (m_hat / (jnp.sqrt(v_hat) + ADAM_EPS) + ADAM_WD * w_ref[...])
        mo_ref[...] = m2
        vo_ref[...] = v2

    spec = pl.BlockSpec((tr, C), lambda i: (i, 0))
    return _call(body, name=name, grid=(R // tr,),
                 in_specs=[pl.BlockSpec((n, tr, C), lambda i: (0, i, 0)), spec, spec, spec],
                 out_specs=[spec] * 4, out_shape=[jax.ShapeDtypeStruct((R, C), F32)] * 4,
                 compiler_params=_params(("parallel",)))(gs, w, m, v)


def _sum_slots(gs, name):
    n, R, C = gs.shape

    def body(g_ref, o_ref):
        g = g_ref[0].astype(F32)
        for p in range(1, n):
            g = g + g_ref[p].astype(F32)
        o_ref[...] = g

    return _call(body, name=name, out_shape=jax.ShapeDtypeStruct((R, C), F32))(gs)


def _rope_tables(S, L):
    P = HEAD_DIM // 4
    rows = S // GRID_W
    row_id = jnp.repeat(jnp.arange(rows), GRID_W)
    col_id = jnp.tile(jnp.arange(GRID_W), rows)
    inv = ROPE_THETA ** (-jnp.arange(P, dtype=F32) / P)
    ar, ac = row_id[:, None] * inv, col_id[:, None] * inv
    cos = jnp.concatenate([jnp.cos(ar), jnp.cos(ar), jnp.cos(ac), jnp.cos(ac)], axis=1)
    sin = jnp.concatenate([-jnp.sin(ar), jnp.sin(ar), -jnp.sin(ac), jnp.sin(ac)], axis=1)
    cos = jnp.concatenate([cos, jnp.ones((L, HEAD_DIM), F32)], axis=0)
    sin = jnp.concatenate([sin, jnp.zeros((L, HEAD_DIM), F32)], axis=0)
    return cos, sin


def _loss_fn(p, x, c, ctx, target):
    B, S, D = x.shape
    L = ctx.shape[1]
    T = S + L
    tm = min(L, 256)
    KVW = D // KV_GROUP
    widths = (D, D, D, KVW, KVW, D, D, KVW, KVW, D, 3 * D)
    offs = [0]
    for w in widths:
        offs.append(offs[-1] + w)
    cos, sin = _rope_tables(S, L)

    X = jnp.concatenate([x, ctx], axis=1)
    sc, scc = jax.nn.silu(c), jax.nn.silu(p["c_ctx"])
    A = jnp.concatenate([scc[None], sc, jnp.zeros((SUBLANE - 1 - B, D), F32)], axis=0)
    for l in range(DEPTH):
        mod = _matmul("mm_mod")(A, p["w_mod"][l]) + p["b_mod"][l]
        modp = jnp.stack([jnp.broadcast_to(mod[0], (B, 3 * D)), mod[1:1 + B]], axis=1)[:, :, None, :]
        (h,) = _rowwise(_prenorm_f(D), "prenorm", [(D, F32)], tm, S, 1, 1, 1)(X, modp, p["norm_g"][l][None])
        proj = _matmul("mm_in")(h.reshape(B * T, D), p["w_in"][l]).reshape(B, T, offs[-1])
        uA, gA, qB, kB, vB, gB, qC, kC, vC, gC, m3 = [proj[:, :, offs[i]:offs[i + 1]] for i in range(len(widths))]
        u = _conv_op(S)(uA, p["conv_w"][l], p["conv_b"][l][None])
        nsp = -LRU_C * jax.nn.softplus(-p["lru_lambda"][l])
        pv = jnp.concatenate([p["lru_ba"][l], p["lru_bx"][l], nsp, jnp.zeros((2, D), F32)], axis=0)
        wm = jnp.concatenate([_gate_blocks(p["lru_wa"][l]), _gate_blocks(p["lru_wx"][l])], axis=0)
        wm = wm.reshape(-1, GATE_BLOCK)
        af, bf, ar, br = _rowwise(_coef_f(D), "lru_coef", [(D, F32)] * 4, tm, S, 1, 0, 2)(u, pv, wm)
        hf = _scan_op(S, False)(af, bf)
        hr = _scan_op(S, True)(ar, br)
        (zA,) = _rowwise(_gate2_f, "gate_a", [(D, F32)], tm, S, 3, 0, 0)(hf, hr, gA)
        yB = _attn_branch(S, tm, True, False, True)(qB, kB, vB, p["attn_sink"][l], cos, sin)
        (zB,) = _rowwise(_gate1_f, "gate_b", [(D, F32)], tm, S, 2, 0, 0)(yB, gB)
        yC = _attn_branch(S, tm, False, True, False)(qC, kC, vC, p["q_norm_g"][l][None], p["k_norm_g"][l][None],
                                                     cos, sin)
        (zC,) = _rowwise(_gate1_f, "gate_c", [(D, F32)], tm, S, 2, 0, 0)(yC, gC)
        pr = [_matmul("mm_branch")(z.reshape(B * T, D), p["w_branch"][l][n]).reshape(B, T, D)
              for n, z in enumerate((zA, zB, zC))]
        (mg,) = _rowwise(_merge_f(D), "merge", [(D, F32)], tm, S, 4, 0, 0)(m3, *pr)
        y = _matmul("mm_out")(mg.reshape(B * T, D), p["w_out"][l]).reshape(B, T, D)
        (X,) = _rowwise(_resid_f(D), "resid", [(D, F32)], tm, S, 2, 1, 0)(X, y, modp)
    return _final_loss(S, tm)(X, target, p["final_g"][None])


def _shard_axis(name):
    return {"w_mod": 2, "w_in": 2, "conv_w": 2, "lru_ba": 2, "lru_bx": 2, "lru_lambda": 2,
            "w_branch": 2, "w_out": 1}.get(name)


def _unshard(g, axis):
    full = jnp.moveaxis(g, 0, axis)
    shape = list(full.shape)
    shape[axis:axis + 2] = [shape[axis] * shape[axis + 1]]
    return full.reshape(shape)


def _reshard(full, axis):
    shape = list(full.shape)
    shape[axis:axis + 1] = [N_DEV, shape[axis] // N_DEV]
    return jnp.moveaxis(full.reshape(shape), axis, 0)


def _pad_to(v, n):
    return jnp.concatenate([v, jnp.zeros((n - v.shape[0],), v.dtype)]) if n > v.shape[0] else v


def _step(x, c, ctx, target, w, m, v):
    full = {n: w[n] for n in REPLICATED}
    for n in BIG:
        full[n] = _unshard(_exchange(w[n].astype(BF16), "gather_" + n, True), _shard_axis(n))
    small_local = jnp.concatenate([w[n].reshape(-1) for n in SMALL_SHARDED])
    small_all = _exchange(small_local.reshape(-1, LANE), "gather_small", True).reshape(N_DEV, -1)
    off = 0
    for n in SMALL_SHARDED:
        sz = w[n].size
        full[n] = _unshard(small_all[:, off:off + sz].reshape((N_DEV,) + w[n].shape), _shard_axis(n))
        off += sz
    p = dict(full)
    for n in ("w_mod", "w_in", "w_out"):
        p[n] = [full[n][l] for l in range(DEPTH)]
    p["w_branch"] = [[full["w_branch"][l][k] for k in range(3)] for l in range(DEPTH)]

    loss, (gp, gx) = jax.value_and_grad(_loss_fn, argnums=(0, 1))(p, x, c, ctx, target)
    for n in ("w_mod", "w_in", "w_out"):
        gp[n] = jnp.stack(gp[n])
    gp["w_branch"] = jnp.stack([jnp.stack(g) for g in gp["w_branch"]])
    loss = lax.psum(loss, AXES)

    out = {}
    for n in BIG:
        recv = _exchange(_reshard(gp[n], _shard_axis(n)), "scatter_" + n, False)
        C = w[n].shape[-1]
        res = _adamw(recv.reshape(N_DEV, -1, C), w[n].reshape(-1, C), m[n].reshape(-1, C), v[n].reshape(-1, C),
                     "adamw_" + n)
        out[n] = [r.reshape(w[n].shape) for r in res]

    rep = jnp.concatenate([gp[n].reshape(-1) for n in REPLICATED])
    n_rep = rep.shape[0]
    chunk = -(-n_rep // (N_DEV * LANE)) * LANE
    rep = _pad_to(rep, N_DEV * chunk).reshape(N_DEV, chunk)
    shards = jnp.concatenate([_reshard(gp[n], _shard_axis(n)).reshape(N_DEV, -1) for n in SMALL_SHARDED], axis=1)
    n_sh = shards.shape[1]
    recv = _exchange(jnp.concatenate([rep, shards], axis=1).reshape(N_DEV, -1, LANE), "scatter_small", False)
    wl =jnp.concatenate([w[n].reshape(-1) for n in SMALL_SHARDED])
    ml = jnp.concatenate([m[n].reshape(-1) for n in SMALL_SHARDED])
    vl = jnp.concatenate([v[n].reshape(-1) for n in SMALL_SHARDED])
    rows = recv.shape[1]
    rrows = chunk // LANE
    g_sh, d_sh, m_sh, v_sh = _adamw(recv[:, rrows:], wl.reshape(-1, LANE), ml.reshape(-1, LANE),
                                    vl.reshape(-1, LANE), "adamw_small_sharded")
    g_rep8 = _sum_slots(recv[:, :rrows], "sum_replicated")
    g_rep = _exchange(g_rep8, "gather_replicated", True).reshape(-1)
    wr = _pad_to(jnp.concatenate([w[n].reshape(-1) for n in REPLICATED]), N_DEV * chunk)
    mr = _pad_to(jnp.concatenate([m[n].reshape(-1) for n in REPLICATED]), N_DEV * chunk)
    vr = _pad_to(jnp.concatenate([v[n].reshape(-1) for n in REPLICATED]), N_DEV * chunk)
    res_rep = _adamw(g_rep.reshape(1, -1, LANE), wr.reshape(-1, LANE), mr.reshape(-1, LANE), vr.reshape(-1, LANE),
                     "adamw_replicated")
    off = 0
    for n in REPLICATED:
        sz = w[n].size
        out[n] = [r.reshape(-1)[off:off + sz].reshape(w[n].shape) for r in res_rep]
        off += sz
    off = 0
    for n in SMALL_SHARDED:
        sz = w[n].size
        out[n] = [r.reshape(-1)[off:off + sz].reshape(w[n].shape) for r in (g_sh, d_sh, m_sh, v_sh)]
        off += sz
    assert off == n_sh and rows == rrows + n_sh // LANE
    return (loss, gx, *[out[n][0] for n in WEIGHTS], *[out[n][1] for n in WEIGHTS],
            *[out[n][2] for n in WEIGHTS], *[out[n][3] for n in WEIGHTS])


def kernel(x, c, ctx, c_ctx, norm_g, w_mod, b_mod, w_in, conv_w, conv_b, lru_wa, lru_ba, lru_wx, lru_bx, lru_lambda, attn_sink, q_norm_g, k_norm_g, w_branch, w_out, final_g, loss_target, m_c_ctx, m_norm_g, m_w_mod, m_b_mod, m_w_in, m_conv_w, m_conv_b, m_lru_wa, m_lru_ba, m_lru_wx, m_lru_bx, m_lru_lambda, m_attn_sink, m_q_norm_g, m_k_norm_g, m_w_branch, m_w_out, m_final_g, v_c_ctx, v_norm_g, v_w_mod, v_b_mod, v_w_in, v_conv_w, v_conv_b, v_lru_wa, v_lru_ba, v_lru_wx, v_lru_bx, v_lru_lambda, v_attn_sink, v_q_norm_g, v_k_norm_g, v_w_branch, v_w_out, v_final_g):
    w = dict(zip(WEIGHTS, (c_ctx, norm_g, w_mod, b_mod, w_in, conv_w, conv_b, lru_wa, lru_ba, lru_wx, lru_bx,
                           lru_lambda, attn_sink, q_norm_g, k_norm_g, w_branch, w_out, final_g)))
    m = dict(zip(WEIGHTS, (m_c_ctx, m_norm_g, m_w_mod, m_b_mod, m_w_in, m_conv_w, m_conv_b, m_lru_wa, m_lru_ba,
                           m_lru_wx, m_lru_bx, m_lru_lambda, m_attn_sink, m_q_norm_g, m_k_norm_g, m_w_branch,
                           m_w_out, m_final_g)))
    v = dict(zip(WEIGHTS, (v_c_ctx, v_norm_g, v_w_mod, v_b_mod, v_w_in, v_conv_w, v_conv_b, v_lru_wa, v_lru_ba,
                           v_lru_wx, v_lru_bx, v_lru_lambda, v_attn_sink, v_q_norm_g, v_k_norm_g, v_w_branch,
                           v_w_out, v_final_g)))
    return _step(x, c, ctx, loss_target, w, m, v)
```

```python
import functools

import jax
import jax.numpy as jnp
from jax import lax
from jax.experimental import pallas as pl
from jax.experimental.pallas import tpu as pltpu

F32 = jnp.float32
BF16 = jnp.bfloat16

AXES = ("x", "y", "c")
N_DEV = 8
DEPTH = 4
HEAD_DIM = 128
GRID_W = 64
WINDOW = 128
LRU_BLOCK_W = 64
GATE_BLOCK = 256
LRU_C = 8.0
ROPE_THETA = 10000.0
EPS = 1e-6
NEG_INF = -1e30
KV_GROUP = 4
LANE = 128
SUBLANE = 8
VMEM_LIMIT = 56 * 1024 * 1024

ADAM_LR = 0.001
ADAM_B1 = 0.9
ADAM_B2 = 0.999
ADAM_EPS = 1e-08
ADAM_WD = 0.01
ADAM_STEP = 10

WEIGHTS = ("c_ctx", "norm_g", "w_mod", "b_mod", "w_in", "conv_w", "conv_b", "lru_wa", "lru_ba", "lru_wx",
           "lru_bx", "lru_lambda", "attn_sink", "q_norm_g", "k_norm_g", "w_branch", "w_out", "final_g")
BIG = ("w_mod", "w_in", "w_branch", "w_out")
SMALL_SHARDED = ("conv_w", "lru_ba", "lru_bx", "lru_lambda")
REPLICATED = ("c_ctx", "norm_g", "b_mod", "conv_b", "lru_wa", "lru_wx", "attn_sink", "q_norm_g", "k_norm_g",
              "final_g")


def _call(body, **kw):
    return pl.pallas_call(body, **kw)


def _params(dims=None, vmem=VMEM_LIMIT):
    return pltpu.CompilerParams(dimension_semantics=dims, vmem_limit_bytes=vmem)


def _pick(n, target, mult):
    for t in range(min(n, target), 0, -1):
        if n % t == 0 and t % mult == 0:
            return t
    return n


def _mm(a, b, *, name, ta=False, tb=False, out_dtype=F32, tm=512, tn=1024, tk=1024):
    M, K = (a.shape[1], a.shape[0]) if ta else a.shape
    N = b.shape[0] if tb else b.shape[1]
    assert (b.shape[1] if tb else b.shape[0]) == K
    tm = _pick(M, tm, LANE if ta else 16)
    tn = _pick(N, tn, LANE)
    tk = _pick(K, tk, 16 if ta and not tb else LANE)
    nk = K // tk
    dn = (((0 if ta else 1,), (1 if tb else 0,)), ((), ()))

    def body(a_ref, b_ref, o_ref, *acc):
        r = lax.dot_general(a_ref[...].astype(BF16), b_ref[...].astype(BF16), dn,
                            preferred_element_type=F32)
        if nk == 1:
            o_ref[...] = r.astype(o_ref.dtype)
        else:
            k = pl.program_id(2)

            @pl.when(k == 0)
            def _():
                acc[0][...] = r

            @pl.when(k > 0)
            def _():
                acc[0][...] += r

            @pl.when(k == nk - 1)
            def _():
                o_ref[...] = acc[0][...].astype(o_ref.dtype)

    a_spec = (pl.BlockSpec((tk, tm), lambda i, j, k: (k, i)) if ta
              else pl.BlockSpec((tm, tk), lambda i, j, k: (i, k)))
    b_spec = (pl.BlockSpec((tn, tk), lambda i, j, k: (j, k)) if tb
              else pl.BlockSpec((tk, tn), lambda i, j, k: (k, j)))
    return _call(
        body, name=name, grid=(M // tm, N // tn, nk),
        in_specs=[a_spec, b_spec],
        out_specs=pl.BlockSpec((tm, tn), lambda i, j, k: (i, j)),
        out_shape=jax.ShapeDtypeStruct((M, N), out_dtype),
        scratch_shapes=[pltpu.VMEM((tm, tn), F32)] if nk > 1 else [],
        compiler_params=_params(("parallel", "parallel", "arbitrary")),
    )(a, b)


def _matmul(name, t_fwd=(512, 1024, 1024), t_da=(512, 1024, 1024), t_dw=(1024, 1024, 512)):
    def tiles(t):
        return dict(tm=t[0], tn=t[1], tk=t[2])

    @jax.custom_vjp
    def f(a, w):
        return _mm(a, w, name=name + "_fwd", **tiles(t_fwd))

    def fwd(a, w):
        return f(a, w), (a, w)

    def bwd(res, g):
        a, w = res
        da = _mm(g, w, name=name + "_da", tb=True, **tiles(t_da))
        dw = _mm(a, g, name=name + "_dw", ta=True, out_dtype=w.dtype, **tiles(t_dw))
        return da, dw

    f.defvjp(fwd, bwd)
    return f


def _row_specs(xs, tps, gps, rts, tm, n_lat_tiles):
    x_specs = [pl.BlockSpec((None, tm, x.shape[2]), lambda b, i: (b, i, 0)) for x in xs]
    tp_specs = [pl.BlockSpec((None, None, 1, p.shape[3]),
                             lambda b, i: (b, (i < n_lat_tiles).astype(jnp.int32), 0, 0)) for p in tps]
    gp_specs = [pl.BlockSpec(p.shape, lambda b, i: (0, 0)) for p in gps]
    rt_specs = [pl.BlockSpec((tm, t.shape[1]), lambda b, i: (i, 0)) for t in rts]
    return x_specs, tp_specs, gp_specs, rt_specs


def _row_fwd(f, name, xs, tps, gps, rts, outs, tm, n_lat):
    B, T, _ = xs[0].shape
    n_in = len(xs) + len(tps) + len(gps) + len(rts)

    def body(*refs):
        vals = f(*[r[...] for r in refs[:n_in]])
        for o, v in zip(refs[n_in:], vals):
            o[...] = v.astype(o.dtype)

    x_specs, tp_specs, gp_specs, rt_specs = _row_specs(xs, tps, gps, rts, tm, n_lat // tm)
    res = _call(
        body, name=name, grid=(B, T // tm),
        in_specs=x_specs + tp_specs + gp_specs + rt_specs,
        out_specs=[pl.BlockSpec((None, tm, w), lambda b, i: (b, i, 0)) for w, _ in outs],
        out_shape=[jax.ShapeDtypeStruct((B, T, w), dt) for w, dt in outs],
        compiler_params=_params(("parallel", "parallel")),
    )(*xs, *tps, *gps, *rts)
    return list(res)


def _row_bwd(f, name, xs, tps, gps, rts, douts, tm, n_lat):
    B, T, _ = xs[0].shape
    nx, ntp, ngp, nd = len(xs), len(tps), len(gps), len(douts)
    n_lat_tiles = n_lat // tm
    n_diff = nx + ntp + ngp
    n_in = n_diff + len(rts)

    def body(*refs):
        diff = [r[...] for r in refs[:n_diff]]
        tabs = [r[...] for r in refs[n_diff:n_in]]
        dos = [r[...] for r in refs[n_in:n_in + nd]]
        o_refs = refs[n_in + nd:]
        prim, vjp = jax.vjp(lambda *d: tuple(f(*d, *tabs)), *diff)
        grads = vjp(tuple(d.astype(p.dtype) for d, p in zip(dos, prim)))
        b, i = pl.program_id(0), pl.program_id(1)
        for k in range(nx):
            o_refs[k][...] = grads[k].astype(o_refs[k].dtype)
        first_tp = jnp.logical_or(i == 0, i == n_lat_tiles)
        first_gp = jnp.logical_and(b == 0, i == 0)
        for k in range(nx, n_diff):
            first = first_tp if k < nx + ntp else first_gp

            @pl.when(first)
            def _(k=k):
                o_refs[k][...] = grads[k]

            @pl.when(jnp.logical_not(first))
            def _(k=k):
                o_refs[k][...] += grads[k]

    x_specs, tp_specs, gp_specs, rt_specs = _row_specs(xs, tps, gps, rts, tm, n_lat_tiles)
    d_specs = [pl.BlockSpec((None, tm, d.shape[2]), lambda b, i: (b, i, 0)) for d in douts]
    res = _call(
        body, name=name, grid=(B, T // tm),
        in_specs=x_specs + tp_specs + gp_specs + rt_specs + d_specs,
        out_specs=x_specs + tp_specs + gp_specs,
        out_shape=[jax.ShapeDtypeStruct(a.shape, F32) for a in (*xs, *tps, *gps)],
        compiler_params=_params(("arbitrary", "arbitrary")),
    )(*xs, *tps, *gps, *rts, *douts)
    res = list(res)
    return res[:nx], res[nx:nx + ntp], res[nx + ntp:]


def _rowwise(f, name, outs, tm, n_lat, n_x, n_tp, n_gp):
    def split(args):
        return (args[:n_x], args[n_x:n_x + n_tp], args[n_x + n_tp:n_x + n_tp + n_gp],
                args[n_x + n_tp + n_gp:])

    @jax.custom_vjp
    def op(*args):
        xs, tps, gps, rts = split(args)
        return tuple(_row_fwd(f, name + "_fwd", xs, tps, gps, rts, outs, tm, n_lat))

    def fwd(*args):
        return op(*args), args

    def bwd(args, g):
        xs, tps, gps, rts = split(args)
        dxs, dtps, dgps = _row_bwd(f, name + "_bwd", xs, tps, gps, rts, list(g), tm, n_lat)
        return (*dxs, *dtps, *dgps, *[jnp.zeros_like(t) for t in rts])

    op.defvjp(fwd, bwd)
    return op


def _shift_impl(u, k):
    n = u.shape[0]
    r = pltpu.roll(u, k % n, axis=0)
    row = lax.broadcasted_iota(jnp.int32, u.shape, 0)
    valid = (row >= k) if k > 0 else (row < n + k)
    return jnp.where(valid, r, 0.0)


@functools.partial(jax.custom_vjp, nondiff_argnums=(1,))
def _shift(u, k):
    return _shift_impl(u, k)


_shift.defvjp(lambda u, k: (_shift_impl(u, k), None), lambda k, _, g: (_shift_impl(g, -k),))


def _swap_impl(x):
    lane = lax.broadcasted_iota(jnp.int32, x.shape, 1)
    q = HEAD_DIM // 4
    return jnp.where((lane % (2 * q)) < q, pltpu.roll(x, HEAD_DIM - q, axis=1), pltpu.roll(x, q, axis=1))


@jax.custom_vjp
def _swap(x):
    return _swap_impl(x)


_swap.defvjp(lambda x: (_swap_impl(x), None), lambda _, g: (_swap_impl(g),))


def _conv_f(ul, uc, cw, cb):
    def conv(u):
        return (_shift(u, 2) * cw[0:1] + _shift(u, 1) * cw[1:2] + u * cw[2:3] + _shift(u, -1) * cw[3:4] + cb)
    return conv(ul), conv(uc)


def _conv_specs(B, T, D):
    u_spec = pl.BlockSpec((None, T, LANE), lambda j, b: (b, 0, j))
    cw_spec = pl.BlockSpec((4, LANE), lambda j, b: (0, j))
    cb_spec = pl.BlockSpec((1, LANE), lambda j, b: (0, j))
    return u_spec, cw_spec, cb_spec


def _conv_fwd(u, cw, cb, S):
    B, T, D = u.shape

    def body(u_ref, cw_ref, cb_ref, o_ref):
        vl, vc = _conv_f(u_ref[0:S, :], u_ref[S:T, :], cw_ref[...], cb_ref[...])
        o_ref[0:S, :] = vl
        o_ref[S:T, :] = vc

    u_spec, cw_spec, cb_spec = _conv_specs(B, T, D)
    return _call(body, name="conv_fwd", grid=(D // LANE, B), in_specs=[u_spec, cw_spec, cb_spec],
                 out_specs=u_spec, out_shape=jax.ShapeDtypeStruct(u.shape, F32),
                 compiler_params=_params(("parallel", "parallel")))(u, cw, cb)


def _conv_bwd(u, cw, cb, dv, S):
    B, T, D = u.shape

    def body(u_ref, cw_ref, cb_ref, dv_ref, du_ref, dcw_ref, dcb_ref):
        _, vjp = jax.vjp(_conv_f, u_ref[0:S, :], u_ref[S:T, :], cw_ref[...], cb_ref[...])
        dul, duc, dcw, dcb = vjp((dv_ref[0:S, :], dv_ref[S:T, :]))
        du_ref[0:S, :] = dul
        du_ref[S:T, :] = duc
        first = pl.program_id(1) == 0

        @pl.when(first)
        def _():
            dcw_ref[...] = dcw
            dcb_ref[...] = dcb

        @pl.when(jnp.logical_not(first))
        def _():
            dcw_ref[...] += dcw
            dcb_ref[...] += dcb

    u_spec, cw_spec, cb_spec = _conv_specs(B, T, D)
    return _call(body, name="conv_bwd", grid=(D // LANE, B), in_specs=[u_spec, cw_spec, cb_spec, u_spec],
                 out_specs=[u_spec, cw_spec, cb_spec],
                 out_shape=[jax.ShapeDtypeStruct(u.shape, F32), jax.ShapeDtypeStruct(cw.shape, F32),
                            jax.ShapeDtypeStruct(cb.shape, F32)],
                 compiler_params=_params(("parallel", "arbitrary")))(u, cw, cb, dv)


def _conv_op(S):
    @jax.custom_vjp
    def op(u, cw, cb):
        return _conv_fwd(u, cw, cb, S)

    def fwd(u, cw, cb):
        return op(u, cw, cb), (u, cw, cb)

    def bwd(res, g):
        u, cw, cb = res
        return tuple(_conv_bwd(u, cw, cb, g, S))

    op.defvjp(fwd, bwd)
    return op


SCAN_UNROLL = 4


def _group_scan(A, Bv, asc):
    row = lax.broadcasted_iota(jnp.int32, A.shape, 0)
    for s in (1, 2, 4):
        sh = s if asc else SUBLANE - s
        valid = (row >= s) if asc else (row < SUBLANE - s)
        A_sh = pltpu.roll(A, sh, axis=0)
        B_sh = pltpu.roll(Bv, sh, axis=0)
        Bv = jnp.where(valid, A * B_sh, 0.0) + Bv
        A = jnp.where(valid, A * A_sh, A)
    return A, Bv


def _chain_step(A, Bv, carry, asc):
    row = lax.broadcasted_iota(jnp.int32, A.shape, 0)
    A2, B2 = _group_scan(A, Bv, asc)
    h = A2 * carry + B2
    if asc:
        prev = jnp.where(row == 0, carry, pltpu.roll(h, 1, axis=0))
        return h, prev, h[SUBLANE - 1:SUBLANE, :]
    prev = jnp.where(row == SUBLANE - 1, carry, pltpu.roll(h, SUBLANE - 1, axis=0))
    return h, prev, h[0:1, :]


def _chain_loop(segments, step):
    carry = jnp.zeros((1, LANE), F32)
    for lo, hi, asc in segments:
        span = SUBLANE * SCAN_UNROLL
        assert (hi - lo) % span == 0

        def it(t, carry, lo=lo, hi=hi, asc=asc, span=span):
            base = lo + t * span if asc else hi - (t + 1) * span
            order = range(SCAN_UNROLL) if asc else reversed(range(SCAN_UNROLL))
            for j in order:
                carry = step(pl.multiple_of(base + SUBLANE * j, SUBLANE), carry, asc)
            return carry

        carry = lax.fori_loop(0, (hi - lo) // span, it, carry)
    return carry


def _scan_specs(T):
    return pl.BlockSpec((None, T, LANE), lambda j, b: (b, 0, j))


def _scan_fwd(a, b, S, reverse):
    B, T, D = a.shape
    asc = not reverse
    segments = [(S, T, asc), (0, S, asc)]

    def body(a_ref, b_ref, h_ref, hp_ref):
        def step(r0, carry, asc):
            rows = pl.ds(r0, SUBLANE)
            h, prev, carry = _chain_step(a_ref[rows, :], b_ref[rows, :], carry, asc)
            h_ref[rows, :] = h
            hp_ref[rows, :] = prev
            return carry
        _chain_loop(segments, step)

    spec = _scan_specs(T)
    return _call(body, name="scan_rev_fwd" if reverse else "scan_fwd_fwd", grid=(D // LANE, B),
                 in_specs=[spec, spec], out_specs=[spec, spec],
                 out_shape=[jax.ShapeDtypeStruct(a.shape, F32)] * 2,
                 compiler_params=_params(("parallel", "parallel")))(a, b)


def _scan_bwd(a, hp, dy, S, reverse):
    B, T, D = a.shape
    asc = reverse
    segments = [(0, S, asc), (S, T, asc)]

    def body(a_ref, hp_ref, dy_ref, da_ref, db_ref):
        def step(r0, carry, asc):
            rows = pl.ds(r0, SUBLANE)
            A, dy = a_ref[rows, :], dy_ref[rows, :]
            _, s_prev, carry = _chain_step(A, A * dy, carry, asc)
            g = dy + s_prev
            db_ref[rows, :] = g
            da_ref[rows, :] = g * hp_ref[rows, :]
            return carry
        _chain_loop(segments, step)

    spec = _scan_specs(T)
    return _call(body, name="scan_rev_bwd" if reverse else "scan_fwd_bwd", grid=(D // LANE, B),
                 in_specs=[spec, spec, spec], out_specs=[spec, spec],
                 out_shape=[jax.ShapeDtypeStruct(a.shape, F32)] * 2,
                 compiler_params=_params(("parallel", "parallel")))(a, hp, dy)


def _scan_op(S, reverse):
    @jax.custom_vjp
    def op(a, b):
        return _scan_fwd(a, b, S, reverse)[0]

    def fwd(a, b):
        h, hp = _scan_fwd(a, b, S, reverse)
        return h, (a, hp)

    def bwd(res, g):
        a, hp = res
        return tuple(_scan_bwd(a, hp, g, S, reverse))

    op.defvjp(fwd, bwd)
    return op


def _band_lo(qi, tq, S):
    span = tq + 2 * WINDOW
    return pl.multiple_of(jnp.clip(qi * tq - WINDOW, 0, S - span), LANE)


def _band_mask(qi, tq, lo, span, transposed):
    shape = (span, tq) if transposed else (tq, span)
    qpos = qi * tq + lax.broadcasted_iota(jnp.int32, shape, 1 if transposed else 0)
    kpos = lo + lax.broadcasted_iota(jnp.int32, shape, 0 if transposed else 1)
    return jnp.abs(kpos - qpos) <= WINDOW


NT = (((1,), (1,)), ((), ()))
LOG2E = 1.4426950408889634
KEY_CHUNK = 1024


def _col_to_row(c):
    return jnp.broadcast_to(c, (c.shape[0], LANE)).T[0:1, :]


def _attn_fwd(q, k, v, sink, S, band):
    B, T, HD = q.shape
    H = HD // HEAD_DIM
    L = T - S
    tq = L
    n_lq = S // tq
    span = tq + 2 * WINDOW
    c2 = HEAD_DIM ** -0.5 * LOG2E
    has_sink = sink is not None
    kc = min(KEY_CHUNK, S)

    def body(*refs):
        if has_sink:
            q_ref, k_ref, v_ref, s_ref, o_ref, lse_ref = refs
            sk2 = s_ref[:, 0:1] * LOG2E
        else:
            q_ref, k_ref, v_ref, o_ref, lse_ref = refs
            sk2 = None
        qi = pl.program_id(2)
        qv = q_ref[...]

        def run(segs):
            m = l = acc = None
            for rows, mask in segs:
                s = lax.dot_general(qv, k_ref[rows, :], NT, preferred_element_type=F32) * c2
                if mask is not None:
                    s = jnp.where(mask, s, NEG_INF)
                ms = jnp.max(s, axis=-1, keepdims=True)
                if m is None:
                    m = ms if sk2 is None else jnp.maximum(ms, sk2)
                    p = jnp.exp2(s - m)
                    l = jnp.sum(p, axis=-1, keepdims=True)
                    if sk2 is not None:
                        l = l + jnp.exp2(sk2 - m)
                    acc = jnp.dot(p.astype(BF16), v_ref[rows, :], preferred_element_type=F32)
                else:
                    m_new = jnp.maximum(m, ms)
                    alpha = jnp.exp2(m - m_new)
                    p = jnp.exp2(s - m_new)
                    l = alpha * l + jnp.sum(p, axis=-1, keepdims=True)
                    acc = alpha * acc + jnp.dot(p.astype(BF16), v_ref[rows, :], preferred_element_type=F32)
                    m = m_new
            o_ref[...] = acc * (1.0 / l)
            lse_ref[...] = _col_to_row(m + jnp.log2(l))

        ctx_rows = pl.ds(S, L)

        @pl.when(qi < n_lq)
        def _():
            if band:
                lo = _band_lo(qi, tq, S)
                run([(ctx_rows, None), (pl.ds(lo, span), _band_mask(qi, tq, lo, span, False))])
            else:
                run([(pl.ds(j * kc, kc), None) for j in range(S // kc)] + [(ctx_rows, None)])

        @pl.when(qi >= n_lq)
        def _():
            run([(ctx_rows, None)])

    q_spec = pl.BlockSpec((None, tq, HEAD_DIM), lambda b, h, i: (b, i, h))
    kv_spec = pl.BlockSpec((None, T, HEAD_DIM), lambda b, h, i: (b, 0, h // KV_GROUP))
    in_specs = [q_spec, kv_spec, kv_spec]
    args = [q, k, v]
    if has_sink:
        in_specs.append(pl.BlockSpec((None, 1, LANE), lambda b, h, i: (h, 0, 0)))
        args.append(sink)
    return _call(body, name="attn_band_fwd" if band else "attn_dense_fwd", grid=(B, H, T // tq),
                 in_specs=in_specs,
                 out_specs=[q_spec, pl.BlockSpec((None, None, 1, tq), lambda b, h, i: (b, h, 0, i))],
                 out_shape=[jax.ShapeDtypeStruct(q.shape, F32), jax.ShapeDtypeStruct((B, H, 1, T), F32)],
                 compiler_params=_params(("parallel", "parallel", "parallel")))(*args)


def _attn_bwd(q, k, v, sink, do, o, lse, S, band):
    B, T, HD = q.shape
    H = HD // HEAD_DIM
    KVH = H // KV_GROUP
    L = T - S
    tq = L
    n_lq = S // tq
    span = tq + 2 * WINDOW
    scale = HEAD_DIM ** -0.5
    c2 = scale * LOG2E
    has_sink = sink is not None
    kc = min(KEY_CHUNK, S)

    def body(*refs):
        if has_sink:
            q_ref, k_ref, v_ref, do_ref, o_ref, lse_ref, s_ref, dq_ref, dk_ref, dv_ref, ds_ref, *scr = refs
        else:
            q_ref, k_ref, v_ref, do_ref, o_ref, lse_ref, dq_ref, dk_ref, dv_ref, *scr = refs
        g, qi = pl.program_id(2), pl.program_id(3)

        @pl.when(jnp.logical_and(g == 0, qi == 0))
        def _():
            dk_ref[...] = jnp.zeros_like(dk_ref)
            dv_ref[...] = jnp.zeros_like(dv_ref)
            if not band:
                scr[0][...] = k_ref[...].astype(F32).T.astype(BF16)

        qv = q_ref[...]
        dov = do_ref[...].astype(BF16)

        def run(segs):
            lse2 = lse_ref[...]
            delta = _col_to_row(jnp.sum(do_ref[...] * o_ref[...], axis=-1, keepdims=True))
            dq, dqT = None, None
            for lo, n, mask in segs:
                rows = pl.ds(lo, n)
                s = lax.dot_general(k_ref[rows, :], qv, NT, preferred_element_type=F32) * c2
                if mask is not None:
                    s = jnp.where(mask, s, NEG_INF)
                p = jnp.exp2(s - lse2)
                dp = lax.dot_general(v_ref[rows, :], dov, NT, preferred_element_type=F32)
                ds = p * (dp - delta)
                dsb = ds.astype(BF16)
                dv_ref[rows, :] += jnp.dot(p.astype(BF16), dov, preferred_element_type=F32)
                dk_ref[rows, :] += jnp.dot(dsb, qv, preferred_element_type=F32) * scale
                if band:
                    part = jnp.dot(ds.T.astype(BF16), k_ref[rows, :], preferred_element_type=F32)
                    dq = part if dq is None else dq + part
                else:
                    part = jnp.dot(scr[0][:, lo:lo + n], dsb, preferred_element_type=F32)
                    dqT = part if dqT is None else dqT + part
            dq_ref[...] = (dq if band else dqT.T) * scale
            if has_sink:
                psk = jnp.exp2(s_ref[:, 0:1] * LOG2E - lse2)
                dsk = -jnp.sum(psk * delta, axis=1, keepdims=True)

                @pl.when(qi == 0)
                def _():
                    ds_ref[...] = jnp.broadcast_to(dsk, ds_ref.shape)

                @pl.when(qi > 0)
                def _():
                    ds_ref[...] += jnp.broadcast_to(dsk, ds_ref.shape)

        @pl.when(qi < n_lq)
        def _():
            if band:
                lo = _band_lo(qi, tq, S)
                run([(S, L, None), (lo, span, _band_mask(qi, tq, lo, span, True))])
            else:
                run([(j * kc, kc, None) for j in range(S // kc)] + [(S, L, None)])

        @pl.when(qi >= n_lq)
        def _():
            run([(S, L, None)])

    q_spec = pl.BlockSpec((None, tq, HEAD_DIM), lambda b, kv, g, i: (b, i, kv * KV_GROUP + g))
    kv_spec = pl.BlockSpec((None, T, HEAD_DIM), lambda b, kv, g, i: (b, 0, kv))
    lse_spec = pl.BlockSpec((None, None, 1, tq), lambda b, kv, g, i: (b, kv * KV_GROUP + g, 0, i))
    in_specs = [q_spec, kv_spec, kv_spec, q_spec, q_spec, lse_spec]
    out_specs = [q_spec, kv_spec, kv_spec]
    out_shape = [jax.ShapeDtypeStruct(q.shape, F32), jax.ShapeDtypeStruct(k.shape, F32),
                 jax.ShapeDtypeStruct(v.shape, F32)]
    args = [q, k, v, do, o, lse]
    if has_sink:
        in_specs.append(pl.BlockSpec((None, 1, LANE), lambda b, kv, g, i: (kv * KV_GROUP + g, 0, 0)))
        args.append(sink)
        out_specs.append(pl.BlockSpec((None, None, 1, LANE), lambda b, kv, g, i: (b, kv * KV_GROUP + g, 0, 0)))
        out_shape.append(jax.ShapeDtypeStruct((B, H, 1, LANE), F32))
    res = _call(body, name="attn_band_bwd" if band else "attn_dense_bwd", grid=(B, KVH, KV_GROUP, T // tq),
                in_specs=in_specs, out_specs=out_specs, out_shape=out_shape,
                scratch_shapes=[] if band else [pltpu.VMEM((HEAD_DIM, T), BF16)],
                compiler_params=_params(("parallel", "parallel", "arbitrary", "arbitrary")))(*args)
    return (res[0], res[1], res[2], res[3] if has_sink else None)


def _prep_f(norm):
    def f(q, k, v, *rest):
        if norm:
            qg, kg, cos, sin = rest
        else:
            (cos, sin), qg, kg = rest, None, None

        def heads(x, g):
            outs = []
            for h in range(x.shape[1] // HEAD_DIM):
                xh = x[:, h * HEAD_DIM:(h + 1) * HEAD_DIM]
                if g is not None:
                    xh = xh * lax.rsqrt(jnp.mean(xh * xh, axis=-1, keepdims=True) + EPS) * g
                outs.append(xh * cos + _swap(xh) * sin)
            return jnp.concatenate(outs, axis=1) if len(outs) > 1 else outs[0]

        return heads(q, qg), heads(k, kg), v

    return f


def _attn_branch(S, tm, band, norm, has_sink):
    f = _prep_f(norm)
    name = "band" if band else "dense"

    def prep(q, k, v, gains, tabs):
        outs = [(q.shape[2], BF16), (k.shape[2], BF16), (v.shape[2], BF16)]
        return _row_fwd(f, "prep_" + name + "_fwd", [q, k, v], [], list(gains), list(tabs), outs, tm, S)

    def unpack(args):
        q, k, v = args[:3]
        rest = list(args[3:])
        gains = [rest.pop(0), rest.pop(0)] if norm else []
        sink = rest.pop(0) if has_sink else None
        return q, k, v, gains, sink, rest

    def sink_lanes(sink):
        return None if sink is None else jnp.broadcast_to(sink[:, None, None], (sink.shape[0], 1, LANE))

    @jax.custom_vjp
    def op(*args):
        q, k, v, gains, sink, tabs = unpack(args)
        qp, kp, vp = prep(q, k, v, gains, tabs)
        return _attn_fwd(qp, kp, vp, sink_lanes(sink), S, band)[0]

    def fwd(*args):
        q, k, v, gains, sink, tabs = unpack(args)
        qp, kp, vp = prep(q, k, v, gains, tabs)
        o, lse = _attn_fwd(qp, kp, vp, sink_lanes(sink), S, band)
        return o, (args, qp, kp, vp, o, lse)

    def bwd(res, do):
        args, qp, kp, vp, o, lse = res
        q, k, v, gains, sink, tabs = unpack(args)
        dqp, dkp, dvp, dsk = _attn_bwd(qp, kp, vp, sink_lanes(sink), do, o, lse, S, band)
        dxs, _, dgains = _row_bwd(f, "prep_" + name + "_bwd", [q, k, v], [], list(gains), list(tabs),
                                  [dqp, dkp, dvp], tm, S)
        out = list(dxs) + list(dgains)
        if has_sink:
            out.append(jnp.sum(dsk[:, :, 0, 0], axis=0))
        return (*out, *[jnp.zeros_like(t) for t in tabs])

    op.defvjp(fwd, bwd)
    return op


def _final_loss(S, tm):
    def run(X, target, g):
        B, T, D = X.shape
        n_lat_tiles = S // tm

        def lossf(x, gg, tgt):
            y = x * lax.rsqrt(jnp.mean(x * x, axis=-1, keepdims=True) + EPS) * gg
            err = y - tgt
            return 0.5 * jnp.sum(jnp.sum(err * err, axis=-1, keepdims=True), axis=0, keepdims=True) / D

        def body(x_ref, t_ref, g_ref, loss_ref, dx_ref, dg_ref):
            b, i = pl.program_id(0), pl.program_id(1)
            tgt = t_ref[...]
            val, vjp = jax.vjp(lambda x, gg: lossf(x, gg, tgt), x_ref[...], g_ref[...])
            dx, dg = vjp(jnp.ones((1, 1), F32))
            lat = (i < n_lat_tiles).astype(F32)
            dx_ref[...] = dx * lat

            @pl.when(i == 0)
            def _():
                loss_ref[...] = jnp.zeros_like(loss_ref)

            @pl.when(jnp.logical_and(b == 0, i == 0))
            def _():
                dg_ref[...] = jnp.zeros_like(dg_ref)

            loss_ref[...] += jnp.broadcast_to(val * lat, loss_ref.shape)
            dg_ref[...] += dg * lat

        x_spec = pl.BlockSpec((None, tm, D), lambda b, i: (b, i, 0))
        t_spec = pl.BlockSpec((None, tm, D), lambda b, i: (b, jnp.minimum(i, n_lat_tiles - 1), 0))
        g_spec = pl.BlockSpec((1, D), lambda b, i: (0, 0))
        loss, dx, dg = _call(
            body, name="final_loss", grid=(B, T // tm), in_specs=[x_spec, t_spec, g_spec],
            out_specs=[pl.BlockSpec((None, 1, LANE), lambda b, i: (b, 0, 0)), x_spec, g_spec],
            out_shape=[jax.ShapeDtypeStruct((B, 1, LANE), F32), jax.ShapeDtypeStruct(X.shape, F32),
                       jax.ShapeDtypeStruct(g.shape, F32)],
            compiler_params=_params(("arbitrary", "arbitrary")))(X, target, g)
        return jnp.sum(loss[:, 0, 0]), dx, dg

    @jax.custom_vjp
    def op(X, target, g):
        return run(X, target, g)[0]

    def fwd(X, target, g):
        loss, dx, dg = run(X, target, g)
        return loss, (dx, dg, target)

    def bwd(res, ct):
        dx, dg, target = res
        return ct * dx, jnp.zeros_like(target), ct * dg

    op.defvjp(fwd, bwd)
    return op


def _prenorm_f(D):
    def f(x, mp, g):
        y = x * lax.rsqrt(jnp.mean(x * x, axis=-1, keepdims=True) + EPS) * g
        return (y * (1.0 + mp[:, D:2 * D]) + mp[:, 0:D],)
    return f


def _resid_f(D):
    def f(x, y, mp):
        return (x + mp[:, 2 * D:3 * D] * y,)
    return f


def _gate2_f(hf, hr, g):
    return ((hf + hr) * (g * jax.nn.sigmoid(g)),)


def _gate1_f(y, g):
    return (y * (g * jax.nn.sigmoid(g)),)


def _merge_f(D):
    def f(m3, pa, pb, pc):
        return (jax.nn.sigmoid(m3[:, 0:D]) * pa + jax.nn.sigmoid(m3[:, D:2 * D]) * pb
                + jax.nn.sigmoid(m3[:, 2 * D:3 * D]) * pc,)
    return f


def _coef_f(D):
    nblk = D // GATE_BLOCK

    def f(v, pv, wm):
        vb = v.astype(BF16)

        def gate(k):
            cols = []
            for j in range(nblk):
                r0 = (k * nblk + j) * GATE_BLOCK
                cols.append(jnp.dot(vb[:, j * GATE_BLOCK:(j + 1) * GATE_BLOCK],
                                    wm[r0:r0 + GATE_BLOCK, :].astype(BF16), preferred_element_type=F32))
            return jnp.concatenate(cols, axis=1)

        outs = []
        for d in range(2):
            r = jax.nn.sigmoid(gate(d) + pv[d:d + 1])
            i = jax.nn.sigmoid(gate(2 + d) + pv[2 + d:3 + d])
            la = r * pv[4 + d:5 + d]
            t = jnp.tanh(la)
            outs += [jnp.exp(la), jnp.sqrt(-2.0 * t / (1.0 - t)) * (i * v)]
        return tuple(outs)

    return f


def _gate_blocks(w):
    per = GATE_BLOCK // LRU_BLOCK_W
    n = w.shape[1]
    w5 = w.reshape(2, n // per, per, LRU_BLOCK_W, LRU_BLOCK_W)
    dense = jnp.einsum("djiab,ik->djiakb", w5, jnp.eye(per, dtype=w.dtype))
    return dense.reshape(2, (n // per) * GATE_BLOCK, GATE_BLOCK)


def _exchange(x, name, gather):
    out_shape = (N_DEV,) + x.shape if gather else x.shape

    def body(x_ref, o_ref, send_sems, recv_sems, local_sem):
        mx, my, mc = lax.axis_index("x"), lax.axis_index("y"), lax.axis_index("c")
        me = 4 * mx + 2 * my + mc

        def src(p):
            return x_ref if gather else x_ref.at[p]

        local = pltpu.make_async_copy(src(me), o_ref.at[me], local_sem)
        local.start()
        sends, recvs = [], []
        for k in range(1, N_DEV):
            px = 1 - mx if k & 4 else mx
            py = 1 - my if k & 2 else my
            pc = 1 - mc if k & 1 else mc
            peer = 4 * px + 2 * py + pc
            sends.append(pltpu.make_async_remote_copy(
                src_ref=src(peer), dst_ref=o_ref.at[me], send_sem=send_sems.at[k - 1],
                recv_sem=recv_sems.at[k - 1], device_id=(px, py, pc), device_id_type=pl.DeviceIdType.MESH))
            recvs.append(pltpu.make_async_remote_copy(
                src_ref=src(peer), dst_ref=o_ref.at[peer], send_sem=send_sems.at[k - 1],
                recv_sem=recv_sems.at[k - 1], device_id=(px, py, pc), device_id_type=pl.DeviceIdType.MESH))
        for cp in sends:
            cp.start()
        for cp in recvs:
            cp.wait_recv()
        for cp in sends:
            cp.wait_send()
        local.wait()

    hbm = pl.BlockSpec(memory_space=pltpu.HBM)
    return _call(body, name=name, in_specs=[hbm], out_specs=hbm,
                 out_shape=jax.ShapeDtypeStruct(out_shape, x.dtype),
                 scratch_shapes=[pltpu.SemaphoreType.DMA((N_DEV - 1,)), pltpu.SemaphoreType.DMA((N_DEV - 1,)),
                                 pltpu.SemaphoreType.DMA(())])(x)


def _adamw(gs, w, m, v, name):
    n, R, C = gs.shape
    tr = _pick(R, 256, 16)

    def body(g_ref, w_ref, m_ref, v_ref, go_ref, d_ref, mo_ref, vo_ref):
        g = g_ref[0].astype(F32)
        for p in range(1, n):
            g = g + g_ref[p].astype(F32)
        m2 = ADAM_B1 * m_ref[...] + (1.0 - ADAM_B1) * g
        v2 = ADAM_B2 * v_ref[...] + (1.0 - ADAM_B2) * (g * g)
        m_hat = m2 / (1.0 - ADAM_B1 ** ADAM_STEP)
        v_hat = v2 / (1.0 - ADAM_B2 ** ADAM_STEP)
        go_ref[...] = g
        d_ref[...] = -ADAM_LR * (m_hat / (jnp.sqrt(v_hat) + ADAM_EPS) + ADAM_WD * w_ref[...])
        mo_ref[...] = m2
        vo_ref[...] = v2

    spec = pl.BlockSpec((tr, C), lambda i: (i, 0))
    return _call(body, name=name, grid=(R // tr,),
                 in_specs=[pl.BlockSpec((n, tr, C), lambda i: (0, i, 0)), spec, spec, spec],
                 out_specs=[spec] * 4, out_shape=[jax.ShapeDtypeStruct((R, C), F32)] * 4,
                 compiler_params=_params(("parallel",)))(gs, w, m, v)


def _sum_slots(gs, name):
    n, R, C = gs.shape

    def body(g_ref, o_ref):
        g = g_ref[0].astype(F32)
        for p in range(1, n):
            g = g + g_ref[p].astype(F32)
        o_ref[...] = g

    return _call(body, name=name, out_shape=jax.ShapeDtypeStruct((R, C), F32))(gs)


def _rope_tables(S, L):
    P = HEAD_DIM // 4
    rows = S // GRID_W
    row_id = jnp.repeat(jnp.arange(rows), GRID_W)
    col_id = jnp.tile(jnp.arange(GRID_W), rows)
    inv = ROPE_THETA ** (-jnp.arange(P, dtype=F32) / P)
    ar, ac = row_id[:, None] * inv, col_id[:, None] * inv
    cos = jnp.concatenate([jnp.cos(ar), jnp.cos(ar), jnp.cos(ac), jnp.cos(ac)], axis=1)
    sin = jnp.concatenate([-jnp.sin(ar), jnp.sin(ar), -jnp.sin(ac), jnp.sin(ac)], axis=1)
    cos = jnp.concatenate([cos, jnp.ones((L, HEAD_DIM), F32)], axis=0)
    sin = jnp.concatenate([sin, jnp.zeros((L, HEAD_DIM), F32)], axis=0)
    return cos, sin


IN_TILES = (
    ((512, 1024, 1024), (512, 1024, 1024), (1024, 1024, 512)),
    ((1088, 1024, 1024), (1088, 1024, 1024), (1024, 2048, 512)),
    ((2176, 512, 1024), (2176, 1024, 512), (1024, 1024, 1088)),
    ((1088, 2048, 1024), (1088, 1024, 2048), (1024, 1024, 2176)),
)


def _loss_fn(p, x, c, ctx, target):
    B, S, D = x.shape
    L = ctx.shape[1]
    T = S + L
    tm = min(L, 256)
    KVW = D // KV_GROUP
    widths = (D, D, D, KVW, KVW, D, D, KVW, KVW, D, 3 * D)
    offs = [0]
    for w in widths:
        offs.append(offs[-1] + w)
    cos, sin = _rope_tables(S, L)

    X = jnp.concatenate([x, ctx], axis=1)
    sc, scc = jax.nn.silu(c), jax.nn.silu(p["c_ctx"])
    A = jnp.concatenate([scc[None], sc, jnp.zeros((SUBLANE - 1 - B, D), F32)], axis=0)
    for l in range(DEPTH):
        mod = _matmul("mm_mod")(A, p["w_mod"][l]) + p["b_mod"][l]
        modp = jnp.stack([jnp.broadcast_to(mod[0], (B, 3 * D)), mod[1:1 + B]], axis=1)[:, :, None, :]
        (h,) = _rowwise(_prenorm_f(D), "prenorm", [(D, F32)], tm, S, 1, 1, 1)(X, modp, p["norm_g"][l][None])
        proj = _matmul("mm_in", *IN_TILES[l])(h.reshape(B * T, D), p["w_in"][l]).reshape(B, T, offs[-1])
        uA, gA, qB, kB, vB, gB, qC, kC, vC, gC, m3 = [proj[:, :, offs[i]:offs[i + 1]] for i in range(len(widths))]
        u = _conv_op(S)(uA, p["conv_w"][l], p["conv_b"][l][None])
        nsp = -LRU_C * jax.nn.softplus(-p["lru_lambda"][l])
        pv = jnp.concatenate([p["lru_ba"][l], p["lru_bx"][l], nsp, jnp.zeros((2, D), F32)], axis=0)
        wm = jnp.concatenate([_gate_blocks(p["lru_wa"][l]), _gate_blocks(p["lru_wx"][l])], axis=0)
        wm = wm.reshape(-1, GATE_BLOCK)
        af, bf, ar, br = _rowwise(_coef_f(D), "lru_coef", [(D, F32)] * 4, tm, S, 1, 0, 2)(u, pv, wm)
        hf = _scan_op(S, False)(af, bf)
        hr = _scan_op(S, True)(ar, br)
        (zA,) = _rowwise(_gate2_f, "gate_a", [(D, F32)], tm, S, 3, 0, 0)(hf, hr, gA)
        yB = _attn_branch(S, tm, True, False, True)(qB, kB, vB, p["attn_sink"][l], cos, sin)
        (zB,) = _rowwise(_gate1_f, "gate_b", [(D, F32)], tm, S, 2, 0, 0)(yB, gB)
        yC = _attn_branch(S, tm, False, True, False)(qC, kC, vC, p["q_norm_g"][l][None], p["k_norm_g"][l][None],
                                                     cos, sin)
        (zC,) = _rowwise(_gate1_f, "gate_c", [(D, F32)], tm, S, 2, 0, 0)(yC, gC)
        pr = [_matmul("mm_branch")(z.reshape(B * T, D), p["w_branch"][l][n]).reshape(B, T, D)
              for n, z in enumerate((zA, zB, zC))]
        (mg,) = _rowwise(_merge_f(D), "merge", [(D, F32)], tm, S, 4, 0, 0)(m3, *pr)
        y = _matmul("mm_out")(mg.reshape(B * T, D), p["w_out"][l]).reshape(B, T, D)
        (X,) = _rowwise(_resid_f(D), "resid", [(D, F32)], tm, S, 2, 1, 0)(X, y, modp)
    return _final_loss(S, tm)(X, target, p["final_g"][None])


def _shard_axis(name):
    return {"w_mod": 2, "w_in": 2, "conv_w": 2, "lru_ba": 2, "lru_bx": 2, "lru_lambda": 2,
            "w_branch": 2, "w_out": 1}.get(name)


def _unshard(g, axis):
    full = jnp.moveaxis(g, 0, axis)
    shape = list(full.shape)
    shape[axis:axis + 2] = [shape[axis] * shape[axis + 1]]
    return full.reshape(shape)


def _reshard(full, axis):
    shape = list(full.shape)
    shape[axis:axis + 1] = [N_DEV, shape[axis] // N_DEV]
    return jnp.moveaxis(full.reshape(shape), axis, 0)


def _pad_to(v, n):
    return jnp.concatenate([v, jnp.zeros((n - v.shape[0],), v.dtype)]) if n > v.shape[0] else v


def _step(x, c, ctx, target, w, m, v):
    full = {n: w[n] for n in REPLICATED}
    for n in BIG:
        full[n] = _unshard(_exchange(w[n].astype(BF16), "gather_" + n, True), _shard_axis(n))
    small_local = jnp.concatenate([w[n].reshape(-1) for n in SMALL_SHARDED])
    small_all = _exchange(small_local.reshape(-1, LANE), "gather_small", True).reshape(N_DEV, -1)
    off = 0
    for n in SMALL_SHARDED:
        sz = w[n].size
        full[n] = _unshard(small_all[:, off:off + sz].reshape((N_DEV,) + w[n].shape), _shard_axis(n))
        off += sz
    p = dict(full)
    for n in ("w_mod", "w_in", "w_out"):
        p[n] = [full[n][l] for l in range(DEPTH)]
    p["w_branch"] = [[full["w_branch"][l][k] for k in range(3)] for l in range(DEPTH)]

    loss, (gp, gx) = jax.value_and_grad(_loss_fn, argnums=(0, 1))(p, x, c, ctx, target)
    for n in ("w_mod", "w_in", "w_out"):
        gp[n] = jnp.stack(gp[n])
    gp["w_branch"] = jnp.stack([jnp.stack(g) for g in gp["w_branch"]])
    loss = lax.psum(loss, AXES)

    out = {}
    for n in BIG:
        recv = _exchange(_reshard(gp[n], _shard_axis(n)), "scatter_" + n, False)
        C = w[n].shape[-1]
        res = _adamw(recv.reshape(N_DEV, -1, C), w[n].reshape(-1, C), m[n].reshape(-1, C), v[n].reshape(-1, C),
                     "adamw_" + n)
        out[n] = [r.reshape(w[n].shape) for r in res]

    rep = jnp.concatenate([gp[n].reshape(-1) for n in REPLICATED])
    n_rep = rep.shape[0]
    chunk = -(-n_rep // (N_DEV * LANE)) * LANE
    rep = _pad_to(rep, N_DEV * chunk).reshape(N_DEV, chunk)
    shards = jnp.concatenate([_reshard(gp[n], _shard_axis(n)).reshape(N_DEV, -1) for n in SMALL_SHARDED], axis=1)
    n_sh = shards.shape[1]
    recv = _exchange(jnp.concatenate([rep, shards], axis=1).reshape(N_DEV, -1, LANE), "scatter_small", False)
    wl =jnp.concatenate([w[n].reshape(-1) for n in SMALL_SHARDED])
    ml = jnp.concatenate([m[n].reshape(-1) for n in SMALL_SHARDED])
    vl = jnp.concatenate([v[n].reshape(-1) for n in SMALL_SHARDED])
    rows = recv.shape[1]
    rrows = chunk // LANE
    g_sh, d_sh, m_sh, v_sh = _adamw(recv[:, rrows:], wl.reshape(-1, LANE), ml.reshape(-1, LANE),
                                    vl.reshape(-1, LANE), "adamw_small_sharded")
    g_rep8 = _sum_slots(recv[:, :rrows], "sum_replicated")
    g_rep = _exchange(g_rep8, "gather_replicated", True).reshape(-1)
    wr = _pad_to(jnp.concatenate([w[n].reshape(-1) for n in REPLICATED]), N_DEV * chunk)
    mr = _pad_to(jnp.concatenate([m[n].reshape(-1) for n in REPLICATED]), N_DEV * chunk)
    vr = _pad_to(jnp.concatenate([v[n].reshape(-1) for n in REPLICATED]), N_DEV * chunk)
    res_rep = _adamw(g_rep.reshape(1, -1, LANE), wr.reshape(-1, LANE), mr.reshape(-1, LANE), vr.reshape(-1, LANE),
                     "adamw_replicated")
    off = 0
    for n in REPLICATED:
        sz = w[n].size
        out[n] = [r.reshape(-1)[off:off + sz].reshape(w[n].shape) for r in res_rep]
        off += sz
    off = 0
    for n in SMALL_SHARDED:
        sz = w[n].size
        out[n] = [r.reshape(-1)[off:off + sz].reshape(w[n].shape) for r in (g_sh, d_sh, m_sh, v_sh)]
        off += sz
    assert off == n_sh and rows == rrows + n_sh // LANE
    return (loss, gx, *[out[n][0] for n in WEIGHTS], *[out[n][1] for n in WEIGHTS],
            *[out[n][2] for n in WEIGHTS], *[out[n][3] for n in WEIGHTS])


def kernel(x, c, ctx, c_ctx, norm_g, w_mod, b_mod, w_in, conv_w, conv_b, lru_wa, lru_ba, lru_wx, lru_bx, lru_lambda, attn_sink, q_norm_g, k_norm_g, w_branch, w_out, final_g, loss_target, m_c_ctx, m_norm_g, m_w_mod, m_b_mod, m_w_in, m_conv_w, m_conv_b, m_lru_wa, m_lru_ba, m_lru_wx, m_lru_bx, m_lru_lambda, m_attn_sink, m_q_norm_g, m_k_norm_g, m_w_branch, m_w_out, m_final_g, v_c_ctx, v_norm_g, v_w_mod, v_b_mod, v_w_in, v_conv_w, v_conv_b, v_lru_wa, v_lru_ba, v_lru_wx, v_lru_bx, v_lru_lambda, v_attn_sink, v_q_norm_g, v_k_norm_g, v_w_branch, v_w_out, v_final_g):
    w = dict(zip(WEIGHTS, (c_ctx, norm_g, w_mod, b_mod, w_in, conv_w, conv_b, lru_wa, lru_ba, lru_wx, lru_bx,
                           lru_lambda, attn_sink, q_norm_g, k_norm_g, w_branch, w_out, final_g)))
    m = dict(zip(WEIGHTS, (m_c_ctx, m_norm_g, m_w_mod, m_b_mod, m_w_in, m_conv_w, m_conv_b, m_lru_wa, m_lru_ba,
                           m_lru_wx, m_lru_bx, m_lru_lambda, m_attn_sink, m_q_norm_g, m_k_norm_g, m_w_branch,
                           m_w_out, m_final_g)))
    v = dict(zip(WEIGHTS, (v_c_ctx, v_norm_g, v_w_mod, v_b_mod, v_w_in, v_conv_w, v_conv_b, v_lru_wa, v_lru_ba,
                           v_lru_wx, v_lru_bx, v_lru_lambda, v_attn_sink, v_q_norm_g, v_k_norm_g, v_w_branch,
                           v_w_out, v_final_g)))
    return _step(x, c, ctx, loss_target, w, m, v)
```

```python
import functools

import jax
import jax.numpy as jnp
from jax import lax
from jax.experimental import pallas as pl
from jax.experimental.pallas import tpu as pltpu

F32 = jnp.float32
BF16 = jnp.bfloat16

AXES = ("x", "y", "c")
N_DEV = 8
DEPTH = 4
HEAD_DIM = 128
GRID_W = 64
WINDOW = 128
LRU_BLOCK_W = 64
GATE_BLOCK = 256
LRU_C = 8.0
ROPE_THETA = 10000.0
EPS = 1e-6
NEG_INF = -1e30
KV_GROUP = 4
LANE = 128
SUBLANE = 8
VMEM_LIMIT = 56 * 1024 * 1024

ADAM_LR = 0.001
ADAM_B1 = 0.9
ADAM_B2 = 0.999
ADAM_EPS = 1e-08
ADAM_WD = 0.01
ADAM_STEP = 10

WEIGHTS = ("c_ctx", "norm_g", "w_mod", "b_mod", "w_in", "conv_w", "conv_b", "lru_wa", "lru_ba", "lru_wx",
           "lru_bx", "lru_lambda", "attn_sink", "q_norm_g", "k_norm_g", "w_branch", "w_out", "final_g")
BIG_PACK = ("w_in", "w_mod", "w_branch", "w_out")
SMALL_SHARDED = ("conv_w", "lru_ba", "lru_bx", "lru_lambda")
REPLICATED = ("c_ctx", "norm_g", "b_mod", "conv_b", "lru_wa", "lru_wx", "attn_sink", "q_norm_g", "k_norm_g",
              "final_g")


def _call(body, **kw):
    return pl.pallas_call(body, **kw)


def _params(dims=None, vmem=VMEM_LIMIT):
    return pltpu.CompilerParams(dimension_semantics=dims, vmem_limit_bytes=vmem)


def _pick(n, target, mult):
    for t in range(min(n, target), 0, -1):
        if n % t == 0 and t % mult == 0:
            return t
    return n


def _mm(a, b, *, name, ta=False, tb=False, out_dtype=F32, tm=512, tn=1024, tk=1024):
    M, K = (a.shape[1], a.shape[0]) if ta else a.shape
    N = b.shape[0] if tb else b.shape[1]
    assert (b.shape[1] if tb else b.shape[0]) == K
    tm = _pick(M, tm, LANE if ta else 16)
    tn = _pick(N, tn, LANE)
    tk = _pick(K, tk, 16 if ta and not tb else LANE)
    nk = K // tk
    dn = (((0 if ta else 1,), (1 if tb else 0,)), ((), ()))

    def body(a_ref, b_ref, o_ref, *acc):
        r = lax.dot_general(a_ref[...].astype(BF16), b_ref[...].astype(BF16), dn,
                            preferred_element_type=F32)
        if nk == 1:
            o_ref[...] = r.astype(o_ref.dtype)
        else:
            k = pl.program_id(2)

            @pl.when(k == 0)
            def _():
                acc[0][...] = r

            @pl.when(k > 0)
            def _():
                acc[0][...] += r

            @pl.when(k == nk - 1)
            def _():
                o_ref[...] = acc[0][...].astype(o_ref.dtype)

    a_spec = (pl.BlockSpec((tk, tm), lambda i, j, k: (k, i)) if ta
              else pl.BlockSpec((tm, tk), lambda i, j, k: (i, k)))
    b_spec = (pl.BlockSpec((tn, tk), lambda i, j, k: (j, k)) if tb
              else pl.BlockSpec((tk, tn), lambda i, j, k: (k, j)))
    return _call(
        body, name=name, grid=(M // tm, N // tn, nk),
        in_specs=[a_spec, b_spec],
        out_specs=pl.BlockSpec((tm, tn), lambda i, j, k: (i, j)),
        out_shape=jax.ShapeDtypeStruct((M, N), out_dtype),
        scratch_shapes=[pltpu.VMEM((tm, tn), F32)] if nk > 1 else [],
        compiler_params=_params(("parallel", "parallel", "arbitrary")),
    )(a, b)


def _matmul(name, t_fwd=(512, 1024, 1024), t_da=(512, 1024, 1024), t_dw=(1024, 1024, 512)):
    def tiles(t):
        return dict(tm=t[0], tn=t[1], tk=t[2])

    @jax.custom_vjp
    def f(a, w):
        return _mm(a, w, name=name + "_fwd", **tiles(t_fwd))

    def fwd(a, w):
        return f(a, w), (a, w)

    def bwd(res, g):
        a, w = res
        da = _mm(g, w, name=name + "_da", tb=True, **tiles(t_da))
        dw = _mm(a, g, name=name + "_dw", ta=True, out_dtype=w.dtype, **tiles(t_dw))
        return da, dw

    f.defvjp(fwd, bwd)
    return f


def _row_specs(xs, tps, gps, rts, tm, n_lat_tiles):
    x_specs = [pl.BlockSpec((None, tm, x.shape[2]), lambda b, i: (b, i, 0)) for x in xs]
    tp_specs = [pl.BlockSpec((None, None, 1, p.shape[3]),
                             lambda b, i: (b, (i < n_lat_tiles).astype(jnp.int32), 0, 0)) for p in tps]
    gp_specs = [pl.BlockSpec(p.shape, lambda b, i: (0, 0)) for p in gps]
    rt_specs = [pl.BlockSpec((tm, t.shape[1]), lambda b, i: (i, 0)) for t in rts]
    return x_specs, tp_specs, gp_specs, rt_specs


def _row_fwd(f, name, xs, tps, gps, rts, outs, tm, n_lat):
    B, T, _ = xs[0].shape
    n_in = len(xs) + len(tps) + len(gps) + len(rts)

    def body(*refs):
        vals = f(*[r[...] for r in refs[:n_in]])
        for o, v in zip(refs[n_in:], vals):
            o[...] = v.astype(o.dtype)

    x_specs, tp_specs, gp_specs, rt_specs = _row_specs(xs, tps, gps, rts, tm, n_lat // tm)
    res = _call(
        body, name=name, grid=(B, T // tm),
        in_specs=x_specs + tp_specs + gp_specs + rt_specs,
        out_specs=[pl.BlockSpec((None, tm, w), lambda b, i: (b, i, 0)) for w, _ in outs],
        out_shape=[jax.ShapeDtypeStruct((B, T, w), dt) for w, dt in outs],
        compiler_params=_params(("parallel", "parallel")),
    )(*xs, *tps, *gps, *rts)
    return list(res)


def _row_bwd(f, name, xs, tps, gps, rts, douts, tm, n_lat):
    B, T, _ = xs[0].shape
    nx, ntp, ngp, nd = len(xs), len(tps), len(gps), len(douts)
    n_lat_tiles = n_lat // tm
    n_diff = nx + ntp + ngp
    n_in = n_diff + len(rts)

    def body(*refs):
        diff = [r[...] for r in refs[:n_diff]]
        tabs = [r[...] for r in refs[n_diff:n_in]]
        dos = [r[...] for r in refs[n_in:n_in + nd]]
        o_refs = refs[n_in + nd:]
        prim, vjp = jax.vjp(lambda *d: tuple(f(*d, *tabs)), *diff)
        grads = vjp(tuple(d.astype(p.dtype) for d, p in zip(dos, prim)))
        b, i = pl.program_id(0), pl.program_id(1)
        for k in range(nx):
            o_refs[k][...] = grads[k].astype(o_refs[k].dtype)
        first_tp = jnp.logical_or(i == 0, i == n_lat_tiles)
        first_gp = jnp.logical_and(b == 0, i == 0)
        for k in range(nx, n_diff):
            first = first_tp if k < nx + ntp else first_gp

            @pl.when(first)
            def _(k=k):
                o_refs[k][...] = grads[k]

            @pl.when(jnp.logical_not(first))
            def _(k=k):
                o_refs[k][...] += grads[k]

    x_specs, tp_specs, gp_specs, rt_specs = _row_specs(xs, tps, gps, rts, tm, n_lat_tiles)
    d_specs = [pl.BlockSpec((None, tm, d.shape[2]), lambda b, i: (b, i, 0)) for d in douts]
    res = _call(
        body, name=name, grid=(B, T // tm),
        in_specs=x_specs + tp_specs + gp_specs + rt_specs + d_specs,
        out_specs=x_specs + tp_specs + gp_specs,
        out_shape=[jax.ShapeDtypeStruct(a.shape, F32) for a in (*xs, *tps, *gps)],
        compiler_params=_params(("arbitrary", "arbitrary")),
    )(*xs, *tps, *gps, *rts, *douts)
    res = list(res)
    return res[:nx], res[nx:nx + ntp], res[nx + ntp:]


def _rowwise(f, name, outs, tm, n_lat, n_x, n_tp, n_gp):
    def split(args):
        return (args[:n_x], args[n_x:n_x + n_tp], args[n_x + n_tp:n_x + n_tp + n_gp],
                args[n_x + n_tp + n_gp:])

    @jax.custom_vjp
    def op(*args):
        xs, tps, gps, rts = split(args)
        return tuple(_row_fwd(f, name + "_fwd", xs, tps, gps, rts, outs, tm, n_lat))

    def fwd(*args):
        return op(*args), args

    def bwd(args, g):
        xs, tps, gps, rts = split(args)
        dxs, dtps, dgps = _row_bwd(f, name + "_bwd", xs, tps, gps, rts, list(g), tm, n_lat)
        return (*dxs, *dtps, *dgps, *[jnp.zeros_like(t) for t in rts])

    op.defvjp(fwd, bwd)
    return op


def _shift_impl(u, k):
    n = u.shape[0]
    r = pltpu.roll(u, k % n, axis=0)
    row = lax.broadcasted_iota(jnp.int32, u.shape, 0)
    valid = (row >= k) if k > 0 else (row < n + k)
    return jnp.where(valid, r, 0.0)


@functools.partial(jax.custom_vjp, nondiff_argnums=(1,))
def _shift(u, k):
    return _shift_impl(u, k)


_shift.defvjp(lambda u, k: (_shift_impl(u, k), None), lambda k, _, g: (_shift_impl(g, -k),))


def _swap_impl(x):
    lane = lax.broadcasted_iota(jnp.int32, x.shape, 1)
    q = HEAD_DIM // 4
    return jnp.where((lane % (2 * q)) < q, pltpu.roll(x, HEAD_DIM - q, axis=1), pltpu.roll(x, q, axis=1))


@jax.custom_vjp
def _swap(x):
    return _swap_impl(x)


_swap.defvjp(lambda x: (_swap_impl(x), None), lambda _, g: (_swap_impl(g),))


def _conv_f(ul, uc, cw, cb):
    def conv(u):
        return (_shift(u, 2) * cw[0:1] + _shift(u, 1) * cw[1:2] + u * cw[2:3] + _shift(u, -1) * cw[3:4] + cb)
    return conv(ul), conv(uc)


def _conv_specs(B, T, D):
    u_spec = pl.BlockSpec((None, T, LANE), lambda j, b: (b, 0, j))
    cw_spec = pl.BlockSpec((4, LANE), lambda j, b: (0, j))
    cb_spec = pl.BlockSpec((1, LANE), lambda j, b: (0, j))
    return u_spec, cw_spec, cb_spec


def _conv_fwd(u, cw, cb, S):
    B, T, D = u.shape

    def body(u_ref, cw_ref, cb_ref, o_ref):
        vl, vc = _conv_f(u_ref[0:S, :], u_ref[S:T, :], cw_ref[...], cb_ref[...])
        o_ref[0:S, :] = vl
        o_ref[S:T, :] = vc

    u_spec, cw_spec, cb_spec = _conv_specs(B, T, D)
    return _call(body, name="conv_fwd", grid=(D // LANE, B), in_specs=[u_spec, cw_spec, cb_spec],
                 out_specs=u_spec, out_shape=jax.ShapeDtypeStruct(u.shape, F32),
                 compiler_params=_params(("parallel", "parallel")))(u, cw, cb)


def _conv_bwd(u, cw, cb, dv, S):
    B, T, D = u.shape

    def body(u_ref, cw_ref, cb_ref, dv_ref, du_ref, dcw_ref, dcb_ref):
        _, vjp = jax.vjp(_conv_f, u_ref[0:S, :], u_ref[S:T, :], cw_ref[...], cb_ref[...])
        dul, duc, dcw, dcb = vjp((dv_ref[0:S, :], dv_ref[S:T, :]))
        du_ref[0:S, :] = dul
        du_ref[S:T, :] = duc
        first = pl.program_id(1) == 0

        @pl.when(first)
        def _():
            dcw_ref[...] = dcw
            dcb_ref[...] = dcb

        @pl.when(jnp.logical_not(first))
        def _():
            dcw_ref[...] += dcw
            dcb_ref[...] += dcb

    u_spec, cw_spec, cb_spec = _conv_specs(B, T, D)
    return _call(body, name="conv_bwd", grid=(D // LANE, B), in_specs=[u_spec, cw_spec, cb_spec, u_spec],
                 out_specs=[u_spec, cw_spec, cb_spec],
                 out_shape=[jax.ShapeDtypeStruct(u.shape, F32), jax.ShapeDtypeStruct(cw.shape, F32),
                            jax.ShapeDtypeStruct(cb.shape, F32)],
                 compiler_params=_params(("parallel", "arbitrary")))(u, cw, cb, dv)


def _conv_op(S):
    @jax.custom_vjp
    def op(u, cw, cb):
        return _conv_fwd(u, cw, cb, S)

    def fwd(u, cw, cb):
        return op(u, cw, cb), (u, cw, cb)

    def bwd(res, g):
        u, cw, cb = res
        return tuple(_conv_bwd(u, cw, cb, g, S))

    op.defvjp(fwd, bwd)
    return op


SCAN_UNROLL = 4


def _group_scan(A, Bv, asc):
    row = lax.broadcasted_iota(jnp.int32, A.shape, 0)
    for s in (1, 2, 4):
        sh = s if asc else SUBLANE - s
        valid = (row >= s) if asc else (row < SUBLANE - s)
        A_sh = pltpu.roll(A, sh, axis=0)
        B_sh = pltpu.roll(Bv, sh, axis=0)
        Bv = jnp.where(valid, A * B_sh, 0.0) + Bv
        A = jnp.where(valid, A * A_sh, A)
    return A, Bv


def _chain_step(A, Bv, carry, asc):
    row = lax.broadcasted_iota(jnp.int32, A.shape, 0)
    A2, B2 = _group_scan(A, Bv, asc)
    h = A2 * carry + B2
    if asc:
        prev = jnp.where(row == 0, carry, pltpu.roll(h, 1, axis=0))
        return h, prev, h[SUBLANE - 1:SUBLANE, :]
    prev = jnp.where(row == SUBLANE - 1, carry, pltpu.roll(h, SUBLANE - 1, axis=0))
    return h, prev, h[0:1, :]


def _chain_loop(segments, step):
    carry = jnp.zeros((1, LANE), F32)
    for lo, hi, asc in segments:
        span = SUBLANE * SCAN_UNROLL
        assert (hi - lo) % span == 0

        def it(t, carry, lo=lo, hi=hi, asc=asc, span=span):
            base = lo + t * span if asc else hi - (t + 1) * span
            order = range(SCAN_UNROLL) if asc else reversed(range(SCAN_UNROLL))
            for j in order:
                carry = step(pl.multiple_of(base + SUBLANE * j, SUBLANE), carry, asc)
            return carry

        carry = lax.fori_loop(0, (hi - lo) // span, it, carry)
    return carry


def _scan_specs(T):
    return pl.BlockSpec((None, T, LANE), lambda j, b: (b, 0, j))


def _scan_fwd(a, b, S, reverse):
    B, T, D = a.shape
    asc = not reverse
    segments = [(S, T, asc), (0, S, asc)]

    def body(a_ref, b_ref, h_ref, hp_ref):
        def step(r0, carry, asc):
            rows = pl.ds(r0, SUBLANE)
            h, prev, carry = _chain_step(a_ref[rows, :], b_ref[rows, :], carry, asc)
            h_ref[rows, :] = h
            hp_ref[rows, :] = prev
            return carry
        _chain_loop(segments, step)

    spec = _scan_specs(T)
    return _call(body, name="scan_rev_fwd" if reverse else "scan_fwd_fwd", grid=(D // LANE, B),
                 in_specs=[spec, spec], out_specs=[spec, spec],
                 out_shape=[jax.ShapeDtypeStruct(a.shape, F32)] * 2,
                 compiler_params=_params(("parallel", "parallel")))(a, b)


def _scan_bwd(a, hp, dy, S, reverse):
    B, T, D = a.shape
    asc = reverse
    segments = [(0, S, asc), (S, T, asc)]

    def body(a_ref, hp_ref, dy_ref, da_ref, db_ref):
        def step(r0, carry, asc):
            rows = pl.ds(r0, SUBLANE)
            A, dy = a_ref[rows, :], dy_ref[rows, :]
            _, s_prev, carry = _chain_step(A, A * dy, carry, asc)
            g = dy + s_prev
            db_ref[rows, :] = g
            da_ref[rows, :] = g * hp_ref[rows, :]
            return carry
        _chain_loop(segments, step)

    spec = _scan_specs(T)
    return _call(body, name="scan_rev_bwd" if reverse else "scan_fwd_bwd", grid=(D // LANE, B),
                 in_specs=[spec, spec, spec], out_specs=[spec, spec],
                 out_shape=[jax.ShapeDtypeStruct(a.shape, F32)] * 2,
                 compiler_params=_params(("parallel", "parallel")))(a, hp, dy)


def _scan_op(S, reverse):
    @jax.custom_vjp
    def op(a, b):
        return _scan_fwd(a, b, S, reverse)[0]

    def fwd(a, b):
        h, hp = _scan_fwd(a, b, S, reverse)
        return h, (a, hp)

    def bwd(res, g):
        a, hp = res
        return tuple(_scan_bwd(a, hp, g, S, reverse))

    op.defvjp(fwd, bwd)
    return op


def _band_lo(qi, tq, S):
    span = tq + 2 * WINDOW
    return pl.multiple_of(jnp.clip(qi * tq - WINDOW, 0, S - span), LANE)


def _band_mask(qi, tq, lo, span, transposed):
    shape = (span, tq) if transposed else (tq, span)
    qpos = qi * tq + lax.broadcasted_iota(jnp.int32, shape, 1 if transposed else 0)
    kpos = lo + lax.broadcasted_iota(jnp.int32, shape, 0 if transposed else 1)
    return jnp.abs(kpos - qpos) <= WINDOW


NT = (((1,), (1,)), ((), ()))
LOG2E = 1.4426950408889634
KEY_CHUNK = 1024


def _col_to_row(c):
    return jnp.broadcast_to(c, (c.shape[0], LANE)).T[0:1, :]


def _attn_fwd(q, k, v, sink, S, band, side=None):
    B, T, HD = q.shape
    H = HD // HEAD_DIM
    L = T - S
    tq = L
    n_lq = S // tq
    span = tq + 2 * WINDOW
    c2 = HEAD_DIM ** -0.5 * LOG2E
    has_sink = sink is not None
    kc = min(KEY_CHUNK, S)

    def body(*refs):
        if has_sink:
            q_ref, k_ref, v_ref, s_ref, o_ref, lse_ref = refs
            sk2 = s_ref[:, 0:1] * LOG2E
        else:
            q_ref, k_ref, v_ref, o_ref, lse_ref = refs
            sk2 = None
        qi = pl.program_id(2)
        qv = q_ref[...]

        def run(segs):
            def scores(seg):
                s = lax.dot_general(qv, k_ref[seg[0], :], NT, preferred_element_type=F32) * c2
                return s if seg[1] is None else jnp.where(seg[1], s, NEG_INF)

            m = l = acc = None
            s_next = scores(segs[0])
            for j, (rows, _) in enumerate(segs):
                s = s_next
                if j + 1 < len(segs):
                    s_next = scores(segs[j + 1])
                ms = jnp.max(s, axis=-1, keepdims=True)
                if m is None:
                    m = ms if sk2 is None else jnp.maximum(ms, sk2)
                    p = jnp.exp2(s - m)
                    l = jnp.sum(p, axis=-1, keepdims=True)
                    if sk2 is not None:
                        l = l + jnp.exp2(sk2 - m)
                    acc = jnp.dot(p.astype(BF16), v_ref[rows, :], preferred_element_type=F32)
                else:
                    m_new = jnp.maximum(m, ms)
                    alpha = jnp.exp2(m - m_new)
                    p = jnp.exp2(s - m_new)
                    l = alpha * l + jnp.sum(p, axis=-1, keepdims=True)
                    acc = alpha * acc + jnp.dot(p.astype(BF16), v_ref[rows, :], preferred_element_type=F32)
                    m = m_new
            o_ref[...] = acc * (1.0 / l)
            lse_ref[...] = _col_to_row(m + jnp.log2(l))

        ctx_rows = pl.ds(S, L)

        @pl.when(qi < n_lq)
        def _():
            if band:
                lo = _band_lo(qi, tq, S)
                run([(ctx_rows, None), (pl.ds(lo, span), _band_mask(qi, tq, lo, span, False))])
            else:
                run([(pl.ds(j * kc, kc), None) for j in range(S // kc)] + [(ctx_rows, None)])

        @pl.when(qi >= n_lq)
        def _():
            run([(ctx_rows, None)])

    q_spec = pl.BlockSpec((None, tq, HEAD_DIM), lambda b, h, i: (b, i, h))
    kv_spec = pl.BlockSpec((None, T, HEAD_DIM), lambda b, h, i: (b, 0, h // KV_GROUP))
    in_specs = [q_spec, kv_spec, kv_spec]
    args = [q, k, v]
    if has_sink:
        in_specs.append(pl.BlockSpec((None, 1, LANE), lambda b, h, i: (h, 0, 0)))
        args.append(sink)
    name = "attn_band_fwd" if band else "attn_dense_fwd"
    return _call_with_side(
        body, side, name=name, grid=(B, H, T // tq), in_specs=in_specs, args=args,
        out_specs=[q_spec, pl.BlockSpec((None, None, 1, tq), lambda b, h, i: (b, h, 0, i))],
        out_shape=[jax.ShapeDtypeStruct(q.shape, F32), jax.ShapeDtypeStruct((B, H, 1, T), F32)],
        scratch_shapes=[], dims=("parallel", "parallel", "parallel"))


def _attn_bwd(q, k, v, sink, do, o, lse, S, band, side=None):
    B, T, HD = q.shape
    H = HD // HEAD_DIM
    KVH = H // KV_GROUP
    L = T - S
    tq = L
    n_lq = S // tq
    span = tq + 2 * WINDOW
    scale = HEAD_DIM ** -0.5
    c2 = scale * LOG2E
    has_sink = sink is not None
    kc = min(KEY_CHUNK, S)

    def body(*refs):
        if has_sink:
            q_ref, k_ref, v_ref, do_ref, o_ref, lse_ref, s_ref, dq_ref, dk_ref, dv_ref, ds_ref, *scr = refs
        else:
            q_ref, k_ref, v_ref, do_ref, o_ref, lse_ref, dq_ref, dk_ref, dv_ref, *scr = refs
        g, qi = pl.program_id(2), pl.program_id(3)

        @pl.when(jnp.logical_and(g == 0, qi == 0))
        def _():
            dk_ref[...] = jnp.zeros_like(dk_ref)
            dv_ref[...] = jnp.zeros_like(dv_ref)
            if not band:
                scr[0][...] = k_ref[...].astype(F32).T.astype(BF16)

        qv = q_ref[...]
        dov = do_ref[...].astype(BF16)

        def run(segs):
            lse2 = lse_ref[...]
            delta = _col_to_row(jnp.sum(do_ref[...] * o_ref[...], axis=-1, keepdims=True))
            def head(seg):
                rows = pl.ds(seg[0], seg[1])
                s = lax.dot_general(k_ref[rows, :], qv, NT, preferred_element_type=F32) * c2
                if seg[2] is not None:
                    s = jnp.where(seg[2], s, NEG_INF)
                return s, lax.dot_general(v_ref[rows, :], dov, NT, preferred_element_type=F32)

            dq, dqT = None, None
            nxt = head(segs[0])
            for j, (lo, n, _) in enumerate(segs):
                rows = pl.ds(lo, n)
                s, dp = nxt
                if j + 1 < len(segs):
                    nxt = head(segs[j + 1])
                p = jnp.exp2(s - lse2)
                ds = p * (dp - delta)
                dsb = ds.astype(BF16)
                dv_ref[rows, :] += jnp.dot(p.astype(BF16), dov, preferred_element_type=F32)
                dk_ref[rows, :] += jnp.dot(dsb, qv, preferred_element_type=F32) * scale
                if band:
                    part = jnp.dot(ds.T.astype(BF16), k_ref[rows, :], preferred_element_type=F32)
                    dq = part if dq is None else dq + part
                else:
                    part = jnp.dot(scr[0][:, lo:lo + n], dsb, preferred_element_type=F32)
                    dqT = part if dqT is None else dqT + part
            dq_ref[...] = (dq if band else dqT.T) * scale
            if has_sink:
                psk = jnp.exp2(s_ref[:, 0:1] * LOG2E - lse2)
                dsk = -jnp.sum(psk * delta, axis=1, keepdims=True)

                @pl.when(qi == 0)
                def _():
                    ds_ref[...] = jnp.broadcast_to(dsk, ds_ref.shape)

                @pl.when(qi > 0)
                def _():
                    ds_ref[...] += jnp.broadcast_to(dsk, ds_ref.shape)

        @pl.when(qi < n_lq)
        def _():
            if band:
                lo = _band_lo(qi, tq, S)
                run([(S, L, None), (lo, span, _band_mask(qi, tq, lo, span, True))])
            else:
                run([(j * kc, kc, None) for j in range(S // kc)] + [(S, L, None)])

        @pl.when(qi >= n_lq)
        def _():
            run([(S, L, None)])

    q_spec = pl.BlockSpec((None, tq, HEAD_DIM), lambda b, kv, g, i: (b, i, kv * KV_GROUP + g))
    kv_spec = pl.BlockSpec((None, T, HEAD_DIM), lambda b, kv, g, i: (b, 0, kv))
    lse_spec = pl.BlockSpec((None, None, 1, tq), lambda b, kv, g, i: (b, kv * KV_GROUP + g, 0, i))
    in_specs = [q_spec, kv_spec, kv_spec, q_spec, q_spec, lse_spec]
    out_specs = [q_spec, kv_spec, kv_spec]
    out_shape = [jax.ShapeDtypeStruct(q.shape, F32), jax.ShapeDtypeStruct(k.shape, F32),
                 jax.ShapeDtypeStruct(v.shape, F32)]
    args = [q, k, v, do, o, lse]
    if has_sink:
        in_specs.append(pl.BlockSpec((None, 1, LANE), lambda b, kv, g, i: (kv * KV_GROUP + g, 0, 0)))
        args.append(sink)
        out_specs.append(pl.BlockSpec((None, None, 1, LANE), lambda b, kv, g, i: (b, kv * KV_GROUP + g, 0, 0)))
        out_shape.append(jax.ShapeDtypeStruct((B, H, 1, LANE), F32))
    res = _call_with_side(
        body, side, name="attn_band_bwd" if band else "attn_dense_bwd", grid=(B, KVH, KV_GROUP, T // tq),
        in_specs=in_specs, args=args, out_specs=out_specs, out_shape=out_shape,
        scratch_shapes=[] if band else [pltpu.VMEM((HEAD_DIM, T), BF16)],
        dims=("parallel", "parallel", "arbitrary", "arbitrary"))
    return (res[0], res[1], res[2], res[3] if has_sink else None, res[-1] if side is not None else None)


def _prep_f(norm):
    def f(q, k, v, *rest):
        if norm:
            qg, kg, cos, sin = rest
        else:
            (cos, sin), qg, kg = rest, None, None

        def heads(x, g):
            outs = []
            for h in range(x.shape[1] // HEAD_DIM):
                xh = x[:, h * HEAD_DIM:(h + 1) * HEAD_DIM]
                if g is not None:
                    xh = xh * lax.rsqrt(jnp.mean(xh * xh, axis=-1, keepdims=True) + EPS) * g
                outs.append(xh * cos + _swap(xh) * sin)
            return jnp.concatenate(outs, axis=1) if len(outs) > 1 else outs[0]

        return heads(q, qg), heads(k, kg), v

    return f


def _attn_branch(S, tm, band, norm, has_sink, carries=False):
    f = _prep_f(norm)
    name = "band" if band else "dense"

    def prep(q, k, v, gains, tabs):
        outs = [(q.shape[2], BF16), (k.shape[2], BF16), (v.shape[2], BF16)]
        return _row_fwd(f, "prep_" + name + "_fwd", [q, k, v], [], list(gains), list(tabs), outs, tm, S)

    def unpack(args):
        q, k, v = args[:3]
        rest = list(args[3:])
        gains = [rest.pop(0), rest.pop(0)] if norm else []
        sink = rest.pop(0) if has_sink else None
        pack = rest.pop(0) if carries else None
        return q, k, v, gains, sink, pack, rest

    def sink_lanes(sink):
        return None if sink is None else jnp.broadcast_to(sink[:, None, None], (sink.shape[0], 1, LANE))

    def run_fwd(args):
        q, k, v, gains, sink, pack, tabs = unpack(args)
        qp, kp, vp = prep(q, k, v, gains, tabs)
        side = (pack.astype(BF16), True) if carries else None
        res = _attn_fwd(qp, kp, vp, sink_lanes(sink), S, band, side)
        return ((res[0], res[2]) if carries else res[0]), (args, qp, kp, vp, res[0], res[1])

    @jax.custom_vjp
    def op(*args):
        return run_fwd(args)[0]

    def fwd(*args):
        return run_fwd(args)

    def bwd(res, ct):
        args, qp, kp, vp, o, lse = res
        q, k, v, gains, sink, pack, tabs = unpack(args)
        do, side = (ct[0], (ct[1], False)) if carries else (ct, None)
        dqp, dkp, dvp, dsk, recv = _attn_bwd(qp, kp, vp, sink_lanes(sink), do, o, lse, S, band, side)
        dxs, _, dgains = _row_bwd(f, "prep_" + name + "_bwd", [q, k, v], [], list(gains), list(tabs),
                                  [dqp, dkp, dvp], tm, S)
        out = list(dxs) + list(dgains)
        if has_sink:
            out.append(jnp.sum(dsk[:, :, 0, 0], axis=0))
        if carries:
            out.append(_sum_slots(recv, "sum_grads"))
        return (*out, *[jnp.zeros_like(t) for t in tabs])

    op.defvjp(fwd, bwd)
    return op


def _final_loss(S, tm):
    def run(X, target, g):
        B, T, D = X.shape
        n_lat_tiles = S // tm

        def lossf(x, gg, tgt):
            y = x * lax.rsqrt(jnp.mean(x * x, axis=-1, keepdims=True) + EPS) * gg
            err = y - tgt
            return 0.5 * jnp.sum(jnp.sum(err * err, axis=-1, keepdims=True), axis=0, keepdims=True) / D

        def body(x_ref, t_ref, g_ref, loss_ref, dx_ref, dg_ref):
            b, i = pl.program_id(0), pl.program_id(1)
            tgt = t_ref[...]
            val, vjp = jax.vjp(lambda x, gg: lossf(x, gg, tgt), x_ref[...], g_ref[...])
            dx, dg = vjp(jnp.ones((1, 1), F32))
            lat = (i < n_lat_tiles).astype(F32)
            dx_ref[...] = dx * lat

            @pl.when(i == 0)
            def _():
                loss_ref[...] = jnp.zeros_like(loss_ref)

            @pl.when(jnp.logical_and(b == 0, i == 0))
            def _():
                dg_ref[...] = jnp.zeros_like(dg_ref)

            loss_ref[...] += jnp.broadcast_to(val * lat, loss_ref.shape)
            dg_ref[...] += dg * lat

        x_spec = pl.BlockSpec((None, tm, D), lambda b, i: (b, i, 0))
        t_spec = pl.BlockSpec((None, tm, D), lambda b, i: (b, jnp.minimum(i, n_lat_tiles - 1), 0))
        g_spec = pl.BlockSpec((1, D), lambda b, i: (0, 0))
        loss, dx, dg = _call(
            body, name="final_loss", grid=(B, T // tm), in_specs=[x_spec, t_spec, g_spec],
            out_specs=[pl.BlockSpec((None, 1, LANE), lambda b, i: (b, 0, 0)), x_spec, g_spec],
            out_shape=[jax.ShapeDtypeStruct((B, 1, LANE), F32), jax.ShapeDtypeStruct(X.shape, F32),
                       jax.ShapeDtypeStruct(g.shape, F32)],
            compiler_params=_params(("arbitrary", "arbitrary")))(X, target, g)
        return jnp.sum(loss[:, 0, 0]), dx, dg

    @jax.custom_vjp
    def op(X, target, g):
        return run(X, target, g)[0]

    def fwd(X, target, g):
        loss, dx, dg = run(X, target, g)
        return loss, (dx, dg, target)

    def bwd(res, ct):
        dx, dg, target = res
        return ct * dx, jnp.zeros_like(target), ct * dg

    op.defvjp(fwd, bwd)
    return op


def _prenorm_f(D):
    def f(x, mp, g):
        y = x * lax.rsqrt(jnp.mean(x * x, axis=-1, keepdims=True) + EPS) * g
        return (y * (1.0 + mp[:, D:2 * D]) + mp[:, 0:D],)
    return f


def _resid_f(D):
    def f(x, y, mp):
        return (x + mp[:, 2 * D:3 * D] * y,)
    return f


def _gate2_f(hf, hr, g):
    return ((hf + hr) * (g * jax.nn.sigmoid(g)),)


def _gate1_f(y, g):
    return (y * (g * jax.nn.sigmoid(g)),)


def _merge_f(D):
    def f(m3, pa, pb, pc):
        return (jax.nn.sigmoid(m3[:, 0:D]) * pa + jax.nn.sigmoid(m3[:, D:2 * D]) * pb
                + jax.nn.sigmoid(m3[:, 2 * D:3 * D]) * pc,)
    return f


def _coef_f(D):
    nblk = D // GATE_BLOCK

    def f(v, pv, wm):
        vb = v.astype(BF16)

        def gate(k):
            cols = []
            for j in range(nblk):
                r0 = (k * nblk + j) * GATE_BLOCK
                cols.append(jnp.dot(vb[:, j * GATE_BLOCK:(j + 1) * GATE_BLOCK],
                                    wm[r0:r0 + GATE_BLOCK, :].astype(BF16), preferred_element_type=F32))
            return jnp.concatenate(cols, axis=1)

        outs = []
        for d in range(2):
            r = jax.nn.sigmoid(gate(d) + pv[d:d + 1])
            i = jax.nn.sigmoid(gate(2 + d) + pv[2 + d:3 + d])
            la = r * pv[4 + d:5 + d]
            t = jnp.tanh(la)
            outs += [jnp.exp(la), jnp.sqrt(-2.0 * t / (1.0 - t)) * (i * v)]
        return tuple(outs)

    return f


def _gate_blocks(w):
    per = GATE_BLOCK // LRU_BLOCK_W
    n = w.shape[1]
    w5 = w.reshape(2, n // per, per, LRU_BLOCK_W, LRU_BLOCK_W)
    dense = jnp.einsum("djiab,ik->djiakb", w5, jnp.eye(per, dtype=w.dtype))
    return dense.reshape(2, (n // per) * GATE_BLOCK, GATE_BLOCK)


def _exchange_shape(x, gather):
    return jax.ShapeDtypeStruct((N_DEV,) + x.shape if gather else x.shape, x.dtype)


EXCHANGE_SEMS = [pltpu.SemaphoreType.DMA((N_DEV - 1,)), pltpu.SemaphoreType.DMA((N_DEV - 1,)),
                 pltpu.SemaphoreType.DMA(())]


def _exchange_ops(x_ref, o_ref, send_sems, recv_sems, local_sem, gather):
    mx, my, mc = lax.axis_index("x"), lax.axis_index("y"), lax.axis_index("c")
    me = 4 * mx + 2 * my + mc

    def src(p):
        return x_ref if gather else x_ref.at[p]

    local = pltpu.make_async_copy(src(me), o_ref.at[me], local_sem)
    sends, recvs = [], []
    for k in range(1, N_DEV):
        px = 1 - mx if k & 4 else mx
        py = 1 - my if k & 2 else my
        pc = 1 - mc if k & 1 else mc
        peer = 4 * px + 2 * py + pc
        sends.append(pltpu.make_async_remote_copy(
            src_ref=src(peer), dst_ref=o_ref.at[me], send_sem=send_sems.at[k - 1],
            recv_sem=recv_sems.at[k - 1], device_id=(px, py, pc), device_id_type=pl.DeviceIdType.MESH))
        recvs.append(pltpu.make_async_remote_copy(
            src_ref=src(peer), dst_ref=o_ref.at[peer], send_sem=send_sems.at[k - 1],
            recv_sem=recv_sems.at[k - 1], device_id=(px, py, pc), device_id_type=pl.DeviceIdType.MESH))

    def start():
        local.start()
        for cp in sends:
            cp.start()

    def finish():
        for cp in recvs:
            cp.wait_recv()
        for cp in sends:
            cp.wait_send()
        local.wait()

    return start, finish


def _exchange(x, name, gather):
    def body(x_ref, o_ref, send_sems, recv_sems, local_sem):
        start, finish = _exchange_ops(x_ref, o_ref, send_sems, recv_sems, local_sem, gather)
        start()
        finish()

    hbm = pl.BlockSpec(memory_space=pltpu.HBM)
    return _call(body, name=name, in_specs=[hbm], out_specs=hbm, out_shape=_exchange_shape(x, gather),
                 scratch_shapes=EXCHANGE_SEMS)(x)


def _call_with_side(body, side, *, name, grid, in_specs, args, out_specs, out_shape, scratch_shapes, dims):
    if side is None:
        return _call(body, name=name, grid=grid, in_specs=in_specs, out_specs=out_specs, out_shape=out_shape,
                     scratch_shapes=scratch_shapes, compiler_params=_params(dims))(*args)
    x, gather = side
    n_in, n_out, n_scr = len(in_specs), len(out_specs), len(scratch_shapes)

    def wrapped(*refs):
        ins, x_ref = refs[:n_in], refs[n_in]
        outs, o_ref = refs[n_in + 1:n_in + 1 + n_out], refs[n_in + 1 + n_out]
        scr = refs[n_in + 2 + n_out:n_in + 2 + n_out + n_scr]
        start, finish = _exchange_ops(x_ref, o_ref, *refs[n_in + 2 + n_out + n_scr:], gather)
        ids = [pl.program_id(a) for a in range(len(grid))]
        first = functools.reduce(jnp.logical_and, [i == 0 for i in ids])
        last = functools.reduce(jnp.logical_and, [i == g - 1 for i, g in zip(ids, grid)])
        pl.when(first)(start)
        body(*ins, *outs, *scr)
        pl.when(last)(finish)

    hbm = pl.BlockSpec(memory_space=pltpu.HBM)
    return _call(wrapped, name=name + "_xchg", grid=grid, in_specs=list(in_specs) + [hbm],
                 out_specs=list(out_specs) + [hbm], out_shape=list(out_shape) + [_exchange_shape(x, gather)],
                 scratch_shapes=list(scratch_shapes) + EXCHANGE_SEMS,
                 compiler_params=_params(("arbitrary",) * len(grid)))(*args, x)


def _adamw(gs, w, m, v, name):
    n, R, C = gs.shape
    tr = _pick(R, 256, 16)

    def body(g_ref, w_ref, m_ref, v_ref, go_ref, d_ref, mo_ref, vo_ref):
        g = g_ref[0].astype(F32)
        for p in range(1, n):
            g = g + g_ref[p].astype(F32)
        m2 = ADAM_B1 * m_ref[...] + (1.0 - ADAM_B1) * g
        v2 = ADAM_B2 * v_ref[...] + (1.0 - ADAM_B2) * (g * g)
        m_hat = m2 / (1.0 - ADAM_B1 ** ADAM_STEP)
        v_hat = v2 / (1.0 - ADAM_B2 ** ADAM_STEP)
        go_ref[...] = g
        d_ref[...] = -ADAM_LR * (m_hat / (jnp.sqrt(v_hat) + ADAM_EPS) + ADAM_WD * w_ref[...])
        mo_ref[...] = m2
        vo_ref[...] = v2

    spec = pl.BlockSpec((tr, C), lambda i: (i, 0))
    return _call(body, name=name, grid=(R // tr,),
                 in_specs=[pl.BlockSpec((n, tr, C), lambda i: (0, i, 0)), spec, spec, spec],
                 out_specs=[spec] * 4, out_shape=[jax.ShapeDtypeStruct((R, C), F32)] * 4,
                 compiler_params=_params(("parallel",)))(gs, w, m, v)


def _sum_slots(gs, name):
    n, R, C = gs.shape
    tr = _pick(R, 256, 16)

    def body(g_ref, o_ref):
        g = g_ref[0].astype(F32)
        for p in range(1, n):
            g = g + g_ref[p].astype(F32)
        o_ref[...] = g

    return _call(body, name=name, grid=(R // tr,), in_specs=[pl.BlockSpec((n, tr, C), lambda i: (0, i, 0))],
                 out_specs=pl.BlockSpec((tr, C), lambda i: (i, 0)), out_shape=jax.ShapeDtypeStruct((R, C), F32),
                 compiler_params=_params(("parallel",)))(gs)


def _gather_op(name):
    @jax.custom_vjp
    def op(pack):
        return _exchange(pack.astype(BF16), name, True)

    def fwd(pack):
        return op(pack), None

    def bwd(_, ct):
        return (_sum_slots(_exchange(ct, name + "_transpose", False), "sum_grads"),)

    op.defvjp(fwd, bwd)
    return op


def _rope_tables(S, L):
    P = HEAD_DIM // 4
    rows = S // GRID_W
    row_id = jnp.repeat(jnp.arange(rows), GRID_W)
    col_id = jnp.tile(jnp.arange(GRID_W), rows)
    inv = ROPE_THETA ** (-jnp.arange(P, dtype=F32) / P)
    ar, ac = row_id[:, None] * inv, col_id[:, None] * inv
    cos = jnp.concatenate([jnp.cos(ar), jnp.cos(ar), jnp.cos(ac), jnp.cos(ac)], axis=1)
    sin = jnp.concatenate([-jnp.sin(ar), jnp.sin(ar), -jnp.sin(ac), jnp.sin(ac)], axis=1)
    cos = jnp.concatenate([cos, jnp.ones((L, HEAD_DIM), F32)], axis=0)
    sin = jnp.concatenate([sin, jnp.zeros((L, HEAD_DIM), F32)], axis=0)
    return cos, sin


IN_TILES = (
    ((1088, 2048, 1024), (1088, 1024, 2048), (1024, 1024, 2176)),
    ((1088, 2560, 1024), (2176, 512, 1024), (1024, 1024, 2176)),
    ((2176, 1024, 1024), (1088, 1024, 2560), (1024, 2048, 1088)),
    ((1088, 2048, 1024), (1088, 1024, 2048), (1024, 1024, 2176)),
)
SQ_TILES = (
    ((1088, 1024, 1024), (1088, 1024, 1024), (1024, 1024, 2176)),
    ((2176, 1024, 1024), (2176, 1024, 1024), (1024, 1024, 2176)),
    ((2176, 512, 1024), (2176, 512, 1024), (1024, 1024, 1088)),
    ((1088, 512, 1024), (1088, 512, 1024), (512, 1024, 2176)),
)


def _layer_shard_shapes(D, IN):
    return (("w_in", (D, IN // N_DEV), 1), ("w_mod", (D, 3 * D // N_DEV), 1),
            ("w_branch", (3, D // N_DEV, D), 1), ("w_out", (D // N_DEV, D), 0))


def _pack_layer(w, l, D):
    return jnp.concatenate([w[n][l].reshape(-1) for n in BIG_PACK]).reshape(-1, D)


def _unpack_shards(pack, D, IN):
    flat, out, off = pack.reshape(-1), {}, 0
    for n, shape, _ in _layer_shard_shapes(D, IN):
        sz = shape[0] * shape[1] * (shape[2] if len(shape) > 2 else 1)
        out[n] = flat[off:off + sz].reshape(shape)
        off += sz
    return out


def _unpack_gathered(g, D, IN):
    flat, out, off = g.reshape(N_DEV, -1), {}, 0
    for n, shape, axis in _layer_shard_shapes(D, IN):
        sz = shape[0] * shape[1] * (shape[2] if len(shape) > 2 else 1)
        out[n] = _unshard(flat[:, off:off + sz].reshape((N_DEV,) + shape), axis)
        off += sz
    return out


def _loss_fn(packs, p, x, c, ctx, target):
    B, S, D = x.shape
    L = ctx.shape[1]
    T = S + L
    tm = min(L, 256)
    KVW = D // KV_GROUP
    widths = (D, D, D, KVW, KVW, D, D, KVW, KVW, D, 3 * D)
    offs = [0]
    for w in widths:
        offs.append(offs[-1] + w)
    cos, sin = _rope_tables(S, L)

    X = jnp.concatenate([x, ctx], axis=1)
    sc, scc = jax.nn.silu(c), jax.nn.silu(p["c_ctx"])
    A = jnp.concatenate([scc[None], sc, jnp.zeros((SUBLANE - 1 - B, D), F32)], axis=0)
    gathered = _gather_op("gather_layer0")(packs[0])
    for l in range(DEPTH):
        big = _unpack_gathered(gathered, D, offs[-1])
        mod = _matmul("mm_mod")(A, big["w_mod"]) + p["b_mod"][l]
        modp = jnp.stack([jnp.broadcast_to(mod[0], (B, 3 * D)), mod[1:1 + B]], axis=1)[:, :, None, :]
        (h,) = _rowwise(_prenorm_f(D), "prenorm", [(D, F32)], tm, S, 1, 1, 1)(X, modp, p["norm_g"][l][None])
        proj = _matmul("mm_in", *IN_TILES[l])(h.reshape(B * T, D), big["w_in"]).reshape(B, T, offs[-1])
        uA, gA, qB, kB, vB, gB, qC, kC, vC, gC, m3 = [proj[:, :, offs[i]:offs[i + 1]] for i in range(len(widths))]
        u = _conv_op(S)(uA, p["conv_w"][l], p["conv_b"][l][None])
        nsp = -LRU_C * jax.nn.softplus(-p["lru_lambda"][l])
        pv = jnp.concatenate([p["lru_ba"][l], p["lru_bx"][l], nsp, jnp.zeros((2, D), F32)], axis=0)
        wm = jnp.concatenate([_gate_blocks(p["lru_wa"][l]), _gate_blocks(p["lru_wx"][l])], axis=0)
        wm = wm.reshape(-1, GATE_BLOCK)
        af, bf, ar, br = _rowwise(_coef_f(D), "lru_coef", [(D, F32)] * 4, tm, S, 1, 0, 2)(u, pv, wm)
        hf = _scan_op(S, False)(af, bf)
        hr = _scan_op(S, True)(ar, br)
        (zA,) = _rowwise(_gate2_f, "gate_a", [(D, F32)], tm, S, 3, 0, 0)(hf, hr, gA)
        yB = _attn_branch(S, tm, True, False, True)(qB, kB, vB, p["attn_sink"][l], cos, sin)
        (zB,) = _rowwise(_gate1_f, "gate_b", [(D, F32)], tm, S, 2, 0, 0)(yB, gB)
        qkv_c = (qC, kC, vC, p["q_norm_g"][l][None], p["k_norm_g"][l][None])
        if l + 1 < DEPTH:
            yC, gathered = _attn_branch(S, tm, False, True, False, True)(*qkv_c, packs[l + 1], cos, sin)
        else:
            yC = _attn_branch(S, tm, False, True, False)(*qkv_c, cos, sin)
        (zC,) = _rowwise(_gate1_f, "gate_c", [(D, F32)], tm, S, 2, 0, 0)(yC, gC)
        pr = [_matmul("mm_branch", *SQ_TILES[l])(z.reshape(B * T, D), big["w_branch"][n]).reshape(B, T, D)
              for n, z in enumerate((zA, zB, zC))]
        (mg,) = _rowwise(_merge_f(D), "merge", [(D, F32)], tm, S, 4, 0, 0)(m3, *pr)
        y = _matmul("mm_out", *SQ_TILES[l])(mg.reshape(B * T, D), big["w_out"]).reshape(B, T, D)
        (X,) = _rowwise(_resid_f(D), "resid", [(D, F32)], tm, S, 2, 1, 0)(X, y, modp)
    return _final_loss(S, tm)(X, target, p["final_g"][None])


def _shard_axis(name):
    return {"w_mod": 2, "w_in": 2, "conv_w": 2, "lru_ba": 2, "lru_bx": 2, "lru_lambda": 2,
            "w_branch": 2, "w_out": 1}.get(name)


def _unshard(g, axis):
    full = jnp.moveaxis(g, 0, axis)
    shape = list(full.shape)
    shape[axis:axis + 2] = [shape[axis] * shape[axis + 1]]
    return full.reshape(shape)


def _reshard(full, axis):
    shape = list(full.shape)
    shape[axis:axis + 1] = [N_DEV, shape[axis] // N_DEV]
    return jnp.moveaxis(full.reshape(shape), axis, 0)


def _pad_to(v, n):
    return jnp.concatenate([v, jnp.zeros((n - v.shape[0],), v.dtype)]) if n > v.shape[0] else v


def _step(x, c, ctx, target, w, m, v):
    D = x.shape[2]
    IN = w["w_in"].shape[2] * N_DEV
    full = {n: w[n] for n in REPLICATED}
    small_local = jnp.concatenate([w[n].reshape(-1) for n in SMALL_SHARDED])
    small_all = _exchange(small_local.reshape(-1, LANE), "gather_small", True).reshape(N_DEV, -1)
    off = 0
    for n in SMALL_SHARDED:
        sz = w[n].size
        full[n] = _unshard(small_all[:, off:off + sz].reshape((N_DEV,) + w[n].shape), _shard_axis(n))
        off += sz
    packs = [_pack_layer(w, l, D) for l in range(DEPTH)]

    loss, (gpacks, gp, gx) = jax.value_and_grad(_loss_fn, argnums=(0, 1, 2))(packs, full, x, c, ctx, target)
    loss = lax.psum(loss, AXES)

    out = {}
    gshards = [_unpack_shards(g, D, IN) for g in gpacks]
    for n in BIG_PACK:
        g = jnp.stack([gs[n] for gs in gshards])
        C = w[n].shape[-1]
        res = _adamw(g.reshape(1, -1, C), w[n].reshape(-1, C), m[n].reshape(-1, C), v[n].reshape(-1, C),
                     "adamw_" + n)
        out[n] = [r.reshape(w[n].shape) for r in res]

    rep = jnp.concatenate([gp[n].reshape(-1) for n in REPLICATED])
    n_rep = rep.shape[0]
    chunk = -(-n_rep // (N_DEV * LANE)) * LANE
    rep = _pad_to(rep, N_DEV * chunk).reshape(N_DEV, chunk)
    shards = jnp.concatenate([_reshard(gp[n], _shard_axis(n)).reshape(N_DEV, -1) for n in SMALL_SHARDED], axis=1)
    n_sh = shards.shape[1]
    recv = _exchange(jnp.concatenate([rep, shards], axis=1).reshape(N_DEV, -1, LANE), "scatter_small", False)
    wl =jnp.concatenate([w[n].reshape(-1) for n in SMALL_SHARDED])
    ml = jnp.concatenate([m[n].reshape(-1) for n in SMALL_SHARDED])
    vl = jnp.concatenate([v[n].reshape(-1) for n in SMALL_SHARDED])
    rows = recv.shape[1]
    rrows = chunk // LANE
    g_sh, d_sh, m_sh, v_sh = _adamw(recv[:, rrows:], wl.reshape(-1, LANE), ml.reshape(-1, LANE),
                                    vl.reshape(-1, LANE), "adamw_small_sharded")
    g_rep8 = _sum_slots(recv[:, :rrows], "sum_replicated")
    g_rep = _exchange(g_rep8, "gather_replicated", True).reshape(-1)
    wr = _pad_to(jnp.concatenate([w[n].reshape(-1) for n in REPLICATED]), N_DEV * chunk)
    mr = _pad_to(jnp.concatenate([m[n].reshape(-1) for n in REPLICATED]), N_DEV * chunk)
    vr = _pad_to(jnp.concatenate([v[n].reshape(-1) for n in REPLICATED]), N_DEV * chunk)
    res_rep = _adamw(g_rep.reshape(1, -1, LANE), wr.reshape(-1, LANE), mr.reshape(-1, LANE), vr.reshape(-1, LANE),
                     "adamw_replicated")
    off = 0
    for n in REPLICATED:
        sz = w[n].size
        out[n] = [r.reshape(-1)[off:off + sz].reshape(w[n].shape) for r in res_rep]
        off += sz
    off = 0
    for n in SMALL_SHARDED:
        sz = w[n].size
        out[n] = [r.reshape(-1)[off:off + sz].reshape(w[n].shape) for r in (g_sh, d_sh, m_sh, v_sh)]
        off += sz
    assert off == n_sh and rows == rrows + n_sh // LANE
    return (loss, gx, *[out[n][0] for n in WEIGHTS], *[out[n][1] for n in WEIGHTS],
            *[out[n][2] for n in WEIGHTS], *[out[n][3] for n in WEIGHTS])


def kernel(x, c, ctx, c_ctx, norm_g, w_mod, b_mod, w_in, conv_w, conv_b, lru_wa, lru_ba, lru_wx, lru_bx, lru_lambda, attn_sink, q_norm_g, k_norm_g, w_branch, w_out, final_g, loss_target, m_c_ctx, m_norm_g, m_w_mod, m_b_mod, m_w_in, m_conv_w, m_conv_b, m_lru_wa, m_lru_ba, m_lru_wx, m_lru_bx, m_lru_lambda, m_attn_sink, m_q_norm_g, m_k_norm_g, m_w_branch, m_w_out, m_final_g, v_c_ctx, v_norm_g, v_w_mod, v_b_mod, v_w_in, v_conv_w, v_conv_b, v_lru_wa, v_lru_ba, v_lru_wx, v_lru_bx, v_lru_lambda, v_attn_sink, v_q_norm_g, v_k_norm_g, v_w_branch, v_w_out, v_final_g):
    w = dict(zip(WEIGHTS, (c_ctx, norm_g, w_mod, b_mod, w_in, conv_w, conv_b, lru_wa, lru_ba, lru_wx, lru_bx,
                           lru_lambda, attn_sink, q_norm_g, k_norm_g, w_branch, w_out, final_g)))
    m = dict(zip(WEIGHTS, (m_c_ctx, m_norm_g, m_w_mod, m_b_mod, m_w_in, m_conv_w, m_conv_b, m_lru_wa, m_lru_ba,
                           m_lru_wx, m_lru_bx, m_lru_lambda, m_attn_sink, m_q_norm_g, m_k_norm_g, m_w_branch,
                           m_w_out, m_final_g)))
    v = dict(zip(WEIGHTS, (v_c_ctx, v_norm_g, v_w_mod, v_b_mod, v_w_in, v_conv_w, v_conv_b, v_lru_wa, v_lru_ba,
                           v_lru_wx, v_lru_bx, v_lru_lambda, v_attn_sink, v_q_norm_g, v_k_norm_g, v_w_branch,
                           v_w_out, v_final_g)))
    return _step(x, c, ctx, loss_target, w, m, v)
```

```python
import functools

import jax
import jax.numpy as jnp
from jax import lax
from jax.experimental import pallas as pl
from jax.experimental.pallas import tpu as pltpu

F32 = jnp.float32
BF16 = jnp.bfloat16

AXES = ("x", "y", "c")
N_DEV = 8
DEPTH = 4
HEAD_DIM = 128
GRID_W = 64
WINDOW = 128
LRU_BLOCK_W = 64
GATE_BLOCK = 256
LRU_C = 8.0
ROPE_THETA = 10000.0
EPS = 1e-6
NEG_INF = -1e30
KV_GROUP = 4
LANE = 128
SUBLANE = 8
VMEM_LIMIT = 56 * 1024 * 1024

ADAM_LR = 0.001
ADAM_B1 = 0.9
ADAM_B2 = 0.999
ADAM_EPS = 1e-08
ADAM_WD = 0.01
ADAM_STEP = 10

WEIGHTS = ("c_ctx", "norm_g", "w_mod", "b_mod", "w_in", "conv_w", "conv_b", "lru_wa", "lru_ba", "lru_wx",
           "lru_bx", "lru_lambda", "attn_sink", "q_norm_g", "k_norm_g", "w_branch", "w_out", "final_g")
BIG_PACK = ("w_in", "w_mod", "w_branch", "w_out")
SMALL_SHARDED = ("conv_w", "lru_ba", "lru_bx", "lru_lambda")
REPLICATED = ("c_ctx", "norm_g", "b_mod", "conv_b", "lru_wa", "lru_wx", "attn_sink", "q_norm_g", "k_norm_g",
              "final_g")


def _call(body, **kw):
    return pl.pallas_call(body, **kw)


def _params(dims=None, vmem=VMEM_LIMIT):
    return pltpu.CompilerParams(dimension_semantics=dims, vmem_limit_bytes=vmem)


def _pick(n, target, mult):
    for t in range(min(n, target), 0, -1):
        if n % t == 0 and t % mult == 0:
            return t
    return n


def _mm(a, b, *, name, ta=False, tb=False, out_dtype=F32, tm=512, tn=1024, tk=1024):
    M, K = (a.shape[1], a.shape[0]) if ta else a.shape
    N = b.shape[0] if tb else b.shape[1]
    assert (b.shape[1] if tb else b.shape[0]) == K
    tm = _pick(M, tm, LANE if ta else 16)
    tn = _pick(N, tn, LANE)
    tk = _pick(K, tk, 16 if ta and not tb else LANE)
    nk = K // tk
    dn = (((0 if ta else 1,), (1 if tb else 0,)), ((), ()))

    def body(a_ref, b_ref, o_ref, *acc):
        r = lax.dot_general(a_ref[...].astype(BF16), b_ref[...].astype(BF16), dn,
                            preferred_element_type=F32)
        if nk == 1:
            o_ref[...] = r.astype(o_ref.dtype)
        else:
            k = pl.program_id(2)

            @pl.when(k == 0)
            def _():
                acc[0][...] = r

            @pl.when(k > 0)
            def _():
                acc[0][...] += r

            @pl.when(k == nk - 1)
            def _():
                o_ref[...] = acc[0][...].astype(o_ref.dtype)

    a_spec = (pl.BlockSpec((tk, tm), lambda i, j, k: (k, i)) if ta
              else pl.BlockSpec((tm, tk), lambda i, j, k: (i, k)))
    b_spec = (pl.BlockSpec((tn, tk), lambda i, j, k: (j, k)) if tb
              else pl.BlockSpec((tk, tn), lambda i, j, k: (k, j)))
    return _call(
        body, name=name, grid=(M // tm, N // tn, nk),
        in_specs=[a_spec, b_spec],
        out_specs=pl.BlockSpec((tm, tn), lambda i, j, k: (i, j)),
        out_shape=jax.ShapeDtypeStruct((M, N), out_dtype),
        scratch_shapes=[pltpu.VMEM((tm, tn), F32)] if nk > 1 else [],
        compiler_params=_params(("parallel", "parallel", "arbitrary")),
    )(a, b)


def _matmul(name, t_fwd=(512, 1024, 1024), t_da=(512, 1024, 1024), t_dw=(1024, 1024, 512)):
    def tiles(t):
        return dict(tm=t[0], tn=t[1], tk=t[2])

    @jax.custom_vjp
    def f(a, w):
        return _mm(a, w, name=name + "_fwd", **tiles(t_fwd))

    def fwd(a, w):
        return f(a, w), (a, w)

    def bwd(res, g):
        a, w = res
        da = _mm(g, w, name=name + "_da", tb=True, out_dtype=a.dtype, **tiles(t_da))
        dw = _mm(a, g, name=name + "_dw", ta=True, out_dtype=w.dtype, **tiles(t_dw))
        return da, dw

    f.defvjp(fwd, bwd)
    return f


def _row_specs(xs, tps, gps, rts, tm, n_lat_tiles):
    x_specs = [pl.BlockSpec((None, tm, x.shape[2]), lambda b, i: (b, i, 0)) for x in xs]
    tp_specs = [pl.BlockSpec((None, None, 1, p.shape[3]),
                             lambda b, i: (b, (i < n_lat_tiles).astype(jnp.int32), 0, 0)) for p in tps]
    gp_specs = [pl.BlockSpec(p.shape, lambda b, i: (0, 0)) for p in gps]
    rt_specs = [pl.BlockSpec((tm, t.shape[1]), lambda b, i: (i, 0)) for t in rts]
    return x_specs, tp_specs, gp_specs, rt_specs


def _row_fwd(f, name, xs, tps, gps, rts, outs, tm, n_lat):
    B, T, _ = xs[0].shape
    n_in = len(xs) + len(tps) + len(gps) + len(rts)

    def body(*refs):
        vals = f(*[r[...] for r in refs[:n_in]])
        for o, v in zip(refs[n_in:], vals):
            o[...] = v.astype(o.dtype)

    x_specs, tp_specs, gp_specs, rt_specs = _row_specs(xs, tps, gps, rts, tm, n_lat // tm)
    res = _call(
        body, name=name, grid=(B, T // tm),
        in_specs=x_specs + tp_specs + gp_specs + rt_specs,
        out_specs=[pl.BlockSpec((None, tm, w), lambda b, i: (b, i, 0)) for w, _ in outs],
        out_shape=[jax.ShapeDtypeStruct((B, T, w), dt) for w, dt in outs],
        compiler_params=_params(("parallel", "parallel")),
    )(*xs, *tps, *gps, *rts)
    return list(res)


def _row_bwd(f, name, xs, tps, gps, rts, douts, tm, n_lat):
    B, T, _ = xs[0].shape
    nx, ntp, ngp, nd = len(xs), len(tps), len(gps), len(douts)
    n_lat_tiles = n_lat // tm
    n_diff = nx + ntp + ngp
    n_in = n_diff + len(rts)

    def body(*refs):
        diff = [r[...] for r in refs[:n_diff]]
        tabs = [r[...] for r in refs[n_diff:n_in]]
        dos = [r[...] for r in refs[n_in:n_in + nd]]
        o_refs = refs[n_in + nd:]
        prim, vjp = jax.vjp(lambda *d: tuple(f(*d, *tabs)), *diff)
        grads = vjp(tuple(d.astype(p.dtype) for d, p in zip(dos, prim)))
        b, i = pl.program_id(0), pl.program_id(1)
        for k in range(nx):
            o_refs[k][...] = grads[k].astype(o_refs[k].dtype)
        first_tp = jnp.logical_or(i == 0, i == n_lat_tiles)
        first_gp = jnp.logical_and(b == 0, i == 0)
        for k in range(nx, n_diff):
            first = first_tp if k < nx + ntp else first_gp

            @pl.when(first)
            def _(k=k):
                o_refs[k][...] = grads[k]

            @pl.when(jnp.logical_not(first))
            def _(k=k):
                o_refs[k][...] += grads[k]

    x_specs, tp_specs, gp_specs, rt_specs = _row_specs(xs, tps, gps, rts, tm, n_lat_tiles)
    d_specs = [pl.BlockSpec((None, tm, d.shape[2]), lambda b, i: (b, i, 0)) for d in douts]
    res = _call(
        body, name=name, grid=(B, T // tm),
        in_specs=x_specs + tp_specs + gp_specs + rt_specs + d_specs,
        out_specs=x_specs + tp_specs + gp_specs,
        out_shape=[jax.ShapeDtypeStruct(a.shape, F32) for a in (*xs, *tps, *gps)],
        compiler_params=_params(("arbitrary", "arbitrary")),
    )(*xs, *tps, *gps, *rts, *douts)
    res = list(res)
    return res[:nx], res[nx:nx + ntp], res[nx + ntp:]


def _rowwise(f, name, outs, tm, n_lat, n_x, n_tp, n_gp):
    def split(args):
        return (args[:n_x], args[n_x:n_x + n_tp], args[n_x + n_tp:n_x + n_tp + n_gp],
                args[n_x + n_tp + n_gp:])

    @jax.custom_vjp
    def op(*args):
        xs, tps, gps, rts = split(args)
        return tuple(_row_fwd(f, name + "_fwd", xs, tps, gps, rts, outs, tm, n_lat))

    def fwd(*args):
        return op(*args), args

    def bwd(args, g):
        xs, tps, gps, rts = split(args)
        dxs, dtps, dgps = _row_bwd(f, name + "_bwd", xs, tps, gps, rts, list(g), tm, n_lat)
        return (*dxs, *dtps, *dgps, *[jnp.zeros_like(t) for t in rts])

    op.defvjp(fwd, bwd)
    return op


def _shift_impl(u, k):
    n = u.shape[0]
    r = pltpu.roll(u, k % n, axis=0)
    row = lax.broadcasted_iota(jnp.int32, u.shape, 0)
    valid = (row >= k) if k > 0 else (row < n + k)
    return jnp.where(valid, r, 0.0)


@functools.partial(jax.custom_vjp, nondiff_argnums=(1,))
def _shift(u, k):
    return _shift_impl(u, k)


_shift.defvjp(lambda u, k: (_shift_impl(u, k), None), lambda k, _, g: (_shift_impl(g, -k),))


def _swap_impl(x):
    lane = lax.broadcasted_iota(jnp.int32, x.shape, 1)
    q = HEAD_DIM // 4
    return jnp.where((lane % (2 * q)) < q, pltpu.roll(x, HEAD_DIM - q, axis=1), pltpu.roll(x, q, axis=1))


@jax.custom_vjp
def _swap(x):
    return _swap_impl(x)


_swap.defvjp(lambda x: (_swap_impl(x), None), lambda _, g: (_swap_impl(g),))


def _conv_f(ul, uc, cw, cb):
    def conv(u):
        return (_shift(u, 2) * cw[0:1] + _shift(u, 1) * cw[1:2] + u * cw[2:3] + _shift(u, -1) * cw[3:4] + cb)
    return conv(ul), conv(uc)


def _conv_specs(B, T, D):
    u_spec = pl.BlockSpec((None, T, LANE), lambda j, b: (b, 0, j))
    cw_spec = pl.BlockSpec((4, LANE), lambda j, b: (0, j))
    cb_spec = pl.BlockSpec((1, LANE), lambda j, b: (0, j))
    return u_spec, cw_spec, cb_spec


def _conv_fwd(u, cw, cb, S):
    B, T, D = u.shape

    def body(u_ref, cw_ref, cb_ref, o_ref):
        vl, vc = _conv_f(u_ref[0:S, :], u_ref[S:T, :], cw_ref[...], cb_ref[...])
        o_ref[0:S, :] = vl
        o_ref[S:T, :] = vc

    u_spec, cw_spec, cb_spec = _conv_specs(B, T, D)
    return _call(body, name="conv_fwd", grid=(D // LANE, B), in_specs=[u_spec, cw_spec, cb_spec],
                 out_specs=u_spec, out_shape=jax.ShapeDtypeStruct(u.shape, F32),
                 compiler_params=_params(("parallel", "parallel")))(u, cw, cb)


def _conv_bwd(u, cw, cb, dv, S):
    B, T, D = u.shape

    def body(u_ref, cw_ref, cb_ref, dv_ref, du_ref, dcw_ref, dcb_ref):
        _, vjp = jax.vjp(_conv_f, u_ref[0:S, :], u_ref[S:T, :], cw_ref[...], cb_ref[...])
        dul, duc, dcw, dcb = vjp((dv_ref[0:S, :], dv_ref[S:T, :]))
        du_ref[0:S, :] = dul
        du_ref[S:T, :] = duc
        first = pl.program_id(1) == 0

        @pl.when(first)
        def _():
            dcw_ref[...] = dcw
            dcb_ref[...] = dcb

        @pl.when(jnp.logical_not(first))
        def _():
            dcw_ref[...] += dcw
            dcb_ref[...] += dcb

    u_spec, cw_spec, cb_spec = _conv_specs(B, T, D)
    return _call(body, name="conv_bwd", grid=(D // LANE, B), in_specs=[u_spec, cw_spec, cb_spec, u_spec],
                 out_specs=[u_spec, cw_spec, cb_spec],
                 out_shape=[jax.ShapeDtypeStruct(u.shape, F32), jax.ShapeDtypeStruct(cw.shape, F32),
                            jax.ShapeDtypeStruct(cb.shape, F32)],
                 compiler_params=_params(("parallel", "arbitrary")))(u, cw, cb, dv)


def _conv_op(S):
    @jax.custom_vjp
    def op(u, cw, cb):
        return _conv_fwd(u, cw, cb, S)

    def fwd(u, cw, cb):
        return op(u, cw, cb), (u, cw, cb)

    def bwd(res, g):
        u, cw, cb = res
        return tuple(_conv_bwd(u, cw, cb, g, S))

    op.defvjp(fwd, bwd)
    return op


SCAN_UNROLL = 4


def _group_scan(A, Bv, asc):
    row = lax.broadcasted_iota(jnp.int32, A.shape, 0)
    for s in (1, 2, 4):
        sh = s if asc else SUBLANE - s
        valid = (row >= s) if asc else (row < SUBLANE - s)
        A_sh = pltpu.roll(A, sh, axis=0)
        B_sh = pltpu.roll(Bv, sh, axis=0)
        Bv = jnp.where(valid, A * B_sh, 0.0) + Bv
        A = jnp.where(valid, A * A_sh, A)
    return A, Bv


def _chain_step(A, Bv, carry, asc):
    row = lax.broadcasted_iota(jnp.int32, A.shape, 0)
    A2, B2 = _group_scan(A, Bv, asc)
    h = A2 * carry + B2
    if asc:
        prev = jnp.where(row == 0, carry, pltpu.roll(h, 1, axis=0))
        return h, prev, h[SUBLANE - 1:SUBLANE, :]
    prev = jnp.where(row == SUBLANE - 1, carry, pltpu.roll(h, SUBLANE - 1, axis=0))
    return h, prev, h[0:1, :]


def _chain_loop(segments, step):
    carry = jnp.zeros((1, LANE), F32)
    for lo, hi, asc in segments:
        span = SUBLANE * SCAN_UNROLL
        assert (hi - lo) % span == 0

        def it(t, carry, lo=lo, hi=hi, asc=asc, span=span):
            base = lo + t * span if asc else hi - (t + 1) * span
            order = range(SCAN_UNROLL) if asc else reversed(range(SCAN_UNROLL))
            for j in order:
                carry = step(pl.multiple_of(base + SUBLANE * j, SUBLANE), carry, asc)
            return carry

        carry = lax.fori_loop(0, (hi - lo) // span, it, carry)
    return carry


def _scan_specs(T):
    return pl.BlockSpec((None, T, LANE), lambda j, b: (b, 0, j))


def _scan_fwd(a, b, S, reverse):
    B, T, D = a.shape
    asc = not reverse
    segments = [(S, T, asc), (0, S, asc)]

    def body(a_ref, b_ref, h_ref, hp_ref):
        def step(r0, carry, asc):
            rows = pl.ds(r0, SUBLANE)
            h, prev, carry = _chain_step(a_ref[rows, :], b_ref[rows, :], carry, asc)
            h_ref[rows, :] = h
            hp_ref[rows, :] = prev
            return carry
        _chain_loop(segments, step)

    spec = _scan_specs(T)
    return _call(body, name="scan_rev_fwd" if reverse else "scan_fwd_fwd", grid=(D // LANE, B),
                 in_specs=[spec, spec], out_specs=[spec, spec],
                 out_shape=[jax.ShapeDtypeStruct(a.shape, F32)] * 2,
                 compiler_params=_params(("parallel", "parallel")))(a, b)


def _scan_bwd(a, hp, dy, S, reverse):
    B, T, D = a.shape
    asc = reverse
    segments = [(0, S, asc), (S, T, asc)]

    def body(a_ref, hp_ref, dy_ref, da_ref, db_ref):
        def step(r0, carry, asc):
            rows = pl.ds(r0, SUBLANE)
            A, dy = a_ref[rows, :], dy_ref[rows, :]
            _, s_prev, carry = _chain_step(A, A * dy, carry, asc)
            g = dy + s_prev
            db_ref[rows, :] = g
            da_ref[rows, :] = g * hp_ref[rows, :]
            return carry
        _chain_loop(segments, step)

    spec = _scan_specs(T)
    return _call(body, name="scan_rev_bwd" if reverse else "scan_fwd_bwd", grid=(D // LANE, B),
                 in_specs=[spec, spec, spec], out_specs=[spec, spec],
                 out_shape=[jax.ShapeDtypeStruct(a.shape, F32)] * 2,
                 compiler_params=_params(("parallel", "parallel")))(a, hp, dy)


def _scan_op(S, reverse):
    @jax.custom_vjp
    def op(a, b):
        return _scan_fwd(a, b, S, reverse)[0]

    def fwd(a, b):
        h, hp = _scan_fwd(a, b, S, reverse)
        return h, (a, hp)

    def bwd(res, g):
        a, hp = res
        return tuple(_scan_bwd(a, hp, g, S, reverse))

    op.defvjp(fwd, bwd)
    return op


def _band_lo(qi, tq, S):
    span = tq + 2 * WINDOW
    return pl.multiple_of(jnp.clip(qi * tq - WINDOW, 0, S - span), LANE)


def _band_mask(qi, tq, lo, span, transposed):
    shape = (span, tq) if transposed else (tq, span)
    qpos = qi * tq + lax.broadcasted_iota(jnp.int32, shape, 1 if transposed else 0)
    kpos = lo + lax.broadcasted_iota(jnp.int32, shape, 0 if transposed else 1)
    return jnp.abs(kpos - qpos) <= WINDOW


NT = (((1,), (1,)), ((), ()))
LOG2E = 1.4426950408889634
KEY_CHUNK = 1024


def _col_to_row(c):
    return jnp.broadcast_to(c, (c.shape[0], LANE)).T[0:1, :]


def _attn_fwd(q, k, v, sink, S, band, side=None):
    B, T, HD = q.shape
    H = HD // HEAD_DIM
    L = T - S
    tq = L
    n_lq = S // tq
    span = tq + 2 * WINDOW
    c2 = HEAD_DIM ** -0.5 * LOG2E
    has_sink = sink is not None
    kc = min(KEY_CHUNK, S)
    hp = KV_GROUP if band else 1

    def body(*refs):
        if has_sink:
            q_ref, k_ref, v_ref, s_ref, o_ref, lse_ref = refs
        else:
            q_ref, k_ref, v_ref, o_ref, lse_ref = refs
        qi = pl.program_id(2)

        def online(qv, sk2, segs):
            def scores(seg):
                s = lax.dot_general(qv, k_ref[seg[0], :], NT, preferred_element_type=F32) * c2
                return s if seg[1] is None else jnp.where(seg[1], s, NEG_INF)

            m = l = acc = None
            s_next = scores(segs[0])
            for j, (rows, _) in enumerate(segs):
                s = s_next
                if j + 1 < len(segs):
                    s_next = scores(segs[j + 1])
                ms = jnp.max(s, axis=-1, keepdims=True)
                if m is None:
                    m = ms if sk2 is None else jnp.maximum(ms, sk2)
                    p = jnp.exp2(s - m)
                    l = jnp.sum(p, axis=-1, keepdims=True)
                    if sk2 is not None:
                        l = l + jnp.exp2(sk2 - m)
                    acc = jnp.dot(p.astype(BF16), v_ref[rows, :], preferred_element_type=F32)
                else:
                    m_new = jnp.maximum(m, ms)
                    alpha = jnp.exp2(m - m_new)
                    p = jnp.exp2(s - m_new)
                    l = alpha * l + jnp.sum(p, axis=-1, keepdims=True)
                    acc = alpha * acc + jnp.dot(p.astype(BF16), v_ref[rows, :], preferred_element_type=F32)
                    m = m_new
            return acc, m, l

        def joint(qv, sk2, segs):
            ss = []
            for rows, mask in segs:
                s = lax.dot_general(qv, k_ref[rows, :], NT, preferred_element_type=F32) * c2
                ss.append(s if mask is None else jnp.where(mask, s, NEG_INF))
            m = functools.reduce(jnp.maximum, [jnp.max(s, axis=-1, keepdims=True) for s in ss])
            if sk2 is not None:
                m = jnp.maximum(m, sk2)
            ps = [jnp.exp2(s - m) for s in ss]
            l = functools.reduce(jnp.add, [jnp.sum(p, axis=-1, keepdims=True) for p in ps])
            if sk2 is not None:
                l = l + jnp.exp2(sk2 - m)
            acc = functools.reduce(jnp.add, [jnp.dot(p.astype(BF16), v_ref[rows, :], preferred_element_type=F32)
                                             for p, (rows, _) in zip(ps, segs)])
            return acc, m, l

        def run(segs, softmax):
            for g in range(hp):
                lanes = slice(g * HEAD_DIM, (g + 1) * HEAD_DIM)
                sk2 = s_ref[g][:, 0:1] * LOG2E if has_sink else None
                acc, m, l = softmax(q_ref[:, lanes], sk2, segs)
                o_ref[:, lanes] = acc * (1.0 / l)
                lse_ref[g] = _col_to_row(m + jnp.log2(l))

        ctx_rows = pl.ds(S, L)

        @pl.when(qi < n_lq)
        def _():
            if band:
                lo = _band_lo(qi, tq, S)
                run([(ctx_rows, None), (pl.ds(lo, span), _band_mask(qi, tq, lo, span, False))], joint)
            else:
                run([(pl.ds(j * kc, kc), None) for j in range(S // kc)] + [(ctx_rows, None)], online)

        @pl.when(qi >= n_lq)
        def _():
            run([(ctx_rows, None)], joint)

    q_spec = pl.BlockSpec((None, tq, hp * HEAD_DIM), lambda b, h, i: (b, i, h))
    kv_spec = pl.BlockSpec((None, T, HEAD_DIM), lambda b, h, i: (b, 0, h * hp // KV_GROUP))
    in_specs = [q_spec, kv_spec, kv_spec]
    args = [q, k, v]
    if has_sink:
        in_specs.append(pl.BlockSpec((hp, 1, LANE), lambda b, h, i: (h, 0, 0)))
        args.append(sink)
    name = "attn_band_fwd" if band else "attn_dense_fwd"
    return _call_with_side(
        body, side, name=name, grid=(B, H // hp, T // tq), in_specs=in_specs, args=args,
        out_specs=[q_spec, pl.BlockSpec((None, hp, 1, tq), lambda b, h, i: (b, h, 0, i))],
        out_shape=[jax.ShapeDtypeStruct(q.shape, F32), jax.ShapeDtypeStruct((B, H, 1, T), F32)],
        scratch_shapes=[], dims=("parallel", "parallel", "parallel"))


def _attn_bwd(q, k, v, sink, do, o, lse, S, band, side=None):
    B, T, HD = q.shape
    H = HD // HEAD_DIM
    KVH = H // KV_GROUP
    L = T - S
    tq = L
    n_lq = S // tq
    span = tq + 2 * WINDOW
    scale = HEAD_DIM ** -0.5
    c2 = scale * LOG2E
    has_sink = sink is not None
    kc = min(KEY_CHUNK, S)
    hp = KV_GROUP if band else 1

    def body(*refs):
        if has_sink:
            q_ref, k_ref, v_ref, do_ref, o_ref, lse_ref, s_ref, dq_ref, dk_ref, dv_ref, ds_ref, *scr = refs
        else:
            q_ref, k_ref, v_ref, do_ref, o_ref, lse_ref, dq_ref, dk_ref, dv_ref, *scr = refs
        g, qi = pl.program_id(2), pl.program_id(3)

        @pl.when(jnp.logical_and(g == 0, qi == 0))
        def _():
            dk_ref[...] = jnp.zeros_like(dk_ref)
            dv_ref[...] = jnp.zeros_like(dv_ref)
            if not band:
                scr[0][...] = k_ref[...].astype(F32).T.astype(BF16)

        def run(segs):
            dks, dvs = [None] * len(segs), [None] * len(segs)
            for h in range(hp):
                lanes = slice(h * HEAD_DIM, (h + 1) * HEAD_DIM)
                qv = q_ref[:, lanes]
                dof = do_ref[:, lanes]
                dov = dof.astype(BF16)
                lse2 = lse_ref[h]
                delta = _col_to_row(jnp.sum(dof * o_ref[:, lanes], axis=-1, keepdims=True))

                def head(seg, qv=qv, dov=dov):
                    rows = pl.ds(seg[0], seg[1])
                    s = lax.dot_general(k_ref[rows, :], qv, NT, preferred_element_type=F32) * c2
                    if seg[2] is not None:
                        s = jnp.where(seg[2], s, NEG_INF)
                    return s, lax.dot_general(v_ref[rows, :], dov, NT, preferred_element_type=F32)

                dq, dqT = None, None
                nxt = head(segs[0])
                for j, (lo, n, _) in enumerate(segs):
                    rows = pl.ds(lo, n)
                    s, dp = nxt
                    if j + 1 < len(segs):
                        nxt = head(segs[j + 1])
                    p = jnp.exp2(s - lse2)
                    ds = p * (dp - delta)
                    dsb = ds.astype(BF16)
                    dv = jnp.dot(p.astype(BF16), dov, preferred_element_type=F32)
                    dk = jnp.dot(dsb, qv, preferred_element_type=F32)
                    dvs[j] = dv if dvs[j] is None else dvs[j] + dv
                    dks[j] = dk if dks[j] is None else dks[j] + dk
                    if band:
                        part = jnp.dot(ds.T.astype(BF16), k_ref[rows, :], preferred_element_type=F32)
                        dq = part if dq is None else dq + part
                    else:
                        part = jnp.dot(scr[0][:, lo:lo + n], dsb, preferred_element_type=F32)
                        dqT = part if dqT is None else dqT + part
                dq_ref[:, lanes] = (dq if band else dqT.T) * scale
                if has_sink:
                    psk = jnp.exp2(s_ref[h][:, 0:1] * LOG2E - lse2)
                    dsk = jnp.broadcast_to(-jnp.sum(psk * delta, axis=1, keepdims=True), (1, LANE))

                    @pl.when(qi == 0)
                    def _(h=h, dsk=dsk):
                        ds_ref[h] = dsk

                    @pl.when(qi > 0)
                    def _(h=h, dsk=dsk):
                        ds_ref[h] += dsk
            for j, (lo, n, _) in enumerate(segs):
                rows = pl.ds(lo, n)
                dv_ref[rows, :] += dvs[j]
                dk_ref[rows, :] += dks[j] * scale

        @pl.when(qi < n_lq)
        def _():
            if band:
                lo = _band_lo(qi, tq, S)
                run([(S, L, None), (lo, span, _band_mask(qi, tq, lo, span, True))])
            else:
                run([(j * kc, kc, None) for j in range(S // kc)] + [(S, L, None)])

        @pl.when(qi >= n_lq)
        def _():
            run([(S, L, None)])

    ng = KV_GROUP // hp
    q_spec = pl.BlockSpec((None, tq, hp * HEAD_DIM), lambda b, kv, g, i: (b, i, kv * ng + g))
    kv_spec = pl.BlockSpec((None, T, HEAD_DIM), lambda b, kv, g, i: (b, 0, kv))
    lse_spec = pl.BlockSpec((None, hp, 1, tq), lambda b, kv, g, i: (b, kv * ng + g, 0, i))
    in_specs = [q_spec, kv_spec, kv_spec, q_spec, q_spec, lse_spec]
    out_specs = [q_spec, kv_spec, kv_spec]
    out_shape = [jax.ShapeDtypeStruct(q.shape, F32), jax.ShapeDtypeStruct(k.shape, F32),
                 jax.ShapeDtypeStruct(v.shape, F32)]
    args = [q, k, v, do, o, lse]
    if has_sink:
        in_specs.append(pl.BlockSpec((hp, 1, LANE), lambda b, kv, g, i: (kv * ng + g, 0, 0)))
        args.append(sink)
        out_specs.append(pl.BlockSpec((None, hp, 1, LANE), lambda b, kv, g, i: (b, kv * ng + g, 0, 0)))
        out_shape.append(jax.ShapeDtypeStruct((B, H, 1, LANE), F32))
    res = _call_with_side(
        body, side, name="attn_band_bwd" if band else "attn_dense_bwd", grid=(B, KVH, ng, T // tq),
        in_specs=in_specs, args=args, out_specs=out_specs, out_shape=out_shape,
        scratch_shapes=[] if band else [pltpu.VMEM((HEAD_DIM, T), BF16)],
        dims=("parallel", "parallel", "arbitrary", "arbitrary"))
    return (res[0], res[1], res[2], res[3] if has_sink else None, res[-1] if side is not None else None)


def _prep_f(norm):
    def f(q, k, v, *rest):
        if norm:
            qg, kg, cos, sin = rest
        else:
            (cos, sin), qg, kg = rest, None, None

        def heads(x, g):
            outs = []
            for h in range(x.shape[1] // HEAD_DIM):
                xh = x[:, h * HEAD_DIM:(h + 1) * HEAD_DIM]
                if g is not None:
                    xh = xh * lax.rsqrt(jnp.mean(xh * xh, axis=-1, keepdims=True) + EPS) * g
                outs.append(xh * cos + _swap(xh) * sin)
            return jnp.concatenate(outs, axis=1) if len(outs) > 1 else outs[0]

        return heads(q, qg), heads(k, kg), v

    return f


def _attn_branch(S, tm, band, norm, has_sink, carries=False):
    f = _prep_f(norm)
    name = "band" if band else "dense"

    def prep(q, k, v, gains, tabs):
        outs = [(q.shape[2], BF16), (k.shape[2], BF16), (v.shape[2], BF16)]
        return _row_fwd(f, "prep_" + name + "_fwd", [q, k, v], [], list(gains), list(tabs), outs, tm, S)

    def unpack(args):
        q, k, v = args[:3]
        rest = list(args[3:])
        gains = [rest.pop(0), rest.pop(0)] if norm else []
        sink = rest.pop(0) if has_sink else None
        pack = rest.pop(0) if carries else None
        return q, k, v, gains, sink, pack, rest

    def sink_lanes(sink):
        return None if sink is None else jnp.broadcast_to(sink[:, None, None], (sink.shape[0], 1, LANE))

    def run_fwd(args):
        q, k, v, gains, sink, pack, tabs = unpack(args)
        qp, kp, vp = prep(q, k, v, gains, tabs)
        side = (pack.astype(BF16), True) if carries else None
        res = _attn_fwd(qp, kp, vp, sink_lanes(sink), S, band, side)
        return ((res[0], res[2]) if carries else res[0]), (args, qp, kp, vp, res[0], res[1])

    @jax.custom_vjp
    def op(*args):
        return run_fwd(args)[0]

    def fwd(*args):
        return run_fwd(args)

    def bwd(res, ct):
        args, qp, kp, vp, o, lse = res
        q, k, v, gains, sink, pack, tabs = unpack(args)
        do, side = (ct[0], (ct[1], False)) if carries else (ct, None)
        dqp, dkp, dvp, dsk, recv = _attn_bwd(qp, kp, vp, sink_lanes(sink), do, o, lse, S, band, side)
        dxs, _, dgains = _row_bwd(f, "prep_" + name + "_bwd", [q, k, v], [], list(gains), list(tabs),
                                  [dqp, dkp, dvp], tm, S)
        out = list(dxs) + list(dgains)
        if has_sink:
            out.append(jnp.sum(dsk[:, :, 0, 0], axis=0))
        if carries:
            out.append(_sum_slots(recv, "sum_grads"))
        return (*out, *[jnp.zeros_like(t) for t in tabs])

    op.defvjp(fwd, bwd)
    return op


def _final_loss(S, tm):
    def run(X, target, g):
        B, T, D = X.shape
        n_lat_tiles = S // tm

        def lossf(x, gg, tgt):
            y = x * lax.rsqrt(jnp.mean(x * x, axis=-1, keepdims=True) + EPS) * gg
            err = y - tgt
            return 0.5 * jnp.sum(jnp.sum(err * err, axis=-1, keepdims=True), axis=0, keepdims=True) / D

        def body(x_ref, t_ref, g_ref, loss_ref, dx_ref, dg_ref):
            b, i = pl.program_id(0), pl.program_id(1)
            tgt = t_ref[...]
            val, vjp = jax.vjp(lambda x, gg: lossf(x, gg, tgt), x_ref[...], g_ref[...])
            dx, dg = vjp(jnp.ones((1, 1), F32))
            lat = (i < n_lat_tiles).astype(F32)
            dx_ref[...] = dx * lat

            @pl.when(i == 0)
            def _():
                loss_ref[...] = jnp.zeros_like(loss_ref)

            @pl.when(jnp.logical_and(b == 0, i == 0))
            def _():
                dg_ref[...] = jnp.zeros_like(dg_ref)

            loss_ref[...] += jnp.broadcast_to(val * lat, loss_ref.shape)
            dg_ref[...] += dg * lat

        x_spec = pl.BlockSpec((None, tm, D), lambda b, i: (b, i, 0))
        t_spec = pl.BlockSpec((None, tm, D), lambda b, i: (b, jnp.minimum(i, n_lat_tiles - 1), 0))
        g_spec = pl.BlockSpec((1, D), lambda b, i: (0, 0))
        loss, dx, dg = _call(
            body, name="final_loss", grid=(B, T // tm), in_specs=[x_spec, t_spec, g_spec],
            out_specs=[pl.BlockSpec((None, 1, LANE), lambda b, i: (b, 0, 0)), x_spec, g_spec],
            out_shape=[jax.ShapeDtypeStruct((B, 1, LANE), F32), jax.ShapeDtypeStruct(X.shape, F32),
                       jax.ShapeDtypeStruct(g.shape, F32)],
            compiler_params=_params(("arbitrary", "arbitrary")))(X, target, g)
        return jnp.sum(loss[:, 0, 0]), dx, dg

    @jax.custom_vjp
    def op(X, target, g):
        return run(X, target, g)[0]

    def fwd(X, target, g):
        loss, dx, dg = run(X, target, g)
        return loss, (dx, dg, target)

    def bwd(res, ct):
        dx, dg, target = res
        return ct * dx, jnp.zeros_like(target), ct * dg

    op.defvjp(fwd, bwd)
    return op


def _prenorm_f(D):
    def f(x, mp, g):
        y = x * lax.rsqrt(jnp.mean(x * x, axis=-1, keepdims=True) + EPS) * g
        return (y * (1.0 + mp[:, D:2 * D]) + mp[:, 0:D],)
    return f


def _resid_f(D):
    def f(x, y, mp):
        return (x + mp[:, 2 * D:3 * D] * y,)
    return f


def _gate2_f(hf, hr, g):
    return ((hf + hr) * (g * jax.nn.sigmoid(g)),)


def _gate1_f(y, g):
    return (y * (g * jax.nn.sigmoid(g)),)


def _merge_f(D):
    def f(m3, pa, pb, pc):
        return (jax.nn.sigmoid(m3[:, 0:D]) * pa + jax.nn.sigmoid(m3[:, D:2 * D]) * pb
                + jax.nn.sigmoid(m3[:, 2 * D:3 * D]) * pc,)
    return f


def _coef_f(D):
    nblk = D // GATE_BLOCK

    def f(v, pv, wm):
        vb = v.astype(BF16)

        def gate(k):
            cols = []
            for j in range(nblk):
                r0 = (k * nblk + j) * GATE_BLOCK
                cols.append(jnp.dot(vb[:, j * GATE_BLOCK:(j + 1) * GATE_BLOCK],
                                    wm[r0:r0 + GATE_BLOCK, :].astype(BF16), preferred_element_type=F32))
            return jnp.concatenate(cols, axis=1)

        outs = []
        for d in range(2):
            r = jax.nn.sigmoid(gate(d) + pv[d:d + 1])
            i = jax.nn.sigmoid(gate(2 + d) + pv[2 + d:3 + d])
            la = r * pv[4 + d:5 + d]
            t = jnp.tanh(la)
            outs += [jnp.exp(la), jnp.sqrt(-2.0 * t / (1.0 - t)) * (i * v)]
        return tuple(outs)

    return f


def _gate_blocks(w):
    per = GATE_BLOCK // LRU_BLOCK_W
    n = w.shape[1]
    w5 = w.reshape(2, n // per, per, LRU_BLOCK_W, LRU_BLOCK_W)
    dense = jnp.einsum("djiab,ik->djiakb", w5, jnp.eye(per, dtype=w.dtype))
    return dense.reshape(2, (n // per) * GATE_BLOCK, GATE_BLOCK)


def _exchange_shape(x, gather):
    return jax.ShapeDtypeStruct((N_DEV,) + x.shape if gather else x.shape, x.dtype)


EXCHANGE_SEMS = [pltpu.SemaphoreType.DMA((N_DEV - 1,)), pltpu.SemaphoreType.DMA((N_DEV - 1,)),
                 pltpu.SemaphoreType.DMA(())]


def _exchange_ops(x_ref, o_ref, send_sems, recv_sems, local_sem, gather):
    mx, my, mc = lax.axis_index("x"), lax.axis_index("y"), lax.axis_index("c")
    me = 4 * mx + 2 * my + mc

    def src(p):
        return x_ref if gather else x_ref.at[p]

    local = pltpu.make_async_copy(src(me), o_ref.at[me], local_sem)
    sends, recvs = [], []
    for k in range(1, N_DEV):
        px = 1 - mx if k & 4 else mx
        py = 1 - my if k & 2 else my
        pc = 1 - mc if k & 1 else mc
        peer = 4 * px + 2 * py + pc
        sends.append(pltpu.make_async_remote_copy(
            src_ref=src(peer), dst_ref=o_ref.at[me], send_sem=send_sems.at[k - 1],
            recv_sem=recv_sems.at[k - 1], device_id=(px, py, pc), device_id_type=pl.DeviceIdType.MESH))
        recvs.append(pltpu.make_async_remote_copy(
            src_ref=src(peer), dst_ref=o_ref.at[peer], send_sem=send_sems.at[k - 1],
            recv_sem=recv_sems.at[k - 1], device_id=(px, py, pc), device_id_type=pl.DeviceIdType.MESH))

    def start():
        local.start()
        for cp in sends:
            cp.start()

    def finish():
        for cp in recvs:
            cp.wait_recv()
        for cp in sends:
            cp.wait_send()
        local.wait()

    return start, finish


def _exchange(x, name, gather):
    def body(x_ref, o_ref, send_sems, recv_sems, local_sem):
        start, finish = _exchange_ops(x_ref, o_ref, send_sems, recv_sems, local_sem, gather)
        start()
        finish()

    hbm = pl.BlockSpec(memory_space=pltpu.HBM)
    return _call(body, name=name, in_specs=[hbm], out_specs=hbm, out_shape=_exchange_shape(x, gather),
                 scratch_shapes=EXCHANGE_SEMS)(x)


def _call_with_side(body, side, *, name, grid, in_specs, args, out_specs, out_shape, scratch_shapes, dims):
    if side is None:
        return _call(body, name=name, grid=grid, in_specs=in_specs, out_specs=out_specs, out_shape=out_shape,
                     scratch_shapes=scratch_shapes, compiler_params=_params(dims))(*args)
    x, gather = side
    n_in, n_out, n_scr = len(in_specs), len(out_specs), len(scratch_shapes)

    def wrapped(*refs):
        ins, x_ref = refs[:n_in], refs[n_in]
        outs, o_ref = refs[n_in + 1:n_in + 1 + n_out], refs[n_in + 1 + n_out]
        scr = refs[n_in + 2 + n_out:n_in + 2 + n_out + n_scr]
        start, finish = _exchange_ops(x_ref, o_ref, *refs[n_in + 2 + n_out + n_scr:], gather)
        ids = [pl.program_id(a) for a in range(len(grid))]
        first = functools.reduce(jnp.logical_and, [i == 0 for i in ids])
        last = functools.reduce(jnp.logical_and, [i == g - 1 for i, g in zip(ids, grid)])
        pl.when(first)(start)
        body(*ins, *outs, *scr)
        pl.when(last)(finish)

    hbm = pl.BlockSpec(memory_space=pltpu.HBM)
    return _call(wrapped, name=name + "_xchg", grid=grid, in_specs=list(in_specs) + [hbm],
                 out_specs=list(out_specs) + [hbm], out_shape=list(out_shape) + [_exchange_shape(x, gather)],
                 scratch_shapes=list(scratch_shapes) + EXCHANGE_SEMS,
                 compiler_params=_params(("arbitrary",) * len(grid)))(*args, x)


def _adamw(gs, w, m, v, name):
    n, R, C = gs.shape
    tr = _pick(R, 256, 16)

    def body(g_ref, w_ref, m_ref, v_ref, go_ref, d_ref, mo_ref, vo_ref):
        g = g_ref[0].astype(F32)
        for p in range(1, n):
            g = g + g_ref[p].astype(F32)
        m2 = ADAM_B1 * m_ref[...] + (1.0 - ADAM_B1) * g
        v2 = ADAM_B2 * v_ref[...] + (1.0 - ADAM_B2) * (g * g)
        m_hat = m2 / (1.0 - ADAM_B1 ** ADAM_STEP)
        v_hat = v2 / (1.0 - ADAM_B2 ** ADAM_STEP)
        go_ref[...] = g
        d_ref[...] = -ADAM_LR * (m_hat / (jnp.sqrt(v_hat) + ADAM_EPS) + ADAM_WD * w_ref[...])
        mo_ref[...] = m2
        vo_ref[...] = v2

    spec = pl.BlockSpec((tr, C), lambda i: (i, 0))
    return _call(body, name=name, grid=(R // tr,),
                 in_specs=[pl.BlockSpec((n, tr, C), lambda i: (0, i, 0)), spec, spec, spec],
                 out_specs=[spec] * 4, out_shape=[jax.ShapeDtypeStruct((R, C), F32)] * 4,
                 compiler_params=_params(("parallel",)))(gs, w, m, v)


def _sum_slots(gs, name):
    n, R, C = gs.shape
    tr = _pick(R, 256, 16)

    def body(g_ref, o_ref):
        g = g_ref[0].astype(F32)
        for p in range(1, n):
            g = g + g_ref[p].astype(F32)
        o_ref[...] = g

    return _call(body, name=name, grid=(R // tr,), in_specs=[pl.BlockSpec((n, tr, C), lambda i: (0, i, 0))],
                 out_specs=pl.BlockSpec((tr, C), lambda i: (i, 0)), out_shape=jax.ShapeDtypeStruct((R, C), F32),
                 compiler_params=_params(("parallel",)))(gs)


def _gather_op(name):
    @jax.custom_vjp
    def op(pack):
        return _exchange(pack.astype(BF16), name, True)

    def fwd(pack):
        return op(pack), None

    def bwd(_, ct):
        return (_sum_slots(_exchange(ct, name + "_transpose", False), "sum_grads"),)

    op.defvjp(fwd, bwd)
    return op


def _rope_tables(S, L):
    P = HEAD_DIM // 4
    rows = S // GRID_W
    row_id = jnp.repeat(jnp.arange(rows), GRID_W)
    col_id = jnp.tile(jnp.arange(GRID_W), rows)
    inv = ROPE_THETA ** (-jnp.arange(P, dtype=F32) / P)
    ar, ac = row_id[:, None] * inv, col_id[:, None] * inv
    cos = jnp.concatenate([jnp.cos(ar), jnp.cos(ar), jnp.cos(ac), jnp.cos(ac)], axis=1)
    sin = jnp.concatenate([-jnp.sin(ar), jnp.sin(ar), -jnp.sin(ac), jnp.sin(ac)], axis=1)
    cos = jnp.concatenate([cos, jnp.ones((L, HEAD_DIM), F32)], axis=0)
    sin = jnp.concatenate([sin, jnp.zeros((L, HEAD_DIM), F32)], axis=0)
    return cos, sin


IN_TILES = (((1088, 2560, 1024), (1088, 1024, 2560), (1024, 1024, 2176)),) * DEPTH
SQ_TILES = (((2176, 1024, 1024), (2176, 1024, 1024), (1024, 1024, 1088)),) * DEPTH


def _layer_shard_shapes(D, IN):
    return (("w_in", (D, IN // N_DEV), 1), ("w_mod", (D, 3 * D // N_DEV), 1),
            ("w_branch", (3, D // N_DEV, D), 1), ("w_out", (D // N_DEV, D), 0))


def _pack_layer(w, l, D):
    return jnp.concatenate([w[n][l].reshape(-1) for n in BIG_PACK]).reshape(-1, D)


def _unpack_shards(pack, D, IN):
    flat, out, off = pack.reshape(-1), {}, 0
    for n, shape, _ in _layer_shard_shapes(D, IN):
        sz = shape[0] * shape[1] * (shape[2] if len(shape) > 2 else 1)
        out[n] = flat[off:off + sz].reshape(shape)
        off += sz
    return out


def _unpack_gathered(g, D, IN):
    flat, out, off = g.reshape(N_DEV, -1), {}, 0
    for n, shape, axis in _layer_shard_shapes(D, IN):
        sz = shape[0] * shape[1] * (shape[2] if len(shape) > 2 else 1)
        out[n] = _unshard(flat[:, off:off + sz].reshape((N_DEV,) + shape), axis)
        off += sz
    return out


def _loss_fn(packs, p, x, c, ctx, target):
    B, S, D = x.shape
    L = ctx.shape[1]
    T = S + L
    tm = min(L, 256)
    KVW = D // KV_GROUP
    widths = (D, D, D, KVW, KVW, D, D, KVW, KVW, D, 3 * D)
    offs = [0]
    for w in widths:
        offs.append(offs[-1] + w)
    cos, sin = _rope_tables(S, L)

    X = jnp.concatenate([x, ctx], axis=1)
    sc, scc = jax.nn.silu(c), jax.nn.silu(p["c_ctx"])
    A = jnp.concatenate([scc[None], sc, jnp.zeros((SUBLANE - 1 - B, D), F32)], axis=0)
    gathered = _gather_op("gather_layer0")(packs[0])
    for l in range(DEPTH):
        big = _unpack_gathered(gathered, D, offs[-1])
        mod = _matmul("mm_mod")(A, big["w_mod"]) + p["b_mod"][l]
        modp = jnp.stack([jnp.broadcast_to(mod[0], (B, 3 * D)), mod[1:1 + B]], axis=1)[:, :, None, :]
        (h,) = _rowwise(_prenorm_f(D), "prenorm", [(D, BF16)], tm, S, 1, 1, 1)(X, modp, p["norm_g"][l][None])
        proj = _matmul("mm_in", *IN_TILES[l])(h.reshape(B * T, D), big["w_in"]).reshape(B, T, offs[-1])
        uA, gA, qB, kB, vB, gB, qC, kC, vC, gC, m3 = [proj[:, :, offs[i]:offs[i + 1]] for i in range(len(widths))]
        u = _conv_op(S)(uA, p["conv_w"][l], p["conv_b"][l][None])
        nsp = -LRU_C * jax.nn.softplus(-p["lru_lambda"][l])
        pv = jnp.concatenate([p["lru_ba"][l], p["lru_bx"][l], nsp, jnp.zeros((2, D), F32)], axis=0)
        wm = jnp.concatenate([_gate_blocks(p["lru_wa"][l]), _gate_blocks(p["lru_wx"][l])], axis=0)
        wm = wm.reshape(-1, GATE_BLOCK)
        af, bf, ar, br = _rowwise(_coef_f(D), "lru_coef", [(D, F32)] * 4, tm, S, 1, 0, 2)(u, pv, wm)
        hf = _scan_op(S, False)(af, bf)
        hr = _scan_op(S, True)(ar, br)
        (zA,) = _rowwise(_gate2_f, "gate_a", [(D, BF16)], tm, S, 3, 0, 0)(hf, hr, gA)
        yB = _attn_branch(S, tm, True, False, True)(qB, kB, vB, p["attn_sink"][l], cos, sin)
        (zB,) = _rowwise(_gate1_f, "gate_b", [(D, BF16)], tm, S, 2, 0, 0)(yB, gB)
        qkv_c = (qC, kC, vC, p["q_norm_g"][l][None], p["k_norm_g"][l][None])
        if l + 1 < DEPTH:
            yC, gathered = _attn_branch(S, tm, False, True, False, True)(*qkv_c, packs[l + 1], cos, sin)
        else:
            yC = _attn_branch(S, tm, False, True, False)(*qkv_c, cos, sin)
        (zC,) = _rowwise(_gate1_f, "gate_c", [(D, BF16)], tm, S, 2, 0, 0)(yC, gC)
        pr = [_matmul("mm_branch", *SQ_TILES[l])(z.reshape(B * T, D), big["w_branch"][n]).reshape(B, T, D)
              for n, z in enumerate((zA, zB, zC))]
        (mg,) = _rowwise(_merge_f(D), "merge", [(D, BF16)], tm, S, 4, 0, 0)(m3, *pr)
        y = _matmul("mm_out", *SQ_TILES[l])(mg.reshape(B * T, D), big["w_out"]).reshape(B, T, D)
        (X,) = _rowwise(_resid_f(D), "resid", [(D, F32)], tm, S, 2, 1, 0)(X, y, modp)
    return _final_loss(S, tm)(X, target, p["final_g"][None])


def _shard_axis(name):
    return {"w_mod": 2, "w_in": 2, "conv_w": 2, "lru_ba": 2, "lru_bx": 2, "lru_lambda": 2,
            "w_branch": 2, "w_out": 1}.get(name)


def _unshard(g, axis):
    full = jnp.moveaxis(g, 0, axis)
    shape = list(full.shape)
    shape[axis:axis + 2] = [shape[axis] * shape[axis + 1]]
    return full.reshape(shape)


def _reshard(full, axis):
    shape = list(full.shape)
    shape[axis:axis + 1] = [N_DEV, shape[axis] // N_DEV]
    return jnp.moveaxis(full.reshape(shape), axis, 0)


def _pad_to(v, n):
    return jnp.concatenate([v, jnp.zeros((n - v.shape[0],), v.dtype)]) if n > v.shape[0] else v


def _step(x, c, ctx, target, w, m, v):
    D = x.shape[2]
    IN = w["w_in"].shape[2] * N_DEV
    full = {n: w[n] for n in REPLICATED}
    small_local = jnp.concatenate([w[n].reshape(-1) for n in SMALL_SHARDED])
    small_all = _exchange(small_local.reshape(-1, LANE), "gather_small", True).reshape(N_DEV, -1)
    off = 0
    for n in SMALL_SHARDED:
        sz = w[n].size
        full[n] = _unshard(small_all[:, off:off + sz].reshape((N_DEV,) + w[n].shape), _shard_axis(n))
        off += sz
    packs = [_pack_layer(w, l, D) for l in range(DEPTH)]

    loss, (gpacks, gp, gx) = jax.value_and_grad(_loss_fn, argnums=(0, 1, 2))(packs, full, x, c, ctx, target)
    loss = lax.psum(loss, AXES)

    out = {}
    gshards = [_unpack_shards(g, D, IN) for g in gpacks]
    for n in BIG_PACK:
        g = jnp.stack([gs[n] for gs in gshards])
        C = w[n].shape[-1]
        res = _adamw(g.reshape(1, -1, C), w[n].reshape(-1, C), m[n].reshape(-1, C), v[n].reshape(-1, C),
                     "adamw_" + n)
        out[n] = [r.reshape(w[n].shape) for r in res]

    rep = jnp.concatenate([gp[n].reshape(-1) for n in REPLICATED])
    n_rep = rep.shape[0]
    chunk = -(-n_rep // (N_DEV * LANE)) * LANE
    rep = _pad_to(rep, N_DEV * chunk).reshape(N_DEV, chunk)
    shards = jnp.concatenate([_reshard(gp[n], _shard_axis(n)).reshape(N_DEV, -1) for n in SMALL_SHARDED], axis=1)
    n_sh = shards.shape[1]
    recv = _exchange(jnp.concatenate([rep, shards], axis=1).reshape(N_DEV, -1, LANE), "scatter_small", False)
    wl =jnp.concatenate([w[n].reshape(-1) for n in SMALL_SHARDED])
    ml = jnp.concatenate([m[n].reshape(-1) for n in SMALL_SHARDED])
    vl = jnp.concatenate([v[n].reshape(-1) for n in SMALL_SHARDED])
    rows = recv.shape[1]
    rrows = chunk // LANE
    g_sh, d_sh, m_sh, v_sh = _adamw(recv[:, rrows:], wl.reshape(-1, LANE), ml.reshape(-1, LANE),
                                    vl.reshape(-1, LANE), "adamw_small_sharded")
    g_rep8 = _sum_slots(recv[:, :rrows], "sum_replicated")
    g_rep = _exchange(g_rep8, "gather_replicated", True).reshape(-1)
    wr = _pad_to(jnp.concatenate([w[n].reshape(-1) for n in REPLICATED]), N_DEV * chunk)
    mr = _pad_to(jnp.concatenate([m[n].reshape(-1) for n in REPLICATED]), N_DEV * chunk)
    vr = _pad_to(jnp.concatenate([v[n].reshape(-1) for n in REPLICATED]), N_DEV * chunk)
    res_rep = _adamw(g_rep.reshape(1, -1, LANE), wr.reshape(-1, LANE), mr.reshape(-1, LANE), vr.reshape(-1, LANE),
                     "adamw_replicated")
    off = 0
    for n in REPLICATED:
        sz = w[n].size
        out[n] = [r.reshape(-1)[off:off + sz].reshape(w[n].shape) for r in res_rep]
        off += sz
    off = 0
    for n in SMALL_SHARDED:
        sz = w[n].size
        out[n] = [r.reshape(-1)[off:off + sz].reshape(w[n].shape) for r in (g_sh, d_sh, m_sh, v_sh)]
        off += sz
    assert off == n_sh and rows == rrows + n_sh // LANE
    return (loss, gx, *[out[n][0] for n in WEIGHTS], *[out[n][1] for n in WEIGHTS],
            *[out[n][2] for n in WEIGHTS], *[out[n][3] for n in WEIGHTS])


def kernel(x, c, ctx, c_ctx, norm_g, w_mod, b_mod, w_in, conv_w, conv_b, lru_wa, lru_ba, lru_wx, lru_bx, lru_lambda, attn_sink, q_norm_g, k_norm_g, w_branch, w_out, final_g, loss_target, m_c_ctx, m_norm_g, m_w_mod, m_b_mod, m_w_in, m_conv_w, m_conv_b, m_lru_wa, m_lru_ba, m_lru_wx, m_lru_bx, m_lru_lambda, m_attn_sink, m_q_norm_g, m_k_norm_g, m_w_branch, m_w_out, m_final_g, v_c_ctx, v_norm_g, v_w_mod, v_b_mod, v_w_in, v_conv_w, v_conv_b, v_lru_wa, v_lru_ba, v_lru_wx, v_lru_bx, v_lru_lambda, v_attn_sink, v_q_norm_g, v_k_norm_g, v_w_branch, v_w_out, v_final_g):
    w = dict(zip(WEIGHTS, (c_ctx, norm_g, w_mod, b_mod, w_in, conv_w, conv_b, lru_wa, lru_ba, lru_wx, lru_bx,
                           lru_lambda, attn_sink, q_norm_g, k_norm_g, w_branch, w_out, final_g)))
    m = dict(zip(WEIGHTS, (m_c_ctx, m_norm_g, m_w_mod, m_b_mod, m_w_in, m_conv_w, m_conv_b, m_lru_wa, m_lru_ba,
                           m_lru_wx, m_lru_bx, m_lru_lambda, m_attn_sink, m_q_norm_g, m_k_norm_g, m_w_branch,
                           m_w_out, m_final_g)))
    v = dict(zip(WEIGHTS, (v_c_ctx, v_norm_g, v_w_mod, v_b_mod, v_w_in, v_conv_w, v_conv_b, v_lru_wa, v_lru_ba,
                           v_lru_wx, v_lru_bx, v_lru_lambda, v_attn_sink, v_q_norm_g, v_k_norm_g, v_w_branch,
                           v_w_out, v_final_g)))
    return _step(x, c, ctx, loss_target, w, m, v)
```

```python
import functools
import math

import jax
import jax.numpy as jnp
from jax import lax
from jax.experimental import pallas as pl
from jax.experimental.pallas import tpu as pltpu

F32 = jnp.float32
BF16 = jnp.bfloat16

AXES = ("x", "y", "c")
N_DEV = 8
DEPTH = 4
HEAD_DIM = 128
GRID_W = 64
WINDOW = 128
LRU_BLOCK_W = 64
GATE_BLOCK = 256
LRU_C = 8.0
ROPE_THETA = 10000.0
EPS = 1e-6
NEG_INF = -1e30
KV_GROUP = 4
LANE = 128
SUBLANE = 8
VMEM_LIMIT = 56 * 1024 * 1024

ADAM_LR = 0.001
ADAM_B1 = 0.9
ADAM_B2 = 0.999
ADAM_EPS = 1e-08
ADAM_WD = 0.01
ADAM_STEP = 10

WEIGHTS = ("c_ctx", "norm_g", "w_mod", "b_mod", "w_in", "conv_w", "conv_b", "lru_wa", "lru_ba", "lru_wx",
           "lru_bx", "lru_lambda", "attn_sink", "q_norm_g", "k_norm_g", "w_branch", "w_out", "final_g")
BIG_PACK = ("w_in", "w_mod", "w_branch", "w_out")
SMALL_SHARDED = ("conv_w", "lru_ba", "lru_bx", "lru_lambda")
REPLICATED = ("c_ctx", "norm_g", "b_mod", "conv_b", "lru_wa", "lru_wx", "attn_sink", "q_norm_g", "k_norm_g",
              "final_g")


def _call(body, **kw):
    return pl.pallas_call(body, **kw)


def _params(dims=None, vmem=VMEM_LIMIT):
    return pltpu.CompilerParams(dimension_semantics=dims, vmem_limit_bytes=vmem)


def _pick(n, target, mult):
    for t in range(min(n, target), 0, -1):
        if n % t == 0 and t % mult == 0:
            return t
    return n


def _mm(a, b, *, name, ta=False, tb=False, out_dtype=F32, tm=512, tn=1024, tk=1024):
    M, K = (a.shape[1], a.shape[0]) if ta else a.shape
    N = b.shape[0] if tb else b.shape[1]
    assert (b.shape[1] if tb else b.shape[0]) == K
    tm = _pick(M, tm, LANE if ta else 16)
    tn = _pick(N, tn, LANE)
    tk = _pick(K, tk, 16 if ta and not tb else LANE)
    nk = K // tk
    dn = (((0 if ta else 1,), (1 if tb else 0,)), ((), ()))

    def body(a_ref, b_ref, o_ref, *acc):
        r = lax.dot_general(a_ref[...].astype(BF16), b_ref[...].astype(BF16), dn,
                            preferred_element_type=F32)
        if nk == 1:
            o_ref[...] = r.astype(o_ref.dtype)
        else:
            k = pl.program_id(2)

            @pl.when(k == 0)
            def _():
                acc[0][...] = r

            @pl.when(k > 0)
            def _():
                acc[0][...] += r

            @pl.when(k == nk - 1)
            def _():
                o_ref[...] = acc[0][...].astype(o_ref.dtype)

    a_spec = (pl.BlockSpec((tk, tm), lambda i, j, k: (k, i)) if ta
              else pl.BlockSpec((tm, tk), lambda i, j, k: (i, k)))
    b_spec = (pl.BlockSpec((tn, tk), lambda i, j, k: (j, k)) if tb
              else pl.BlockSpec((tk, tn), lambda i, j, k: (k, j)))
    return _call(
        body, name=name, grid=(M // tm, N // tn, nk),
        in_specs=[a_spec, b_spec],
        out_specs=pl.BlockSpec((tm, tn), lambda i, j, k: (i, j)),
        out_shape=jax.ShapeDtypeStruct((M, N), out_dtype),
        scratch_shapes=[pltpu.VMEM((tm, tn), F32)] if nk > 1 else [],
        compiler_params=_params(("parallel", "parallel", "arbitrary")),
    )(a, b)


def _matmul(name, t_fwd=(512, 1024, 1024), t_da=(512, 1024, 1024), t_dw=(1024, 1024, 512)):
    def tiles(t):
        return dict(tm=t[0], tn=t[1], tk=t[2])

    @jax.custom_vjp
    def f(a, w):
        return _mm(a, w, name=name + "_fwd", **tiles(t_fwd))

    def fwd(a, w):
        return f(a, w), (a, w)

    def bwd(res, g):
        a, w = res
        da = _mm(g, w, name=name + "_da", tb=True, out_dtype=a.dtype, **tiles(t_da))
        dw = _mm(a, g, name=name + "_dw", ta=True, out_dtype=w.dtype, **tiles(t_dw))
        return da, dw

    f.defvjp(fwd, bwd)
    return f


def _mm_parts_nt(parts, w, col0, init, out_dtype, *, name, tm=1088, tk=512):
    M = parts[0].shape[0]
    D = w.shape[0]
    tm = _pick(M, tm, 16)
    tk = math.gcd(tk, *[a.shape[1] for a in parts])
    starts, n = [], 0
    for a in parts:
        assert a.shape[1] % tk == 0
        starts.append(n)
        n += a.shape[1] // tk
    k0 = col0 // tk
    has_init = init is not None

    def body(*refs):
        a_refs, w_ref = refs[:len(parts)], refs[len(parts)]
        o_ref, acc = refs[-2], refs[-1]
        k = pl.program_id(1)

        @pl.when(k == 0)
        def _():
            acc[...] = refs[len(parts) + 1][...].astype(F32) if has_init else jnp.zeros_like(acc)

        for a_ref, s, a in zip(a_refs, starts, parts):
            @pl.when(jnp.logical_and(k >= s, k < s + a.shape[1] // tk))
            def _(a_ref=a_ref):
                acc[...] += lax.dot_general(a_ref[...].astype(BF16), w_ref[...], NT, preferred_element_type=F32)

        @pl.when(k == n - 1)
        def _():
            o_ref[...] = acc[...].astype(o_ref.dtype)

    in_specs = [pl.BlockSpec((tm, tk), lambda i, k, s=s, c=a.shape[1] // tk: (i, jnp.clip(k - s, 0, c - 1)))
                for a, s in zip(parts, starts)]
    in_specs.append(pl.BlockSpec((D, tk), lambda i, k: (0, k0 + k)))
    o_spec = pl.BlockSpec((tm, D), lambda i, k: (i, 0))
    args = list(parts) + [w]
    if has_init:
        in_specs.append(o_spec)
        args.append(init)
    return _call(body, name=name, grid=(M // tm, n), in_specs=in_specs, out_specs=o_spec,
                 out_shape=jax.ShapeDtypeStruct((M, D), out_dtype),
                 scratch_shapes=[pltpu.VMEM((tm, D), F32)],
                 compiler_params=_params(("parallel", "arbitrary")))(*args)


def _in_proj(widths, t_fwd=(2176, 1024, 1024), t_dw=(1024, 1024, 2176), group=3):
    offs = [0]
    for wd in widths:
        offs.append(offs[-1] + wd)

    def cols(w, i):
        return w[:, offs[i]:offs[i + 1]]

    @jax.custom_vjp
    def op(h, w):
        return tuple(_mm(h, cols(w, i), name="mm_in_fwd", tm=t_fwd[0], tn=t_fwd[1], tk=t_fwd[2])
                     for i in range(len(widths)))

    def fwd(h, w):
        return op(h, w), (h, w)

    def bwd(res, gs):
        h, w = res
        da = None
        for i0 in range(0, len(widths), group):
            last = i0 + group >= len(widths)
            da = _mm_parts_nt(list(gs[i0:i0 + group]), w, offs[i0], da, h.dtype if last else F32, name="mm_in_da")
        dw = jnp.concatenate([_mm(h, g, name="mm_in_dw", ta=True, out_dtype=w.dtype, tm=t_dw[0], tn=t_dw[1],
                                  tk=t_dw[2]) for g in gs], axis=1)
        return da, dw

    op.defvjp(fwd, bwd)
    return op


def _row_specs(xs, tps, gps, rts, tm, n_lat_tiles):
    x_specs = [pl.BlockSpec((None, tm, x.shape[2]), lambda b, i: (b, i, 0)) for x in xs]
    tp_specs = [pl.BlockSpec((None, None, 1, p.shape[3]),
                             lambda b, i: (b, (i < n_lat_tiles).astype(jnp.int32), 0, 0)) for p in tps]
    gp_specs = [pl.BlockSpec(p.shape, lambda b, i: (0, 0)) for p in gps]
    rt_specs = [pl.BlockSpec((tm, t.shape[1]), lambda b, i: (i, 0)) for t in rts]
    return x_specs, tp_specs, gp_specs, rt_specs


def _row_fwd(f, name, xs, tps, gps, rts, outs, tm, n_lat):
    B, T, _ = xs[0].shape
    n_in = len(xs) + len(tps) + len(gps) + len(rts)

    def body(*refs):
        vals = f(*[r[...] for r in refs[:n_in]])
        for o, v in zip(refs[n_in:], vals):
            o[...] = v.astype(o.dtype)

    x_specs, tp_specs, gp_specs, rt_specs = _row_specs(xs, tps, gps, rts, tm, n_lat // tm)
    res = _call(
        body, name=name, grid=(B, T // tm),
        in_specs=x_specs + tp_specs + gp_specs + rt_specs,
        out_specs=[pl.BlockSpec((None, tm, w), lambda b, i: (b, i, 0)) for w, _ in outs],
        out_shape=[jax.ShapeDtypeStruct((B, T, w), dt) for w, dt in outs],
        compiler_params=_params(("parallel", "parallel")),
    )(*xs, *tps, *gps, *rts)
    return list(res)


def _row_bwd(f, name, xs, tps, gps, rts, douts, tm, n_lat):
    B, T, _ = xs[0].shape
    nx, ntp, ngp, nd = len(xs), len(tps), len(gps), len(douts)
    n_lat_tiles = n_lat // tm
    n_diff = nx + ntp + ngp
    n_in = n_diff + len(rts)

    def body(*refs):
        diff = [r[...] for r in refs[:n_diff]]
        tabs = [r[...] for r in refs[n_diff:n_in]]
        dos = [r[...] for r in refs[n_in:n_in + nd]]
        o_refs = refs[n_in + nd:]
        prim, vjp = jax.vjp(lambda *d: tuple(f(*d, *tabs)), *diff)
        grads = vjp(tuple(d.astype(p.dtype) for d, p in zip(dos, prim)))
        b, i = pl.program_id(0), pl.program_id(1)
        for k in range(nx):
            o_refs[k][...] = grads[k].astype(o_refs[k].dtype)
        first_tp = jnp.logical_or(i == 0, i == n_lat_tiles)
        first_gp = jnp.logical_and(b == 0, i == 0)
        for k in range(nx, n_diff):
            first = first_tp if k < nx + ntp else first_gp

            @pl.when(first)
            def _(k=k):
                o_refs[k][...] = grads[k]

            @pl.when(jnp.logical_not(first))
            def _(k=k):
                o_refs[k][...] += grads[k]

    x_specs, tp_specs, gp_specs, rt_specs = _row_specs(xs, tps, gps, rts, tm, n_lat_tiles)
    d_specs = [pl.BlockSpec((None, tm, d.shape[2]), lambda b, i: (b, i, 0)) for d in douts]
    res = _call(
        body, name=name, grid=(B, T // tm),
        in_specs=x_specs + tp_specs + gp_specs + rt_specs + d_specs,
        out_specs=x_specs + tp_specs + gp_specs,
        out_shape=[jax.ShapeDtypeStruct(a.shape, F32) for a in (*xs, *tps, *gps)],
        compiler_params=_params(("arbitrary", "arbitrary")),
    )(*xs, *tps, *gps, *rts, *douts)
    res = list(res)
    return res[:nx], res[nx:nx + ntp], res[nx + ntp:]


def _rowwise(f, name, outs, tm, n_lat, n_x, n_tp, n_gp):
    def split(args):
        return (args[:n_x], args[n_x:n_x + n_tp], args[n_x + n_tp:n_x + n_tp + n_gp],
                args[n_x + n_tp + n_gp:])

    @jax.custom_vjp
    def op(*args):
        xs, tps, gps, rts = split(args)
        return tuple(_row_fwd(f, name + "_fwd", xs, tps, gps, rts, outs, tm, n_lat))

    def fwd(*args):
        return op(*args), args

    def bwd(args, g):
        xs, tps, gps, rts = split(args)
        dxs, dtps, dgps = _row_bwd(f, name + "_bwd", xs, tps, gps, rts, list(g), tm, n_lat)
        return (*dxs, *dtps, *dgps, *[jnp.zeros_like(t) for t in rts])

    op.defvjp(fwd, bwd)
    return op


def _shift_impl(u, k):
    n = u.shape[0]
    r = pltpu.roll(u, k % n, axis=0)
    row = lax.broadcasted_iota(jnp.int32, u.shape, 0)
    valid = (row >= k) if k > 0 else (row < n + k)
    return jnp.where(valid, r, 0.0)


@functools.partial(jax.custom_vjp, nondiff_argnums=(1,))
def _shift(u, k):
    return _shift_impl(u, k)


_shift.defvjp(lambda u, k: (_shift_impl(u, k), None), lambda k, _, g: (_shift_impl(g, -k),))


def _swap_impl(x):
    lane = lax.broadcasted_iota(jnp.int32, x.shape, 1)
    q = HEAD_DIM // 4
    return jnp.where((lane % (2 * q)) < q, pltpu.roll(x, HEAD_DIM - q, axis=1), pltpu.roll(x, q, axis=1))


@jax.custom_vjp
def _swap(x):
    return _swap_impl(x)


_swap.defvjp(lambda x: (_swap_impl(x), None), lambda _, g: (_swap_impl(g),))


def _conv_f(ul, uc, cw, cb):
    def conv(u):
        return (_shift(u, 2) * cw[0:1] + _shift(u, 1) * cw[1:2] + u * cw[2:3] + _shift(u, -1) * cw[3:4] + cb)
    return conv(ul), conv(uc)


def _conv_specs(B, T, D):
    u_spec = pl.BlockSpec((None, T, LANE), lambda j, b: (b, 0, j))
    cw_spec = pl.BlockSpec((4, LANE), lambda j, b: (0, j))
    cb_spec = pl.BlockSpec((1, LANE), lambda j, b: (0, j))
    return u_spec, cw_spec, cb_spec


def _conv_fwd(u, cw, cb, S):
    B, T, D = u.shape

    def body(u_ref, cw_ref, cb_ref, o_ref):
        vl, vc = _conv_f(u_ref[0:S, :], u_ref[S:T, :], cw_ref[...], cb_ref[...])
        o_ref[0:S, :] = vl
        o_ref[S:T, :] = vc

    u_spec, cw_spec, cb_spec = _conv_specs(B, T, D)
    return _call(body, name="conv_fwd", grid=(D // LANE, B), in_specs=[u_spec, cw_spec, cb_spec],
                 out_specs=u_spec, out_shape=jax.ShapeDtypeStruct(u.shape, F32),
                 compiler_params=_params(("parallel", "parallel")))(u, cw, cb)


def _conv_bwd(u, cw, cb, dv, S):
    B, T, D = u.shape

    def body(u_ref, cw_ref, cb_ref, dv_ref, du_ref, dcw_ref, dcb_ref):
        _, vjp = jax.vjp(_conv_f, u_ref[0:S, :], u_ref[S:T, :], cw_ref[...], cb_ref[...])
        dul, duc, dcw, dcb = vjp((dv_ref[0:S, :], dv_ref[S:T, :]))
        du_ref[0:S, :] = dul
        du_ref[S:T, :] = duc
        first = pl.program_id(1) == 0

        @pl.when(first)
        def _():
            dcw_ref[...] = dcw
            dcb_ref[...] = dcb

        @pl.when(jnp.logical_not(first))
        def _():
            dcw_ref[...] += dcw
            dcb_ref[...] += dcb

    u_spec, cw_spec, cb_spec = _conv_specs(B, T, D)
    return _call(body, name="conv_bwd", grid=(D // LANE, B), in_specs=[u_spec, cw_spec, cb_spec, u_spec],
                 out_specs=[u_spec, cw_spec, cb_spec],
                 out_shape=[jax.ShapeDtypeStruct(u.shape, F32), jax.ShapeDtypeStruct(cw.shape, F32),
                            jax.ShapeDtypeStruct(cb.shape, F32)],
                 compiler_params=_params(("parallel", "arbitrary")))(u, cw, cb, dv)


def _conv_op(S):
    @jax.custom_vjp
    def op(u, cw, cb):
        return _conv_fwd(u, cw, cb, S)

    def fwd(u, cw, cb):
        return op(u, cw, cb), (u, cw, cb)

    def bwd(res, g):
        u, cw, cb = res
        return tuple(_conv_bwd(u, cw, cb, g, S))

    op.defvjp(fwd, bwd)
    return op


SCAN_UNROLL = 4


def _group_scan(A, Bv, asc):
    row = lax.broadcasted_iota(jnp.int32, A.shape, 0)
    for s in (1, 2, 4):
        sh = s if asc else SUBLANE - s
        valid = (row >= s) if asc else (row < SUBLANE - s)
        A_sh = pltpu.roll(A, sh, axis=0)
        B_sh = pltpu.roll(Bv, sh, axis=0)
        Bv = jnp.where(valid, A * B_sh, 0.0) + Bv
        A = jnp.where(valid, A * A_sh, A)
    return A, Bv


def _chain_step(A, Bv, carry, asc):
    row = lax.broadcasted_iota(jnp.int32, A.shape, 0)
    A2, B2 = _group_scan(A, Bv, asc)
    h = A2 * carry + B2
    if asc:
        prev = jnp.where(row == 0, carry, pltpu.roll(h, 1, axis=0))
        return h, prev, h[SUBLANE - 1:SUBLANE, :]
    prev = jnp.where(row == SUBLANE - 1, carry, pltpu.roll(h, SUBLANE - 1, axis=0))
    return h, prev, h[0:1, :]


def _chain_loop(segments, step):
    carry = jnp.zeros((1, LANE), F32)
    for lo, hi, asc in segments:
        span = SUBLANE * SCAN_UNROLL
        assert (hi - lo) % span == 0

        def it(t, carry, lo=lo, hi=hi, asc=asc, span=span):
            base = lo + t * span if asc else hi - (t + 1) * span
            order = range(SCAN_UNROLL) if asc else reversed(range(SCAN_UNROLL))
            for j in order:
                carry = step(pl.multiple_of(base + SUBLANE * j, SUBLANE), carry, asc)
            return carry

        carry = lax.fori_loop(0, (hi - lo) // span, it, carry)
    return carry


def _scan_specs(T):
    return pl.BlockSpec((None, T, LANE), lambda j, b: (b, 0, j))


def _scan_fwd(a, b, S, reverse):
    B, T, D = a.shape
    asc = not reverse
    segments = [(S, T, asc), (0, S, asc)]

    def body(a_ref, b_ref, h_ref, hp_ref):
        def step(r0, carry, asc):
            rows = pl.ds(r0, SUBLANE)
            h, prev, carry = _chain_step(a_ref[rows, :], b_ref[rows, :], carry, asc)
            h_ref[rows, :] = h
            hp_ref[rows, :] = prev
            return carry
        _chain_loop(segments, step)

    spec = _scan_specs(T)
    return _call(body, name="scan_rev_fwd" if reverse else "scan_fwd_fwd", grid=(D // LANE, B),
                 in_specs=[spec, spec], out_specs=[spec, spec],
                 out_shape=[jax.ShapeDtypeStruct(a.shape, F32)] * 2,
                 compiler_params=_params(("parallel", "parallel")))(a, b)


def _scan_bwd(a, hp, dy, S, reverse):
    B, T, D = a.shape
    asc = reverse
    segments = [(0, S, asc), (S, T, asc)]

    def body(a_ref, hp_ref, dy_ref, da_ref, db_ref):
        def step(r0, carry, asc):
            rows = pl.ds(r0, SUBLANE)
            A, dy = a_ref[rows, :], dy_ref[rows, :]
            _, s_prev, carry = _chain_step(A, A * dy, carry, asc)
            g = dy + s_prev
            db_ref[rows, :] = g
            da_ref[rows, :] = g * hp_ref[rows, :]
            return carry
        _chain_loop(segments, step)

    spec = _scan_specs(T)
    return _call(body, name="scan_rev_bwd" if reverse else "scan_fwd_bwd", grid=(D // LANE, B),
                 in_specs=[spec, spec, spec], out_specs=[spec, spec],
                 out_shape=[jax.ShapeDtypeStruct(a.shape, F32)] * 2,
                 compiler_params=_params(("parallel", "parallel")))(a, hp, dy)


def _scan_op(S, reverse):
    @jax.custom_vjp
    def op(a, b):
        return _scan_fwd(a, b, S, reverse)[0]

    def fwd(a, b):
        h, hp = _scan_fwd(a, b, S, reverse)
        return h, (a, hp)

    def bwd(res, g):
        a, hp = res
        return tuple(_scan_bwd(a, hp, g, S, reverse))

    op.defvjp(fwd, bwd)
    return op


def _band_lo(qi, tq, S):
    span = tq + 2 * WINDOW
    return pl.multiple_of(jnp.clip(qi * tq - WINDOW, 0, S - span), LANE)


def _band_mask(qi, tq, lo, span, transposed):
    shape = (span, tq) if transposed else (tq, span)
    qpos = qi * tq + lax.broadcasted_iota(jnp.int32, shape, 1 if transposed else 0)
    kpos = lo + lax.broadcasted_iota(jnp.int32, shape, 0 if transposed else 1)
    return jnp.abs(kpos - qpos) <= WINDOW


NT = (((1,), (1,)), ((), ()))
LOG2E = 1.4426950408889634
KEY_CHUNK = 1024


def _col_to_row(c):
    return jnp.broadcast_to(c, (c.shape[0], LANE)).T[0:1, :]


def _attn_fwd(q, k, v, sink, S, band, side=None):
    B, T, HD = q.shape
    H = HD // HEAD_DIM
    L = T - S
    tq = L
    n_lq = S // tq
    span = tq + 2 * WINDOW
    c2 = HEAD_DIM ** -0.5 * LOG2E
    has_sink = sink is not None
    kc = min(KEY_CHUNK, S)
    hp = KV_GROUP if band else 1

    def body(*refs):
        if has_sink:
            q_ref, k_ref, v_ref, s_ref, o_ref, lse_ref = refs
        else:
            q_ref, k_ref, v_ref, o_ref, lse_ref = refs
        qi = pl.program_id(2)

        def online(qv, sk2, segs):
            def scores(seg):
                s = lax.dot_general(qv, k_ref[seg[0], :], NT, preferred_element_type=F32) * c2
                return s if seg[1] is None else jnp.where(seg[1], s, NEG_INF)

            m = l = acc = None
            s_next = scores(segs[0])
            for j, (rows, _) in enumerate(segs):
                s = s_next
                if j + 1 < len(segs):
                    s_next = scores(segs[j + 1])
                ms = jnp.max(s, axis=-1, keepdims=True)
                if m is None:
                    m = ms if sk2 is None else jnp.maximum(ms, sk2)
                    p = jnp.exp2(s - m)
                    l = jnp.sum(p, axis=-1, keepdims=True)
                    if sk2 is not None:
                        l = l + jnp.exp2(sk2 - m)
                    acc = jnp.dot(p.astype(BF16), v_ref[rows, :], preferred_element_type=F32)
                else:
                    m_new = jnp.maximum(m, ms)
                    alpha = jnp.exp2(m - m_new)
                    p = jnp.exp2(s - m_new)
                    l = alpha * l + jnp.sum(p, axis=-1, keepdims=True)
                    acc = alpha * acc + jnp.dot(p.astype(BF16), v_ref[rows, :], preferred_element_type=F32)
                    m = m_new
            return acc, m, l

        def joint(qv, sk2, segs):
            ss = []
            for rows, mask in segs:
                s = lax.dot_general(qv, k_ref[rows, :], NT, preferred_element_type=F32) * c2
                ss.append(s if mask is None else jnp.where(mask, s, NEG_INF))
            m = functools.reduce(jnp.maximum, [jnp.max(s, axis=-1, keepdims=True) for s in ss])
            if sk2 is not None:
                m = jnp.maximum(m, sk2)
            ps = [jnp.exp2(s - m) for s in ss]
            l = functools.reduce(jnp.add, [jnp.sum(p, axis=-1, keepdims=True) for p in ps])
            if sk2 is not None:
                l = l + jnp.exp2(sk2 - m)
            acc = functools.reduce(jnp.add, [jnp.dot(p.astype(BF16), v_ref[rows, :], preferred_element_type=F32)
                                             for p, (rows, _) in zip(ps, segs)])
            return acc, m, l

        def run(segs, softmax):
            for g in range(hp):
                lanes = slice(g * HEAD_DIM, (g + 1) * HEAD_DIM)
                sk2 = s_ref[g][:, 0:1] * LOG2E if has_sink else None
                acc, m, l = softmax(q_ref[:, lanes], sk2, segs)
                o_ref[:, lanes] = acc * (1.0 / l)
                lse_ref[g] = _col_to_row(m + jnp.log2(l))

        ctx_rows = pl.ds(S, L)

        @pl.when(qi < n_lq)
        def _():
            if band:
                lo = _band_lo(qi, tq, S)
                run([(ctx_rows, None), (pl.ds(lo, span), _band_mask(qi, tq, lo, span, False))], joint)
            else:
                run([(pl.ds(j * kc, kc), None) for j in range(S // kc)] + [(ctx_rows, None)], online)

        @pl.when(qi >= n_lq)
        def _():
            run([(ctx_rows, None)], joint)

    q_spec = pl.BlockSpec((None, tq, hp * HEAD_DIM), lambda b, h, i: (b, i, h))
    kv_spec = pl.BlockSpec((None, T, HEAD_DIM), lambda b, h, i: (b, 0, h * hp // KV_GROUP))
    in_specs = [q_spec, kv_spec, kv_spec]
    args = [q, k, v]
    if has_sink:
        in_specs.append(pl.BlockSpec((hp, 1, LANE), lambda b, h, i: (h, 0, 0)))
        args.append(sink)
    name = "attn_band_fwd" if band else "attn_dense_fwd"
    return _call_with_side(
        body, side, name=name, grid=(B, H // hp, T // tq), in_specs=in_specs, args=args,
        out_specs=[q_spec, pl.BlockSpec((None, hp, 1, tq), lambda b, h, i: (b, h, 0, i))],
        out_shape=[jax.ShapeDtypeStruct(q.shape, F32), jax.ShapeDtypeStruct((B, H, 1, T), F32)],
        scratch_shapes=[], dims=("parallel", "parallel", "parallel"))


def _attn_bwd(q, k, v, sink, do, o, lse, S, band, side=None):
    B, T, HD = q.shape
    H = HD // HEAD_DIM
    KVH = H // KV_GROUP
    L = T - S
    tq = L
    n_lq = S // tq
    span = tq + 2 * WINDOW
    scale = HEAD_DIM ** -0.5
    c2 = scale * LOG2E
    has_sink = sink is not None
    kc = min(KEY_CHUNK, S)
    hp = KV_GROUP if band else 1

    def body(*refs):
        if has_sink:
            q_ref, k_ref, v_ref, do_ref, o_ref, lse_ref, s_ref, dq_ref, dk_ref, dv_ref, ds_ref, *scr = refs
        else:
            q_ref, k_ref, v_ref, do_ref, o_ref, lse_ref, dq_ref, dk_ref, dv_ref, *scr = refs
        g, qi = pl.program_id(2), pl.program_id(3)

        @pl.when(jnp.logical_and(g == 0, qi == 0))
        def _():
            dk_ref[...] = jnp.zeros_like(dk_ref)
            dv_ref[...] = jnp.zeros_like(dv_ref)
            if not band:
                scr[0][...] = k_ref[...].astype(F32).T.astype(BF16)

        def run(segs):
            dks, dvs = [None] * len(segs), [None] * len(segs)
            for h in range(hp):
                lanes = slice(h * HEAD_DIM, (h + 1) * HEAD_DIM)
                qv = q_ref[:, lanes]
                dof = do_ref[:, lanes]
                dov = dof.astype(BF16)
                lse2 = lse_ref[h]
                delta = _col_to_row(jnp.sum(dof * o_ref[:, lanes], axis=-1, keepdims=True))

                def head(seg, qv=qv, dov=dov):
                    rows = pl.ds(seg[0], seg[1])
                    s = lax.dot_general(k_ref[rows, :], qv, NT, preferred_element_type=F32) * c2
                    if seg[2] is not None:
                        s = jnp.where(seg[2], s, NEG_INF)
                    return s, lax.dot_general(v_ref[rows, :], dov, NT, preferred_element_type=F32)

                dq, dqT = None, None
                nxt = head(segs[0])
                for j, (lo, n, _) in enumerate(segs):
                    rows = pl.ds(lo, n)
                    s, dp = nxt
                    if j + 1 < len(segs):
                        nxt = head(segs[j + 1])
                    p = jnp.exp2(s - lse2)
                    ds = p * (dp - delta)
                    dsb = ds.astype(BF16)
                    dv = jnp.dot(p.astype(BF16), dov, preferred_element_type=F32)
                    dk = jnp.dot(dsb, qv, preferred_element_type=F32)
                    dvs[j] = dv if dvs[j] is None else dvs[j] + dv
                    dks[j] = dk if dks[j] is None else dks[j] + dk
                    if band:
                        part = jnp.dot(ds.T.astype(BF16), k_ref[rows, :], preferred_element_type=F32)
                        dq = part if dq is None else dq + part
                    else:
                        part = jnp.dot(scr[0][:, lo:lo + n], dsb, preferred_element_type=F32)
                        dqT = part if dqT is None else dqT + part
                dq_ref[:, lanes] = (dq if band else dqT.T) * scale
                if has_sink:
                    psk = jnp.exp2(s_ref[h][:, 0:1] * LOG2E - lse2)
                    dsk = jnp.broadcast_to(-jnp.sum(psk * delta, axis=1, keepdims=True), (1, LANE))

                    @pl.when(qi == 0)
                    def _(h=h, dsk=dsk):
                        ds_ref[h] = dsk

                    @pl.when(qi > 0)
                    def _(h=h, dsk=dsk):
                        ds_ref[h] += dsk
            for j, (lo, n, _) in enumerate(segs):
                rows = pl.ds(lo, n)
                dv_ref[rows, :] += dvs[j]
                dk_ref[rows, :] += dks[j] * scale

        @pl.when(qi < n_lq)
        def _():
            if band:
                lo = _band_lo(qi, tq, S)
                run([(S, L, None), (lo, span, _band_mask(qi, tq, lo, span, True))])
            else:
                run([(j * kc, kc, None) for j in range(S // kc)] + [(S, L, None)])

        @pl.when(qi >= n_lq)
        def _():
            run([(S, L, None)])

    ng = KV_GROUP // hp
    q_spec = pl.BlockSpec((None, tq, hp * HEAD_DIM), lambda b, kv, g, i: (b, i, kv * ng + g))
    kv_spec = pl.BlockSpec((None, T, HEAD_DIM), lambda b, kv, g, i: (b, 0, kv))
    lse_spec = pl.BlockSpec((None, hp, 1, tq), lambda b, kv, g, i: (b, kv * ng + g, 0, i))
    in_specs = [q_spec, kv_spec, kv_spec, q_spec, q_spec, lse_spec]
    out_specs = [q_spec, kv_spec, kv_spec]
    out_shape = [jax.ShapeDtypeStruct(q.shape, F32), jax.ShapeDtypeStruct(k.shape, F32),
                 jax.ShapeDtypeStruct(v.shape, F32)]
    args = [q, k, v, do, o, lse]
    if has_sink:
        in_specs.append(pl.BlockSpec((hp, 1, LANE), lambda b, kv, g, i: (kv * ng + g, 0, 0)))
        args.append(sink)
        out_specs.append(pl.BlockSpec((None, hp, 1, LANE), lambda b, kv, g, i: (b, kv * ng + g, 0, 0)))
        out_shape.append(jax.ShapeDtypeStruct((B, H, 1, LANE), F32))
    res = _call_with_side(
        body, side, name="attn_band_bwd" if band else "attn_dense_bwd", grid=(B, KVH, ng, T // tq),
        in_specs=in_specs, args=args, out_specs=out_specs, out_shape=out_shape,
        scratch_shapes=[] if band else [pltpu.VMEM((HEAD_DIM, T), BF16)],
        dims=("parallel", "parallel", "arbitrary", "arbitrary"))
    return (res[0], res[1], res[2], res[3] if has_sink else None, res[-1] if side is not None else None)


def _prep_f(norm):
    def f(q, kv, *rest):
        kw = kv.shape[1] // 2
        k, v = kv[:, :kw], kv[:, kw:]
        if norm:
            qg, kg, cos, sin = rest
        else:
            (cos, sin), qg, kg = rest, None, None

        def heads(x, g):
            outs = []
            for h in range(x.shape[1] // HEAD_DIM):
                xh = x[:, h * HEAD_DIM:(h + 1) * HEAD_DIM]
                if g is not None:
                    xh = xh * lax.rsqrt(jnp.mean(xh * xh, axis=-1, keepdims=True) + EPS) * g
                outs.append(xh * cos + _swap(xh) * sin)
            return jnp.concatenate(outs, axis=1) if len(outs) > 1 else outs[0]

        return heads(q, qg), heads(k, kg), v

    return f


def _attn_branch(S, tm, band, norm, has_sink, carries=False):
    f = _prep_f(norm)
    name = "band" if band else "dense"

    def prep(q, kv, gains, tabs):
        outs = [(q.shape[2], BF16), (kv.shape[2] // 2, BF16), (kv.shape[2] // 2, BF16)]
        return _row_fwd(f, "prep_" + name + "_fwd", [q, kv], [], list(gains), list(tabs), outs, tm, S)

    def unpack(args):
        q, kv = args[:2]
        rest = list(args[2:])
        gains = [rest.pop(0), rest.pop(0)] if norm else []
        sink = rest.pop(0) if has_sink else None
        pack = rest.pop(0) if carries else None
        return q, kv, gains, sink, pack, rest

    def sink_lanes(sink):
        return None if sink is None else jnp.broadcast_to(sink[:, None, None], (sink.shape[0], 1, LANE))

    def run_fwd(args):
        q, kv, gains, sink, pack, tabs = unpack(args)
        qp, kp, vp = prep(q, kv, gains, tabs)
        side = (pack.astype(BF16), True) if carries else None
        res = _attn_fwd(qp, kp, vp, sink_lanes(sink), S, band, side)
        return ((res[0], res[2]) if carries else res[0]), (args, qp, kp, vp, res[0], res[1])

    @jax.custom_vjp
    def op(*args):
        return run_fwd(args)[0]

    def fwd(*args):
        return run_fwd(args)

    def bwd(res, ct):
        args, qp, kp, vp, o, lse = res
        q, kv, gains, sink, pack, tabs = unpack(args)
        do, side = (ct[0], (ct[1], False)) if carries else (ct, None)
        dqp, dkp, dvp, dsk, recv = _attn_bwd(qp, kp, vp, sink_lanes(sink), do, o, lse, S, band, side)
        dxs, _, dgains = _row_bwd(f, "prep_" + name + "_bwd", [q, kv], [], list(gains), list(tabs),
                                  [dqp, dkp, dvp], tm, S)
        out = list(dxs) + list(dgains)
        if has_sink:
            out.append(jnp.sum(dsk[:, :, 0, 0], axis=0))
        if carries:
            out.append(_sum_slots(recv, "sum_grads"))
        return (*out, *[jnp.zeros_like(t) for t in tabs])

    op.defvjp(fwd, bwd)
    return op


def _final_loss(S, tm):
    def run(X, target, g):
        B, T, D = X.shape
        n_lat_tiles = S // tm

        def lossf(x, gg, tgt):
            y = x * lax.rsqrt(jnp.mean(x * x, axis=-1, keepdims=True) + EPS) * gg
            err = y - tgt
            return 0.5 * jnp.sum(jnp.sum(err * err, axis=-1, keepdims=True), axis=0, keepdims=True) / D

        def body(x_ref, t_ref, g_ref, loss_ref, dx_ref, dg_ref):
            b, i = pl.program_id(0), pl.program_id(1)
            tgt = t_ref[...]
            val, vjp = jax.vjp(lambda x, gg: lossf(x, gg, tgt), x_ref[...], g_ref[...])
            dx, dg = vjp(jnp.ones((1, 1), F32))
            lat = (i < n_lat_tiles).astype(F32)
            dx_ref[...] = dx * lat

            @pl.when(i == 0)
            def _():
                loss_ref[...] = jnp.zeros_like(loss_ref)

            @pl.when(jnp.logical_and(b == 0, i == 0))
            def _():
                dg_ref[...] = jnp.zeros_like(dg_ref)

            loss_ref[...] += jnp.broadcast_to(val * lat, loss_ref.shape)
            dg_ref[...] += dg * lat

        x_spec = pl.BlockSpec((None, tm, D), lambda b, i: (b, i, 0))
        t_spec = pl.BlockSpec((None, tm, D), lambda b, i: (b, jnp.minimum(i, n_lat_tiles - 1), 0))
        g_spec = pl.BlockSpec((1, D), lambda b, i: (0, 0))
        loss, dx, dg = _call(
            body, name="final_loss", grid=(B, T // tm), in_specs=[x_spec, t_spec, g_spec],
            out_specs=[pl.BlockSpec((None, 1, LANE), lambda b, i: (b, 0, 0)), x_spec, g_spec],
            out_shape=[jax.ShapeDtypeStruct((B, 1, LANE), F32), jax.ShapeDtypeStruct(X.shape, F32),
                       jax.ShapeDtypeStruct(g.shape, F32)],
            compiler_params=_params(("arbitrary", "arbitrary")))(X, target, g)
        return jnp.sum(loss[:, 0, 0]), dx, dg

    @jax.custom_vjp
    def op(X, target, g):
        return run(X, target, g)[0]

    def fwd(X, target, g):
        loss, dx, dg = run(X, target, g)
        return loss, (dx, dg, target)

    def bwd(res, ct):
        dx, dg, target = res
        return ct * dx, jnp.zeros_like(target), ct * dg

    op.defvjp(fwd, bwd)
    return op


def _prenorm_f(D):
    def f(x, mp, g):
        y = x * lax.rsqrt(jnp.mean(x * x, axis=-1, keepdims=True) + EPS) * g
        return (y * (1.0 + mp[:, D:2 * D]) + mp[:, 0:D],)
    return f


def _resid_f(D):
    def f(x, y, mp):
        return (x + mp[:, 2 * D:3 * D] * y,)
    return f


def _gate2_f(hf, hr, g):
    return ((hf + hr) * (g * jax.nn.sigmoid(g)),)


def _gate1_f(y, g):
    return (y * (g * jax.nn.sigmoid(g)),)


def _merge_f(D):
    def f(m3, pa, pb, pc):
        return (jax.nn.sigmoid(m3[:, 0:D]) * pa + jax.nn.sigmoid(m3[:, D:2 * D]) * pb
                + jax.nn.sigmoid(m3[:, 2 * D:3 * D]) * pc,)
    return f


def _coef_f(D):
    nblk = D // GATE_BLOCK

    def f(v, pv, wm):
        vb = v.astype(BF16)

        def gate(k):
            cols = []
            for j in range(nblk):
                r0 = (k * nblk + j) * GATE_BLOCK
                cols.append(jnp.dot(vb[:, j * GATE_BLOCK:(j + 1) * GATE_BLOCK],
                                    wm[r0:r0 + GATE_BLOCK, :].astype(BF16), preferred_element_type=F32))
            return jnp.concatenate(cols, axis=1)

        outs = []
        for d in range(2):
            r = jax.nn.sigmoid(gate(d) + pv[d:d + 1])
            i = jax.nn.sigmoid(gate(2 + d) + pv[2 + d:3 + d])
            la = r * pv[4 + d:5 + d]
            t = jnp.tanh(la)
            outs += [jnp.exp(la), jnp.sqrt(-2.0 * t / (1.0 - t)) * (i * v)]
        return tuple(outs)

    return f


def _gate_blocks(w):
    per = GATE_BLOCK // LRU_BLOCK_W
    n = w.shape[1]
    w5 = w.reshape(2, n // per, per, LRU_BLOCK_W, LRU_BLOCK_W)
    dense = jnp.einsum("djiab,ik->djiakb", w5, jnp.eye(per, dtype=w.dtype))
    return dense.reshape(2, (n // per) * GATE_BLOCK, GATE_BLOCK)


def _exchange_shape(x, gather):
    return jax.ShapeDtypeStruct((N_DEV,) + x.shape if gather else x.shape, x.dtype)


EXCHANGE_SEMS = [pltpu.SemaphoreType.DMA((N_DEV - 1,)), pltpu.SemaphoreType.DMA((N_DEV - 1,)),
                 pltpu.SemaphoreType.DMA(())]


def _exchange_ops(x_ref, o_ref, send_sems, recv_sems, local_sem, gather):
    mx, my, mc = lax.axis_index("x"), lax.axis_index("y"), lax.axis_index("c")
    me = 4 * mx + 2 * my + mc

    def src(p):
        return x_ref if gather else x_ref.at[p]

    local = pltpu.make_async_copy(src(me), o_ref.at[me], local_sem)
    sends, recvs = [], []
    for k in range(1, N_DEV):
        px = 1 - mx if k & 4 else mx
        py = 1 - my if k & 2 else my
        pc = 1 - mc if k & 1 else mc
        peer = 4 * px + 2 * py + pc
        sends.append(pltpu.make_async_remote_copy(
            src_ref=src(peer), dst_ref=o_ref.at[me], send_sem=send_sems.at[k - 1],
            recv_sem=recv_sems.at[k - 1], device_id=(px, py, pc), device_id_type=pl.DeviceIdType.MESH))
        recvs.append(pltpu.make_async_remote_copy(
            src_ref=src(peer), dst_ref=o_ref.at[peer], send_sem=send_sems.at[k - 1],
            recv_sem=recv_sems.at[k - 1], device_id=(px, py, pc), device_id_type=pl.DeviceIdType.MESH))

    def start():
        local.start()
        for cp in sends:
            cp.start()

    def finish():
        for cp in recvs:
            cp.wait_recv()
        for cp in sends:
            cp.wait_send()
        local.wait()

    return start, finish


def _exchange(x, name, gather):
    def body(x_ref, o_ref, send_sems, recv_sems, local_sem):
        start, finish = _exchange_ops(x_ref, o_ref, send_sems, recv_sems, local_sem, gather)
        start()
        finish()

    hbm = pl.BlockSpec(memory_space=pltpu.HBM)
    return _call(body, name=name, in_specs=[hbm], out_specs=hbm, out_shape=_exchange_shape(x, gather),
                 scratch_shapes=EXCHANGE_SEMS)(x)


def _call_with_side(body, side, *, name, grid, in_specs, args, out_specs, out_shape, scratch_shapes, dims):
    if side is None:
        return _call(body, name=name, grid=grid, in_specs=in_specs, out_specs=out_specs, out_shape=out_shape,
                     scratch_shapes=scratch_shapes, compiler_params=_params(dims))(*args)
    x, gather = side
    n_in, n_out, n_scr = len(in_specs), len(out_specs), len(scratch_shapes)

    def wrapped(*refs):
        ins, x_ref = refs[:n_in], refs[n_in]
        outs, o_ref = refs[n_in + 1:n_in + 1 + n_out], refs[n_in + 1 + n_out]
        scr = refs[n_in + 2 + n_out:n_in + 2 + n_out + n_scr]
        start, finish = _exchange_ops(x_ref, o_ref, *refs[n_in + 2 + n_out + n_scr:], gather)
        ids = [pl.program_id(a) for a in range(len(grid))]
        first = functools.reduce(jnp.logical_and, [i == 0 for i in ids])
        last = functools.reduce(jnp.logical_and, [i == g - 1 for i, g in zip(ids, grid)])
        pl.when(first)(start)
        body(*ins, *outs, *scr)
        pl.when(last)(finish)

    hbm = pl.BlockSpec(memory_space=pltpu.HBM)
    return _call(wrapped, name=name + "_xchg", grid=grid, in_specs=list(in_specs) + [hbm],
                 out_specs=list(out_specs) + [hbm], out_shape=list(out_shape) + [_exchange_shape(x, gather)],
                 scratch_shapes=list(scratch_shapes) + EXCHANGE_SEMS,
                 compiler_params=_params(("arbitrary",) * len(grid)))(*args, x)


def _adamw(gs, w, m, v, name):
    n, R, C = gs.shape
    tr = _pick(R, 256, 16)

    def body(g_ref, w_ref, m_ref, v_ref, go_ref, d_ref, mo_ref, vo_ref):
        g = g_ref[0].astype(F32)
        for p in range(1, n):
            g = g + g_ref[p].astype(F32)
        m2 = ADAM_B1 * m_ref[...] + (1.0 - ADAM_B1) * g
        v2 = ADAM_B2 * v_ref[...] + (1.0 - ADAM_B2) * (g * g)
        m_hat = m2 / (1.0 - ADAM_B1 ** ADAM_STEP)
        v_hat = v2 / (1.0 - ADAM_B2 ** ADAM_STEP)
        go_ref[...] = g
        d_ref[...] = -ADAM_LR * (m_hat / (jnp.sqrt(v_hat) + ADAM_EPS) + ADAM_WD * w_ref[...])
        mo_ref[...] = m2
        vo_ref[...] = v2

    spec = pl.BlockSpec((tr, C), lambda i: (i, 0))
    return _call(body, name=name, grid=(R // tr,),
                 in_specs=[pl.BlockSpec((n, tr, C), lambda i: (0, i, 0)), spec, spec, spec],
                 out_specs=[spec] * 4, out_shape=[jax.ShapeDtypeStruct((R, C), F32)] * 4,
                 compiler_params=_params(("parallel",)))(gs, w, m, v)


def _sum_slots(gs, name):
    n, R, C = gs.shape
    tr = _pick(R, 256, 16)

    def body(g_ref, o_ref):
        g = g_ref[0].astype(F32)
        for p in range(1, n):
            g = g + g_ref[p].astype(F32)
        o_ref[...] = g

    return _call(body, name=name, grid=(R // tr,), in_specs=[pl.BlockSpec((n, tr, C), lambda i: (0, i, 0))],
                 out_specs=pl.BlockSpec((tr, C), lambda i: (i, 0)), out_shape=jax.ShapeDtypeStruct((R, C), F32),
                 compiler_params=_params(("parallel",)))(gs)


def _gather_op(name):
    @jax.custom_vjp
    def op(pack):
        return _exchange(pack.astype(BF16), name, True)

    def fwd(pack):
        return op(pack), None

    def bwd(_, ct):
        return (_sum_slots(_exchange(ct, name + "_transpose", False), "sum_grads"),)

    op.defvjp(fwd, bwd)
    return op


def _rope_tables(S, L):
    P = HEAD_DIM // 4
    rows = S // GRID_W
    row_id = jnp.repeat(jnp.arange(rows), GRID_W)
    col_id = jnp.tile(jnp.arange(GRID_W), rows)
    inv = ROPE_THETA ** (-jnp.arange(P, dtype=F32) / P)
    ar, ac = row_id[:, None] * inv, col_id[:, None] * inv
    cos = jnp.concatenate([jnp.cos(ar), jnp.cos(ar), jnp.cos(ac), jnp.cos(ac)], axis=1)
    sin = jnp.concatenate([-jnp.sin(ar), jnp.sin(ar), -jnp.sin(ac), jnp.sin(ac)], axis=1)
    cos = jnp.concatenate([cos, jnp.ones((L, HEAD_DIM), F32)], axis=0)
    sin = jnp.concatenate([sin, jnp.zeros((L, HEAD_DIM), F32)], axis=0)
    return cos, sin


SQ_TILES = (((2176, 1024, 1024), (2176, 1024, 1024), (1024, 1024, 1088)),) * DEPTH


def _layer_shard_shapes(D, IN):
    return (("w_in", (D, IN // N_DEV), 1), ("w_mod", (D, 3 * D // N_DEV), 1),
            ("w_branch", (3, D // N_DEV, D), 1), ("w_out", (D // N_DEV, D), 0))


def _pack_layer(w, l, D):
    return jnp.concatenate([w[n][l].reshape(-1) for n in BIG_PACK]).reshape(-1, D)


def _unpack_shards(pack, D, IN):
    flat, out, off = pack.reshape(-1), {}, 0
    for n, shape, _ in _layer_shard_shapes(D, IN):
        sz = shape[0] * shape[1] * (shape[2] if len(shape) > 2 else 1)
        out[n] = flat[off:off + sz].reshape(shape)
        off += sz
    return out


def _unpack_gathered(g, D, IN):
    flat, out, off = g.reshape(N_DEV, -1), {}, 0
    for n, shape, axis in _layer_shard_shapes(D, IN):
        sz = shape[0] * shape[1] * (shape[2] if len(shape) > 2 else 1)
        out[n] = _unshard(flat[:, off:off + sz].reshape((N_DEV,) + shape), axis)
        off += sz
    return out


def _loss_fn(packs, p, x, c, ctx, target):
    B, S, D = x.shape
    L = ctx.shape[1]
    T = S + L
    tm = min(L, 256)
    KVW = D // KV_GROUP
    widths = (D, D, D, 2 * KVW, D, D, 2 * KVW, D, 3 * D)
    IN = sum(widths)
    cos, sin = _rope_tables(S, L)

    X = jnp.concatenate([x, ctx], axis=1)
    sc, scc = jax.nn.silu(c), jax.nn.silu(p["c_ctx"])
    A = jnp.concatenate([scc[None], sc, jnp.zeros((SUBLANE - 1 - B, D), F32)], axis=0)
    gathered = _gather_op("gather_layer0")(packs[0])
    for l in range(DEPTH):
        big = _unpack_gathered(gathered, D, IN)
        mod = _matmul("mm_mod")(A, big["w_mod"]) + p["b_mod"][l]
        modp = jnp.stack([jnp.broadcast_to(mod[0], (B, 3 * D)), mod[1:1 + B]], axis=1)[:, :, None, :]
        (h,) = _rowwise(_prenorm_f(D), "prenorm", [(D, BF16)], tm, S, 1, 1, 1)(X, modp, p["norm_g"][l][None])
        parts = _in_proj(widths)(h.reshape(B * T, D), big["w_in"])
        uA, gA, qB, kvB, gB, qC, kvC, gC, m3 = [t.reshape(B, T, -1) for t in parts]
        u = _conv_op(S)(uA, p["conv_w"][l], p["conv_b"][l][None])
        nsp = -LRU_C * jax.nn.softplus(-p["lru_lambda"][l])
        pv = jnp.concatenate([p["lru_ba"][l], p["lru_bx"][l], nsp, jnp.zeros((2, D), F32)], axis=0)
        wm = jnp.concatenate([_gate_blocks(p["lru_wa"][l]), _gate_blocks(p["lru_wx"][l])], axis=0)
        wm = wm.reshape(-1, GATE_BLOCK)
        af, bf, ar, br = _rowwise(_coef_f(D), "lru_coef", [(D, F32)] * 4, tm, S, 1, 0, 2)(u, pv, wm)
        hf = _scan_op(S, False)(af, bf)
        hr = _scan_op(S, True)(ar, br)
        (zA,) = _rowwise(_gate2_f, "gate_a", [(D, BF16)], tm, S, 3, 0, 0)(hf, hr, gA)
        yB = _attn_branch(S, tm, True, False, True)(qB, kvB, p["attn_sink"][l], cos, sin)
        (zB,) = _rowwise(_gate1_f, "gate_b", [(D, BF16)], tm, S, 2, 0, 0)(yB, gB)
        qkv_c = (qC, kvC, p["q_norm_g"][l][None], p["k_norm_g"][l][None])
        if l + 1 < DEPTH:
            yC, gathered = _attn_branch(S, tm, False, True, False, True)(*qkv_c, packs[l + 1], cos, sin)
        else:
            yC = _attn_branch(S, tm, False, True, False)(*qkv_c, cos, sin)
        (zC,) = _rowwise(_gate1_f, "gate_c", [(D, BF16)], tm, S, 2, 0, 0)(yC, gC)
        pr = [_matmul("mm_branch", *SQ_TILES[l])(z.reshape(B * T, D), big["w_branch"][n]).reshape(B, T, D)
              for n, z in enumerate((zA, zB, zC))]
        (mg,) = _rowwise(_merge_f(D), "merge", [(D, BF16)], tm, S, 4, 0, 0)(m3, *pr)
        y = _matmul("mm_out", *SQ_TILES[l])(mg.reshape(B * T, D), big["w_out"]).reshape(B, T, D)
        (X,) = _rowwise(_resid_f(D), "resid", [(D, F32)], tm, S, 2, 1, 0)(X, y, modp)
    return _final_loss(S, tm)(X, target, p["final_g"][None])


def _shard_axis(name):
    return {"w_mod": 2, "w_in": 2, "conv_w": 2, "lru_ba": 2, "lru_bx": 2, "lru_lambda": 2,
            "w_branch": 2, "w_out": 1}.get(name)


def _unshard(g, axis):
    full = jnp.moveaxis(g, 0, axis)
    shape = list(full.shape)
    shape[axis:axis + 2] = [shape[axis] * shape[axis + 1]]
    return full.reshape(shape)


def _reshard(full, axis):
    shape = list(full.shape)
    shape[axis:axis + 1] = [N_DEV, shape[axis] // N_DEV]
    return jnp.moveaxis(full.reshape(shape), axis, 0)


def _pad_to(v, n):
    return jnp.concatenate([v, jnp.zeros((n - v.shape[0],), v.dtype)]) if n > v.shape[0] else v


def _step(x, c, ctx, target, w, m, v):
    D = x.shape[2]
    IN = w["w_in"].shape[2] * N_DEV
    full = {n: w[n] for n in REPLICATED}
    small_local = jnp.concatenate([w[n].reshape(-1) for n in SMALL_SHARDED])
    small_all = _exchange(small_local.reshape(-1, LANE), "gather_small", True).reshape(N_DEV, -1)
    off = 0
    for n in SMALL_SHARDED:
        sz = w[n].size
        full[n] = _unshard(small_all[:, off:off + sz].reshape((N_DEV,) + w[n].shape), _shard_axis(n))
        off += sz
    packs = [_pack_layer(w, l, D) for l in range(DEPTH)]

    loss, (gpacks, gp, gx) = jax.value_and_grad(_loss_fn, argnums=(0, 1, 2))(packs, full, x, c, ctx, target)
    loss = lax.psum(loss, AXES)

    out = {}
    gshards = [_unpack_shards(g, D, IN) for g in gpacks]
    for n in BIG_PACK:
        g = jnp.stack([gs[n] for gs in gshards])
        C = w[n].shape[-1]
        res = _adamw(g.reshape(1, -1, C), w[n].reshape(-1, C), m[n].reshape(-1, C), v[n].reshape(-1, C),
                     "adamw_" + n)
        out[n] = [r.reshape(w[n].shape) for r in res]

    rep = jnp.concatenate([gp[n].reshape(-1) for n in REPLICATED])
    n_rep = rep.shape[0]
    chunk = -(-n_rep // (N_DEV * LANE)) * LANE
    rep = _pad_to(rep, N_DEV * chunk).reshape(N_DEV, chunk)
    shards = jnp.concatenate([_reshard(gp[n], _shard_axis(n)).reshape(N_DEV, -1) for n in SMALL_SHARDED], axis=1)
    n_sh = shards.shape[1]
    recv = _exchange(jnp.concatenate([rep, shards], axis=1).reshape(N_DEV, -1, LANE), "scatter_small", False)
    wl =jnp.concatenate([w[n].reshape(-1) for n in SMALL_SHARDED])
    ml = jnp.concatenate([m[n].reshape(-1) for n in SMALL_SHARDED])
    vl = jnp.concatenate([v[n].reshape(-1) for n in SMALL_SHARDED])
    rows = recv.shape[1]
    rrows = chunk // LANE
    g_sh, d_sh, m_sh, v_sh = _adamw(recv[:, rrows:], wl.reshape(-1, LANE), ml.reshape(-1, LANE),
                                    vl.reshape(-1, LANE), "adamw_small_sharded")
    g_rep8 = _sum_slots(recv[:, :rrows], "sum_replicated")
    g_rep = _exchange(g_rep8, "gather_replicated", True).reshape(-1)
    wr = _pad_to(jnp.concatenate([w[n].reshape(-1) for n in REPLICATED]), N_DEV * chunk)
    mr = _pad_to(jnp.concatenate([m[n].reshape(-1) for n in REPLICATED]), N_DEV * chunk)
    vr = _pad_to(jnp.concatenate([v[n].reshape(-1) for n in REPLICATED]), N_DEV * chunk)
    res_rep = _adamw(g_rep.reshape(1, -1, LANE), wr.reshape(-1, LANE), mr.reshape(-1, LANE), vr.reshape(-1, LANE),
                     "adamw_replicated")
    off = 0
    for n in REPLICATED:
        sz = w[n].size
        out[n] = [r.reshape(-1)[off:off + sz].reshape(w[n].shape) for r in res_rep]
        off += sz
    off = 0
    for n in SMALL_SHARDED:
        sz = w[n].size
        out[n] = [r.reshape(-1)[off:off + sz].reshape(w[n].shape) for r in (g_sh, d_sh, m_sh, v_sh)]
        off += sz
    assert off == n_sh and rows == rrows + n_sh // LANE
    return (loss, gx, *[out[n][0] for n in WEIGHTS], *[out[n][1] for n in WEIGHTS],
            *[out[n][2] for n in WEIGHTS], *[out[n][3] for n in WEIGHTS])


def kernel(x, c, ctx, c_ctx, norm_g, w_mod, b_mod, w_in, conv_w, conv_b, lru_wa, lru_ba, lru_wx, lru_bx, lru_lambda, attn_sink, q_norm_g, k_norm_g, w_branch, w_out, final_g, loss_target, m_c_ctx, m_norm_g, m_w_mod, m_b_mod, m_w_in, m_conv_w, m_conv_b, m_lru_wa, m_lru_ba, m_lru_wx, m_lru_bx, m_lru_lambda, m_attn_sink, m_q_norm_g, m_k_norm_g, m_w_branch, m_w_out, m_final_g, v_c_ctx, v_norm_g, v_w_mod, v_b_mod, v_w_in, v_conv_w, v_conv_b, v_lru_wa, v_lru_ba, v_lru_wx, v_lru_bx, v_lru_lambda, v_attn_sink, v_q_norm_g, v_k_norm_g, v_w_branch, v_w_out, v_final_g):
    w = dict(zip(WEIGHTS, (c_ctx, norm_g, w_mod, b_mod, w_in, conv_w, conv_b, lru_wa, lru_ba, lru_wx, lru_bx,
                           lru_lambda, attn_sink, q_norm_g, k_norm_g, w_branch, w_out, final_g)))
    m = dict(zip(WEIGHTS, (m_c_ctx, m_norm_g, m_w_mod, m_b_mod, m_w_in, m_conv_w, m_conv_b, m_lru_wa, m_lru_ba,
                           m_lru_wx, m_lru_bx, m_lru_lambda, m_attn_sink, m_q_norm_g, m_k_norm_g, m_w_branch,
                           m_w_out, m_final_g)))
    v = dict(zip(WEIGHTS, (v_c_ctx, v_norm_g, v_w_mod, v_b_mod, v_w_in, v_conv_w, v_conv_b, v_lru_wa, v_lru_ba,
                           v_lru_wx, v_lru_bx, v_lru_lambda, v_attn_sink, v_q_norm_g, v_k_norm_g, v_w_branch,
                           v_w_out, v_final_g)))
    return _step(x, c, ctx, loss_target, w, m, v)
```

```python
import functools
import math

import jax
import jax.numpy as jnp
from jax import lax
from jax.experimental import pallas as pl
from jax.experimental.pallas import tpu as pltpu

F32 = jnp.float32
BF16 = jnp.bfloat16

AXES = ("x", "y", "c")
N_DEV = 8
DEPTH = 4
HEAD_DIM = 128
GRID_W = 64
WINDOW = 128
LRU_BLOCK_W = 64
GATE_BLOCK = 256
LRU_C = 8.0
ROPE_THETA = 10000.0
EPS = 1e-6
NEG_INF = -1e30
KV_GROUP = 4
LANE = 128
SUBLANE = 8
VMEM_LIMIT = 56 * 1024 * 1024

ADAM_LR = 0.001
ADAM_B1 = 0.9
ADAM_B2 = 0.999
ADAM_EPS = 1e-08
ADAM_WD = 0.01
ADAM_STEP = 10

WEIGHTS = ("c_ctx", "norm_g", "w_mod", "b_mod", "w_in", "conv_w", "conv_b", "lru_wa", "lru_ba", "lru_wx",
           "lru_bx", "lru_lambda", "attn_sink", "q_norm_g", "k_norm_g", "w_branch", "w_out", "final_g")
BIG_PACK = ("w_in", "w_mod", "w_branch", "w_out")
SMALL_SHARDED = ("conv_w", "lru_ba", "lru_bx", "lru_lambda")
REPLICATED = ("c_ctx", "norm_g", "b_mod", "conv_b", "lru_wa", "lru_wx", "attn_sink", "q_norm_g", "k_norm_g",
              "final_g")


def _call(body, **kw):
    return pl.pallas_call(body, **kw)


def _params(dims=None, vmem=VMEM_LIMIT):
    return pltpu.CompilerParams(dimension_semantics=dims, vmem_limit_bytes=vmem)


def _pick(n, target, mult):
    for t in range(min(n, target), 0, -1):
        if n % t == 0 and t % mult == 0:
            return t
    return n


def _mm(a, b, *, name, ta=False, tb=False, out_dtype=F32, tm=512, tn=1024, tk=1024):
    M, K = (a.shape[1], a.shape[0]) if ta else a.shape
    N = b.shape[0] if tb else b.shape[1]
    assert (b.shape[1] if tb else b.shape[0]) == K
    tm = _pick(M, tm, LANE if ta else 16)
    tn = _pick(N, tn, LANE)
    tk = _pick(K, tk, 16 if ta and not tb else LANE)
    nk = K // tk
    dn = (((0 if ta else 1,), (1 if tb else 0,)), ((), ()))

    def body(a_ref, b_ref, o_ref, *acc):
        r = lax.dot_general(a_ref[...].astype(BF16), b_ref[...].astype(BF16), dn,
                            preferred_element_type=F32)
        if nk == 1:
            o_ref[...] = r.astype(o_ref.dtype)
        else:
            k = pl.program_id(2)

            @pl.when(k == 0)
            def _():
                acc[0][...] = r

            @pl.when(k > 0)
            def _():
                acc[0][...] += r

            @pl.when(k == nk - 1)
            def _():
                o_ref[...] = acc[0][...].astype(o_ref.dtype)

    a_spec = (pl.BlockSpec((tk, tm), lambda i, j, k: (k, i)) if ta
              else pl.BlockSpec((tm, tk), lambda i, j, k: (i, k)))
    b_spec = (pl.BlockSpec((tn, tk), lambda i, j, k: (j, k)) if tb
              else pl.BlockSpec((tk, tn), lambda i, j, k: (k, j)))
    return _call(
        body, name=name, grid=(M // tm, N // tn, nk),
        in_specs=[a_spec, b_spec],
        out_specs=pl.BlockSpec((tm, tn), lambda i, j, k: (i, j)),
        out_shape=jax.ShapeDtypeStruct((M, N), out_dtype),
        scratch_shapes=[pltpu.VMEM((tm, tn), F32)] if nk > 1 else [],
        compiler_params=_params(("parallel", "parallel", "arbitrary")),
    )(a, b)


def _matmul(name, t_fwd=(512, 1024, 1024), t_da=(512, 1024, 1024), t_dw=(1024, 1024, 512), out_dtype=F32):
    def tiles(t):
        return dict(tm=t[0], tn=t[1], tk=t[2])

    @jax.custom_vjp
    def f(a, w):
        return _mm(a, w, name=name + "_fwd", out_dtype=out_dtype, **tiles(t_fwd))

    def fwd(a, w):
        return f(a, w), (a, w)

    def bwd(res, g):
        a, w = res
        da = _mm(g, w, name=name + "_da", tb=True, out_dtype=a.dtype, **tiles(t_da))
        dw = _mm(a, g, name=name + "_dw", ta=True, out_dtype=w.dtype, **tiles(t_dw))
        return da, dw

    f.defvjp(fwd, bwd)
    return f


def _mm_parts_nt(parts, w, col0, init, out_dtype, *, name, tm=1088, tk=512):
    M = parts[0].shape[0]
    D = w.shape[0]
    tm = _pick(M, tm, 16)
    tk = math.gcd(tk, *[a.shape[1] for a in parts])
    starts, n = [], 0
    for a in parts:
        assert a.shape[1] % tk == 0
        starts.append(n)
        n += a.shape[1] // tk
    k0 = col0 // tk
    has_init = init is not None

    def body(*refs):
        a_refs, w_ref = refs[:len(parts)], refs[len(parts)]
        o_ref, acc = refs[-2], refs[-1]
        k = pl.program_id(1)

        @pl.when(k == 0)
        def _():
            acc[...] = refs[len(parts) + 1][...].astype(F32) if has_init else jnp.zeros_like(acc)

        for a_ref, s, a in zip(a_refs, starts, parts):
            @pl.when(jnp.logical_and(k >= s, k < s + a.shape[1] // tk))
            def _(a_ref=a_ref):
                acc[...] += lax.dot_general(a_ref[...].astype(BF16), w_ref[...], NT, preferred_element_type=F32)

        @pl.when(k == n - 1)
        def _():
            o_ref[...] = acc[...].astype(o_ref.dtype)

    in_specs = [pl.BlockSpec((tm, tk), lambda i, k, s=s, c=a.shape[1] // tk: (i, jnp.clip(k - s, 0, c - 1)))
                for a, s in zip(parts, starts)]
    in_specs.append(pl.BlockSpec((D, tk), lambda i, k: (0, k0 + k)))
    o_spec = pl.BlockSpec((tm, D), lambda i, k: (i, 0))
    args = list(parts) + [w]
    if has_init:
        in_specs.append(o_spec)
        args.append(init)
    return _call(body, name=name, grid=(M // tm, n), in_specs=in_specs, out_specs=o_spec,
                 out_shape=jax.ShapeDtypeStruct((M, D), out_dtype),
                 scratch_shapes=[pltpu.VMEM((tm, D), F32)],
                 compiler_params=_params(("parallel", "arbitrary")))(*args)


def _in_proj(widths, dtypes, t_fwd=(2176, 1024, 1024), t_dw=(1024, 1024, 2176), group=5):
    offs = [0]
    for wd in widths:
        offs.append(offs[-1] + wd)

    def cols(w, i):
        return w[:, offs[i]:offs[i + 1]]

    @jax.custom_vjp
    def op(h, w):
        return tuple(_mm(h, cols(w, i), name="mm_in_fwd", out_dtype=dtypes[i], tm=t_fwd[0], tn=t_fwd[1],
                         tk=t_fwd[2]) for i in range(len(widths)))

    def fwd(h, w):
        return op(h, w), (h, w)

    def bwd(res, gs):
        h, w = res
        da = None
        for i0 in range(0, len(widths), group):
            last = i0 + group >= len(widths)
            da = _mm_parts_nt(list(gs[i0:i0 + group]), w, offs[i0], da, h.dtype if last else F32, name="mm_in_da")
        dw = jnp.concatenate([_mm(h, g, name="mm_in_dw", ta=True, out_dtype=w.dtype, tm=t_dw[0], tn=t_dw[1],
                                  tk=t_dw[2]) for g in gs], axis=1)
        return da, dw

    op.defvjp(fwd, bwd)
    return op


def _row_specs(xs, tps, gps, rts, tm, n_lat_tiles):
    x_specs = [pl.BlockSpec((None, tm, x.shape[2]), lambda b, i: (b, i, 0)) for x in xs]
    tp_specs = [pl.BlockSpec((None, None, 1, p.shape[3]),
                             lambda b, i: (b, (i < n_lat_tiles).astype(jnp.int32), 0, 0)) for p in tps]
    gp_specs = [pl.BlockSpec(p.shape, lambda b, i: (0, 0)) for p in gps]
    rt_specs = [pl.BlockSpec((tm, t.shape[1]), lambda b, i: (i, 0)) for t in rts]
    return x_specs, tp_specs, gp_specs, rt_specs


def _row_fwd(f, name, xs, tps, gps, rts, outs, tm, n_lat):
    B, T, _ = xs[0].shape
    n_in = len(xs) + len(tps) + len(gps) + len(rts)

    def body(*refs):
        vals = f(*[r[...] for r in refs[:n_in]])
        for o, v in zip(refs[n_in:], vals):
            o[...] = v.astype(o.dtype)

    x_specs, tp_specs, gp_specs, rt_specs = _row_specs(xs, tps, gps, rts, tm, n_lat // tm)
    res = _call(
        body, name=name, grid=(B, T // tm),
        in_specs=x_specs + tp_specs + gp_specs + rt_specs,
        out_specs=[pl.BlockSpec((None, tm, w), lambda b, i: (b, i, 0)) for w, _ in outs],
        out_shape=[jax.ShapeDtypeStruct((B, T, w), dt) for w, dt in outs],
        compiler_params=_params(("parallel", "parallel")),
    )(*xs, *tps, *gps, *rts)
    return list(res)


def _row_bwd(f, name, xs, tps, gps, rts, douts, tm, n_lat):
    B, T, _ = xs[0].shape
    nx, ntp, ngp, nd = len(xs), len(tps), len(gps), len(douts)
    n_lat_tiles = n_lat // tm
    n_diff = nx + ntp + ngp
    n_in = n_diff + len(rts)

    def body(*refs):
        diff = [r[...] for r in refs[:n_diff]]
        tabs = [r[...] for r in refs[n_diff:n_in]]
        dos = [r[...] for r in refs[n_in:n_in + nd]]
        o_refs = refs[n_in + nd:]
        prim, vjp = jax.vjp(lambda *d: tuple(f(*d, *tabs)), *diff)
        grads = vjp(tuple(d.astype(p.dtype) for d, p in zip(dos, prim)))
        b, i = pl.program_id(0), pl.program_id(1)
        for k in range(nx):
            o_refs[k][...] = grads[k].astype(o_refs[k].dtype)
        first_tp = jnp.logical_or(i == 0, i == n_lat_tiles)
        first_gp = jnp.logical_and(b == 0, i == 0)
        for k in range(nx, n_diff):
            first = first_tp if k < nx + ntp else first_gp

            @pl.when(first)
            def _(k=k):
                o_refs[k][...] = grads[k]

            @pl.when(jnp.logical_not(first))
            def _(k=k):
                o_refs[k][...] += grads[k]

    x_specs, tp_specs, gp_specs, rt_specs = _row_specs(xs, tps, gps, rts, tm, n_lat_tiles)
    d_specs = [pl.BlockSpec((None, tm, d.shape[2]), lambda b, i: (b, i, 0)) for d in douts]
    res = _call(
        body, name=name, grid=(B, T // tm),
        in_specs=x_specs + tp_specs + gp_specs + rt_specs + d_specs,
        out_specs=x_specs + tp_specs + gp_specs,
        out_shape=[jax.ShapeDtypeStruct(a.shape, a.dtype) for a in xs]
        + [jax.ShapeDtypeStruct(a.shape, F32) for a in (*tps, *gps)],
        compiler_params=_params(("arbitrary", "arbitrary")),
    )(*xs, *tps, *gps, *rts, *douts)
    res = list(res)
    return res[:nx], res[nx:nx + ntp], res[nx + ntp:]


def _rowwise(f, name, outs, tm, n_lat, n_x, n_tp, n_gp):
    def split(args):
        return (args[:n_x], args[n_x:n_x + n_tp], args[n_x + n_tp:n_x + n_tp + n_gp],
                args[n_x + n_tp + n_gp:])

    @jax.custom_vjp
    def op(*args):
        xs, tps, gps, rts = split(args)
        return tuple(_row_fwd(f, name + "_fwd", xs, tps, gps, rts, outs, tm, n_lat))

    def fwd(*args):
        return op(*args), args

    def bwd(args, g):
        xs, tps, gps, rts = split(args)
        dxs, dtps, dgps = _row_bwd(f, name + "_bwd", xs, tps, gps, rts, list(g), tm, n_lat)
        return (*dxs, *dtps, *dgps, *[jnp.zeros_like(t) for t in rts])

    op.defvjp(fwd, bwd)
    return op


def _shift_impl(u, k):
    n = u.shape[0]
    r = pltpu.roll(u, k % n, axis=0)
    row = lax.broadcasted_iota(jnp.int32, u.shape, 0)
    valid = (row >= k) if k > 0 else (row < n + k)
    return jnp.where(valid, r, 0.0)


@functools.partial(jax.custom_vjp, nondiff_argnums=(1,))
def _shift(u, k):
    return _shift_impl(u, k)


_shift.defvjp(lambda u, k: (_shift_impl(u, k), None), lambda k, _, g: (_shift_impl(g, -k),))


def _swap_impl(x):
    lane = lax.broadcasted_iota(jnp.int32, x.shape, 1)
    q = HEAD_DIM // 4
    return jnp.where((lane % (2 * q)) < q, pltpu.roll(x, HEAD_DIM - q, axis=1), pltpu.roll(x, q, axis=1))


@jax.custom_vjp
def _swap(x):
    return _swap_impl(x)


_swap.defvjp(lambda x: (_swap_impl(x), None), lambda _, g: (_swap_impl(g),))


def _conv_f(ul, uc, cw, cb):
    def conv(u):
        return (_shift(u, 2) * cw[0:1] + _shift(u, 1) * cw[1:2] + u * cw[2:3] + _shift(u, -1) * cw[3:4] + cb)
    return conv(ul), conv(uc)


def _conv_specs(B, T, D):
    u_spec = pl.BlockSpec((None, T, LANE), lambda j, b: (b, 0, j))
    cw_spec = pl.BlockSpec((4, LANE), lambda j, b: (0, j))
    cb_spec = pl.BlockSpec((1, LANE), lambda j, b: (0, j))
    return u_spec, cw_spec, cb_spec


def _conv_fwd(u, cw, cb, S):
    B, T, D = u.shape

    def body(u_ref, cw_ref, cb_ref, o_ref):
        vl, vc = _conv_f(u_ref[0:S, :], u_ref[S:T, :], cw_ref[...], cb_ref[...])
        o_ref[0:S, :] = vl
        o_ref[S:T, :] = vc

    u_spec, cw_spec, cb_spec = _conv_specs(B, T, D)
    return _call(body, name="conv_fwd", grid=(D // LANE, B), in_specs=[u_spec, cw_spec, cb_spec],
                 out_specs=u_spec, out_shape=jax.ShapeDtypeStruct(u.shape, F32),
                 compiler_params=_params(("parallel", "parallel")))(u, cw, cb)


def _conv_bwd(u, cw, cb, dv, S):
    B, T, D = u.shape

    def body(u_ref, cw_ref, cb_ref, dv_ref, du_ref, dcw_ref, dcb_ref):
        _, vjp = jax.vjp(_conv_f, u_ref[0:S, :], u_ref[S:T, :], cw_ref[...], cb_ref[...])
        dul, duc, dcw, dcb = vjp((dv_ref[0:S, :], dv_ref[S:T, :]))
        du_ref[0:S, :] = dul
        du_ref[S:T, :] = duc
        first = pl.program_id(1) == 0

        @pl.when(first)
        def _():
            dcw_ref[...] = dcw
            dcb_ref[...] = dcb

        @pl.when(jnp.logical_not(first))
        def _():
            dcw_ref[...] += dcw
            dcb_ref[...] += dcb

    u_spec, cw_spec, cb_spec = _conv_specs(B, T, D)
    return _call(body, name="conv_bwd", grid=(D // LANE, B), in_specs=[u_spec, cw_spec, cb_spec, u_spec],
                 out_specs=[u_spec, cw_spec, cb_spec],
                 out_shape=[jax.ShapeDtypeStruct(u.shape, F32), jax.ShapeDtypeStruct(cw.shape, F32),
                            jax.ShapeDtypeStruct(cb.shape, F32)],
                 compiler_params=_params(("parallel", "arbitrary")))(u, cw, cb, dv)


def _conv_op(S):
    @jax.custom_vjp
    def op(u, cw, cb):
        return _conv_fwd(u, cw, cb, S)

    def fwd(u, cw, cb):
        return op(u, cw, cb), (u, cw, cb)

    def bwd(res, g):
        u, cw, cb = res
        return tuple(_conv_bwd(u, cw, cb, g, S))

    op.defvjp(fwd, bwd)
    return op


SCAN_UNROLL = 4


def _group_scan(A, Bv, asc):
    row = lax.broadcasted_iota(jnp.int32, A.shape, 0)
    for s in (1, 2, 4):
        sh = s if asc else SUBLANE - s
        valid = (row >= s) if asc else (row < SUBLANE - s)
        A_sh = pltpu.roll(A, sh, axis=0)
        B_sh = pltpu.roll(Bv, sh, axis=0)
        Bv = jnp.where(valid, A * B_sh, 0.0) + Bv
        A = jnp.where(valid, A * A_sh, A)
    return A, Bv


def _chain_step(A, Bv, carry, asc):
    row = lax.broadcasted_iota(jnp.int32, A.shape, 0)
    A2, B2 = _group_scan(A, Bv, asc)
    h = A2 * carry + B2
    if asc:
        prev = jnp.where(row == 0, carry, pltpu.roll(h, 1, axis=0))
        return h, prev, h[SUBLANE - 1:SUBLANE, :]
    prev = jnp.where(row == SUBLANE - 1, carry, pltpu.roll(h, SUBLANE - 1, axis=0))
    return h, prev, h[0:1, :]


def _chain_loop(segments, step):
    carry = jnp.zeros((1, LANE), F32)
    for lo, hi, asc in segments:
        span = SUBLANE * SCAN_UNROLL
        assert (hi - lo) % span == 0

        def it(t, carry, lo=lo, hi=hi, asc=asc, span=span):
            base = lo + t * span if asc else hi - (t + 1) * span
            order = range(SCAN_UNROLL) if asc else reversed(range(SCAN_UNROLL))
            for j in order:
                carry = step(pl.multiple_of(base + SUBLANE * j, SUBLANE), carry, asc)
            return carry

        carry = lax.fori_loop(0, (hi - lo) // span, it, carry)
    return carry


def _scan_specs(T):
    return pl.BlockSpec((None, T, LANE), lambda j, b: (b, 0, j))


def _scan_fwd(a, b, S, reverse):
    B, T, D = a.shape
    asc = not reverse
    segments = [(S, T, asc), (0, S, asc)]

    def body(a_ref, b_ref, h_ref, hp_ref):
        def step(r0, carry, asc):
            rows = pl.ds(r0, SUBLANE)
            h, prev, carry = _chain_step(a_ref[rows, :], b_ref[rows, :], carry, asc)
            h_ref[rows, :] = h
            hp_ref[rows, :] = prev
            return carry
        _chain_loop(segments, step)

    spec = _scan_specs(T)
    return _call(body, name="scan_rev_fwd" if reverse else "scan_fwd_fwd", grid=(D // LANE, B),
                 in_specs=[spec, spec], out_specs=[spec, spec],
                 out_shape=[jax.ShapeDtypeStruct(a.shape, F32)] * 2,
                 compiler_params=_params(("parallel", "parallel")))(a, b)


def _scan_bwd(a, hp, dy, S, reverse):
    B, T, D = a.shape
    asc = reverse
    segments = [(0, S, asc), (S, T, asc)]

    def body(a_ref, hp_ref, dy_ref, da_ref, db_ref):
        def step(r0, carry, asc):
            rows = pl.ds(r0, SUBLANE)
            A, dy = a_ref[rows, :], dy_ref[rows, :]
            _, s_prev, carry = _chain_step(A, A * dy, carry, asc)
            g = dy + s_prev
            db_ref[rows, :] = g
            da_ref[rows, :] = g * hp_ref[rows, :]
            return carry
        _chain_loop(segments, step)

    spec = _scan_specs(T)
    return _call(body, name="scan_rev_bwd" if reverse else "scan_fwd_bwd", grid=(D // LANE, B),
                 in_specs=[spec, spec, spec], out_specs=[spec, spec],
                 out_shape=[jax.ShapeDtypeStruct(a.shape, F32)] * 2,
                 compiler_params=_params(("parallel", "parallel")))(a, hp, dy)


def _scan_op(S, reverse):
    @jax.custom_vjp
    def op(a, b):
        return _scan_fwd(a, b, S, reverse)[0]

    def fwd(a, b):
        h, hp = _scan_fwd(a, b, S, reverse)
        return h, (a, hp)

    def bwd(res, g):
        a, hp = res
        return tuple(_scan_bwd(a, hp, g, S, reverse))

    op.defvjp(fwd, bwd)
    return op


def _band_lo(qi, tq, S):
    span = tq + 2 * WINDOW
    return pl.multiple_of(jnp.clip(qi * tq - WINDOW, 0, S - span), LANE)


def _band_mask(qi, tq, lo, span, transposed):
    shape = (span, tq) if transposed else (tq, span)
    qpos = qi * tq + lax.broadcasted_iota(jnp.int32, shape, 1 if transposed else 0)
    kpos = lo + lax.broadcasted_iota(jnp.int32, shape, 0 if transposed else 1)
    return jnp.abs(kpos - qpos) <= WINDOW


NT = (((1,), (1,)), ((), ()))
LOG2E = 1.4426950408889634
KEY_CHUNK = 1024


def _col_to_row(c):
    return jnp.broadcast_to(c, (c.shape[0], LANE)).T[0:1, :]


def _attn_fwd(q, k, v, sink, S, band, side=None):
    B, T, HD = q.shape
    H = HD // HEAD_DIM
    L = T - S
    tq = L
    n_lq = S // tq
    span = tq + 2 * WINDOW
    c2 = HEAD_DIM ** -0.5 * LOG2E
    has_sink = sink is not None
    kc = min(KEY_CHUNK, S)
    hp = KV_GROUP if band else 1

    def body(*refs):
        if has_sink:
            q_ref, k_ref, v_ref, s_ref, o_ref, lse_ref = refs
        else:
            q_ref, k_ref, v_ref, o_ref, lse_ref = refs
        qi = pl.program_id(2)

        def online(qv, sk2, segs):
            def scores(seg):
                s = lax.dot_general(qv, k_ref[seg[0], :], NT, preferred_element_type=F32) * c2
                return s if seg[1] is None else jnp.where(seg[1], s, NEG_INF)

            m = l = acc = None
            s_next = scores(segs[0])
            for j, (rows, _) in enumerate(segs):
                s = s_next
                if j + 1 < len(segs):
                    s_next = scores(segs[j + 1])
                ms = jnp.max(s, axis=-1, keepdims=True)
                if m is None:
                    m = ms if sk2 is None else jnp.maximum(ms, sk2)
                    p = jnp.exp2(s - m)
                    l = jnp.sum(p, axis=-1, keepdims=True)
                    if sk2 is not None:
                        l = l + jnp.exp2(sk2 - m)
                    acc = jnp.dot(p.astype(BF16), v_ref[rows, :], preferred_element_type=F32)
                else:
                    m_new = jnp.maximum(m, ms)
                    alpha = jnp.exp2(m - m_new)
                    p = jnp.exp2(s - m_new)
                    l = alpha * l + jnp.sum(p, axis=-1, keepdims=True)
                    acc = alpha * acc + jnp.dot(p.astype(BF16), v_ref[rows, :], preferred_element_type=F32)
                    m = m_new
            return acc, m, l

        def joint(qv, sk2, segs):
            ss = []
            for rows, mask in segs:
                s = lax.dot_general(qv, k_ref[rows, :], NT, preferred_element_type=F32) * c2
                ss.append(s if mask is None else jnp.where(mask, s, NEG_INF))
            m = functools.reduce(jnp.maximum, [jnp.max(s, axis=-1, keepdims=True) for s in ss])
            if sk2 is not None:
                m = jnp.maximum(m, sk2)
            ps = [jnp.exp2(s - m) for s in ss]
            l = functools.reduce(jnp.add, [jnp.sum(p, axis=-1, keepdims=True) for p in ps])
            if sk2 is not None:
                l = l + jnp.exp2(sk2 - m)
            acc = functools.reduce(jnp.add, [jnp.dot(p.astype(BF16), v_ref[rows, :], preferred_element_type=F32)
                                             for p, (rows, _) in zip(ps, segs)])
            return acc, m, l

        def run(segs, softmax):
            for g in range(hp):
                lanes = slice(g * HEAD_DIM, (g + 1) * HEAD_DIM)
                sk2 = s_ref[g][:, 0:1] * LOG2E if has_sink else None
                acc, m, l = softmax(q_ref[:, lanes], sk2, segs)
                o_ref[:, lanes] = acc * (1.0 / l)
                lse_ref[g] = _col_to_row(m + jnp.log2(l))

        ctx_rows = pl.ds(S, L)

        @pl.when(qi < n_lq)
        def _():
            if band:
                lo = _band_lo(qi, tq, S)
                run([(ctx_rows, None), (pl.ds(lo, span), _band_mask(qi, tq, lo, span, False))], joint)
            else:
                run([(pl.ds(j * kc, kc), None) for j in range(S // kc)] + [(ctx_rows, None)], online)

        @pl.when(qi >= n_lq)
        def _():
            run([(ctx_rows, None)], joint)

    q_spec = pl.BlockSpec((None, tq, hp * HEAD_DIM), lambda b, h, i: (b, i, h))
    kv_spec = pl.BlockSpec((None, T, HEAD_DIM), lambda b, h, i: (b, 0, h * hp // KV_GROUP))
    in_specs = [q_spec, kv_spec, kv_spec]
    args = [q, k, v]
    if has_sink:
        in_specs.append(pl.BlockSpec((hp, 1, LANE), lambda b, h, i: (h, 0, 0)))
        args.append(sink)
    name = "attn_band_fwd" if band else "attn_dense_fwd"
    return _call_with_side(
        body, side, name=name, grid=(B, H // hp, T // tq), in_specs=in_specs, args=args,
        out_specs=[q_spec, pl.BlockSpec((None, hp, 1, tq), lambda b, h, i: (b, h, 0, i))],
        out_shape=[jax.ShapeDtypeStruct(q.shape, F32), jax.ShapeDtypeStruct((B, H, 1, T), F32)],
        scratch_shapes=[], dims=("parallel", "parallel", "parallel"))


def _attn_bwd(q, k, v, sink, do, o, lse, S, band, side=None):
    B, T, HD = q.shape
    H = HD // HEAD_DIM
    KVH = H // KV_GROUP
    L = T - S
    tq = L
    n_lq = S // tq
    span = tq + 2 * WINDOW
    scale = HEAD_DIM ** -0.5
    c2 = scale * LOG2E
    has_sink = sink is not None
    kc = min(KEY_CHUNK, S)
    hp = KV_GROUP if band else 1

    def body(*refs):
        if has_sink:
            q_ref, k_ref, v_ref, do_ref, o_ref, lse_ref, s_ref, dq_ref, dk_ref, dv_ref, ds_ref, *scr = refs
        else:
            q_ref, k_ref, v_ref, do_ref, o_ref, lse_ref, dq_ref, dk_ref, dv_ref, *scr = refs
        g, qi = pl.program_id(2), pl.program_id(3)

        @pl.when(jnp.logical_and(g == 0, qi == 0))
        def _():
            dk_ref[...] = jnp.zeros_like(dk_ref)
            dv_ref[...] = jnp.zeros_like(dv_ref)
            if not band:
                scr[0][...] = k_ref[...].astype(F32).T.astype(BF16)

        def run(segs):
            dks, dvs = [None] * len(segs), [None] * len(segs)
            for h in range(hp):
                lanes = slice(h * HEAD_DIM, (h + 1) * HEAD_DIM)
                qv = q_ref[:, lanes]
                dof = do_ref[:, lanes]
                dov = dof.astype(BF16)
                lse2 = lse_ref[h]
                delta = _col_to_row(jnp.sum(dof * o_ref[:, lanes], axis=-1, keepdims=True))

                def head(seg, qv=qv, dov=dov):
                    rows = pl.ds(seg[0], seg[1])
                    s = lax.dot_general(k_ref[rows, :], qv, NT, preferred_element_type=F32) * c2
                    if seg[2] is not None:
                        s = jnp.where(seg[2], s, NEG_INF)
                    return s, lax.dot_general(v_ref[rows, :], dov, NT, preferred_element_type=F32)

                dq, dqT = None, None
                nxt = head(segs[0])
                for j, (lo, n, _) in enumerate(segs):
                    rows = pl.ds(lo, n)
                    s, dp = nxt
                    if j + 1 < len(segs):
                        nxt = head(segs[j + 1])
                    p = jnp.exp2(s - lse2)
                    ds = p * (dp - delta)
                    dsb = ds.astype(BF16)
                    dv = jnp.dot(p.astype(BF16), dov, preferred_element_type=F32)
                    dk = jnp.dot(dsb, qv, preferred_element_type=F32)
                    dvs[j] = dv if dvs[j] is None else dvs[j] + dv
                    dks[j] = dk if dks[j] is None else dks[j] + dk
                    if band:
                        part = jnp.dot(ds.T.astype(BF16), k_ref[rows, :], preferred_element_type=F32)
                        dq = part if dq is None else dq + part
                    else:
                        part = jnp.dot(scr[0][:, lo:lo + n], dsb, preferred_element_type=F32)
                        dqT = part if dqT is None else dqT + part
                dq_ref[:, lanes] = ((dq if band else dqT.T) * scale).astype(dq_ref.dtype)
                if has_sink:
                    psk = jnp.exp2(s_ref[h][:, 0:1] * LOG2E - lse2)
                    dsk = jnp.broadcast_to(-jnp.sum(psk * delta, axis=1, keepdims=True), (1, LANE))

                    @pl.when(qi == 0)
                    def _(h=h, dsk=dsk):
                        ds_ref[h] = dsk

                    @pl.when(qi > 0)
                    def _(h=h, dsk=dsk):
                        ds_ref[h] += dsk
            for j, (lo, n, _) in enumerate(segs):
                rows = pl.ds(lo, n)
                dv_ref[rows, :] += dvs[j]
                dk_ref[rows, :] += dks[j] * scale

        @pl.when(qi < n_lq)
        def _():
            if band:
                lo = _band_lo(qi, tq, S)
                run([(S, L, None), (lo, span, _band_mask(qi, tq, lo, span, True))])
            else:
                run([(j * kc, kc, None) for j in range(S // kc)] + [(S, L, None)])

        @pl.when(qi >= n_lq)
        def _():
            run([(S, L, None)])

    ng = KV_GROUP // hp
    q_spec = pl.BlockSpec((None, tq, hp * HEAD_DIM), lambda b, kv, g, i: (b, i, kv * ng + g))
    kv_spec = pl.BlockSpec((None, T, HEAD_DIM), lambda b, kv, g, i: (b, 0, kv))
    lse_spec = pl.BlockSpec((None, hp, 1, tq), lambda b, kv, g, i: (b, kv * ng + g, 0, i))
    in_specs = [q_spec, kv_spec, kv_spec, q_spec, q_spec, lse_spec]
    out_specs = [q_spec, kv_spec, kv_spec]
    out_shape = [jax.ShapeDtypeStruct(q.shape, BF16), jax.ShapeDtypeStruct(k.shape, F32),
                 jax.ShapeDtypeStruct(v.shape, F32)]
    args = [q, k, v, do, o, lse]
    if has_sink:
        in_specs.append(pl.BlockSpec((hp, 1, LANE), lambda b, kv, g, i: (kv * ng + g, 0, 0)))
        args.append(sink)
        out_specs.append(pl.BlockSpec((None, hp, 1, LANE), lambda b, kv, g, i: (b, kv * ng + g, 0, 0)))
        out_shape.append(jax.ShapeDtypeStruct((B, H, 1, LANE), F32))
    res = _call_with_side(
        body, side, name="attn_band_bwd" if band else "attn_dense_bwd", grid=(B, KVH, ng, T // tq),
        in_specs=in_specs, args=args, out_specs=out_specs, out_shape=out_shape,
        scratch_shapes=[] if band else [pltpu.VMEM((HEAD_DIM, T), BF16)],
        dims=("parallel", "parallel", "arbitrary", "arbitrary"))
    return (res[0], res[1], res[2], res[3] if has_sink else None, res[-1] if side is not None else None)


def _prep_f(norm):
    def f(q, kv, *rest):
        q, kv = q.astype(F32), kv.astype(F32)
        kw = kv.shape[1] // 2
        k, v = kv[:, :kw], kv[:, kw:]
        if norm:
            qg, kg, cos, sin = rest
        else:
            (cos, sin), qg, kg = rest, None, None

        def heads(x, g):
            outs = []
            for h in range(x.shape[1] // HEAD_DIM):
                xh = x[:, h * HEAD_DIM:(h + 1) * HEAD_DIM]
                if g is not None:
                    xh = xh * lax.rsqrt(jnp.mean(xh * xh, axis=-1, keepdims=True) + EPS) * g
                outs.append(xh * cos + _swap(xh) * sin)
            return jnp.concatenate(outs, axis=1) if len(outs) > 1 else outs[0]

        return heads(q, qg), heads(k, kg), v

    return f


def _attn_branch(S, tm, band, norm, has_sink, carries=False):
    f = _prep_f(norm)
    name = "band" if band else "dense"

    def prep(q, kv, gains, tabs):
        outs = [(q.shape[2], BF16), (kv.shape[2] // 2, BF16), (kv.shape[2] // 2, BF16)]
        return _row_fwd(f, "prep_" + name + "_fwd", [q, kv], [], list(gains), list(tabs), outs, tm, S)

    def unpack(args):
        q, kv = args[:2]
        rest = list(args[2:])
        gains = [rest.pop(0), rest.pop(0)] if norm else []
        sink = rest.pop(0) if has_sink else None
        pack = rest.pop(0) if carries else None
        return q, kv, gains, sink, pack, rest

    def sink_lanes(sink):
        return None if sink is None else jnp.broadcast_to(sink[:, None, None], (sink.shape[0], 1, LANE))

    def run_fwd(args):
        q, kv, gains, sink, pack, tabs = unpack(args)
        qp, kp, vp = prep(q, kv, gains, tabs)
        side = (pack.astype(BF16), True) if carries else None
        res = _attn_fwd(qp, kp, vp, sink_lanes(sink), S, band, side)
        return ((res[0], res[2]) if carries else res[0]), (args, qp, kp, vp, res[0], res[1])

    @jax.custom_vjp
    def op(*args):
        return run_fwd(args)[0]

    def fwd(*args):
        return run_fwd(args)

    def bwd(res, ct):
        args, qp, kp, vp, o, lse = res
        q, kv, gains, sink, pack, tabs = unpack(args)
        do, side = (ct[0], (ct[1], False)) if carries else (ct, None)
        dqp, dkp, dvp, dsk, recv = _attn_bwd(qp, kp, vp, sink_lanes(sink), do, o, lse, S, band, side)
        dxs, _, dgains = _row_bwd(f, "prep_" + name + "_bwd", [q, kv], [], list(gains), list(tabs),
                                  [dqp, dkp, dvp], tm, S)
        out = list(dxs) + list(dgains)
        if has_sink:
            out.append(jnp.sum(dsk[:, :, 0, 0], axis=0))
        if carries:
            out.append(_sum_slots(recv, "sum_grads"))
        return (*out, *[jnp.zeros_like(t) for t in tabs])

    op.defvjp(fwd, bwd)
    return op


def _final_loss(S, tm):
    def run(X, target, g):
        B, T, D = X.shape
        n_lat_tiles = S // tm

        def lossf(x, gg, tgt):
            y = x * lax.rsqrt(jnp.mean(x * x, axis=-1, keepdims=True) + EPS) * gg
            err = y - tgt
            return 0.5 * jnp.sum(jnp.sum(err * err, axis=-1, keepdims=True), axis=0, keepdims=True) / D

        def body(x_ref, t_ref, g_ref, loss_ref, dx_ref, dg_ref):
            b, i = pl.program_id(0), pl.program_id(1)
            tgt = t_ref[...]
            val, vjp = jax.vjp(lambda x, gg: lossf(x, gg, tgt), x_ref[...], g_ref[...])
            dx, dg = vjp(jnp.ones((1, 1), F32))
            lat = (i < n_lat_tiles).astype(F32)
            dx_ref[...] = dx * lat

            @pl.when(i == 0)
            def _():
                loss_ref[...] = jnp.zeros_like(loss_ref)

            @pl.when(jnp.logical_and(b == 0, i == 0))
            def _():
                dg_ref[...] = jnp.zeros_like(dg_ref)

            loss_ref[...] += jnp.broadcast_to(val * lat, loss_ref.shape)
            dg_ref[...] += dg * lat

        x_spec = pl.BlockSpec((None, tm, D), lambda b, i: (b, i, 0))
        t_spec = pl.BlockSpec((None, tm, D), lambda b, i: (b, jnp.minimum(i, n_lat_tiles - 1), 0))
        g_spec = pl.BlockSpec((1, D), lambda b, i: (0, 0))
        loss, dx, dg = _call(
            body, name="final_loss", grid=(B, T // tm), in_specs=[x_spec, t_spec, g_spec],
            out_specs=[pl.BlockSpec((None, 1, LANE), lambda b, i: (b, 0, 0)), x_spec, g_spec],
            out_shape=[jax.ShapeDtypeStruct((B, 1, LANE), F32), jax.ShapeDtypeStruct(X.shape, F32),
                       jax.ShapeDtypeStruct(g.shape, F32)],
            compiler_params=_params(("arbitrary", "arbitrary")))(X, target, g)
        return jnp.sum(loss[:, 0, 0]), dx, dg

    @jax.custom_vjp
    def op(X, target, g):
        return run(X, target, g)[0]

    def fwd(X, target, g):
        loss, dx, dg = run(X, target, g)
        return loss, (dx, dg, target)

    def bwd(res, ct):
        dx, dg, target = res
        return ct * dx, jnp.zeros_like(target), ct * dg

    op.defvjp(fwd, bwd)
    return op


def _prenorm_f(D):
    def f(x, mp, g):
        y = x * lax.rsqrt(jnp.mean(x * x, axis=-1, keepdims=True) + EPS) * g
        return (y * (1.0 + mp[:, D:2 * D]) + mp[:, 0:D],)
    return f


def _resid_f(D):
    def f(x, y, mp):
        return (x + mp[:, 2 * D:3 * D] * y,)
    return f


def _gate2_f(hf, hr, g):
    return ((hf + hr) * (g * jax.nn.sigmoid(g)),)


def _gate1_f(y, g):
    return (y * (g * jax.nn.sigmoid(g)),)


def _merge_f(D):
    def f(m3, pa, pb, pc):
        return (jax.nn.sigmoid(m3[:, 0:D]) * pa.astype(F32) + jax.nn.sigmoid(m3[:, D:2 * D]) * pb.astype(F32)
                + jax.nn.sigmoid(m3[:, 2 * D:3 * D]) * pc.astype(F32),)
    return f


def _coef_f(D):
    nblk = D // GATE_BLOCK

    def f(v, pv, wm):
        vb = v.astype(BF16)

        def gate(k):
            cols = []
            for j in range(nblk):
                r0 = (k * nblk + j) * GATE_BLOCK
                cols.append(jnp.dot(vb[:, j * GATE_BLOCK:(j + 1) * GATE_BLOCK],
                                    wm[r0:r0 + GATE_BLOCK, :].astype(BF16), preferred_element_type=F32))
            return jnp.concatenate(cols, axis=1)

        outs = []
        for d in range(2):
            r = jax.nn.sigmoid(gate(d) + pv[d:d + 1])
            i = jax.nn.sigmoid(gate(2 + d) + pv[2 + d:3 + d])
            la = r * pv[4 + d:5 + d]
            t = jnp.tanh(la)
            outs += [jnp.exp(la), jnp.sqrt(-2.0 * t / (1.0 - t)) * (i * v)]
        return tuple(outs)

    return f


def _gate_blocks(w):
    per = GATE_BLOCK // LRU_BLOCK_W
    n = w.shape[1]
    w5 = w.reshape(2, n // per, per, LRU_BLOCK_W, LRU_BLOCK_W)
    dense = jnp.einsum("djiab,ik->djiakb", w5, jnp.eye(per, dtype=w.dtype))
    return dense.reshape(2, (n // per) * GATE_BLOCK, GATE_BLOCK)


def _exchange_shape(x, gather):
    return jax.ShapeDtypeStruct((N_DEV,) + x.shape if gather else x.shape, x.dtype)


EXCHANGE_SEMS = [pltpu.SemaphoreType.DMA((N_DEV - 1,)), pltpu.SemaphoreType.DMA((N_DEV - 1,)),
                 pltpu.SemaphoreType.DMA(())]


def _exchange_ops(x_ref, o_ref, send_sems, recv_sems, local_sem, gather):
    mx, my, mc = lax.axis_index("x"), lax.axis_index("y"), lax.axis_index("c")
    me = 4 * mx + 2 * my + mc

    def src(p):
        return x_ref if gather else x_ref.at[p]

    local = pltpu.make_async_copy(src(me), o_ref.at[me], local_sem)
    sends, recvs = [], []
    for k in range(1, N_DEV):
        px = 1 - mx if k & 4 else mx
        py = 1 - my if k & 2 else my
        pc = 1 - mc if k & 1 else mc
        peer = 4 * px + 2 * py + pc
        sends.append(pltpu.make_async_remote_copy(
            src_ref=src(peer), dst_ref=o_ref.at[me], send_sem=send_sems.at[k - 1],
            recv_sem=recv_sems.at[k - 1], device_id=(px, py, pc), device_id_type=pl.DeviceIdType.MESH))
        recvs.append(pltpu.make_async_remote_copy(
            src_ref=src(peer), dst_ref=o_ref.at[peer], send_sem=send_sems.at[k - 1],
            recv_sem=recv_sems.at[k - 1], device_id=(px, py, pc), device_id_type=pl.DeviceIdType.MESH))

    def start():
        local.start()
        for cp in sends:
            cp.start()

    def finish():
        for cp in recvs:
            cp.wait_recv()
        for cp in sends:
            cp.wait_send()
        local.wait()

    return start, finish


def _exchange(x, name, gather):
    def body(x_ref, o_ref, send_sems, recv_sems, local_sem):
        start, finish = _exchange_ops(x_ref, o_ref, send_sems, recv_sems, local_sem, gather)
        start()
        finish()

    hbm = pl.BlockSpec(memory_space=pltpu.HBM)
    return _call(body, name=name, in_specs=[hbm], out_specs=hbm, out_shape=_exchange_shape(x, gather),
                 scratch_shapes=EXCHANGE_SEMS)(x)


def _call_with_side(body, side, *, name, grid, in_specs, args, out_specs, out_shape, scratch_shapes, dims):
    if side is None:
        return _call(body, name=name, grid=grid, in_specs=in_specs, out_specs=out_specs, out_shape=out_shape,
                     scratch_shapes=scratch_shapes, compiler_params=_params(dims))(*args)
    x, gather = side
    n_in, n_out, n_scr = len(in_specs), len(out_specs), len(scratch_shapes)

    def wrapped(*refs):
        ins, x_ref = refs[:n_in], refs[n_in]
        outs, o_ref = refs[n_in + 1:n_in + 1 + n_out], refs[n_in + 1 + n_out]
        scr = refs[n_in + 2 + n_out:n_in + 2 + n_out + n_scr]
        start, finish = _exchange_ops(x_ref, o_ref, *refs[n_in + 2 + n_out + n_scr:], gather)
        ids = [pl.program_id(a) for a in range(len(grid))]
        first = functools.reduce(jnp.logical_and, [i == 0 for i in ids])
        last = functools.reduce(jnp.logical_and, [i == g - 1 for i, g in zip(ids, grid)])
        pl.when(first)(start)
        body(*ins, *outs, *scr)
        pl.when(last)(finish)

    hbm = pl.BlockSpec(memory_space=pltpu.HBM)
    return _call(wrapped, name=name + "_xchg", grid=grid, in_specs=list(in_specs) + [hbm],
                 out_specs=list(out_specs) + [hbm], out_shape=list(out_shape) + [_exchange_shape(x, gather)],
                 scratch_shapes=list(scratch_shapes) + EXCHANGE_SEMS,
                 compiler_params=_params(("arbitrary",) * len(grid)))(*args, x)


def _adamw(gs, w, m, v, name):
    n, R, C = gs.shape
    tr = _pick(R, 256, 16)

    def body(g_ref, w_ref, m_ref, v_ref, go_ref, d_ref, mo_ref, vo_ref):
        g = g_ref[0].astype(F32)
        for p in range(1, n):
            g = g + g_ref[p].astype(F32)
        m2 = ADAM_B1 * m_ref[...] + (1.0 - ADAM_B1) * g
        v2 = ADAM_B2 * v_ref[...] + (1.0 - ADAM_B2) * (g * g)
        m_hat = m2 / (1.0 - ADAM_B1 ** ADAM_STEP)
        v_hat = v2 / (1.0 - ADAM_B2 ** ADAM_STEP)
        go_ref[...] = g
        d_ref[...] = -ADAM_LR * (m_hat / (jnp.sqrt(v_hat) + ADAM_EPS) + ADAM_WD * w_ref[...])
        mo_ref[...] = m2
        vo_ref[...] = v2

    spec = pl.BlockSpec((tr, C), lambda i: (i, 0))
    return _call(body, name=name, grid=(R // tr,),
                 in_specs=[pl.BlockSpec((n, tr, C), lambda i: (0, i, 0)), spec, spec, spec],
                 out_specs=[spec] * 4, out_shape=[jax.ShapeDtypeStruct((R, C), F32)] * 4,
                 compiler_params=_params(("parallel",)))(gs, w, m, v)


def _sum_slots(gs, name):
    n, R, C = gs.shape
    tr = _pick(R, 256, 16)

    def body(g_ref, o_ref):
        g = g_ref[0].astype(F32)
        for p in range(1, n):
            g = g + g_ref[p].astype(F32)
        o_ref[...] = g

    return _call(body, name=name, grid=(R // tr,), in_specs=[pl.BlockSpec((n, tr, C), lambda i: (0, i, 0))],
                 out_specs=pl.BlockSpec((tr, C), lambda i: (i, 0)), out_shape=jax.ShapeDtypeStruct((R, C), F32),
                 compiler_params=_params(("parallel",)))(gs)


def _gather_op(name):
    @jax.custom_vjp
    def op(pack):
        return _exchange(pack.astype(BF16), name, True)

    def fwd(pack):
        return op(pack), None

    def bwd(_, ct):
        return (_sum_slots(_exchange(ct, name + "_transpose", False), "sum_grads"),)

    op.defvjp(fwd, bwd)
    return op


def _rope_tables(S, L):
    P = HEAD_DIM // 4
    rows = S // GRID_W
    row_id = jnp.repeat(jnp.arange(rows), GRID_W)
    col_id = jnp.tile(jnp.arange(GRID_W), rows)
    inv = ROPE_THETA ** (-jnp.arange(P, dtype=F32) / P)
    ar, ac = row_id[:, None] * inv, col_id[:, None] * inv
    cos = jnp.concatenate([jnp.cos(ar), jnp.cos(ar), jnp.cos(ac), jnp.cos(ac)], axis=1)
    sin = jnp.concatenate([-jnp.sin(ar), jnp.sin(ar), -jnp.sin(ac), jnp.sin(ac)], axis=1)
    cos = jnp.concatenate([cos, jnp.ones((L, HEAD_DIM), F32)], axis=0)
    sin = jnp.concatenate([sin, jnp.zeros((L, HEAD_DIM), F32)], axis=0)
    return cos, sin


SQ_TILES = (((2176, 1024, 1024), (2176, 1024, 1024), (1024, 1024, 1088)),) * DEPTH


def _layer_shard_shapes(D, IN):
    return (("w_in", (D, IN // N_DEV), 1), ("w_mod", (D, 3 * D // N_DEV), 1),
            ("w_branch", (3, D // N_DEV, D), 1), ("w_out", (D // N_DEV, D), 0))


def _pack_layer(w, l, D):
    return jnp.concatenate([w[n][l].reshape(-1) for n in BIG_PACK]).reshape(-1, D)


def _unpack_shards(pack, D, IN):
    flat, out, off = pack.reshape(-1), {}, 0
    for n, shape, _ in _layer_shard_shapes(D, IN):
        sz = shape[0] * shape[1] * (shape[2] if len(shape) > 2 else 1)
        out[n] = flat[off:off + sz].reshape(shape)
        off += sz
    return out


def _unpack_gathered(g, D, IN):
    flat, out, off = g.reshape(N_DEV, -1), {}, 0
    for n, shape, axis in _layer_shard_shapes(D, IN):
        sz = shape[0] * shape[1] * (shape[2] if len(shape) > 2 else 1)
        out[n] = _unshard(flat[:, off:off + sz].reshape((N_DEV,) + shape), axis)
        off += sz
    return out


def _loss_fn(packs, p, x, c, ctx, target):
    B, S, D = x.shape
    L = ctx.shape[1]
    T = S + L
    tm = min(L, 256)
    KVW = D // KV_GROUP
    widths = (D, D, D, 2 * KVW, D, D, 2 * KVW, D, 3 * D)
    part_dtypes = (F32, F32, BF16, BF16, F32, BF16, BF16, F32, F32)
    IN = sum(widths)
    cos, sin = _rope_tables(S, L)

    X = jnp.concatenate([x, ctx], axis=1)
    sc, scc = jax.nn.silu(c), jax.nn.silu(p["c_ctx"])
    A = jnp.concatenate([scc[None], sc, jnp.zeros((SUBLANE - 1 - B, D), F32)], axis=0)
    gathered = _gather_op("gather_layer0")(packs[0])
    for l in range(DEPTH):
        big = _unpack_gathered(gathered, D, IN)
        mod = _matmul("mm_mod")(A, big["w_mod"]) + p["b_mod"][l]
        modp = jnp.stack([jnp.broadcast_to(mod[0], (B, 3 * D)), mod[1:1 + B]], axis=1)[:, :, None, :]
        (h,) = _rowwise(_prenorm_f(D), "prenorm", [(D, BF16)], tm, S, 1, 1, 1)(X, modp, p["norm_g"][l][None])
        parts = _in_proj(widths, part_dtypes)(h.reshape(B * T, D), big["w_in"])
        uA, gA, qB, kvB, gB, qC, kvC, gC, m3 = [t.reshape(B, T, -1) for t in parts]
        u = _conv_op(S)(uA, p["conv_w"][l], p["conv_b"][l][None])
        nsp = -LRU_C * jax.nn.softplus(-p["lru_lambda"][l])
        pv = jnp.concatenate([p["lru_ba"][l], p["lru_bx"][l], nsp, jnp.zeros((2, D), F32)], axis=0)
        wm = jnp.concatenate([_gate_blocks(p["lru_wa"][l]), _gate_blocks(p["lru_wx"][l])], axis=0)
        wm = wm.reshape(-1, GATE_BLOCK)
        af, bf, ar, br = _rowwise(_coef_f(D), "lru_coef", [(D, F32)] * 4, tm, S, 1, 0, 2)(u, pv, wm)
        hf = _scan_op(S, False)(af, bf)
        hr = _scan_op(S, True)(ar, br)
        (zA,) = _rowwise(_gate2_f, "gate_a", [(D, BF16)], tm, S, 3, 0, 0)(hf, hr, gA)
        yB = _attn_branch(S, tm, True, False, True)(qB, kvB, p["attn_sink"][l], cos, sin)
        (zB,) = _rowwise(_gate1_f, "gate_b", [(D, BF16)], tm, S, 2, 0, 0)(yB, gB)
        qkv_c = (qC, kvC, p["q_norm_g"][l][None], p["k_norm_g"][l][None])
        if l + 1 < DEPTH:
            yC, gathered = _attn_branch(S, tm, False, True, False, True)(*qkv_c, packs[l + 1], cos, sin)
        else:
            yC = _attn_branch(S, tm, False, True, False)(*qkv_c, cos, sin)
        (zC,) = _rowwise(_gate1_f, "gate_c", [(D, BF16)], tm, S, 2, 0, 0)(yC, gC)
        pr = [_matmul("mm_branch", *SQ_TILES[l], out_dtype=BF16)(z.reshape(B * T, D), big["w_branch"][n])
              .reshape(B, T, D) for n, z in enumerate((zA, zB, zC))]
        (mg,) = _rowwise(_merge_f(D), "merge", [(D, BF16)], tm, S, 4, 0, 0)(m3, *pr)
        y = _matmul("mm_out", *SQ_TILES[l])(mg.reshape(B * T, D), big["w_out"]).reshape(B, T, D)
        (X,) = _rowwise(_resid_f(D), "resid", [(D, F32)], tm, S, 2, 1, 0)(X, y, modp)
    return _final_loss(S, tm)(X, target, p["final_g"][None])


def _shard_axis(name):
    return {"w_mod": 2, "w_in": 2, "conv_w": 2, "lru_ba": 2, "lru_bx": 2, "lru_lambda": 2,
            "w_branch": 2, "w_out": 1}.get(name)


def _unshard(g, axis):
    full = jnp.moveaxis(g, 0, axis)
    shape = list(full.shape)
    shape[axis:axis + 2] = [shape[axis] * shape[axis + 1]]
    return full.reshape(shape)


def _reshard(full, axis):
    shape = list(full.shape)
    shape[axis:axis + 1] = [N_DEV, shape[axis] // N_DEV]
    return jnp.moveaxis(full.reshape(shape), axis, 0)


def _pad_to(v, n):
    return jnp.concatenate([v, jnp.zeros((n - v.shape[0],), v.dtype)]) if n > v.shape[0] else v


def _step(x, c, ctx, target, w, m, v):
    D = x.shape[2]
    IN = w["w_in"].shape[2] * N_DEV
    full = {n: w[n] for n in REPLICATED}
    small_local = jnp.concatenate([w[n].reshape(-1) for n in SMALL_SHARDED])
    small_all = _exchange(small_local.reshape(-1, LANE), "gather_small", True).reshape(N_DEV, -1)
    off = 0
    for n in SMALL_SHARDED:
        sz = w[n].size
        full[n] = _unshard(small_all[:, off:off + sz].reshape((N_DEV,) + w[n].shape), _shard_axis(n))
        off += sz
    packs = [_pack_layer(w, l, D) for l in range(DEPTH)]

    loss, (gpacks, gp, gx) = jax.value_and_grad(_loss_fn, argnums=(0, 1, 2))(packs, full, x, c, ctx, target)
    loss = lax.psum(loss, AXES)

    out = {}
    gshards = [_unpack_shards(g, D, IN) for g in gpacks]
    for n in BIG_PACK:
        g = jnp.stack([gs[n] for gs in gshards])
        C = w[n].shape[-1]
        res = _adamw(g.reshape(1, -1, C), w[n].reshape(-1, C), m[n].reshape(-1, C), v[n].reshape(-1, C),
                     "adamw_" + n)
        out[n] = [r.reshape(w[n].shape) for r in res]

    rep = jnp.concatenate([gp[n].reshape(-1) for n in REPLICATED])
    n_rep = rep.shape[0]
    chunk = -(-n_rep // (N_DEV * LANE)) * LANE
    rep = _pad_to(rep, N_DEV * chunk).reshape(N_DEV, chunk)
    shards = jnp.concatenate([_reshard(gp[n], _shard_axis(n)).reshape(N_DEV, -1) for n in SMALL_SHARDED], axis=1)
    n_sh = shards.shape[1]
    recv = _exchange(jnp.concatenate([rep, shards], axis=1).reshape(N_DEV, -1, LANE), "scatter_small", False)
    wl =jnp.concatenate([w[n].reshape(-1) for n in SMALL_SHARDED])
    ml = jnp.concatenate([m[n].reshape(-1) for n in SMALL_SHARDED])
    vl = jnp.concatenate([v[n].reshape(-1) for n in SMALL_SHARDED])
    rows = recv.shape[1]
    rrows = chunk // LANE
    g_sh, d_sh, m_sh, v_sh = _adamw(recv[:, rrows:], wl.reshape(-1, LANE), ml.reshape(-1, LANE),
                                    vl.reshape(-1, LANE), "adamw_small_sharded")
    g_rep8 = _sum_slots(recv[:, :rrows], "sum_replicated")
    g_rep = _exchange(g_rep8, "gather_replicated", True).reshape(-1)
    wr = _pad_to(jnp.concatenate([w[n].reshape(-1) for n in REPLICATED]), N_DEV * chunk)
    mr = _pad_to(jnp.concatenate([m[n].reshape(-1) for n in REPLICATED]), N_DEV * chunk)
    vr = _pad_to(jnp.concatenate([v[n].reshape(-1) for n in REPLICATED]), N_DEV * chunk)
    res_rep = _adamw(g_rep.reshape(1, -1, LANE), wr.reshape(-1, LANE), mr.reshape(-1, LANE), vr.reshape(-1, LANE),
                     "adamw_replicated")
    off = 0
    for n in REPLICATED:
        sz = w[n].size
        out[n] = [r.reshape(-1)[off:off + sz].reshape(w[n].shape) for r in res_rep]
        off += sz
    off = 0
    for n in SMALL_SHARDED:
        sz = w[n].size
        out[n] = [r.reshape(-1)[off:off + sz].reshape(w[n].shape) for r in (g_sh, d_sh, m_sh, v_sh)]
        off += sz
    assert off == n_sh and rows == rrows + n_sh // LANE
    return (loss, gx, *[out[n][0] for n in WEIGHTS], *[out[n][1] for n in WEIGHTS],
            *[out[n][2] for n in WEIGHTS], *[out[n][3] for n in WEIGHTS])


def kernel(x, c, ctx, c_ctx, norm_g, w_mod, b_mod, w_in, conv_w, conv_b, lru_wa, lru_ba, lru_wx, lru_bx, lru_lambda, attn_sink, q_norm_g, k_norm_g, w_branch, w_out, final_g, loss_target, m_c_ctx, m_norm_g, m_w_mod, m_b_mod, m_w_in, m_conv_w, m_conv_b, m_lru_wa, m_lru_ba, m_lru_wx, m_lru_bx, m_lru_lambda, m_attn_sink, m_q_norm_g, m_k_norm_g, m_w_branch, m_w_out, m_final_g, v_c_ctx, v_norm_g, v_w_mod, v_b_mod, v_w_in, v_conv_w, v_conv_b, v_lru_wa, v_lru_ba, v_lru_wx, v_lru_bx, v_lru_lambda, v_attn_sink, v_q_norm_g, v_k_norm_g, v_w_branch, v_w_out, v_final_g):
    w = dict(zip(WEIGHTS, (c_ctx, norm_g, w_mod, b_mod, w_in, conv_w, conv_b, lru_wa, lru_ba, lru_wx, lru_bx,
                           lru_lambda, attn_sink, q_norm_g, k_norm_g, w_branch, w_out, final_g)))
    m = dict(zip(WEIGHTS, (m_c_ctx, m_norm_g, m_w_mod, m_b_mod, m_w_in, m_conv_w, m_conv_b, m_lru_wa, m_lru_ba,
                           m_lru_wx, m_lru_bx, m_lru_lambda, m_attn_sink, m_q_norm_g, m_k_norm_g, m_w_branch,
                           m_w_out, m_final_g)))
    v = dict(zip(WEIGHTS, (v_c_ctx, v_norm_g, v_w_mod, v_b_mod, v_w_in, v_conv_w, v_conv_b, v_lru_wa, v_lru_ba,
                           v_lru_wx, v_lru_bx, v_lru_lambda, v_attn_sink, v_q_norm_g, v_k_norm_g, v_w_branch,
                           v_w_out, v_final_g)))
    return _step(x, c, ctx, loss_target, w, m, v)
```

```python
import functools
import math

import jax
import jax.numpy as jnp
from jax import lax
from jax.experimental import pallas as pl
from jax.experimental.pallas import tpu as pltpu

F32 = jnp.float32
BF16 = jnp.bfloat16

AXES = ("x", "y", "c")
N_DEV = 8
DEPTH = 4
HEAD_DIM = 128
GRID_W = 64
WINDOW = 128
LRU_BLOCK_W = 64
GATE_BLOCK = 256
LRU_C = 8.0
ROPE_THETA = 10000.0
EPS = 1e-6
NEG_INF = -1e30
KV_GROUP = 4
LANE = 128
SUBLANE = 8
VMEM_LIMIT = 56 * 1024 * 1024

ADAM_LR = 0.001
ADAM_B1 = 0.9
ADAM_B2 = 0.999
ADAM_EPS = 1e-08
ADAM_WD = 0.01
ADAM_STEP = 10

WEIGHTS = ("c_ctx", "norm_g", "w_mod", "b_mod", "w_in", "conv_w", "conv_b", "lru_wa", "lru_ba", "lru_wx",
           "lru_bx", "lru_lambda", "attn_sink", "q_norm_g", "k_norm_g", "w_branch", "w_out", "final_g")
BIG_PACK = ("w_in", "w_mod", "w_branch", "w_out")
SMALL_SHARDED = ("conv_w", "lru_ba", "lru_bx", "lru_lambda")
REPLICATED = ("c_ctx", "norm_g", "b_mod", "conv_b", "lru_wa", "lru_wx", "attn_sink", "q_norm_g", "k_norm_g",
              "final_g")


def _call(body, **kw):
    return pl.pallas_call(body, **kw)


def _params(dims=None, vmem=VMEM_LIMIT):
    return pltpu.CompilerParams(dimension_semantics=dims, vmem_limit_bytes=vmem)


def _pick(n, target, mult):
    for t in range(min(n, target), 0, -1):
        if n % t == 0 and t % mult == 0:
            return t
    return n


def _mm(a, b, *, name, ta=False, tb=False, out_dtype=F32, tm=512, tn=1024, tk=1024):
    M, K = (a.shape[1], a.shape[0]) if ta else a.shape
    N = b.shape[0] if tb else b.shape[1]
    assert (b.shape[1] if tb else b.shape[0]) == K
    tm = _pick(M, tm, LANE if ta else 16)
    tn = _pick(N, tn, LANE)
    tk = _pick(K, tk, 16 if ta and not tb else LANE)
    nk = K // tk
    dn = (((0 if ta else 1,), (1 if tb else 0,)), ((), ()))

    def body(a_ref, b_ref, o_ref, *acc):
        r = lax.dot_general(a_ref[...].astype(BF16), b_ref[...].astype(BF16), dn,
                            preferred_element_type=F32)
        if nk == 1:
            o_ref[...] = r.astype(o_ref.dtype)
        else:
            k = pl.program_id(2)

            @pl.when(k == 0)
            def _():
                acc[0][...] = r

            @pl.when(k > 0)
            def _():
                acc[0][...] += r

            @pl.when(k == nk - 1)
            def _():
                o_ref[...] = acc[0][...].astype(o_ref.dtype)

    a_spec = (pl.BlockSpec((tk, tm), lambda i, j, k: (k, i)) if ta
              else pl.BlockSpec((tm, tk), lambda i, j, k: (i, k)))
    b_spec = (pl.BlockSpec((tn, tk), lambda i, j, k: (j, k)) if tb
              else pl.BlockSpec((tk, tn), lambda i, j, k: (k, j)))
    return _call(
        body, name=name, grid=(M // tm, N // tn, nk),
        in_specs=[a_spec, b_spec],
        out_specs=pl.BlockSpec((tm, tn), lambda i, j, k: (i, j)),
        out_shape=jax.ShapeDtypeStruct((M, N), out_dtype),
        scratch_shapes=[pltpu.VMEM((tm, tn), F32)] if nk > 1 else [],
        compiler_params=_params(("parallel", "parallel", "arbitrary")),
    )(a, b)


def _matmul(name, t_fwd=(512, 1024, 1024), t_da=(512, 1024, 1024), t_dw=(1024, 1024, 512), out_dtype=F32):
    def tiles(t):
        return dict(tm=t[0], tn=t[1], tk=t[2])

    @jax.custom_vjp
    def f(a, w):
        return _mm(a, w, name=name + "_fwd", out_dtype=out_dtype, **tiles(t_fwd))

    def fwd(a, w):
        return f(a, w), (a, w)

    def bwd(res, g):
        a, w = res
        da = _mm(g, w, name=name + "_da", tb=True, out_dtype=a.dtype, **tiles(t_da))
        dw = _mm(a, g, name=name + "_dw", ta=True, out_dtype=w.dtype, **tiles(t_dw))
        return da, dw

    f.defvjp(fwd, bwd)
    return f


def _mm_parts_nt(parts, w, col0, init, out_dtype, *, name, tm=1088, tk=512):
    M = parts[0].shape[0]
    D = w.shape[0]
    tm = _pick(M, tm, 16)
    tk = math.gcd(tk, *[a.shape[1] for a in parts])
    starts, n = [], 0
    for a in parts:
        assert a.shape[1] % tk == 0
        starts.append(n)
        n += a.shape[1] // tk
    k0 = col0 // tk
    has_init = init is not None

    def body(*refs):
        a_refs, w_ref = refs[:len(parts)], refs[len(parts)]
        o_ref, acc = refs[-2], refs[-1]
        k = pl.program_id(1)

        @pl.when(k == 0)
        def _():
            acc[...] = refs[len(parts) + 1][...].astype(F32) if has_init else jnp.zeros_like(acc)

        for a_ref, s, a in zip(a_refs, starts, parts):
            @pl.when(jnp.logical_and(k >= s, k < s + a.shape[1] // tk))
            def _(a_ref=a_ref):
                acc[...] += lax.dot_general(a_ref[...].astype(BF16), w_ref[...], NT, preferred_element_type=F32)

        @pl.when(k == n - 1)
        def _():
            o_ref[...] = acc[...].astype(o_ref.dtype)

    in_specs = [pl.BlockSpec((tm, tk), lambda i, k, s=s, c=a.shape[1] // tk: (i, jnp.clip(k - s, 0, c - 1)))
                for a, s in zip(parts, starts)]
    in_specs.append(pl.BlockSpec((D, tk), lambda i, k: (0, k0 + k)))
    o_spec = pl.BlockSpec((tm, D), lambda i, k: (i, 0))
    args = list(parts) + [w]
    if has_init:
        in_specs.append(o_spec)
        args.append(init)
    return _call(body, name=name, grid=(M // tm, n), in_specs=in_specs, out_specs=o_spec,
                 out_shape=jax.ShapeDtypeStruct((M, D), out_dtype),
                 scratch_shapes=[pltpu.VMEM((tm, D), F32)],
                 compiler_params=_params(("parallel", "arbitrary")))(*args)


def _in_proj(widths, dtypes, t_fwd=(2176, 1024, 1024), t_dw=(1024, 1024, 2176), group=5):
    offs = [0]
    for wd in widths:
        offs.append(offs[-1] + wd)

    def cols(w, i):
        return w[:, offs[i]:offs[i + 1]]

    @jax.custom_vjp
    def op(h, w):
        return tuple(_mm(h, cols(w, i), name="mm_in_fwd", out_dtype=dtypes[i], tm=t_fwd[0], tn=t_fwd[1],
                         tk=t_fwd[2]) for i in range(len(widths)))

    def fwd(h, w):
        return op(h, w), (h, w)

    def bwd(res, gs):
        h, w = res
        da = None
        for i0 in range(0, len(widths), group):
            last = i0 + group >= len(widths)
            da = _mm_parts_nt(list(gs[i0:i0 + group]), w, offs[i0], da, h.dtype if last else F32, name="mm_in_da")
        dw = jnp.concatenate([_mm(h, g, name="mm_in_dw", ta=True, out_dtype=w.dtype, tm=t_dw[0], tn=t_dw[1],
                                  tk=t_dw[2]) for g in gs], axis=1)
        return da, dw

    op.defvjp(fwd, bwd)
    return op


def _row_specs(xs, tps, gps, rts, tm, n_lat_tiles):
    x_specs = [pl.BlockSpec((None, tm, x.shape[2]), lambda b, i: (b, i, 0)) for x in xs]
    tp_specs = [pl.BlockSpec((None, None, 1, p.shape[3]),
                             lambda b, i: (b, (i < n_lat_tiles).astype(jnp.int32), 0, 0)) for p in tps]
    gp_specs = [pl.BlockSpec(p.shape, lambda b, i: (0, 0)) for p in gps]
    rt_specs = [pl.BlockSpec((tm, t.shape[1]), lambda b, i: (i, 0)) for t in rts]
    return x_specs, tp_specs, gp_specs, rt_specs


def _row_fwd(f, name, xs, tps, gps, rts, outs, tm, n_lat):
    B, T, _ = xs[0].shape
    n_in = len(xs) + len(tps) + len(gps) + len(rts)

    def body(*refs):
        vals = f(*[r[...] for r in refs[:n_in]])
        for o, v in zip(refs[n_in:], vals):
            o[...] = v.astype(o.dtype)

    x_specs, tp_specs, gp_specs, rt_specs = _row_specs(xs, tps, gps, rts, tm, n_lat // tm)
    res = _call(
        body, name=name, grid=(B, T // tm),
        in_specs=x_specs + tp_specs + gp_specs + rt_specs,
        out_specs=[pl.BlockSpec((None, tm, w), lambda b, i: (b, i, 0)) for w, _ in outs],
        out_shape=[jax.ShapeDtypeStruct((B, T, w), dt) for w, dt in outs],
        compiler_params=_params(("parallel", "parallel")),
    )(*xs, *tps, *gps, *rts)
    return list(res)


def _row_bwd(f, name, xs, tps, gps, rts, douts, tm, n_lat):
    B, T, _ = xs[0].shape
    nx, ntp, ngp, nd = len(xs), len(tps), len(gps), len(douts)
    n_lat_tiles = n_lat // tm
    n_diff = nx + ntp + ngp
    n_in = n_diff + len(rts)

    def body(*refs):
        diff = [r[...] for r in refs[:n_diff]]
        tabs = [r[...] for r in refs[n_diff:n_in]]
        dos = [r[...] for r in refs[n_in:n_in + nd]]
        o_refs = refs[n_in + nd:]
        prim, vjp = jax.vjp(lambda *d: tuple(f(*d, *tabs)), *diff)
        grads = vjp(tuple(d.astype(p.dtype) for d, p in zip(dos, prim)))
        b, i = pl.program_id(0), pl.program_id(1)
        for k in range(nx):
            o_refs[k][...] = grads[k].astype(o_refs[k].dtype)
        first_tp = jnp.logical_or(i == 0, i == n_lat_tiles)
        first_gp = jnp.logical_and(b == 0, i == 0)
        for k in range(nx, n_diff):
            first = first_tp if k < nx + ntp else first_gp

            @pl.when(first)
            def _(k=k):
                o_refs[k][...] = grads[k]

            @pl.when(jnp.logical_not(first))
            def _(k=k):
                o_refs[k][...] += grads[k]

    x_specs, tp_specs, gp_specs, rt_specs = _row_specs(xs, tps, gps, rts, tm, n_lat_tiles)
    d_specs = [pl.BlockSpec((None, tm, d.shape[2]), lambda b, i: (b, i, 0)) for d in douts]
    res = _call(
        body, name=name, grid=(B, T // tm),
        in_specs=x_specs + tp_specs + gp_specs + rt_specs + d_specs,
        out_specs=x_specs + tp_specs + gp_specs,
        out_shape=[jax.ShapeDtypeStruct(a.shape, a.dtype) for a in xs]
        + [jax.ShapeDtypeStruct(a.shape, F32) for a in (*tps, *gps)],
        compiler_params=_params(("arbitrary", "arbitrary")),
    )(*xs, *tps, *gps, *rts, *douts)
    res = list(res)
    return res[:nx], res[nx:nx + ntp], res[nx + ntp:]


def _rowwise(f, name, outs, tm, n_lat, n_x, n_tp, n_gp):
    def split(args):
        return (args[:n_x], args[n_x:n_x + n_tp], args[n_x + n_tp:n_x + n_tp + n_gp],
                args[n_x + n_tp + n_gp:])

    @jax.custom_vjp
    def op(*args):
        xs, tps, gps, rts = split(args)
        return tuple(_row_fwd(f, name + "_fwd", xs, tps, gps, rts, outs, tm, n_lat))

    def fwd(*args):
        return op(*args), args

    def bwd(args, g):
        xs, tps, gps, rts = split(args)
        dxs, dtps, dgps = _row_bwd(f, name + "_bwd", xs, tps, gps, rts, list(g), tm, n_lat)
        return (*dxs, *dtps, *dgps, *[jnp.zeros_like(t) for t in rts])

    op.defvjp(fwd, bwd)
    return op


def _shift_impl(u, k):
    n = u.shape[0]
    r = pltpu.roll(u, k % n, axis=0)
    row = lax.broadcasted_iota(jnp.int32, u.shape, 0)
    valid = (row >= k) if k > 0 else (row < n + k)
    return jnp.where(valid, r, 0.0)


@functools.partial(jax.custom_vjp, nondiff_argnums=(1,))
def _shift(u, k):
    return _shift_impl(u, k)


_shift.defvjp(lambda u, k: (_shift_impl(u, k), None), lambda k, _, g: (_shift_impl(g, -k),))


def _swap_impl(x):
    lane = lax.broadcasted_iota(jnp.int32, x.shape, 1)
    q = HEAD_DIM // 4
    return jnp.where((lane % (2 * q)) < q, pltpu.roll(x, HEAD_DIM - q, axis=1), pltpu.roll(x, q, axis=1))


@jax.custom_vjp
def _swap(x):
    return _swap_impl(x)


_swap.defvjp(lambda x: (_swap_impl(x), None), lambda _, g: (_swap_impl(g),))


def _conv_f(ul, uc, cw, cb):
    def conv(u):
        return (_shift(u, 2) * cw[0:1] + _shift(u, 1) * cw[1:2] + u * cw[2:3] + _shift(u, -1) * cw[3:4] + cb)
    return conv(ul), conv(uc)


def _conv_specs(B, T, D):
    u_spec = pl.BlockSpec((None, T, LANE), lambda j, b: (b, 0, j))
    cw_spec = pl.BlockSpec((4, LANE), lambda j, b: (0, j))
    cb_spec = pl.BlockSpec((1, LANE), lambda j, b: (0, j))
    return u_spec, cw_spec, cb_spec


def _conv_fwd(u, cw, cb, S):
    B, T, D = u.shape

    def body(u_ref, cw_ref, cb_ref, o_ref):
        vl, vc = _conv_f(u_ref[0:S, :], u_ref[S:T, :], cw_ref[...], cb_ref[...])
        o_ref[0:S, :] = vl
        o_ref[S:T, :] = vc

    u_spec, cw_spec, cb_spec = _conv_specs(B, T, D)
    return _call(body, name="conv_fwd", grid=(D // LANE, B), in_specs=[u_spec, cw_spec, cb_spec],
                 out_specs=u_spec, out_shape=jax.ShapeDtypeStruct(u.shape, F32),
                 compiler_params=_params(("parallel", "parallel")))(u, cw, cb)


def _conv_bwd(u, cw, cb, dv, S):
    B, T, D = u.shape

    def body(u_ref, cw_ref, cb_ref, dv_ref, du_ref, dcw_ref, dcb_ref):
        _, vjp = jax.vjp(_conv_f, u_ref[0:S, :], u_ref[S:T, :], cw_ref[...], cb_ref[...])
        dul, duc, dcw, dcb = vjp((dv_ref[0:S, :], dv_ref[S:T, :]))
        du_ref[0:S, :] = dul
        du_ref[S:T, :] = duc
        first = pl.program_id(1) == 0

        @pl.when(first)
        def _():
            dcw_ref[...] = dcw
            dcb_ref[...] = dcb

        @pl.when(jnp.logical_not(first))
        def _():
            dcw_ref[...] += dcw
            dcb_ref[...] += dcb

    u_spec, cw_spec, cb_spec = _conv_specs(B, T, D)
    return _call(body, name="conv_bwd", grid=(D // LANE, B), in_specs=[u_spec, cw_spec, cb_spec, u_spec],
                 out_specs=[u_spec, cw_spec, cb_spec],
                 out_shape=[jax.ShapeDtypeStruct(u.shape, F32), jax.ShapeDtypeStruct(cw.shape, F32),
                            jax.ShapeDtypeStruct(cb.shape, F32)],
                 compiler_params=_params(("parallel", "arbitrary")))(u, cw, cb, dv)


def _conv_op(S):
    @jax.custom_vjp
    def op(u, cw, cb):
        return _conv_fwd(u, cw, cb, S)

    def fwd(u, cw, cb):
        return op(u, cw, cb), (u, cw, cb)

    def bwd(res, g):
        u, cw, cb = res
        return tuple(_conv_bwd(u, cw, cb, g, S))

    op.defvjp(fwd, bwd)
    return op


SCAN_UNROLL = 4


def _group_scan(A, Bv, asc):
    row = lax.broadcasted_iota(jnp.int32, A.shape, 0)
    for s in (1, 2, 4):
        sh = s if asc else SUBLANE - s
        valid = (row >= s) if asc else (row < SUBLANE - s)
        A_sh = pltpu.roll(A, sh, axis=0)
        B_sh = pltpu.roll(Bv, sh, axis=0)
        Bv = jnp.where(valid, A * B_sh, 0.0) + Bv
        A = jnp.where(valid, A * A_sh, A)
    return A, Bv


def _chain_step(A, Bv, carry, asc):
    row = lax.broadcasted_iota(jnp.int32, A.shape, 0)
    A2, B2 = _group_scan(A, Bv, asc)
    h = A2 * carry + B2
    if asc:
        prev = jnp.where(row == 0, carry, pltpu.roll(h, 1, axis=0))
        return h, prev, h[SUBLANE - 1:SUBLANE, :]
    prev = jnp.where(row == SUBLANE - 1, carry, pltpu.roll(h, SUBLANE - 1, axis=0))
    return h, prev, h[0:1, :]


def _chain_loop(segments, step):
    carry = jnp.zeros((1, LANE), F32)
    for lo, hi, asc in segments:
        span = SUBLANE * SCAN_UNROLL
        assert (hi - lo) % span == 0

        def it(t, carry, lo=lo, hi=hi, asc=asc, span=span):
            base = lo + t * span if asc else hi - (t + 1) * span
            order = range(SCAN_UNROLL) if asc else reversed(range(SCAN_UNROLL))
            for j in order:
                carry = step(pl.multiple_of(base + SUBLANE * j, SUBLANE), carry, asc)
            return carry

        carry = lax.fori_loop(0, (hi - lo) // span, it, carry)
    return carry


def _scan_specs(T):
    return pl.BlockSpec((None, T, LANE), lambda j, b: (b, 0, j))


def _scan_fwd(a, b, S, reverse):
    B, T, D = a.shape
    asc = not reverse
    segments = [(S, T, asc), (0, S, asc)]

    def body(a_ref, b_ref, h_ref, hp_ref):
        def step(r0, carry, asc):
            rows = pl.ds(r0, SUBLANE)
            h, prev, carry = _chain_step(a_ref[rows, :], b_ref[rows, :], carry, asc)
            h_ref[rows, :] = h
            hp_ref[rows, :] = prev
            return carry
        _chain_loop(segments, step)

    spec = _scan_specs(T)
    return _call(body, name="scan_rev_fwd" if reverse else "scan_fwd_fwd", grid=(D // LANE, B),
                 in_specs=[spec, spec], out_specs=[spec, spec],
                 out_shape=[jax.ShapeDtypeStruct(a.shape, F32)] * 2,
                 compiler_params=_params(("parallel", "parallel")))(a, b)


def _scan_bwd(a, hp, dy, S, reverse):
    B, T, D = a.shape
    asc = reverse
    segments = [(0, S, asc), (S, T, asc)]

    def body(a_ref, hp_ref, dy_ref, da_ref, db_ref):
        def step(r0, carry, asc):
            rows = pl.ds(r0, SUBLANE)
            A, dy = a_ref[rows, :], dy_ref[rows, :]
            _, s_prev, carry = _chain_step(A, A * dy, carry, asc)
            g = dy + s_prev
            db_ref[rows, :] = g
            da_ref[rows, :] = g * hp_ref[rows, :]
            return carry
        _chain_loop(segments, step)

    spec = _scan_specs(T)
    return _call(body, name="scan_rev_bwd" if reverse else "scan_fwd_bwd", grid=(D // LANE, B),
                 in_specs=[spec, spec, spec], out_specs=[spec, spec],
                 out_shape=[jax.ShapeDtypeStruct(a.shape, F32)] * 2,
                 compiler_params=_params(("parallel", "parallel")))(a, hp, dy)


def _scan_op(S, reverse):
    @jax.custom_vjp
    def op(a, b):
        return _scan_fwd(a, b, S, reverse)[0]

    def fwd(a, b):
        h, hp = _scan_fwd(a, b, S, reverse)
        return h, (a, hp)

    def bwd(res, g):
        a, hp = res
        return tuple(_scan_bwd(a, hp, g, S, reverse))

    op.defvjp(fwd, bwd)
    return op


def _band_lo(qi, tq, S):
    span = tq + 2 * WINDOW
    return pl.multiple_of(jnp.clip(qi * tq - WINDOW, 0, S - span), LANE)


def _band_mask(qi, tq, lo, span, transposed):
    shape = (span, tq) if transposed else (tq, span)
    qpos = qi * tq + lax.broadcasted_iota(jnp.int32, shape, 1 if transposed else 0)
    kpos = lo + lax.broadcasted_iota(jnp.int32, shape, 0 if transposed else 1)
    return jnp.abs(kpos - qpos) <= WINDOW


NT = (((1,), (1,)), ((), ()))
LOG2E = 1.4426950408889634
KEY_CHUNK = 1024


def _col_to_row(c):
    return jnp.broadcast_to(c, (c.shape[0], LANE)).T[0:1, :]


def _attn_fwd(q, k, v, sink, gate, S, band, side=None):
    B, T, HD = q.shape
    H = HD // HEAD_DIM
    L = T - S
    tq = L
    n_lq = S // tq
    span = tq + 2 * WINDOW
    c2 = HEAD_DIM ** -0.5 * LOG2E
    has_sink = sink is not None
    kc = min(KEY_CHUNK, S)
    hp = KV_GROUP if band else 1

    def body(*refs):
        if has_sink:
            q_ref, k_ref, v_ref, g_ref, s_ref, z_ref, o_ref, lse_ref = refs
        else:
            q_ref, k_ref, v_ref, g_ref, z_ref, o_ref, lse_ref = refs
        qi = pl.program_id(2)

        def online(qv, sk2, segs):
            def scores(seg):
                s = lax.dot_general(qv, k_ref[seg[0], :], NT, preferred_element_type=F32) * c2
                return s if seg[1] is None else jnp.where(seg[1], s, NEG_INF)

            m = l = acc = None
            s_next = scores(segs[0])
            for j, (rows, _) in enumerate(segs):
                s = s_next
                if j + 1 < len(segs):
                    s_next = scores(segs[j + 1])
                ms = jnp.max(s, axis=-1, keepdims=True)
                if m is None:
                    m = ms if sk2 is None else jnp.maximum(ms, sk2)
                    p = jnp.exp2(s - m)
                    l = jnp.sum(p, axis=-1, keepdims=True)
                    if sk2 is not None:
                        l = l + jnp.exp2(sk2 - m)
                    acc = jnp.dot(p.astype(BF16), v_ref[rows, :], preferred_element_type=F32)
                else:
                    m_new = jnp.maximum(m, ms)
                    alpha = jnp.exp2(m - m_new)
                    p = jnp.exp2(s - m_new)
                    l = alpha * l + jnp.sum(p, axis=-1, keepdims=True)
                    acc = alpha * acc + jnp.dot(p.astype(BF16), v_ref[rows, :], preferred_element_type=F32)
                    m = m_new
            return acc, m, l

        def joint(qv, sk2, segs):
            ss = []
            for rows, mask in segs:
                s = lax.dot_general(qv, k_ref[rows, :], NT, preferred_element_type=F32) * c2
                ss.append(s if mask is None else jnp.where(mask, s, NEG_INF))
            m = functools.reduce(jnp.maximum, [jnp.max(s, axis=-1, keepdims=True) for s in ss])
            if sk2 is not None:
                m = jnp.maximum(m, sk2)
            ps = [jnp.exp2(s - m) for s in ss]
            l = functools.reduce(jnp.add, [jnp.sum(p, axis=-1, keepdims=True) for p in ps])
            if sk2 is not None:
                l = l + jnp.exp2(sk2 - m)
            acc = functools.reduce(jnp.add, [jnp.dot(p.astype(BF16), v_ref[rows, :], preferred_element_type=F32)
                                             for p, (rows, _) in zip(ps, segs)])
            return acc, m, l

        def run(segs, softmax):
            for g in range(hp):
                lanes = slice(g * HEAD_DIM, (g + 1) * HEAD_DIM)
                sk2 = s_ref[g][:, 0:1] * LOG2E if has_sink else None
                acc, m, l = softmax(q_ref[:, lanes], sk2, segs)
                o = acc * (1.0 / l)
                gv = g_ref[:, lanes]
                o_ref[:, lanes] = o
                z_ref[:, lanes] = (o * (gv * jax.nn.sigmoid(gv))).astype(z_ref.dtype)
                lse_ref[g] = _col_to_row(m + jnp.log2(l))

        ctx_rows = pl.ds(S, L)

        @pl.when(qi < n_lq)
        def _():
            if band:
                lo = _band_lo(qi, tq, S)
                run([(ctx_rows, None), (pl.ds(lo, span), _band_mask(qi, tq, lo, span, False))], joint)
            else:
                run([(pl.ds(j * kc, kc), None) for j in range(S // kc)] + [(ctx_rows, None)], online)

        @pl.when(qi >= n_lq)
        def _():
            run([(ctx_rows, None)], joint)

    q_spec = pl.BlockSpec((None, tq, hp * HEAD_DIM), lambda b, h, i: (b, i, h))
    kv_spec = pl.BlockSpec((None, T, HEAD_DIM), lambda b, h, i: (b, 0, h * hp // KV_GROUP))
    in_specs = [q_spec, kv_spec, kv_spec, q_spec]
    args = [q, k, v, gate]
    if has_sink:
        in_specs.append(pl.BlockSpec((hp, 1, LANE), lambda b, h, i: (h, 0, 0)))
        args.append(sink)
    name = "attn_band_fwd" if band else "attn_dense_fwd"
    return _call_with_side(
        body, side, name=name, grid=(B, H // hp, T // tq), in_specs=in_specs, args=args,
        out_specs=[q_spec, q_spec, pl.BlockSpec((None, hp, 1, tq), lambda b, h, i: (b, h, 0, i))],
        out_shape=[jax.ShapeDtypeStruct(q.shape, BF16), jax.ShapeDtypeStruct(q.shape, F32),
                   jax.ShapeDtypeStruct((B, H, 1, T), F32)],
        scratch_shapes=[], dims=("parallel", "parallel", "parallel"))


def _attn_bwd(q, k, v, sink, gate, dz, o, lse, S, band, side=None):
    B, T, HD = q.shape
    H = HD // HEAD_DIM
    KVH = H // KV_GROUP
    L = T - S
    tq = L
    n_lq = S // tq
    span = tq + 2 * WINDOW
    scale = HEAD_DIM ** -0.5
    c2 = scale * LOG2E
    has_sink = sink is not None
    kc = min(KEY_CHUNK, S)
    hp = KV_GROUP if band else 1

    def body(*refs):
        if has_sink:
            (q_ref, k_ref, v_ref, g_ref, dz_ref, o_ref, lse_ref, s_ref,
             dq_ref, dk_ref, dv_ref, dg_ref, ds_ref, *scr) = refs
        else:
            q_ref, k_ref, v_ref, g_ref, dz_ref, o_ref, lse_ref, dq_ref, dk_ref, dv_ref, dg_ref, *scr = refs
        g, qi = pl.program_id(2), pl.program_id(3)

        @pl.when(jnp.logical_and(g == 0, qi == 0))
        def _():
            dk_ref[...] = jnp.zeros_like(dk_ref)
            dv_ref[...] = jnp.zeros_like(dv_ref)
            if not band:
                scr[0][...] = k_ref[...].astype(F32).T.astype(BF16)

        def run(segs):
            dks, dvs = [None] * len(segs), [None] * len(segs)
            for h in range(hp):
                lanes = slice(h * HEAD_DIM, (h + 1) * HEAD_DIM)
                qv = q_ref[:, lanes]
                gv, ov, dzv = g_ref[:, lanes], o_ref[:, lanes], dz_ref[:, lanes].astype(F32)
                sg = jax.nn.sigmoid(gv)
                dof = dzv * (gv * sg)
                dg_ref[:, lanes] = dzv * ov * (sg * (1.0 + gv * (1.0 - sg)))
                dov = dof.astype(BF16)
                lse2 = lse_ref[h]
                delta = _col_to_row(jnp.sum(dof * ov, axis=-1, keepdims=True))

                def head(seg, qv=qv, dov=dov):
                    rows = pl.ds(seg[0], seg[1])
                    s = lax.dot_general(k_ref[rows, :], qv, NT, preferred_element_type=F32) * c2
                    if seg[2] is not None:
                        s = jnp.where(seg[2], s, NEG_INF)
                    return s, lax.dot_general(v_ref[rows, :], dov, NT, preferred_element_type=F32)

                dq, dqT = None, None
                nxt = head(segs[0])
                for j, (lo, n, _) in enumerate(segs):
                    rows = pl.ds(lo, n)
                    s, dp = nxt
                    if j + 1 < len(segs):
                        nxt = head(segs[j + 1])
                    p = jnp.exp2(s - lse2)
                    ds = p * (dp - delta)
                    dsb = ds.astype(BF16)
                    dv = jnp.dot(p.astype(BF16), dov, preferred_element_type=F32)
                    dk = jnp.dot(dsb, qv, preferred_element_type=F32)
                    dvs[j] = dv if dvs[j] is None else dvs[j] + dv
                    dks[j] = dk if dks[j] is None else dks[j] + dk
                    if band:
                        part = jnp.dot(ds.T.astype(BF16), k_ref[rows, :], preferred_element_type=F32)
                        dq = part if dq is None else dq + part
                    else:
                        part = jnp.dot(scr[0][:, lo:lo + n], dsb, preferred_element_type=F32)
                        dqT = part if dqT is None else dqT + part
                dq_ref[:, lanes] = ((dq if band else dqT.T) * scale).astype(dq_ref.dtype)
                if has_sink:
                    psk = jnp.exp2(s_ref[h][:, 0:1] * LOG2E - lse2)
                    dsk = jnp.broadcast_to(-jnp.sum(psk * delta, axis=1, keepdims=True), (1, LANE))

                    @pl.when(qi == 0)
                    def _(h=h, dsk=dsk):
                        ds_ref[h] = dsk

                    @pl.when(qi > 0)
                    def _(h=h, dsk=dsk):
                        ds_ref[h] += dsk
            for j, (lo, n, _) in enumerate(segs):
                rows = pl.ds(lo, n)
                dv_ref[rows, :] += dvs[j]
                dk_ref[rows, :] += dks[j] * scale

        @pl.when(qi < n_lq)
        def _():
            if band:
                lo = _band_lo(qi, tq, S)
                run([(S, L, None), (lo, span, _band_mask(qi, tq, lo, span, True))])
            else:
                run([(j * kc, kc, None) for j in range(S // kc)] + [(S, L, None)])

        @pl.when(qi >= n_lq)
        def _():
            run([(S, L, None)])

    ng = KV_GROUP // hp
    q_spec = pl.BlockSpec((None, tq, hp * HEAD_DIM), lambda b, kv, g, i: (b, i, kv * ng + g))
    kv_spec = pl.BlockSpec((None, T, HEAD_DIM), lambda b, kv, g, i: (b, 0, kv))
    lse_spec = pl.BlockSpec((None, hp, 1, tq), lambda b, kv, g, i: (b, kv * ng + g, 0, i))
    in_specs = [q_spec, kv_spec, kv_spec, q_spec, q_spec, q_spec, lse_spec]
    out_specs = [q_spec, kv_spec, kv_spec, q_spec]
    out_shape = [jax.ShapeDtypeStruct(q.shape, BF16), jax.ShapeDtypeStruct(k.shape, F32),
                 jax.ShapeDtypeStruct(v.shape, F32), jax.ShapeDtypeStruct(q.shape, F32)]
    args = [q, k, v, gate, dz, o, lse]
    if has_sink:
        in_specs.append(pl.BlockSpec((hp, 1, LANE), lambda b, kv, g, i: (kv * ng + g, 0, 0)))
        args.append(sink)
        out_specs.append(pl.BlockSpec((None, hp, 1, LANE), lambda b, kv, g, i: (b, kv * ng + g, 0, 0)))
        out_shape.append(jax.ShapeDtypeStruct((B, H, 1, LANE), F32))
    res = _call_with_side(
        body, side, name="attn_band_bwd" if band else "attn_dense_bwd", grid=(B, KVH, ng, T // tq),
        in_specs=in_specs, args=args, out_specs=out_specs, out_shape=out_shape,
        scratch_shapes=[] if band else [pltpu.VMEM((HEAD_DIM, T), BF16)],
        dims=("parallel", "parallel", "arbitrary", "arbitrary"))
    return (res[0], res[1], res[2], res[3], res[4] if has_sink else None, res[-1] if side is not None else None)


def _prep_f(norm):
    def f(q, kv, *rest):
        q, kv = q.astype(F32), kv.astype(F32)
        kw = kv.shape[1] // 2
        k, v = kv[:, :kw], kv[:, kw:]
        if norm:
            qg, kg, cos, sin = rest
        else:
            (cos, sin), qg, kg = rest, None, None

        def heads(x, g):
            outs = []
            for h in range(x.shape[1] // HEAD_DIM):
                xh = x[:, h * HEAD_DIM:(h + 1) * HEAD_DIM]
                if g is not None:
                    xh = xh * lax.rsqrt(jnp.mean(xh * xh, axis=-1, keepdims=True) + EPS) * g
                outs.append(xh * cos + _swap(xh) * sin)
            return jnp.concatenate(outs, axis=1) if len(outs) > 1 else outs[0]

        return heads(q, qg), heads(k, kg), v

    return f


def _attn_branch(S, tm, band, norm, has_sink, carries=False):
    f = _prep_f(norm)
    name = "band" if band else "dense"

    def prep(q, kv, gains, tabs):
        outs = [(q.shape[2], BF16), (kv.shape[2] // 2, BF16), (kv.shape[2] // 2, BF16)]
        return _row_fwd(f, "prep_" + name + "_fwd", [q, kv], [], list(gains), list(tabs), outs, tm, S)

    def unpack(args):
        q, kv, gate = args[:3]
        rest = list(args[3:])
        gains = [rest.pop(0), rest.pop(0)] if norm else []
        sink = rest.pop(0) if has_sink else None
        pack = rest.pop(0) if carries else None
        return q, kv, gate, gains, sink, pack, rest

    def sink_lanes(sink):
        return None if sink is None else jnp.broadcast_to(sink[:, None, None], (sink.shape[0], 1, LANE))

    def run_fwd(args):
        q, kv, gate, gains, sink, pack, tabs = unpack(args)
        qp, kp, vp = prep(q, kv, gains, tabs)
        side = (pack.astype(BF16), True) if carries else None
        res = _attn_fwd(qp, kp, vp, sink_lanes(sink), gate, S, band, side)
        return ((res[0], res[3]) if carries else res[0]), (args, qp, kp, vp, res[1], res[2])

    @jax.custom_vjp
    def op(*args):
        return run_fwd(args)[0]

    def fwd(*args):
        return run_fwd(args)

    def bwd(res, ct):
        args, qp, kp, vp, o, lse = res
        q, kv, gate, gains, sink, pack, tabs = unpack(args)
        dz, side = (ct[0], (ct[1], False)) if carries else (ct, None)
        dqp, dkp, dvp, dgate, dsk, recv = _attn_bwd(qp, kp, vp, sink_lanes(sink), gate, dz, o, lse, S, band, side)
        dxs, _, dgains = _row_bwd(f, "prep_" + name + "_bwd", [q, kv], [], list(gains), list(tabs),
                                  [dqp, dkp, dvp], tm, S)
        out = list(dxs) + [dgate] + list(dgains)
        if has_sink:
            out.append(jnp.sum(dsk[:, :, 0, 0], axis=0))
        if carries:
            out.append(_sum_slots(recv, "sum_grads"))
        return (*out, *[jnp.zeros_like(t) for t in tabs])

    op.defvjp(fwd, bwd)
    return op


def _final_loss(S, tm):
    def run(X, target, g):
        B, T, D = X.shape
        n_lat_tiles = S // tm

        def lossf(x, gg, tgt):
            y = x * lax.rsqrt(jnp.mean(x * x, axis=-1, keepdims=True) + EPS) * gg
            err = y - tgt
            return 0.5 * jnp.sum(jnp.sum(err * err, axis=-1, keepdims=True), axis=0, keepdims=True) / D

        def body(x_ref, t_ref, g_ref, loss_ref, dx_ref, dg_ref):
            b, i = pl.program_id(0), pl.program_id(1)
            tgt = t_ref[...]
            val, vjp = jax.vjp(lambda x, gg: lossf(x, gg, tgt), x_ref[...], g_ref[...])
            dx, dg = vjp(jnp.ones((1, 1), F32))
            lat = (i < n_lat_tiles).astype(F32)
            dx_ref[...] = dx * lat

            @pl.when(i == 0)
            def _():
                loss_ref[...] = jnp.zeros_like(loss_ref)

            @pl.when(jnp.logical_and(b == 0, i == 0))
            def _():
                dg_ref[...] = jnp.zeros_like(dg_ref)

            loss_ref[...] += jnp.broadcast_to(val * lat, loss_ref.shape)
            dg_ref[...] += dg * lat

        x_spec = pl.BlockSpec((None, tm, D), lambda b, i: (b, i, 0))
        t_spec = pl.BlockSpec((None, tm, D), lambda b, i: (b, jnp.minimum(i, n_lat_tiles - 1), 0))
        g_spec = pl.BlockSpec((1, D), lambda b, i: (0, 0))
        loss, dx, dg = _call(
            body, name="final_loss", grid=(B, T // tm), in_specs=[x_spec, t_spec, g_spec],
            out_specs=[pl.BlockSpec((None, 1, LANE), lambda b, i: (b, 0, 0)), x_spec, g_spec],
            out_shape=[jax.ShapeDtypeStruct((B, 1, LANE), F32), jax.ShapeDtypeStruct(X.shape, F32),
                       jax.ShapeDtypeStruct(g.shape, F32)],
            compiler_params=_params(("arbitrary", "arbitrary")))(X, target, g)
        return jnp.sum(loss[:, 0, 0]), dx, dg

    @jax.custom_vjp
    def op(X, target, g):
        return run(X, target, g)[0]

    def fwd(X, target, g):
        loss, dx, dg = run(X, target, g)
        return loss, (dx, dg, target)

    def bwd(res, ct):
        dx, dg, target = res
        return ct * dx, jnp.zeros_like(target), ct * dg

    op.defvjp(fwd, bwd)
    return op


def _prenorm_f(D):
    def f(x, mp, g):
        y = x * lax.rsqrt(jnp.mean(x * x, axis=-1, keepdims=True) + EPS) * g
        return (y * (1.0 + mp[:, D:2 * D]) + mp[:, 0:D],)
    return f


def _resid_f(D):
    def f(x, y, mp):
        return (x + mp[:, 2 * D:3 * D] * y,)
    return f


def _sigmoid(x):
    return 0.5 * jnp.tanh(0.5 * x) + 0.5


def _gate2_f(hf, hr, g):
    return ((hf + hr) * (g * jax.nn.sigmoid(g)),)


def _merge_f(D):
    def f(m3, pa, pb, pc):
        return (jax.nn.sigmoid(m3[:, 0:D]) * pa.astype(F32) + jax.nn.sigmoid(m3[:, D:2 * D]) * pb.astype(F32)
                + jax.nn.sigmoid(m3[:, 2 * D:3 * D]) * pc.astype(F32),)
    return f


def _coef_f(D):
    nblk = D // GATE_BLOCK

    def f(v, pv, wm):
        vb = v.astype(BF16)

        def gate(k):
            cols = []
            for j in range(nblk):
                r0 = (k * nblk + j) * GATE_BLOCK
                cols.append(jnp.dot(vb[:, j * GATE_BLOCK:(j + 1) * GATE_BLOCK],
                                    wm[r0:r0 + GATE_BLOCK, :].astype(BF16), preferred_element_type=F32))
            return jnp.concatenate(cols, axis=1)

        outs = []
        for d in range(2):
            r = _sigmoid(gate(d) + pv[d:d + 1])
            i = _sigmoid(gate(2 + d) + pv[2 + d:3 + d])
            la = r * pv[4 + d:5 + d]
            t = jnp.tanh(la)
            outs += [jnp.exp(la), jnp.sqrt(-2.0 * t / (1.0 - t)) * (i * v)]
        return tuple(outs)

    return f


def _gate_blocks(w):
    per = GATE_BLOCK // LRU_BLOCK_W
    n = w.shape[1]
    w5 = w.reshape(2, n // per, per, LRU_BLOCK_W, LRU_BLOCK_W)
    dense = jnp.einsum("djiab,ik->djiakb", w5, jnp.eye(per, dtype=w.dtype))
    return dense.reshape(2, (n // per) * GATE_BLOCK, GATE_BLOCK)


def _exchange_shape(x, gather):
    return jax.ShapeDtypeStruct((N_DEV,) + x.shape if gather else x.shape, x.dtype)


EXCHANGE_SEMS = [pltpu.SemaphoreType.DMA((N_DEV - 1,)), pltpu.SemaphoreType.DMA((N_DEV - 1,)),
                 pltpu.SemaphoreType.DMA(())]


def _exchange_ops(x_ref, o_ref, send_sems, recv_sems, local_sem, gather):
    mx, my, mc = lax.axis_index("x"), lax.axis_index("y"), lax.axis_index("c")
    me = 4 * mx + 2 * my + mc

    def src(p):
        return x_ref if gather else x_ref.at[p]

    local = pltpu.make_async_copy(src(me), o_ref.at[me], local_sem)
    sends, recvs = [], []
    for k in range(1, N_DEV):
        px = 1 - mx if k & 4 else mx
        py = 1 - my if k & 2 else my
        pc = 1 - mc if k & 1 else mc
        peer = 4 * px + 2 * py + pc
        sends.append(pltpu.make_async_remote_copy(
            src_ref=src(peer), dst_ref=o_ref.at[me], send_sem=send_sems.at[k - 1],
            recv_sem=recv_sems.at[k - 1], device_id=(px, py, pc), device_id_type=pl.DeviceIdType.MESH))
        recvs.append(pltpu.make_async_remote_copy(
            src_ref=src(peer), dst_ref=o_ref.at[peer], send_sem=send_sems.at[k - 1],
            recv_sem=recv_sems.at[k - 1], device_id=(px, py, pc), device_id_type=pl.DeviceIdType.MESH))

    def start():
        local.start()
        for cp in sends:
            cp.start()

    def finish():
        for cp in recvs:
            cp.wait_recv()
        for cp in sends:
            cp.wait_send()
        local.wait()

    return start, finish


def _exchange(x, name, gather):
    def body(x_ref, o_ref, send_sems, recv_sems, local_sem):
        start, finish = _exchange_ops(x_ref, o_ref, send_sems, recv_sems, local_sem, gather)
        start()
        finish()

    hbm = pl.BlockSpec(memory_space=pltpu.HBM)
    return _call(body, name=name, in_specs=[hbm], out_specs=hbm, out_shape=_exchange_shape(x, gather),
                 scratch_shapes=EXCHANGE_SEMS)(x)


def _call_with_side(body, side, *, name, grid, in_specs, args, out_specs, out_shape, scratch_shapes, dims):
    if side is None:
        return _call(body, name=name, grid=grid, in_specs=in_specs, out_specs=out_specs, out_shape=out_shape,
                     scratch_shapes=scratch_shapes, compiler_params=_params(dims))(*args)
    x, gather = side
    n_in, n_out, n_scr = len(in_specs), len(out_specs), len(scratch_shapes)

    def wrapped(*refs):
        ins, x_ref = refs[:n_in], refs[n_in]
        outs, o_ref = refs[n_in + 1:n_in + 1 + n_out], refs[n_in + 1 + n_out]
        scr = refs[n_in + 2 + n_out:n_in + 2 + n_out + n_scr]
        start, finish = _exchange_ops(x_ref, o_ref, *refs[n_in + 2 + n_out + n_scr:], gather)
        ids = [pl.program_id(a) for a in range(len(grid))]
        first = functools.reduce(jnp.logical_and, [i == 0 for i in ids])
        last = functools.reduce(jnp.logical_and, [i == g - 1 for i, g in zip(ids, grid)])
        pl.when(first)(start)
        body(*ins, *outs, *scr)
        pl.when(last)(finish)

    hbm = pl.BlockSpec(memory_space=pltpu.HBM)
    return _call(wrapped, name=name + "_xchg", grid=grid, in_specs=list(in_specs) + [hbm],
                 out_specs=list(out_specs) + [hbm], out_shape=list(out_shape) + [_exchange_shape(x, gather)],
                 scratch_shapes=list(scratch_shapes) + EXCHANGE_SEMS,
                 compiler_params=_params(("arbitrary",) * len(grid)))(*args, x)


def _adamw(gs, w, m, v, name):
    n, R, C = gs.shape
    tr = _pick(R, 256, 16)

    def body(g_ref, w_ref, m_ref, v_ref, go_ref, d_ref, mo_ref, vo_ref):
        g = g_ref[0].astype(F32)
        for p in range(1, n):
            g = g + g_ref[p].astype(F32)
        m2 = ADAM_B1 * m_ref[...] + (1.0 - ADAM_B1) * g
        v2 = ADAM_B2 * v_ref[...] + (1.0 - ADAM_B2) * (g * g)
        m_hat = m2 / (1.0 - ADAM_B1 ** ADAM_STEP)
        v_hat = v2 / (1.0 - ADAM_B2 ** ADAM_STEP)
        go_ref[...] = g
        d_ref[...] = -ADAM_LR * (m_hat / (jnp.sqrt(v_hat) + ADAM_EPS) + ADAM_WD * w_ref[...])
        mo_ref[...] = m2
        vo_ref[...] = v2

    spec = pl.BlockSpec((tr, C), lambda i: (i, 0))
    return _call(body, name=name, grid=(R // tr,),
                 in_specs=[pl.BlockSpec((n, tr, C), lambda i: (0, i, 0)), spec, spec, spec],
                 out_specs=[spec] * 4, out_shape=[jax.ShapeDtypeStruct((R, C), F32)] * 4,
                 compiler_params=_params(("parallel",)))(gs, w, m, v)


def _sum_slots(gs, name):
    n, R, C = gs.shape
    tr = _pick(R, 256, 16)

    def body(g_ref, o_ref):
        g = g_ref[0].astype(F32)
        for p in range(1, n):
            g = g + g_ref[p].astype(F32)
        o_ref[...] = g

    return _call(body, name=name, grid=(R // tr,), in_specs=[pl.BlockSpec((n, tr, C), lambda i: (0, i, 0))],
                 out_specs=pl.BlockSpec((tr, C), lambda i: (i, 0)), out_shape=jax.ShapeDtypeStruct((R, C), F32),
                 compiler_params=_params(("parallel",)))(gs)


def _gather_op(name):
    @jax.custom_vjp
    def op(pack):
        return _exchange(pack.astype(BF16), name, True)

    def fwd(pack):
        return op(pack), None

    def bwd(_, ct):
        return (_sum_slots(_exchange(ct, name + "_transpose", False), "sum_grads"),)

    op.defvjp(fwd, bwd)
    return op


def _rope_tables(S, L):
    P = HEAD_DIM // 4
    rows = S // GRID_W
    row_id = jnp.repeat(jnp.arange(rows), GRID_W)
    col_id = jnp.tile(jnp.arange(GRID_W), rows)
    inv = ROPE_THETA ** (-jnp.arange(P, dtype=F32) / P)
    ar, ac = row_id[:, None] * inv, col_id[:, None] * inv
    cos = jnp.concatenate([jnp.cos(ar), jnp.cos(ar), jnp.cos(ac), jnp.cos(ac)], axis=1)
    sin = jnp.concatenate([-jnp.sin(ar), jnp.sin(ar), -jnp.sin(ac), jnp.sin(ac)], axis=1)
    cos = jnp.concatenate([cos, jnp.ones((L, HEAD_DIM), F32)], axis=0)
    sin = jnp.concatenate([sin, jnp.zeros((L, HEAD_DIM), F32)], axis=0)
    return cos, sin


SQ_TILES = (((2176, 1024, 1024), (2176, 1024, 1024), (1024, 1024, 1088)),) * DEPTH


def _layer_shard_shapes(D, IN):
    return (("w_in", (D, IN // N_DEV), 1), ("w_mod", (D, 3 * D // N_DEV), 1),
            ("w_branch", (3, D // N_DEV, D), 1), ("w_out", (D // N_DEV, D), 0))


def _pack_layer(w, l, D):
    return jnp.concatenate([w[n][l].reshape(-1) for n in BIG_PACK]).reshape(-1, D)


def _unpack_shards(pack, D, IN):
    flat, out, off = pack.reshape(-1), {}, 0
    for n, shape, _ in _layer_shard_shapes(D, IN):
        sz = shape[0] * shape[1] * (shape[2] if len(shape) > 2 else 1)
        out[n] = flat[off:off + sz].reshape(shape)
        off += sz
    return out


def _unpack_gathered(g, D, IN):
    flat, out, off = g.reshape(N_DEV, -1), {}, 0
    for n, shape, axis in _layer_shard_shapes(D, IN):
        sz = shape[0] * shape[1] * (shape[2] if len(shape) > 2 else 1)
        out[n] = _unshard(flat[:, off:off + sz].reshape((N_DEV,) + shape), axis)
        off += sz
    return out


def _loss_fn(packs, p, x, c, ctx, target):
    B, S, D = x.shape
    L = ctx.shape[1]
    T = S + L
    tm = min(L, 256)
    KVW = D // KV_GROUP
    widths = (D, D, D, 2 * KVW, D, D, 2 * KVW, D, 3 * D)
    part_dtypes = (F32, F32, BF16, BF16, F32, BF16, BF16, F32, F32)
    IN = sum(widths)
    cos, sin = _rope_tables(S, L)

    X = jnp.concatenate([x, ctx], axis=1)
    sc, scc = jax.nn.silu(c), jax.nn.silu(p["c_ctx"])
    A = jnp.concatenate([scc[None], sc, jnp.zeros((SUBLANE - 1 - B, D), F32)], axis=0)
    gathered = _gather_op("gather_layer0")(packs[0])
    for l in range(DEPTH):
        big = _unpack_gathered(gathered, D, IN)
        mod = _matmul("mm_mod")(A, big["w_mod"]) + p["b_mod"][l]
        modp = jnp.stack([jnp.broadcast_to(mod[0], (B, 3 * D)), mod[1:1 + B]], axis=1)[:, :, None, :]
        (h,) = _rowwise(_prenorm_f(D), "prenorm", [(D, BF16)], tm, S, 1, 1, 1)(X, modp, p["norm_g"][l][None])
        parts = _in_proj(widths, part_dtypes)(h.reshape(B * T, D), big["w_in"])
        uA, gA, qB, kvB, gB, qC, kvC, gC, m3 = [t.reshape(B, T, -1) for t in parts]
        u = _conv_op(S)(uA, p["conv_w"][l], p["conv_b"][l][None])
        nsp = -LRU_C * jax.nn.softplus(-p["lru_lambda"][l])
        pv = jnp.concatenate([p["lru_ba"][l], p["lru_bx"][l], nsp, jnp.zeros((2, D), F32)], axis=0)
        wm = jnp.concatenate([_gate_blocks(p["lru_wa"][l]), _gate_blocks(p["lru_wx"][l])], axis=0)
        wm = wm.reshape(-1, GATE_BLOCK)
        af, bf, ar, br = _rowwise(_coef_f(D), "lru_coef", [(D, F32)] * 4, tm, S, 1, 0, 2)(u, pv, wm)
        hf = _scan_op(S, False)(af, bf)
        hr = _scan_op(S, True)(ar, br)
        (zA,) = _rowwise(_gate2_f, "gate_a", [(D, BF16)], tm, S, 3, 0, 0)(hf, hr, gA)
        zB = _attn_branch(S, tm, True, False, True)(qB, kvB, gB, p["attn_sink"][l], cos, sin)
        qkv_c = (qC, kvC, gC, p["q_norm_g"][l][None], p["k_norm_g"][l][None])
        if l + 1 < DEPTH:
            zC, gathered = _attn_branch(S, tm, False, True, False, True)(*qkv_c, packs[l + 1], cos, sin)
        else:
            zC = _attn_branch(S, tm, False, True, False)(*qkv_c, cos, sin)
        pr = [_matmul("mm_branch", *SQ_TILES[l], out_dtype=BF16)(z.reshape(B * T, D), big["w_branch"][n])
              .reshape(B, T, D) for n, z in enumerate((zA, zB, zC))]
        (mg,) = _rowwise(_merge_f(D), "merge", [(D, BF16)], tm, S, 4, 0, 0)(m3, *pr)
        y = _matmul("mm_out", *SQ_TILES[l])(mg.reshape(B * T, D), big["w_out"]).reshape(B, T, D)
        (X,) = _rowwise(_resid_f(D), "resid", [(D, F32)], tm, S, 2, 1, 0)(X, y, modp)
    return _final_loss(S, tm)(X, target, p["final_g"][None])


def _shard_axis(name):
    return {"w_mod": 2, "w_in": 2, "conv_w": 2, "lru_ba": 2, "lru_bx": 2, "lru_lambda": 2,
            "w_branch": 2, "w_out": 1}.get(name)


def _unshard(g, axis):
    full = jnp.moveaxis(g, 0, axis)
    shape = list(full.shape)
    shape[axis:axis + 2] = [shape[axis] * shape[axis + 1]]
    return full.reshape(shape)


def _reshard(full, axis):
    shape = list(full.shape)
    shape[axis:axis + 1] = [N_DEV, shape[axis] // N_DEV]
    return jnp.moveaxis(full.reshape(shape), axis, 0)


def _pad_to(v, n):
    return jnp.concatenate([v, jnp.zeros((n - v.shape[0],), v.dtype)]) if n > v.shape[0] else v


def _step(x, c, ctx, target, w, m, v):
    D = x.shape[2]
    IN = w["w_in"].shape[2] * N_DEV
    full = {n: w[n] for n in REPLICATED}
    small_local = jnp.concatenate([w[n].reshape(-1) for n in SMALL_SHARDED])
    small_all = _exchange(small_local.reshape(-1, LANE), "gather_small", True).reshape(N_DEV, -1)
    off = 0
    for n in SMALL_SHARDED:
        sz = w[n].size
        full[n] = _unshard(small_all[:, off:off + sz].reshape((N_DEV,) + w[n].shape), _shard_axis(n))
        off += sz
    packs = [_pack_layer(w, l, D) for l in range(DEPTH)]

    loss, (gpacks, gp, gx) = jax.value_and_grad(_loss_fn, argnums=(0, 1, 2))(packs, full, x, c, ctx, target)
    loss = lax.psum(loss, AXES)

    out = {}
    gshards = [_unpack_shards(g, D, IN) for g in gpacks]
    for n in BIG_PACK:
        g = jnp.stack([gs[n] for gs in gshards])
        C = w[n].shape[-1]
        res = _adamw(g.reshape(1, -1, C), w[n].reshape(-1, C), m[n].reshape(-1, C), v[n].reshape(-1, C),
                     "adamw_" + n)
        out[n] = [r.reshape(w[n].shape) for r in res]

    rep = jnp.concatenate([gp[n].reshape(-1) for n in REPLICATED])
    n_rep = rep.shape[0]
    chunk = -(-n_rep // (N_DEV * LANE)) * LANE
    rep = _pad_to(rep, N_DEV * chunk).reshape(N_DEV, chunk)
    shards = jnp.concatenate([_reshard(gp[n], _shard_axis(n)).reshape(N_DEV, -1) for n in SMALL_SHARDED], axis=1)
    n_sh = shards.shape[1]
    recv = _exchange(jnp.concatenate([rep, shards], axis=1).reshape(N_DEV, -1, LANE), "scatter_small", False)
    wl =jnp.concatenate([w[n].reshape(-1) for n in SMALL_SHARDED])
    ml = jnp.concatenate([m[n].reshape(-1) for n in SMALL_SHARDED])
    vl = jnp.concatenate([v[n].reshape(-1) for n in SMALL_SHARDED])
    rows = recv.shape[1]
    rrows = chunk // LANE
    g_sh, d_sh, m_sh, v_sh = _adamw(recv[:, rrows:], wl.reshape(-1, LANE), ml.reshape(-1, LANE),
                                    vl.reshape(-1, LANE), "adamw_small_sharded")
    g_rep8 = _sum_slots(recv[:, :rrows], "sum_replicated")
    g_rep = _exchange(g_rep8, "gather_replicated", True).reshape(-1)
    wr = _pad_to(jnp.concatenate([w[n].reshape(-1) for n in REPLICATED]), N_DEV * chunk)
    mr = _pad_to(jnp.concatenate([m[n].reshape(-1) for n in REPLICATED]), N_DEV * chunk)
    vr = _pad_to(jnp.concatenate([v[n].reshape(-1) for n in REPLICATED]), N_DEV * chunk)
    res_rep = _adamw(g_rep.reshape(1, -1, LANE), wr.reshape(-1, LANE), mr.reshape(-1, LANE), vr.reshape(-1, LANE),
                     "adamw_replicated")
    off = 0
    for n in REPLICATED:
        sz = w[n].size
        out[n] = [r.reshape(-1)[off:off + sz].reshape(w[n].shape) for r in res_rep]
        off += sz
    off = 0
    for n in SMALL_SHARDED:
        sz = w[n].size
        out[n] = [r.reshape(-1)[off:off + sz].reshape(w[n].shape) for r in (g_sh, d_sh, m_sh, v_sh)]
        off += sz
    assert off == n_sh and rows == rrows + n_sh // LANE
    return (loss, gx, *[out[n][0] for n in WEIGHTS], *[out[n][1] for n in WEIGHTS],
            *[out[n][2] for n in WEIGHTS], *[out[n][3] for n in WEIGHTS])


def kernel(x, c, ctx, c_ctx, norm_g, w_mod, b_mod, w_in, conv_w, conv_b, lru_wa, lru_ba, lru_wx, lru_bx, lru_lambda, attn_sink, q_norm_g, k_norm_g, w_branch, w_out, final_g, loss_target, m_c_ctx, m_norm_g, m_w_mod, m_b_mod, m_w_in, m_conv_w, m_conv_b, m_lru_wa, m_lru_ba, m_lru_wx, m_lru_bx, m_lru_lambda, m_attn_sink, m_q_norm_g, m_k_norm_g, m_w_branch, m_w_out, m_final_g, v_c_ctx, v_norm_g, v_w_mod, v_b_mod, v_w_in, v_conv_w, v_conv_b, v_lru_wa, v_lru_ba, v_lru_wx, v_lru_bx, v_lru_lambda, v_attn_sink, v_q_norm_g, v_k_norm_g, v_w_branch, v_w_out, v_final_g):
    w = dict(zip(WEIGHTS, (c_ctx, norm_g, w_mod, b_mod, w_in, conv_w, conv_b, lru_wa, lru_ba, lru_wx, lru_bx,
                           lru_lambda, attn_sink, q_norm_g, k_norm_g, w_branch, w_out, final_g)))
    m = dict(zip(WEIGHTS, (m_c_ctx, m_norm_g, m_w_mod, m_b_mod, m_w_in, m_conv_w, m_conv_b, m_lru_wa, m_lru_ba,
                           m_lru_wx, m_lru_bx, m_lru_lambda, m_attn_sink, m_q_norm_g, m_k_norm_g, m_w_branch,
                           m_w_out, m_final_g)))
    v = dict(zip(WEIGHTS, (v_c_ctx, v_norm_g, v_w_mod, v_b_mod, v_w_in, v_conv_w, v_conv_b, v_lru_wa, v_lru_ba,
                           v_lru_wx, v_lru_bx, v_lru_lambda, v_attn_sink, v_q_norm_g, v_k_norm_g, v_w_branch,
                           v_w_out, v_final_g)))
    return _step(x, c, ctx, loss_target, w, m, v)
```

```python
import functools
import math

import jax
import jax.numpy as jnp
from jax import lax
from jax.experimental import pallas as pl
from jax.experimental.pallas import tpu as pltpu

F32 = jnp.float32
BF16 = jnp.bfloat16

AXES = ("x", "y", "c")
N_DEV = 8
DEPTH = 4
HEAD_DIM = 128
GRID_W = 64
WINDOW = 128
LRU_BLOCK_W = 64
GATE_BLOCK = 256
LRU_C = 8.0
ROPE_THETA = 10000.0
EPS = 1e-6
NEG_INF = -1e30
KV_GROUP = 4
LANE = 128
SUBLANE = 8
VMEM_LIMIT = 56 * 1024 * 1024

ADAM_LR = 0.001
ADAM_B1 = 0.9
ADAM_B2 = 0.999
ADAM_EPS = 1e-08
ADAM_WD = 0.01
ADAM_STEP = 10

WEIGHTS = ("c_ctx", "norm_g", "w_mod", "b_mod", "w_in", "conv_w", "conv_b", "lru_wa", "lru_ba", "lru_wx",
           "lru_bx", "lru_lambda", "attn_sink", "q_norm_g", "k_norm_g", "w_branch", "w_out", "final_g")
BIG_PACK = ("w_in", "w_mod", "w_branch", "w_out")
SMALL_SHARDED = ("conv_w", "lru_ba", "lru_bx", "lru_lambda")
REPLICATED = ("c_ctx", "norm_g", "b_mod", "conv_b", "lru_wa", "lru_wx", "attn_sink", "q_norm_g", "k_norm_g",
              "final_g")


def _call(body, **kw):
    return pl.pallas_call(body, **kw)


def _params(dims=None, vmem=VMEM_LIMIT):
    return pltpu.CompilerParams(dimension_semantics=dims, vmem_limit_bytes=vmem)


def _pick(n, target, mult):
    for t in range(min(n, target), 0, -1):
        if n % t == 0 and t % mult == 0:
            return t
    return n


def _mm(a, b, *, name, ta=False, tb=False, out_dtype=F32, tm=512, tn=1024, tk=1024):
    M, K = (a.shape[1], a.shape[0]) if ta else a.shape
    N = b.shape[0] if tb else b.shape[1]
    assert (b.shape[1] if tb else b.shape[0]) == K
    tm = _pick(M, tm, LANE if ta else 16)
    tn = _pick(N, tn, LANE)
    tk = _pick(K, tk, 16 if ta and not tb else LANE)
    nk = K // tk
    dn = (((0 if ta else 1,), (1 if tb else 0,)), ((), ()))

    def body(a_ref, b_ref, o_ref, *acc):
        r = lax.dot_general(a_ref[...].astype(BF16), b_ref[...].astype(BF16), dn,
                            preferred_element_type=F32)
        if nk == 1:
            o_ref[...] = r.astype(o_ref.dtype)
        else:
            k = pl.program_id(2)

            @pl.when(k == 0)
            def _():
                acc[0][...] = r

            @pl.when(k > 0)
            def _():
                acc[0][...] += r

            @pl.when(k == nk - 1)
            def _():
                o_ref[...] = acc[0][...].astype(o_ref.dtype)

    a_spec = (pl.BlockSpec((tk, tm), lambda i, j, k: (k, i)) if ta
              else pl.BlockSpec((tm, tk), lambda i, j, k: (i, k)))
    b_spec = (pl.BlockSpec((tn, tk), lambda i, j, k: (j, k)) if tb
              else pl.BlockSpec((tk, tn), lambda i, j, k: (k, j)))
    return _call(
        body, name=name, grid=(M // tm, N // tn, nk),
        in_specs=[a_spec, b_spec],
        out_specs=pl.BlockSpec((tm, tn), lambda i, j, k: (i, j)),
        out_shape=jax.ShapeDtypeStruct((M, N), out_dtype),
        scratch_shapes=[pltpu.VMEM((tm, tn), F32)] if nk > 1 else [],
        compiler_params=_params(("parallel", "parallel", "arbitrary")),
    )(a, b)


def _matmul(name, t_fwd=(512, 1024, 1024), t_da=(512, 1024, 1024), t_dw=(1024, 1024, 512), out_dtype=F32):
    def tiles(t):
        return dict(tm=t[0], tn=t[1], tk=t[2])

    @jax.custom_vjp
    def f(a, w):
        return _mm(a, w, name=name + "_fwd", out_dtype=out_dtype, **tiles(t_fwd))

    def fwd(a, w):
        return f(a, w), (a, w)

    def bwd(res, g):
        a, w = res
        da = _mm(g, w, name=name + "_da", tb=True, out_dtype=a.dtype, **tiles(t_da))
        dw = _mm(a, g, name=name + "_dw", ta=True, out_dtype=w.dtype, **tiles(t_dw))
        return da, dw

    f.defvjp(fwd, bwd)
    return f


def _mm_parts_nt(parts, w, col0, init, out_dtype, *, name, tm=1088, tk=512):
    M = parts[0].shape[0]
    D = w.shape[0]
    tm = _pick(M, tm, 16)
    tk = math.gcd(tk, *[a.shape[1] for a in parts])
    starts, n = [], 0
    for a in parts:
        assert a.shape[1] % tk == 0
        starts.append(n)
        n += a.shape[1] // tk
    k0 = col0 // tk
    has_init = init is not None

    def body(*refs):
        a_refs, w_ref = refs[:len(parts)], refs[len(parts)]
        o_ref, acc = refs[-2], refs[-1]
        k = pl.program_id(1)

        @pl.when(k == 0)
        def _():
            acc[...] = refs[len(parts) + 1][...].astype(F32) if has_init else jnp.zeros_like(acc)

        for a_ref, s, a in zip(a_refs, starts, parts):
            @pl.when(jnp.logical_and(k >= s, k < s + a.shape[1] // tk))
            def _(a_ref=a_ref):
                acc[...] += lax.dot_general(a_ref[...].astype(BF16), w_ref[...], NT, preferred_element_type=F32)

        @pl.when(k == n - 1)
        def _():
            o_ref[...] = acc[...].astype(o_ref.dtype)

    in_specs = [pl.BlockSpec((tm, tk), lambda i, k, s=s, c=a.shape[1] // tk: (i, jnp.clip(k - s, 0, c - 1)))
                for a, s in zip(parts, starts)]
    in_specs.append(pl.BlockSpec((D, tk), lambda i, k: (0, k0 + k)))
    o_spec = pl.BlockSpec((tm, D), lambda i, k: (i, 0))
    args = list(parts) + [w]
    if has_init:
        in_specs.append(o_spec)
        args.append(init)
    return _call(body, name=name, grid=(M // tm, n), in_specs=in_specs, out_specs=o_spec,
                 out_shape=jax.ShapeDtypeStruct((M, D), out_dtype),
                 scratch_shapes=[pltpu.VMEM((tm, D), F32)],
                 compiler_params=_params(("parallel", "arbitrary")))(*args)


def _in_proj(widths, dtypes, t_fwd=(2176, 1024, 1024), t_dw=(1024, 1024, 2176), group=5):
    offs = [0]
    for wd in widths:
        offs.append(offs[-1] + wd)

    def cols(w, i):
        return w[:, offs[i]:offs[i + 1]]

    @jax.custom_vjp
    def op(h, w):
        return tuple(_mm(h, cols(w, i), name="mm_in_fwd", out_dtype=dtypes[i], tm=t_fwd[0], tn=t_fwd[1],
                         tk=t_fwd[2]) for i in range(len(widths)))

    def fwd(h, w):
        return op(h, w), (h, w)

    def bwd(res, gs):
        h, w = res
        da = None
        for i0 in range(0, len(widths), group):
            last = i0 + group >= len(widths)
            da = _mm_parts_nt(list(gs[i0:i0 + group]), w, offs[i0], da, h.dtype if last else F32, name="mm_in_da")
        dw = jnp.concatenate([_mm(h, g, name="mm_in_dw", ta=True, out_dtype=w.dtype, tm=t_dw[0], tn=t_dw[1],
                                  tk=t_dw[2]) for g in gs], axis=1)
        return da, dw

    op.defvjp(fwd, bwd)
    return op


def _row_specs(xs, tps, gps, rts, tm, n_lat_tiles):
    x_specs = [pl.BlockSpec((None, tm, x.shape[2]), lambda b, i: (b, i, 0)) for x in xs]
    tp_specs = [pl.BlockSpec((None, None, 1, p.shape[3]),
                             lambda b, i: (b, (i < n_lat_tiles).astype(jnp.int32), 0, 0)) for p in tps]
    gp_specs = [pl.BlockSpec(p.shape, lambda b, i: (0, 0)) for p in gps]
    rt_specs = [pl.BlockSpec((tm, t.shape[1]), lambda b, i: (i, 0)) for t in rts]
    return x_specs, tp_specs, gp_specs, rt_specs


def _row_fwd(f, name, xs, tps, gps, rts, outs, tm, n_lat):
    B, T, _ = xs[0].shape
    n_in = len(xs) + len(tps) + len(gps) + len(rts)

    def body(*refs):
        vals = f(*[r[...] for r in refs[:n_in]])
        for o, v in zip(refs[n_in:], vals):
            o[...] = v.astype(o.dtype)

    x_specs, tp_specs, gp_specs, rt_specs = _row_specs(xs, tps, gps, rts, tm, n_lat // tm)
    res = _call(
        body, name=name, grid=(B, T // tm),
        in_specs=x_specs + tp_specs + gp_specs + rt_specs,
        out_specs=[pl.BlockSpec((None, tm, w), lambda b, i: (b, i, 0)) for w, _ in outs],
        out_shape=[jax.ShapeDtypeStruct((B, T, w), dt) for w, dt in outs],
        compiler_params=_params(("parallel", "parallel")),
    )(*xs, *tps, *gps, *rts)
    return list(res)


def _row_bwd(f, name, xs, tps, gps, rts, douts, tm, n_lat):
    B, T, _ = xs[0].shape
    nx, ntp, ngp, nd = len(xs), len(tps), len(gps), len(douts)
    n_lat_tiles = n_lat // tm
    n_diff = nx + ntp + ngp
    n_in = n_diff + len(rts)

    def body(*refs):
        diff = [r[...] for r in refs[:n_diff]]
        tabs = [r[...] for r in refs[n_diff:n_in]]
        dos = [r[...] for r in refs[n_in:n_in + nd]]
        o_refs = refs[n_in + nd:]
        prim, vjp = jax.vjp(lambda *d: tuple(f(*d, *tabs)), *diff)
        grads = vjp(tuple(d.astype(p.dtype) for d, p in zip(dos, prim)))
        b, i = pl.program_id(0), pl.program_id(1)
        for k in range(nx):
            o_refs[k][...] = grads[k].astype(o_refs[k].dtype)
        first_tp = jnp.logical_or(i == 0, i == n_lat_tiles)
        first_gp = jnp.logical_and(b == 0, i == 0)
        for k in range(nx, n_diff):
            first = first_tp if k < nx + ntp else first_gp

            @pl.when(first)
            def _(k=k):
                o_refs[k][...] = grads[k]

            @pl.when(jnp.logical_not(first))
            def _(k=k):
                o_refs[k][...] += grads[k]

    x_specs, tp_specs, gp_specs, rt_specs = _row_specs(xs, tps, gps, rts, tm, n_lat_tiles)
    d_specs = [pl.BlockSpec((None, tm, d.shape[2]), lambda b, i: (b, i, 0)) for d in douts]
    res = _call(
        body, name=name, grid=(B, T // tm),
        in_specs=x_specs + tp_specs + gp_specs + rt_specs + d_specs,
        out_specs=x_specs + tp_specs + gp_specs,
        out_shape=[jax.ShapeDtypeStruct(a.shape, a.dtype) for a in xs]
        + [jax.ShapeDtypeStruct(a.shape, F32) for a in (*tps, *gps)],
        compiler_params=_params(("arbitrary", "arbitrary")),
    )(*xs, *tps, *gps, *rts, *douts)
    res = list(res)
    return res[:nx], res[nx:nx + ntp], res[nx + ntp:]


def _rowwise(f, name, outs, tm, n_lat, n_x, n_tp, n_gp):
    def split(args):
        return (args[:n_x], args[n_x:n_x + n_tp], args[n_x + n_tp:n_x + n_tp + n_gp],
                args[n_x + n_tp + n_gp:])

    @jax.custom_vjp
    def op(*args):
        xs, tps, gps, rts = split(args)
        return tuple(_row_fwd(f, name + "_fwd", xs, tps, gps, rts, outs, tm, n_lat))

    def fwd(*args):
        return op(*args), args

    def bwd(args, g):
        xs, tps, gps, rts = split(args)
        dxs, dtps, dgps = _row_bwd(f, name + "_bwd", xs, tps, gps, rts, list(g), tm, n_lat)
        return (*dxs, *dtps, *dgps, *[jnp.zeros_like(t) for t in rts])

    op.defvjp(fwd, bwd)
    return op


def _shift_impl(u, k):
    n = u.shape[0]
    r = pltpu.roll(u, k % n, axis=0)
    row = lax.broadcasted_iota(jnp.int32, u.shape, 0)
    valid = (row >= k) if k > 0 else (row < n + k)
    return jnp.where(valid, r, 0.0)


@functools.partial(jax.custom_vjp, nondiff_argnums=(1,))
def _shift(u, k):
    return _shift_impl(u, k)


_shift.defvjp(lambda u, k: (_shift_impl(u, k), None), lambda k, _, g: (_shift_impl(g, -k),))


def _swap_impl(x):
    lane = lax.broadcasted_iota(jnp.int32, x.shape, 1)
    q = HEAD_DIM // 4
    return jnp.where((lane % (2 * q)) < q, pltpu.roll(x, HEAD_DIM - q, axis=1), pltpu.roll(x, q, axis=1))


@jax.custom_vjp
def _swap(x):
    return _swap_impl(x)


_swap.defvjp(lambda x: (_swap_impl(x), None), lambda _, g: (_swap_impl(g),))


def _conv_f(ul, uc, cw, cb):
    def conv(u):
        return (_shift(u, 2) * cw[0:1] + _shift(u, 1) * cw[1:2] + u * cw[2:3] + _shift(u, -1) * cw[3:4] + cb)
    return conv(ul), conv(uc)


def _conv_specs(B, T, D):
    u_spec = pl.BlockSpec((None, T, LANE), lambda j, b: (b, 0, j))
    cw_spec = pl.BlockSpec((4, LANE), lambda j, b: (0, j))
    cb_spec = pl.BlockSpec((1, LANE), lambda j, b: (0, j))
    return u_spec, cw_spec, cb_spec


def _conv_fwd(u, cw, cb, S):
    B, T, D = u.shape

    def body(u_ref, cw_ref, cb_ref, o_ref):
        vl, vc = _conv_f(u_ref[0:S, :], u_ref[S:T, :], cw_ref[...], cb_ref[...])
        o_ref[0:S, :] = vl
        o_ref[S:T, :] = vc

    u_spec, cw_spec, cb_spec = _conv_specs(B, T, D)
    return _call(body, name="conv_fwd", grid=(D // LANE, B), in_specs=[u_spec, cw_spec, cb_spec],
                 out_specs=u_spec, out_shape=jax.ShapeDtypeStruct(u.shape, F32),
                 compiler_params=_params(("parallel", "parallel")))(u, cw, cb)


def _conv_bwd(u, cw, cb, dv, S):
    B, T, D = u.shape

    def body(u_ref, cw_ref, cb_ref, dv_ref, du_ref, dcw_ref, dcb_ref):
        _, vjp = jax.vjp(_conv_f, u_ref[0:S, :], u_ref[S:T, :], cw_ref[...], cb_ref[...])
        dul, duc, dcw, dcb = vjp((dv_ref[0:S, :], dv_ref[S:T, :]))
        du_ref[0:S, :] = dul
        du_ref[S:T, :] = duc
        first = pl.program_id(1) == 0

        @pl.when(first)
        def _():
            dcw_ref[...] = dcw
            dcb_ref[...] = dcb

        @pl.when(jnp.logical_not(first))
        def _():
            dcw_ref[...] += dcw
            dcb_ref[...] += dcb

    u_spec, cw_spec, cb_spec = _conv_specs(B, T, D)
    return _call(body, name="conv_bwd", grid=(D // LANE, B), in_specs=[u_spec, cw_spec, cb_spec, u_spec],
                 out_specs=[u_spec, cw_spec, cb_spec],
                 out_shape=[jax.ShapeDtypeStruct(u.shape, F32), jax.ShapeDtypeStruct(cw.shape, F32),
                            jax.ShapeDtypeStruct(cb.shape, F32)],
                 compiler_params=_params(("parallel", "arbitrary")))(u, cw, cb, dv)


def _conv_op(S):
    @jax.custom_vjp
    def op(u, cw, cb):
        return _conv_fwd(u, cw, cb, S)

    def fwd(u, cw, cb):
        return op(u, cw, cb), (u, cw, cb)

    def bwd(res, g):
        u, cw, cb = res
        return tuple(_conv_bwd(u, cw, cb, g, S))

    op.defvjp(fwd, bwd)
    return op


SCAN_UNROLL = 4


def _group_scan(A, Bv, asc):
    row = lax.broadcasted_iota(jnp.int32, A.shape, 0)
    for s in (1, 2, 4):
        sh = s if asc else SUBLANE - s
        valid = (row >= s) if asc else (row < SUBLANE - s)
        A_sh = pltpu.roll(A, sh, axis=0)
        B_sh = pltpu.roll(Bv, sh, axis=0)
        Bv = jnp.where(valid, A * B_sh, 0.0) + Bv
        A = jnp.where(valid, A * A_sh, A)
    return A, Bv


def _chain_step(A, Bv, carry, asc):
    row = lax.broadcasted_iota(jnp.int32, A.shape, 0)
    A2, B2 = _group_scan(A, Bv, asc)
    h = A2 * carry + B2
    if asc:
        prev = jnp.where(row == 0, carry, pltpu.roll(h, 1, axis=0))
        return h, prev, h[SUBLANE - 1:SUBLANE, :]
    prev = jnp.where(row == SUBLANE - 1, carry, pltpu.roll(h, SUBLANE - 1, axis=0))
    return h, prev, h[0:1, :]


def _chain_loop(segments, step):
    carry = jnp.zeros((1, LANE), F32)
    for lo, hi, asc in segments:
        span = SUBLANE * SCAN_UNROLL
        assert (hi - lo) % span == 0

        def it(t, carry, lo=lo, hi=hi, asc=asc, span=span):
            base = lo + t * span if asc else hi - (t + 1) * span
            order = range(SCAN_UNROLL) if asc else reversed(range(SCAN_UNROLL))
            for j in order:
                carry = step(pl.multiple_of(base + SUBLANE * j, SUBLANE), carry, asc)
            return carry

        carry = lax.fori_loop(0, (hi - lo) // span, it, carry)
    return carry


def _scan_specs(T):
    return pl.BlockSpec((None, T, LANE), lambda j, b: (b, 0, j))


def _scan_fwd(a, b, S, reverse):
    B, T, D = a.shape
    asc = not reverse
    segments = [(S, T, asc), (0, S, asc)]

    def body(a_ref, b_ref, h_ref, hp_ref):
        def step(r0, carry, asc):
            rows = pl.ds(r0, SUBLANE)
            h, prev, carry = _chain_step(a_ref[rows, :], b_ref[rows, :], carry, asc)
            h_ref[rows, :] = h
            hp_ref[rows, :] = prev
            return carry
        _chain_loop(segments, step)

    spec = _scan_specs(T)
    return _call(body, name="scan_rev_fwd" if reverse else "scan_fwd_fwd", grid=(D // LANE, B),
                 in_specs=[spec, spec], out_specs=[spec, spec],
                 out_shape=[jax.ShapeDtypeStruct(a.shape, F32)] * 2,
                 compiler_params=_params(("parallel", "parallel")))(a, b)


def _scan_bwd(a, hp, dy, S, reverse):
    B, T, D = a.shape
    asc = reverse
    segments = [(0, S, asc), (S, T, asc)]

    def body(a_ref, hp_ref, dy_ref, da_ref, db_ref):
        def step(r0, carry, asc):
            rows = pl.ds(r0, SUBLANE)
            A, dy = a_ref[rows, :], dy_ref[rows, :]
            _, s_prev, carry = _chain_step(A, A * dy, carry, asc)
            g = dy + s_prev
            db_ref[rows, :] = g
            da_ref[rows, :] = g * hp_ref[rows, :]
            return carry
        _chain_loop(segments, step)

    spec = _scan_specs(T)
    return _call(body, name="scan_rev_bwd" if reverse else "scan_fwd_bwd", grid=(D // LANE, B),
                 in_specs=[spec, spec, spec], out_specs=[spec, spec],
                 out_shape=[jax.ShapeDtypeStruct(a.shape, F32)] * 2,
                 compiler_params=_params(("parallel", "parallel")))(a, hp, dy)


def _scan_op(S, reverse):
    @jax.custom_vjp
    def op(a, b):
        return _scan_fwd(a, b, S, reverse)[0]

    def fwd(a, b):
        h, hp = _scan_fwd(a, b, S, reverse)
        return h, (a, hp)

    def bwd(res, g):
        a, hp = res
        return tuple(_scan_bwd(a, hp, g, S, reverse))

    op.defvjp(fwd, bwd)
    return op


def _band_lo(qi, tq, S):
    span = tq + 2 * WINDOW
    return pl.multiple_of(jnp.clip(qi * tq - WINDOW, 0, S - span), LANE)


def _band_mask(qi, tq, lo, span, transposed, heads=1):
    shape = (span, heads * tq) if transposed else (heads * tq, span)
    assert tq & (tq - 1) == 0
    qpos = qi * tq + jnp.bitwise_and(lax.broadcasted_iota(jnp.int32, shape, 1 if transposed else 0), tq - 1)
    kpos = lo + lax.broadcasted_iota(jnp.int32, shape, 0 if transposed else 1)
    return jnp.abs(kpos - qpos) <= WINDOW


NT = (((1,), (1,)), ((), ()))
LOG2E = 1.4426950408889634
KEY_CHUNK = 1024


def _col_to_row(c):
    return jnp.broadcast_to(c, (c.shape[0], LANE)).T[0:1, :]


def _attn_fwd(q, k, v, sink, gate, S, band, side=None):
    B, T, HD = q.shape
    H = HD // HEAD_DIM
    L = T - S
    tq = L
    n_lq = S // tq
    span = tq + 2 * WINDOW
    c2 = HEAD_DIM ** -0.5 * LOG2E
    has_sink = sink is not None
    kc = min(KEY_CHUNK, S)
    hp = KV_GROUP if band else 1

    def body(*refs):
        if has_sink:
            q_ref, k_ref, v_ref, g_ref, s_ref, z_ref, o_ref, lse_ref = refs
        else:
            q_ref, k_ref, v_ref, g_ref, z_ref, o_ref, lse_ref = refs
        qi = pl.program_id(2)

        def online(qv, sk2, segs):
            def scores(seg):
                s = lax.dot_general(qv, k_ref[seg[0], :], NT, preferred_element_type=F32) * c2
                return s if seg[1] is None else jnp.where(seg[1], s, NEG_INF)

            m = l = acc = None
            s_next = scores(segs[0])
            for j, (rows, _) in enumerate(segs):
                s = s_next
                if j + 1 < len(segs):
                    s_next = scores(segs[j + 1])
                ms = jnp.max(s, axis=-1, keepdims=True)
                if m is None:
                    m = ms if sk2 is None else jnp.maximum(ms, sk2)
                    p = jnp.exp2(s - m)
                    l = jnp.sum(p, axis=-1, keepdims=True)
                    if sk2 is not None:
                        l = l + jnp.exp2(sk2 - m)
                    acc = jnp.dot(p.astype(BF16), v_ref[rows, :], preferred_element_type=F32)
                else:
                    m_new = jnp.maximum(m, ms)
                    alpha = jnp.exp2(m - m_new)
                    p = jnp.exp2(s - m_new)
                    l = alpha * l + jnp.sum(p, axis=-1, keepdims=True)
                    acc = alpha * acc + jnp.dot(p.astype(BF16), v_ref[rows, :], preferred_element_type=F32)
                    m = m_new
            return acc, m, l

        def joint(qv, sk2, segs):
            ss = []
            for rows, mask in segs:
                s = lax.dot_general(qv, k_ref[rows, :], NT, preferred_element_type=F32) * c2
                ss.append(s if mask is None else jnp.where(mask, s, NEG_INF))
            m = functools.reduce(jnp.maximum, [jnp.max(s, axis=-1, keepdims=True) for s in ss])
            if sk2 is not None:
                m = jnp.maximum(m, sk2)
            ps = [jnp.exp2(s - m) for s in ss]
            l = functools.reduce(jnp.add, [jnp.sum(p, axis=-1, keepdims=True) for p in ps])
            if sk2 is not None:
                l = l + jnp.exp2(sk2 - m)
            acc = functools.reduce(jnp.add, [jnp.dot(p.astype(BF16), v_ref[rows, :], preferred_element_type=F32)
                                             for p, (rows, _) in zip(ps, segs)])
            return acc, m, l

        def run(segs, softmax):
            for g in range(hp):
                lanes = slice(g * HEAD_DIM, (g + 1) * HEAD_DIM)
                sk2 = s_ref[g][:, 0:1] * LOG2E if has_sink else None
                acc, m, l = softmax(q_ref[:, lanes], sk2, segs)
                o = acc * (1.0 / l)
                gv = g_ref[:, lanes]
                o_ref[:, lanes] = o
                z_ref[:, lanes] = (o * (gv * jax.nn.sigmoid(gv))).astype(z_ref.dtype)
                lse_ref[g] = _col_to_row(m + jnp.log2(l))

        ctx_rows = pl.ds(S, L)

        @pl.when(qi < n_lq)
        def _():
            if band:
                lo = _band_lo(qi, tq, S)
                run([(ctx_rows, None), (pl.ds(lo, span), _band_mask(qi, tq, lo, span, False))], joint)
            else:
                run([(pl.ds(j * kc, kc), None) for j in range(S // kc)] + [(ctx_rows, None)], online)

        @pl.when(qi >= n_lq)
        def _():
            run([(ctx_rows, None)], joint)

    q_spec = pl.BlockSpec((None, tq, hp * HEAD_DIM), lambda b, h, i: (b, i, h))
    kv_spec = pl.BlockSpec((None, T, HEAD_DIM), lambda b, h, i: (b, 0, h * hp // KV_GROUP))
    in_specs = [q_spec, kv_spec, kv_spec, q_spec]
    args = [q, k, v, gate]
    if has_sink:
        in_specs.append(pl.BlockSpec((hp, 1, LANE), lambda b, h, i: (h, 0, 0)))
        args.append(sink)
    name = "attn_band_fwd" if band else "attn_dense_fwd"
    return _call_with_side(
        body, side, name=name, grid=(B, H // hp, T // tq), in_specs=in_specs, args=args,
        out_specs=[q_spec, q_spec, pl.BlockSpec((None, hp, 1, tq), lambda b, h, i: (b, h, 0, i))],
        out_shape=[jax.ShapeDtypeStruct(q.shape, BF16), jax.ShapeDtypeStruct(q.shape, F32),
                   jax.ShapeDtypeStruct((B, H, 1, T), F32)],
        scratch_shapes=[], dims=("parallel", "parallel", "parallel"))


def _attn_bwd(q, k, v, sink, gate, dz, o, lse, S, band, side=None):
    B, T, HD = q.shape
    H = HD // HEAD_DIM
    KVH = H // KV_GROUP
    L = T - S
    tq = L
    n_lq = S // tq
    span = tq + 2 * WINDOW
    scale = HEAD_DIM ** -0.5
    c2 = scale * LOG2E
    has_sink = sink is not None
    kc = min(KEY_CHUNK, S)
    hp = KV_GROUP if band else 1

    def body(*refs):
        if has_sink:
            (q_ref, k_ref, v_ref, g_ref, dz_ref, o_ref, lse_ref, s_ref,
             dq_ref, dk_ref, dv_ref, dg_ref, ds_ref, *scr) = refs
        else:
            q_ref, k_ref, v_ref, g_ref, dz_ref, o_ref, lse_ref, dq_ref, dk_ref, dv_ref, dg_ref, *scr = refs
        g, qi = pl.program_id(2), pl.program_id(3)

        @pl.when(jnp.logical_and(g == 0, qi == 0))
        def _():
            dk_ref[...] = jnp.zeros_like(dk_ref)
            dv_ref[...] = jnp.zeros_like(dv_ref)
            if not band:
                scr[0][...] = k_ref[...].astype(F32).T.astype(BF16)

        def run(segs):
            lanes = [slice(h * HEAD_DIM, (h + 1) * HEAD_DIM) for h in range(hp)]

            def stack(ref, axis=0):
                return jnp.concatenate([ref[:, ln] for ln in lanes], axis=axis) if hp > 1 else ref[...]

            qv, gv, ov, dzv = stack(q_ref), stack(g_ref), stack(o_ref), stack(dz_ref).astype(F32)
            sg = jax.nn.sigmoid(gv)
            dof = dzv * (gv * sg)
            dgate = dzv * ov * (sg * (1.0 + gv * (1.0 - sg)))
            dov = dof.astype(BF16)
            lse2 = jnp.concatenate([lse_ref[h] for h in range(hp)], axis=1) if hp > 1 else lse_ref[0]
            delta = _col_to_row(jnp.sum(dof * ov, axis=-1, keepdims=True))

            def head(seg):
                rows = pl.ds(seg[0], seg[1])
                s = lax.dot_general(k_ref[rows, :], qv, NT, preferred_element_type=F32) * c2
                if seg[2] is not None:
                    s = jnp.where(seg[2], s, NEG_INF)
                return s, lax.dot_general(v_ref[rows, :], dov, NT, preferred_element_type=F32)

            dq, dqT = None, None
            nxt = head(segs[0])
            for j, (lo, n, _) in enumerate(segs):
                rows = pl.ds(lo, n)
                s, dp = nxt
                if j + 1 < len(segs):
                    nxt = head(segs[j + 1])
                p = jnp.exp2(s - lse2)
                ds = p * (dp - delta)
                dsb = ds.astype(BF16)
                dv_ref[rows, :] += jnp.dot(p.astype(BF16), dov, preferred_element_type=F32)
                dk_ref[rows, :] += jnp.dot(dsb, qv, preferred_element_type=F32) * scale
                if band:
                    part = jnp.dot(ds.T.astype(BF16), k_ref[rows, :], preferred_element_type=F32)
                    dq = part if dq is None else dq + part
                else:
                    part = jnp.dot(scr[0][:, lo:lo + n], dsb, preferred_element_type=F32)
                    dqT = part if dqT is None else dqT + part
            dq = (dq if band else dqT.T) * scale
            for h, ln in enumerate(lanes):
                rows = slice(h * tq, (h + 1) * tq)
                dq_ref[:, ln] = dq[rows].astype(dq_ref.dtype)
                dg_ref[:, ln] = dgate[rows]
                if has_sink:
                    psk = jnp.exp2(s_ref[h][:, 0:1] * LOG2E - lse2[:, rows])
                    dsk = jnp.broadcast_to(-jnp.sum(psk * delta[:, rows], axis=1, keepdims=True), (1, LANE))

                    @pl.when(qi == 0)
                    def _(h=h, dsk=dsk):
                        ds_ref[h] = dsk

                    @pl.when(qi > 0)
                    def _(h=h, dsk=dsk):
                        ds_ref[h] += dsk

        @pl.when(qi < n_lq)
        def _():
            if band:
                lo = _band_lo(qi, tq, S)
                run([(S, L, None), (lo, span, _band_mask(qi, tq, lo, span, True, hp))])
            else:
                run([(j * kc, kc, None) for j in range(S // kc)] + [(S, L, None)])

        @pl.when(qi >= n_lq)
        def _():
            run([(S, L, None)])

    ng = KV_GROUP // hp
    q_spec = pl.BlockSpec((None, tq, hp * HEAD_DIM), lambda b, kv, g, i: (b, i, kv * ng + g))
    kv_spec = pl.BlockSpec((None, T, HEAD_DIM), lambda b, kv, g, i: (b, 0, kv))
    lse_spec = pl.BlockSpec((None, hp, 1, tq), lambda b, kv, g, i: (b, kv * ng + g, 0, i))
    in_specs = [q_spec, kv_spec, kv_spec, q_spec, q_spec, q_spec, lse_spec]
    out_specs = [q_spec, kv_spec, kv_spec, q_spec]
    out_shape = [jax.ShapeDtypeStruct(q.shape, BF16), jax.ShapeDtypeStruct(k.shape, F32),
                 jax.ShapeDtypeStruct(v.shape, F32), jax.ShapeDtypeStruct(q.shape, F32)]
    args = [q, k, v, gate, dz, o, lse]
    if has_sink:
        in_specs.append(pl.BlockSpec((hp, 1, LANE), lambda b, kv, g, i: (kv * ng + g, 0, 0)))
        args.append(sink)
        out_specs.append(pl.BlockSpec((None, hp, 1, LANE), lambda b, kv, g, i: (b, kv * ng + g, 0, 0)))
        out_shape.append(jax.ShapeDtypeStruct((B, H, 1, LANE), F32))
    res = _call_with_side(
        body, side, name="attn_band_bwd" if band else "attn_dense_bwd", grid=(B, KVH, ng, T // tq),
        in_specs=in_specs, args=args, out_specs=out_specs, out_shape=out_shape,
        scratch_shapes=[] if band else [pltpu.VMEM((HEAD_DIM, T), BF16)],
        dims=("parallel", "parallel", "arbitrary", "arbitrary"))
    return (res[0], res[1], res[2], res[3], res[4] if has_sink else None, res[-1] if side is not None else None)


def _prep_f(norm):
    def f(q, kv, *rest):
        q, kv = q.astype(F32), kv.astype(F32)
        kw = kv.shape[1] // 2
        k, v = kv[:, :kw], kv[:, kw:]
        if norm:
            qg, kg, cos, sin = rest
        else:
            (cos, sin), qg, kg = rest, None, None

        def heads(x, g):
            outs = []
            for h in range(x.shape[1] // HEAD_DIM):
                xh = x[:, h * HEAD_DIM:(h + 1) * HEAD_DIM]
                if g is not None:
                    xh = xh * lax.rsqrt(jnp.mean(xh * xh, axis=-1, keepdims=True) + EPS) * g
                outs.append(xh * cos + _swap(xh) * sin)
            return jnp.concatenate(outs, axis=1) if len(outs) > 1 else outs[0]

        return heads(q, qg), heads(k, kg), v

    return f


def _attn_branch(S, tm, band, norm, has_sink, carries=False):
    f = _prep_f(norm)
    name = "band" if band else "dense"

    def prep(q, kv, gains, tabs):
        outs = [(q.shape[2], BF16), (kv.shape[2] // 2, BF16), (kv.shape[2] // 2, BF16)]
        return _row_fwd(f, "prep_" + name + "_fwd", [q, kv], [], list(gains), list(tabs), outs, tm, S)

    def unpack(args):
        q, kv, gate = args[:3]
        rest = list(args[3:])
        gains = [rest.pop(0), rest.pop(0)] if norm else []
        sink = rest.pop(0) if has_sink else None
        pack = rest.pop(0) if carries else None
        return q, kv, gate, gains, sink, pack, rest

    def sink_lanes(sink):
        return None if sink is None else jnp.broadcast_to(sink[:, None, None], (sink.shape[0], 1, LANE))

    def run_fwd(args):
        q, kv, gate, gains, sink, pack, tabs = unpack(args)
        qp, kp, vp = prep(q, kv, gains, tabs)
        side = (pack.astype(BF16), True) if carries else None
        res = _attn_fwd(qp, kp, vp, sink_lanes(sink), gate, S, band, side)
        return ((res[0], res[3]) if carries else res[0]), (args, qp, kp, vp, res[1], res[2])

    @jax.custom_vjp
    def op(*args):
        return run_fwd(args)[0]

    def fwd(*args):
        return run_fwd(args)

    def bwd(res, ct):
        args, qp, kp, vp, o, lse = res
        q, kv, gate, gains, sink, pack, tabs = unpack(args)
        dz, side = (ct[0], (ct[1], False)) if carries else (ct, None)
        dqp, dkp, dvp, dgate, dsk, recv = _attn_bwd(qp, kp, vp, sink_lanes(sink), gate, dz, o, lse, S, band, side)
        dxs, _, dgains = _row_bwd(f, "prep_" + name + "_bwd", [q, kv], [], list(gains), list(tabs),
                                  [dqp, dkp, dvp], tm, S)
        out = list(dxs) + [dgate] + list(dgains)
        if has_sink:
            out.append(jnp.sum(dsk[:, :, 0, 0], axis=0))
        if carries:
            out.append(_sum_slots(recv, "sum_grads"))
        return (*out, *[jnp.zeros_like(t) for t in tabs])

    op.defvjp(fwd, bwd)
    return op


def _final_loss(S, tm):
    def run(X, target, g):
        B, T, D = X.shape
        n_lat_tiles = S // tm

        def lossf(x, gg, tgt):
            y = x * lax.rsqrt(jnp.mean(x * x, axis=-1, keepdims=True) + EPS) * gg
            err = y - tgt
            return 0.5 * jnp.sum(jnp.sum(err * err, axis=-1, keepdims=True), axis=0, keepdims=True) / D

        def body(x_ref, t_ref, g_ref, loss_ref, dx_ref, dg_ref):
            b, i = pl.program_id(0), pl.program_id(1)
            tgt = t_ref[...]
            val, vjp = jax.vjp(lambda x, gg: lossf(x, gg, tgt), x_ref[...], g_ref[...])
            dx, dg = vjp(jnp.ones((1, 1), F32))
            lat = (i < n_lat_tiles).astype(F32)
            dx_ref[...] = dx * lat

            @pl.when(i == 0)
            def _():
                loss_ref[...] = jnp.zeros_like(loss_ref)

            @pl.when(jnp.logical_and(b == 0, i == 0))
            def _():
                dg_ref[...] = jnp.zeros_like(dg_ref)

            loss_ref[...] += jnp.broadcast_to(val * lat, loss_ref.shape)
            dg_ref[...] += dg * lat

        x_spec = pl.BlockSpec((None, tm, D), lambda b, i: (b, i, 0))
        t_spec = pl.BlockSpec((None, tm, D), lambda b, i: (b, jnp.minimum(i, n_lat_tiles - 1), 0))
        g_spec = pl.BlockSpec((1, D), lambda b, i: (0, 0))
        loss, dx, dg = _call(
            body, name="final_loss", grid=(B, T // tm), in_specs=[x_spec, t_spec, g_spec],
            out_specs=[pl.BlockSpec((None, 1, LANE), lambda b, i: (b, 0, 0)), x_spec, g_spec],
            out_shape=[jax.ShapeDtypeStruct((B, 1, LANE), F32), jax.ShapeDtypeStruct(X.shape, F32),
                       jax.ShapeDtypeStruct(g.shape, F32)],
            compiler_params=_params(("arbitrary", "arbitrary")))(X, target, g)
        return jnp.sum(loss[:, 0, 0]), dx, dg

    @jax.custom_vjp
    def op(X, target, g):
        return run(X, target, g)[0]

    def fwd(X, target, g):
        loss, dx, dg = run(X, target, g)
        return loss, (dx, dg, target)

    def bwd(res, ct):
        dx, dg, target = res
        return ct * dx, jnp.zeros_like(target), ct * dg

    op.defvjp(fwd, bwd)
    return op


def _prenorm_f(D):
    def f(x, mp, g):
        y = x * lax.rsqrt(jnp.mean(x * x, axis=-1, keepdims=True) + EPS) * g
        return (y * (1.0 + mp[:, D:2 * D]) + mp[:, 0:D],)
    return f


def _resid_f(D):
    def f(x, y, mp):
        return (x + mp[:, 2 * D:3 * D] * y,)
    return f


def _decay_impl(la):
    t = jnp.tanh(la)
    x = -2.0 * t / (1.0 - t)
    r = lax.rsqrt(x)
    return jnp.exp(la), x * r, r


@jax.custom_vjp
def _decay(la):
    a, bc, _ = _decay_impl(la)
    return a, bc


def _decay_fwd(la):
    a, bc, r = _decay_impl(la)
    return (a, bc), (a, r)


def _decay_bwd(res, ct):
    a, r = res
    return (ct[0] * a - ct[1] * (a * a * r),)


_decay.defvjp(_decay_fwd, _decay_bwd)


def _gate2_f(hf, hr, g):
    return ((hf + hr) * (g * jax.nn.sigmoid(g)),)


def _merge_f(D):
    def f(m3, pa, pb, pc):
        return (jax.nn.sigmoid(m3[:, 0:D]) * pa.astype(F32) + jax.nn.sigmoid(m3[:, D:2 * D]) * pb.astype(F32)
                + jax.nn.sigmoid(m3[:, 2 * D:3 * D]) * pc.astype(F32),)
    return f


def _coef_f(D):
    nblk = D // GATE_BLOCK

    def f(v, pv, wm):
        vb = v.astype(BF16)

        def gate(k):
            cols = []
            for j in range(nblk):
                r0 = (k * nblk + j) * GATE_BLOCK
                cols.append(jnp.dot(vb[:, j * GATE_BLOCK:(j + 1) * GATE_BLOCK],
                                    wm[r0:r0 + GATE_BLOCK, :].astype(BF16), preferred_element_type=F32))
            return jnp.concatenate(cols, axis=1)

        outs = []
        for d in range(2):
            r = jax.nn.sigmoid(gate(d) + pv[d:d + 1])
            i = jax.nn.sigmoid(gate(2 + d) + pv[2 + d:3 + d])
            a, bc = _decay(r * pv[4 + d:5 + d])
            outs += [a, bc * (i * v)]
        return tuple(outs)

    return f


def _gate_blocks(w):
    per = GATE_BLOCK // LRU_BLOCK_W
    n = w.shape[1]
    w5 = w.reshape(2, n // per, per, LRU_BLOCK_W, LRU_BLOCK_W)
    dense = jnp.einsum("djiab,ik->djiakb", w5, jnp.eye(per, dtype=w.dtype))
    return dense.reshape(2, (n // per) * GATE_BLOCK, GATE_BLOCK)


def _exchange_shape(x, gather):
    return jax.ShapeDtypeStruct((N_DEV,) + x.shape if gather else x.shape, x.dtype)


EXCHANGE_SEMS = [pltpu.SemaphoreType.DMA((N_DEV - 1,)), pltpu.SemaphoreType.DMA((N_DEV - 1,)),
                 pltpu.SemaphoreType.DMA(())]


def _exchange_ops(x_ref, o_ref, send_sems, recv_sems, local_sem, gather):
    mx, my, mc = lax.axis_index("x"), lax.axis_index("y"), lax.axis_index("c")
    me = 4 * mx + 2 * my + mc

    def src(p):
        return x_ref if gather else x_ref.at[p]

    local = pltpu.make_async_copy(src(me), o_ref.at[me], local_sem)
    sends, recvs = [], []
    for k in range(1, N_DEV):
        px = 1 - mx if k & 4 else mx
        py = 1 - my if k & 2 else my
        pc = 1 - mc if k & 1 else mc
        peer = 4 * px + 2 * py + pc
        sends.append(pltpu.make_async_remote_copy(
            src_ref=src(peer), dst_ref=o_ref.at[me], send_sem=send_sems.at[k - 1],
            recv_sem=recv_sems.at[k - 1], device_id=(px, py, pc), device_id_type=pl.DeviceIdType.MESH))
        recvs.append(pltpu.make_async_remote_copy(
            src_ref=src(peer), dst_ref=o_ref.at[peer], send_sem=send_sems.at[k - 1],
            recv_sem=recv_sems.at[k - 1], device_id=(px, py, pc), device_id_type=pl.DeviceIdType.MESH))

    def start():
        local.start()
        for cp in sends:
            cp.start()

    def finish():
        for cp in recvs:
            cp.wait_recv()
        for cp in sends:
            cp.wait_send()
        local.wait()

    return start, finish


def _exchange(x, name, gather):
    def body(x_ref, o_ref, send_sems, recv_sems, local_sem):
        start, finish = _exchange_ops(x_ref, o_ref, send_sems, recv_sems, local_sem, gather)
        start()
        finish()

    hbm = pl.BlockSpec(memory_space=pltpu.HBM)
    return _call(body, name=name, in_specs=[hbm], out_specs=hbm, out_shape=_exchange_shape(x, gather),
                 scratch_shapes=EXCHANGE_SEMS)(x)


def _call_with_side(body, side, *, name, grid, in_specs, args, out_specs, out_shape, scratch_shapes, dims):
    if side is None:
        return _call(body, name=name, grid=grid, in_specs=in_specs, out_specs=out_specs, out_shape=out_shape,
                     scratch_shapes=scratch_shapes, compiler_params=_params(dims))(*args)
    x, gather = side
    n_in, n_out, n_scr = len(in_specs), len(out_specs), len(scratch_shapes)

    def wrapped(*refs):
        ins, x_ref = refs[:n_in], refs[n_in]
        outs, o_ref = refs[n_in + 1:n_in + 1 + n_out], refs[n_in + 1 + n_out]
        scr = refs[n_in + 2 + n_out:n_in + 2 + n_out + n_scr]
        start, finish = _exchange_ops(x_ref, o_ref, *refs[n_in + 2 + n_out + n_scr:], gather)
        ids = [pl.program_id(a) for a in range(len(grid))]
        first = functools.reduce(jnp.logical_and, [i == 0 for i in ids])
        last = functools.reduce(jnp.logical_and, [i == g - 1 for i, g in zip(ids, grid)])
        pl.when(first)(start)
        body(*ins, *outs, *scr)
        pl.when(last)(finish)

    hbm = pl.BlockSpec(memory_space=pltpu.HBM)
    return _call(wrapped, name=name + "_xchg", grid=grid, in_specs=list(in_specs) + [hbm],
                 out_specs=list(out_specs) + [hbm], out_shape=list(out_shape) + [_exchange_shape(x, gather)],
                 scratch_shapes=list(scratch_shapes) + EXCHANGE_SEMS,
                 compiler_params=_params(("arbitrary",) * len(grid)))(*args, x)


def _adamw(gs, w, m, v, name):
    n, R, C = gs.shape
    tr = _pick(R, 256, 16)

    def body(g_ref, w_ref, m_ref, v_ref, go_ref, d_ref, mo_ref, vo_ref):
        g = g_ref[0].astype(F32)
        for p in range(1, n):
            g = g + g_ref[p].astype(F32)
        m2 = ADAM_B1 * m_ref[...] + (1.0 - ADAM_B1) * g
        v2 = ADAM_B2 * v_ref[...] + (1.0 - ADAM_B2) * (g * g)
        m_hat = m2 / (1.0 - ADAM_B1 ** ADAM_STEP)
        v_hat = v2 / (1.0 - ADAM_B2 ** ADAM_STEP)
        go_ref[...] = g
        d_ref[...] = -ADAM_LR * (m_hat / (jnp.sqrt(v_hat) + ADAM_EPS) + ADAM_WD * w_ref[...])
        mo_ref[...] = m2
        vo_ref[...] = v2

    spec = pl.BlockSpec((tr, C), lambda i: (i, 0))
    return _call(body, name=name, grid=(R // tr,),
                 in_specs=[pl.BlockSpec((n, tr, C), lambda i: (0, i, 0)), spec, spec, spec],
                 out_specs=[spec] * 4, out_shape=[jax.ShapeDtypeStruct((R, C), F32)] * 4,
                 compiler_params=_params(("parallel",)))(gs, w, m, v)


def _sum_slots(gs, name):
    n, R, C = gs.shape
    tr = _pick(R, 256, 16)

    def body(g_ref, o_ref):
        g = g_ref[0].astype(F32)
        for p in range(1, n):
            g = g + g_ref[p].astype(F32)
        o_ref[...] = g

    return _call(body, name=name, grid=(R // tr,), in_specs=[pl.BlockSpec((n, tr, C), lambda i: (0, i, 0))],
                 out_specs=pl.BlockSpec((tr, C), lambda i: (i, 0)), out_shape=jax.ShapeDtypeStruct((R, C), F32),
                 compiler_params=_params(("parallel",)))(gs)


def _gather_op(name):
    @jax.custom_vjp
    def op(pack):
        return _exchange(pack.astype(BF16), name, True)

    def fwd(pack):
        return op(pack), None

    def bwd(_, ct):
        return (_sum_slots(_exchange(ct, name + "_transpose", False), "sum_grads"),)

    op.defvjp(fwd, bwd)
    return op


def _rope_tables(S, L):
    P = HEAD_DIM // 4
    rows = S // GRID_W
    row_id = jnp.repeat(jnp.arange(rows), GRID_W)
    col_id = jnp.tile(jnp.arange(GRID_W), rows)
    inv = ROPE_THETA ** (-jnp.arange(P, dtype=F32) / P)
    ar, ac = row_id[:, None] * inv, col_id[:, None] * inv
    cos = jnp.concatenate([jnp.cos(ar), jnp.cos(ar), jnp.cos(ac), jnp.cos(ac)], axis=1)
    sin = jnp.concatenate([-jnp.sin(ar), jnp.sin(ar), -jnp.sin(ac), jnp.sin(ac)], axis=1)
    cos = jnp.concatenate([cos, jnp.ones((L, HEAD_DIM), F32)], axis=0)
    sin = jnp.concatenate([sin, jnp.zeros((L, HEAD_DIM), F32)], axis=0)
    return cos, sin


SQ_TILES = (((2176, 1024, 1024), (2176, 1024, 1024), (1024, 1024, 1088)),) * DEPTH


def _layer_shard_shapes(D, IN):
    return (("w_in", (D, IN // N_DEV), 1), ("w_mod", (D, 3 * D // N_DEV), 1),
            ("w_branch", (3, D // N_DEV, D), 1), ("w_out", (D // N_DEV, D), 0))


def _pack_layer(w, l, D):
    return jnp.concatenate([w[n][l].reshape(-1) for n in BIG_PACK]).reshape(-1, D)


def _unpack_shards(pack, D, IN):
    flat, out, off = pack.reshape(-1), {}, 0
    for n, shape, _ in _layer_shard_shapes(D, IN):
        sz = shape[0] * shape[1] * (shape[2] if len(shape) > 2 else 1)
        out[n] = flat[off:off + sz].reshape(shape)
        off += sz
    return out


def _unpack_gathered(g, D, IN):
    flat, out, off = g.reshape(N_DEV, -1), {}, 0
    for n, shape, axis in _layer_shard_shapes(D, IN):
        sz = shape[0] * shape[1] * (shape[2] if len(shape) > 2 else 1)
        out[n] = _unshard(flat[:, off:off + sz].reshape((N_DEV,) + shape), axis)
        off += sz
    return out


def _loss_fn(packs, p, x, c, ctx, target):
    B, S, D = x.shape
    L = ctx.shape[1]
    T = S + L
    tm = min(L, 256)
    KVW = D // KV_GROUP
    widths = (D, D, D, 2 * KVW, D, D, 2 * KVW, D, 3 * D)
    part_dtypes = (F32, F32, BF16, BF16, F32, BF16, BF16, F32, F32)
    IN = sum(widths)
    cos, sin = _rope_tables(S, L)

    X = jnp.concatenate([x, ctx], axis=1)
    sc, scc = jax.nn.silu(c), jax.nn.silu(p["c_ctx"])
    A = jnp.concatenate([scc[None], sc, jnp.zeros((SUBLANE - 1 - B, D), F32)], axis=0)
    gathered = _gather_op("gather_layer0")(packs[0])
    for l in range(DEPTH):
        big = _unpack_gathered(gathered, D, IN)
        mod = _matmul("mm_mod")(A, big["w_mod"]) + p["b_mod"][l]
        modp = jnp.stack([jnp.broadcast_to(mod[0], (B, 3 * D)), mod[1:1 + B]], axis=1)[:, :, None, :]
        (h,) = _rowwise(_prenorm_f(D), "prenorm", [(D, BF16)], tm, S, 1, 1, 1)(X, modp, p["norm_g"][l][None])
        parts = _in_proj(widths, part_dtypes)(h.reshape(B * T, D), big["w_in"])
        uA, gA, qB, kvB, gB, qC, kvC, gC, m3 = [t.reshape(B, T, -1) for t in parts]
        u = _conv_op(S)(uA, p["conv_w"][l], p["conv_b"][l][None])
        nsp = -LRU_C * jax.nn.softplus(-p["lru_lambda"][l])
        pv = jnp.concatenate([p["lru_ba"][l], p["lru_bx"][l], nsp, jnp.zeros((2, D), F32)], axis=0)
        wm = jnp.concatenate([_gate_blocks(p["lru_wa"][l]), _gate_blocks(p["lru_wx"][l])], axis=0)
        wm = wm.reshape(-1, GATE_BLOCK)
        af, bf, ar, br = _rowwise(_coef_f(D), "lru_coef", [(D, F32)] * 4, tm, S, 1, 0, 2)(u, pv, wm)
        hf = _scan_op(S, False)(af, bf)
        hr = _scan_op(S, True)(ar, br)
        (zA,) = _rowwise(_gate2_f, "gate_a", [(D, BF16)], tm, S, 3, 0, 0)(hf, hr, gA)
        zB = _attn_branch(S, tm, True, False, True)(qB, kvB, gB, p["attn_sink"][l], cos, sin)
        qkv_c = (qC, kvC, gC, p["q_norm_g"][l][None], p["k_norm_g"][l][None])
        if l + 1 < DEPTH:
            zC, gathered = _attn_branch(S, tm, False, True, False, True)(*qkv_c, packs[l + 1], cos, sin)
        else:
            zC = _attn_branch(S, tm, False, True, False)(*qkv_c, cos, sin)
        pr = [_matmul("mm_branch", *SQ_TILES[l], out_dtype=BF16)(z.reshape(B * T, D), big["w_branch"][n])
              .reshape(B, T, D) for n, z in enumerate((zA, zB, zC))]
        (mg,) = _rowwise(_merge_f(D), "merge", [(D, BF16)], tm, S, 4, 0, 0)(m3, *pr)
        y = _matmul("mm_out", *SQ_TILES[l])(mg.reshape(B * T, D), big["w_out"]).reshape(B, T, D)
        (X,) = _rowwise(_resid_f(D), "resid", [(D, F32)], tm, S, 2, 1, 0)(X, y, modp)
    return _final_loss(S, tm)(X, target, p["final_g"][None])


def _shard_axis(name):
    return {"w_mod": 2, "w_in": 2, "conv_w": 2, "lru_ba": 2, "lru_bx": 2, "lru_lambda": 2,
            "w_branch": 2, "w_out": 1}.get(name)


def _unshard(g, axis):
    full = jnp.moveaxis(g, 0, axis)
    shape = list(full.shape)
    shape[axis:axis + 2] = [shape[axis] * shape[axis + 1]]
    return full.reshape(shape)


def _reshard(full, axis):
    shape = list(full.shape)
    shape[axis:axis + 1] = [N_DEV, shape[axis] // N_DEV]
    return jnp.moveaxis(full.reshape(shape), axis, 0)


def _pad_to(v, n):
    return jnp.concatenate([v, jnp.zeros((n - v.shape[0],), v.dtype)]) if n > v.shape[0] else v


def _step(x, c, ctx, target, w, m, v):
    D = x.shape[2]
    IN = w["w_in"].shape[2] * N_DEV
    full = {n: w[n] for n in REPLICATED}
    small_local = jnp.concatenate([w[n].reshape(-1) for n in SMALL_SHARDED])
    small_all = _exchange(small_local.reshape(-1, LANE), "gather_small", True).reshape(N_DEV, -1)
    off = 0
    for n in SMALL_SHARDED:
        sz = w[n].size
        full[n] = _unshard(small_all[:, off:off + sz].reshape((N_DEV,) + w[n].shape), _shard_axis(n))
        off += sz
    packs = [_pack_layer(w, l, D) for l in range(DEPTH)]

    loss, (gpacks, gp, gx) = jax.value_and_grad(_loss_fn, argnums=(0, 1, 2))(packs, full, x, c, ctx, target)
    loss = lax.psum(loss, AXES)

    out = {}
    gshards = [_unpack_shards(g, D, IN) for g in gpacks]
    for n in BIG_PACK:
        g = jnp.stack([gs[n] for gs in gshards])
        C = w[n].shape[-1]
        res = _adamw(g.reshape(1, -1, C), w[n].reshape(-1, C), m[n].reshape(-1, C), v[n].reshape(-1, C),
                     "adamw_" + n)
        out[n] = [r.reshape(w[n].shape) for r in res]

    rep = jnp.concatenate([gp[n].reshape(-1) for n in REPLICATED])
    n_rep = rep.shape[0]
    chunk = -(-n_rep // (N_DEV * LANE)) * LANE
    rep = _pad_to(rep, N_DEV * chunk).reshape(N_DEV, chunk)
    shards = jnp.concatenate([_reshard(gp[n], _shard_axis(n)).reshape(N_DEV, -1) for n in SMALL_SHARDED], axis=1)
    n_sh = shards.shape[1]
    recv = _exchange(jnp.concatenate([rep, shards], axis=1).reshape(N_DEV, -1, LANE), "scatter_small", False)
    wl =jnp.concatenate([w[n].reshape(-1) for n in SMALL_SHARDED])
    ml = jnp.concatenate([m[n].reshape(-1) for n in SMALL_SHARDED])
    vl = jnp.concatenate([v[n].reshape(-1) for n in SMALL_SHARDED])
    rows = recv.shape[1]
    rrows = chunk // LANE
    g_sh, d_sh, m_sh, v_sh = _adamw(recv[:, rrows:], wl.reshape(-1, LANE), ml.reshape(-1, LANE),
                                    vl.reshape(-1, LANE), "adamw_small_sharded")
    g_rep8 = _sum_slots(recv[:, :rrows], "sum_replicated")
    g_rep = _exchange(g_rep8, "gather_replicated", True).reshape(-1)
    wr = _pad_to(jnp.concatenate([w[n].reshape(-1) for n in REPLICATED]), N_DEV * chunk)
    mr = _pad_to(jnp.concatenate([m[n].reshape(-1) for n in REPLICATED]), N_DEV * chunk)
    vr = _pad_to(jnp.concatenate([v[n].reshape(-1) for n in REPLICATED]), N_DEV * chunk)
    res_rep = _adamw(g_rep.reshape(1, -1, LANE), wr.reshape(-1, LANE), mr.reshape(-1, LANE), vr.reshape(-1, LANE),
                     "adamw_replicated")
    off = 0
    for n in REPLICATED:
        sz = w[n].size
        out[n] = [r.reshape(-1)[off:off + sz].reshape(w[n].shape) for r in res_rep]
        off += sz
    off = 0
    for n in SMALL_SHARDED:
        sz = w[n].size
        out[n] = [r.reshape(-1)[off:off + sz].reshape(w[n].shape) for r in (g_sh, d_sh, m_sh, v_sh)]
        off += sz
    assert off == n_sh and rows == rrows + n_sh // LANE
    return (loss, gx, *[out[n][0] for n in WEIGHTS], *[out[n][1] for n in WEIGHTS],
            *[out[n][2] for n in WEIGHTS], *[out[n][3] for n in WEIGHTS])


def kernel(x, c, ctx, c_ctx, norm_g, w_mod, b_mod, w_in, conv_w, conv_b, lru_wa, lru_ba, lru_wx, lru_bx, lru_lambda, attn_sink, q_norm_g, k_norm_g, w_branch, w_out, final_g, loss_target, m_c_ctx, m_norm_g, m_w_mod, m_b_mod, m_w_in, m_conv_w, m_conv_b, m_lru_wa, m_lru_ba, m_lru_wx, m_lru_bx, m_lru_lambda, m_attn_sink, m_q_norm_g, m_k_norm_g, m_w_branch, m_w_out, m_final_g, v_c_ctx, v_norm_g, v_w_mod, v_b_mod, v_w_in, v_conv_w, v_conv_b, v_lru_wa, v_lru_ba, v_lru_wx, v_lru_bx, v_lru_lambda, v_attn_sink, v_q_norm_g, v_k_norm_g, v_w_branch, v_w_out, v_final_g):
    w = dict(zip(WEIGHTS, (c_ctx, norm_g, w_mod, b_mod, w_in, conv_w, conv_b, lru_wa, lru_ba, lru_wx, lru_bx,
                           lru_lambda, attn_sink, q_norm_g, k_norm_g, w_branch, w_out, final_g)))
    m = dict(zip(WEIGHTS, (m_c_ctx, m_norm_g, m_w_mod, m_b_mod, m_w_in, m_conv_w, m_conv_b, m_lru_wa, m_lru_ba,
                           m_lru_wx, m_lru_bx, m_lru_lambda, m_attn_sink, m_q_norm_g, m_k_norm_g, m_w_branch,
                           m_w_out, m_final_g)))
    v = dict(zip(WEIGHTS, (v_c_ctx, v_norm_g, v_w_mod, v_b_mod, v_w_in, v_conv_w, v_conv_b, v_lru_wa, v_lru_ba,
                           v_lru_wx, v_lru_bx, v_lru_lambda, v_attn_sink, v_q_norm_g, v_k_norm_g, v_w_branch,
                           v_w_out, v_final_g)))
    return _step(x, c, ctx, loss_target, w, m, v)
```

```python
import functools
import math

import jax
import jax.numpy as jnp
from jax import lax
from jax.experimental import pallas as pl
from jax.experimental.pallas import tpu as pltpu

F32 = jnp.float32
BF16 = jnp.bfloat16

AXES = ("x", "y", "c")
N_DEV = 8
DEPTH = 4
HEAD_DIM = 128
GRID_W = 64
WINDOW = 128
LRU_BLOCK_W = 64
GATE_BLOCK = 256
LRU_C = 8.0
ROPE_THETA = 10000.0
EPS = 1e-6
NEG_INF = -1e30
KV_GROUP = 4
LANE = 128
SUBLANE = 8
VMEM_LIMIT = 56 * 1024 * 1024

ADAM_LR = 0.001
ADAM_B1 = 0.9
ADAM_B2 = 0.999
ADAM_EPS = 1e-08
ADAM_WD = 0.01
ADAM_STEP = 10

WEIGHTS = ("c_ctx", "norm_g", "w_mod", "b_mod", "w_in", "conv_w", "conv_b", "lru_wa", "lru_ba", "lru_wx",
           "lru_bx", "lru_lambda", "attn_sink", "q_norm_g", "k_norm_g", "w_branch", "w_out", "final_g")
BIG_PACK = ("w_in", "w_mod", "w_branch", "w_out")
SMALL_SHARDED = ("conv_w", "lru_ba", "lru_bx", "lru_lambda")
REPLICATED = ("c_ctx", "norm_g", "b_mod", "conv_b", "lru_wa", "lru_wx", "attn_sink", "q_norm_g", "k_norm_g",
              "final_g")


def _call(body, **kw):
    return pl.pallas_call(body, **kw)


def _params(dims=None, vmem=VMEM_LIMIT):
    return pltpu.CompilerParams(dimension_semantics=dims, vmem_limit_bytes=vmem)


def _pick(n, target, mult):
    for t in range(min(n, target), 0, -1):
        if n % t == 0 and t % mult == 0:
            return t
    return n


def _mm(a, b, *, name, ta=False, tb=False, out_dtype=F32, tm=512, tn=1024, tk=1024):
    M, K = (a.shape[1], a.shape[0]) if ta else a.shape
    N = b.shape[0] if tb else b.shape[1]
    assert (b.shape[1] if tb else b.shape[0]) == K
    tm = _pick(M, tm, LANE if ta else 16)
    tn = _pick(N, tn, LANE)
    tk = _pick(K, tk, 16 if ta and not tb else LANE)
    nk = K // tk
    dn = (((0 if ta else 1,), (1 if tb else 0,)), ((), ()))

    def body(a_ref, b_ref, o_ref, *acc):
        r = lax.dot_general(a_ref[...].astype(BF16), b_ref[...].astype(BF16), dn,
                            preferred_element_type=F32)
        if nk == 1:
            o_ref[...] = r.astype(o_ref.dtype)
        else:
            k = pl.program_id(2)

            @pl.when(k == 0)
            def _():
                acc[0][...] = r

            @pl.when(k > 0)
            def _():
                acc[0][...] += r

            @pl.when(k == nk - 1)
            def _():
                o_ref[...] = acc[0][...].astype(o_ref.dtype)

    a_spec = (pl.BlockSpec((tk, tm), lambda i, j, k: (k, i)) if ta
              else pl.BlockSpec((tm, tk), lambda i, j, k: (i, k)))
    b_spec = (pl.BlockSpec((tn, tk), lambda i, j, k: (j, k)) if tb
              else pl.BlockSpec((tk, tn), lambda i, j, k: (k, j)))
    return _call(
        body, name=name, grid=(M // tm, N // tn, nk),
        in_specs=[a_spec, b_spec],
        out_specs=pl.BlockSpec((tm, tn), lambda i, j, k: (i, j)),
        out_shape=jax.ShapeDtypeStruct((M, N), out_dtype),
        scratch_shapes=[pltpu.VMEM((tm, tn), F32)] if nk > 1 else [],
        compiler_params=_params(("parallel", "parallel", "arbitrary")),
    )(a, b)


def _matmul(name, t_fwd=(512, 1024, 1024), t_da=(512, 1024, 1024), t_dw=(1024, 1024, 512), out_dtype=F32):
    def tiles(t):
        return dict(tm=t[0], tn=t[1], tk=t[2])

    @jax.custom_vjp
    def f(a, w):
        return _mm(a, w, name=name + "_fwd", out_dtype=out_dtype, **tiles(t_fwd))

    def fwd(a, w):
        return f(a, w), (a, w)

    def bwd(res, g):
        a, w = res
        da = _mm(g, w, name=name + "_da", tb=True, out_dtype=a.dtype, **tiles(t_da))
        dw = _mm(a, g, name=name + "_dw", ta=True, out_dtype=w.dtype, **tiles(t_dw))
        return da, dw

    f.defvjp(fwd, bwd)
    return f


def _mm_parts_nt(parts, w, col0, init, out_dtype, *, name, tm=1088, tk=512):
    M = parts[0].shape[0]
    D = w.shape[0]
    tm = _pick(M, tm, 16)
    tk = math.gcd(tk, *[a.shape[1] for a in parts])
    starts, n = [], 0
    for a in parts:
        assert a.shape[1] % tk == 0
        starts.append(n)
        n += a.shape[1] // tk
    k0 = col0 // tk
    has_init = init is not None

    def body(*refs):
        a_refs, w_ref = refs[:len(parts)], refs[len(parts)]
        o_ref, acc = refs[-2], refs[-1]
        k = pl.program_id(1)

        @pl.when(k == 0)
        def _():
            acc[...] = refs[len(parts) + 1][...].astype(F32) if has_init else jnp.zeros_like(acc)

        for a_ref, s, a in zip(a_refs, starts, parts):
            @pl.when(jnp.logical_and(k >= s, k < s + a.shape[1] // tk))
            def _(a_ref=a_ref):
                acc[...] += lax.dot_general(a_ref[...].astype(BF16), w_ref[...], NT, preferred_element_type=F32)

        @pl.when(k == n - 1)
        def _():
            o_ref[...] = acc[...].astype(o_ref.dtype)

    in_specs = [pl.BlockSpec((tm, tk), lambda i, k, s=s, c=a.shape[1] // tk: (i, jnp.clip(k - s, 0, c - 1)))
                for a, s in zip(parts, starts)]
    in_specs.append(pl.BlockSpec((D, tk), lambda i, k: (0, k0 + k)))
    o_spec = pl.BlockSpec((tm, D), lambda i, k: (i, 0))
    args = list(parts) + [w]
    if has_init:
        in_specs.append(o_spec)
        args.append(init)
    return _call(body, name=name, grid=(M // tm, n), in_specs=in_specs, out_specs=o_spec,
                 out_shape=jax.ShapeDtypeStruct((M, D), out_dtype),
                 scratch_shapes=[pltpu.VMEM((tm, D), F32)],
                 compiler_params=_params(("parallel", "arbitrary")))(*args)


def _in_proj(widths, dtypes, t_fwd=(2176, 1024, 1024), t_dw=(1024, 1024, 2176), group=5):
    offs = [0]
    for wd in widths:
        offs.append(offs[-1] + wd)

    def cols(w, i):
        return w[:, offs[i]:offs[i + 1]]

    @jax.custom_vjp
    def op(h, w):
        return tuple(_mm(h, cols(w, i), name="mm_in_fwd", out_dtype=dtypes[i], tm=t_fwd[0], tn=t_fwd[1],
                         tk=t_fwd[2]) for i in range(len(widths)))

    def fwd(h, w):
        return op(h, w), (h, w)

    def bwd(res, gs):
        h, w = res
        da = None
        for i0 in range(0, len(widths), group):
            last = i0 + group >= len(widths)
            da = _mm_parts_nt(list(gs[i0:i0 + group]), w, offs[i0], da, h.dtype if last else F32, name="mm_in_da")
        dw = jnp.concatenate([_mm(h, g, name="mm_in_dw", ta=True, out_dtype=w.dtype, tm=t_dw[0], tn=t_dw[1],
                                  tk=t_dw[2]) for g in gs], axis=1)
        return da, dw

    op.defvjp(fwd, bwd)
    return op


def _row_specs(xs, tps, gps, rts, tm, n_lat_tiles):
    x_specs = [pl.BlockSpec((None, tm, x.shape[2]), lambda b, i: (b, i, 0)) for x in xs]
    tp_specs = [pl.BlockSpec((None, None, 1, p.shape[3]),
                             lambda b, i: (b, (i < n_lat_tiles).astype(jnp.int32), 0, 0)) for p in tps]
    gp_specs = [pl.BlockSpec(p.shape, lambda b, i: (0, 0)) for p in gps]
    rt_specs = [pl.BlockSpec((tm, t.shape[1]), lambda b, i: (i, 0)) for t in rts]
    return x_specs, tp_specs, gp_specs, rt_specs


def _row_fwd(f, name, xs, tps, gps, rts, outs, tm, n_lat):
    B, T, _ = xs[0].shape
    n_in = len(xs) + len(tps) + len(gps) + len(rts)

    def body(*refs):
        vals = f(*[r[...] for r in refs[:n_in]])
        for o, v in zip(refs[n_in:], vals):
            o[...] = v.astype(o.dtype)

    x_specs, tp_specs, gp_specs, rt_specs = _row_specs(xs, tps, gps, rts, tm, n_lat // tm)
    res = _call(
        body, name=name, grid=(B, T // tm),
        in_specs=x_specs + tp_specs + gp_specs + rt_specs,
        out_specs=[pl.BlockSpec((None, tm, w), lambda b, i: (b, i, 0)) for w, _ in outs],
        out_shape=[jax.ShapeDtypeStruct((B, T, w), dt) for w, dt in outs],
        compiler_params=_params(("parallel", "parallel")),
    )(*xs, *tps, *gps, *rts)
    return list(res)


def _row_bwd(f, name, xs, tps, gps, rts, douts, tm, n_lat):
    B, T, _ = xs[0].shape
    nx, ntp, ngp, nd = len(xs), len(tps), len(gps), len(douts)
    n_lat_tiles = n_lat // tm
    n_diff = nx + ntp + ngp
    n_in = n_diff + len(rts)

    def body(*refs):
        diff = [r[...] for r in refs[:n_diff]]
        tabs = [r[...] for r in refs[n_diff:n_in]]
        dos = [r[...] for r in refs[n_in:n_in + nd]]
        o_refs = refs[n_in + nd:]
        prim, vjp = jax.vjp(lambda *d: tuple(f(*d, *tabs)), *diff)
        grads = vjp(tuple(d.astype(p.dtype) for d, p in zip(dos, prim)))
        b, i = pl.program_id(0), pl.program_id(1)
        for k in range(nx):
            o_refs[k][...] = grads[k].astype(o_refs[k].dtype)
        first_tp = jnp.logical_or(i == 0, i == n_lat_tiles)
        first_gp = jnp.logical_and(b == 0, i == 0)
        for k in range(nx, n_diff):
            first = first_tp if k < nx + ntp else first_gp

            @pl.when(first)
            def _(k=k):
                o_refs[k][...] = grads[k]

            @pl.when(jnp.logical_not(first))
            def _(k=k):
                o_refs[k][...] += grads[k]

    x_specs, tp_specs, gp_specs, rt_specs = _row_specs(xs, tps, gps, rts, tm, n_lat_tiles)
    d_specs = [pl.BlockSpec((None, tm, d.shape[2]), lambda b, i: (b, i, 0)) for d in douts]
    res = _call(
        body, name=name, grid=(B, T // tm),
        in_specs=x_specs + tp_specs + gp_specs + rt_specs + d_specs,
        out_specs=x_specs + tp_specs + gp_specs,
        out_shape=[jax.ShapeDtypeStruct(a.shape, a.dtype) for a in xs]
        + [jax.ShapeDtypeStruct(a.shape, F32) for a in (*tps, *gps)],
        compiler_params=_params(("arbitrary", "arbitrary")),
    )(*xs, *tps, *gps, *rts, *douts)
    res = list(res)
    return res[:nx], res[nx:nx + ntp], res[nx + ntp:]


def _rowwise(f, name, outs, tm, n_lat, n_x, n_tp, n_gp):
    def split(args):
        return (args[:n_x], args[n_x:n_x + n_tp], args[n_x + n_tp:n_x + n_tp + n_gp],
                args[n_x + n_tp + n_gp:])

    @jax.custom_vjp
    def op(*args):
        xs, tps, gps, rts = split(args)
        return tuple(_row_fwd(f, name + "_fwd", xs, tps, gps, rts, outs, tm, n_lat))

    def fwd(*args):
        return op(*args), args

    def bwd(args, g):
        xs, tps, gps, rts = split(args)
        dxs, dtps, dgps = _row_bwd(f, name + "_bwd", xs, tps, gps, rts, list(g), tm, n_lat)
        return (*dxs, *dtps, *dgps, *[jnp.zeros_like(t) for t in rts])

    op.defvjp(fwd, bwd)
    return op


def _shift_impl(u, k):
    n = u.shape[0]
    r = pltpu.roll(u, k % n, axis=0)
    row = lax.broadcasted_iota(jnp.int32, u.shape, 0)
    valid = (row >= k) if k > 0 else (row < n + k)
    return jnp.where(valid, r, 0.0)


@functools.partial(jax.custom_vjp, nondiff_argnums=(1,))
def _shift(u, k):
    return _shift_impl(u, k)


_shift.defvjp(lambda u, k: (_shift_impl(u, k), None), lambda k, _, g: (_shift_impl(g, -k),))


def _swap_impl(x):
    lane = lax.broadcasted_iota(jnp.int32, x.shape, 1)
    q = HEAD_DIM // 4
    return jnp.where((lane % (2 * q)) < q, pltpu.roll(x, HEAD_DIM - q, axis=1), pltpu.roll(x, q, axis=1))


@jax.custom_vjp
def _swap(x):
    return _swap_impl(x)


_swap.defvjp(lambda x: (_swap_impl(x), None), lambda _, g: (_swap_impl(g),))


def _conv_f(ul, uc, cw, cb):
    def conv(u):
        return (_shift(u, 2) * cw[0:1] + _shift(u, 1) * cw[1:2] + u * cw[2:3] + _shift(u, -1) * cw[3:4] + cb)
    return conv(ul), conv(uc)


def _conv_specs(B, T, D):
    u_spec = pl.BlockSpec((None, T, LANE), lambda j, b: (b, 0, j))
    cw_spec = pl.BlockSpec((4, LANE), lambda j, b: (0, j))
    cb_spec = pl.BlockSpec((1, LANE), lambda j, b: (0, j))
    return u_spec, cw_spec, cb_spec


def _conv_fwd(u, cw, cb, S):
    B, T, D = u.shape

    def body(u_ref, cw_ref, cb_ref, o_ref):
        vl, vc = _conv_f(u_ref[0:S, :], u_ref[S:T, :], cw_ref[...], cb_ref[...])
        o_ref[0:S, :] = vl
        o_ref[S:T, :] = vc

    u_spec, cw_spec, cb_spec = _conv_specs(B, T, D)
    return _call(body, name="conv_fwd", grid=(D // LANE, B), in_specs=[u_spec, cw_spec, cb_spec],
                 out_specs=u_spec, out_shape=jax.ShapeDtypeStruct(u.shape, F32),
                 compiler_params=_params(("parallel", "parallel")))(u, cw, cb)


def _conv_bwd(u, cw, cb, dv, S):
    B, T, D = u.shape

    def body(u_ref, cw_ref, cb_ref, dv_ref, du_ref, dcw_ref, dcb_ref):
        _, vjp = jax.vjp(_conv_f, u_ref[0:S, :], u_ref[S:T, :], cw_ref[...], cb_ref[...])
        dul, duc, dcw, dcb = vjp((dv_ref[0:S, :], dv_ref[S:T, :]))
        du_ref[0:S, :] = dul
        du_ref[S:T, :] = duc
        first = pl.program_id(1) == 0

        @pl.when(first)
        def _():
            dcw_ref[...] = dcw
            dcb_ref[...] = dcb

        @pl.when(jnp.logical_not(first))
        def _():
            dcw_ref[...] += dcw
            dcb_ref[...] += dcb

    u_spec, cw_spec, cb_spec = _conv_specs(B, T, D)
    return _call(body, name="conv_bwd", grid=(D // LANE, B), in_specs=[u_spec, cw_spec, cb_spec, u_spec],
                 out_specs=[u_spec, cw_spec, cb_spec],
                 out_shape=[jax.ShapeDtypeStruct(u.shape, F32), jax.ShapeDtypeStruct(cw.shape, F32),
                            jax.ShapeDtypeStruct(cb.shape, F32)],
                 compiler_params=_params(("parallel", "arbitrary")))(u, cw, cb, dv)


def _conv_op(S):
    @jax.custom_vjp
    def op(u, cw, cb):
        return _conv_fwd(u, cw, cb, S)

    def fwd(u, cw, cb):
        return op(u, cw, cb), (u, cw, cb)

    def bwd(res, g):
        u, cw, cb = res
        return tuple(_conv_bwd(u, cw, cb, g, S))

    op.defvjp(fwd, bwd)
    return op


SCAN_UNROLL = 4


def _group_scan(A, Bv, asc):
    row = lax.broadcasted_iota(jnp.int32, A.shape, 0)
    for s in (1, 2, 4):
        sh = s if asc else SUBLANE - s
        valid = (row >= s) if asc else (row < SUBLANE - s)
        A_sh = pltpu.roll(A, sh, axis=0)
        B_sh = pltpu.roll(Bv, sh, axis=0)
        Bv = jnp.where(valid, A * B_sh, 0.0) + Bv
        A = jnp.where(valid, A * A_sh, A)
    return A, Bv


def _chain_step(A, Bv, carry, asc):
    row = lax.broadcasted_iota(jnp.int32, A.shape, 0)
    A2, B2 = _group_scan(A, Bv, asc)
    h = A2 * carry + B2
    if asc:
        prev = jnp.where(row == 0, carry, pltpu.roll(h, 1, axis=0))
        return h, prev, h[SUBLANE - 1:SUBLANE, :]
    prev = jnp.where(row == SUBLANE - 1, carry, pltpu.roll(h, SUBLANE - 1, axis=0))
    return h, prev, h[0:1, :]


def _chain_loop(segments, step):
    carry = jnp.zeros((1, LANE), F32)
    for lo, hi, asc in segments:
        span = SUBLANE * SCAN_UNROLL
        assert (hi - lo) % span == 0

        def it(t, carry, lo=lo, hi=hi, asc=asc, span=span):
            base = lo + t * span if asc else hi - (t + 1) * span
            order = range(SCAN_UNROLL) if asc else reversed(range(SCAN_UNROLL))
            for j in order:
                carry = step(pl.multiple_of(base + SUBLANE * j, SUBLANE), carry, asc)
            return carry

        carry = lax.fori_loop(0, (hi - lo) // span, it, carry)
    return carry


def _scan_specs(T):
    return pl.BlockSpec((None, T, LANE), lambda j, b: (b, 0, j))


def _scan_fwd(a, b, S, reverse):
    B, T, D = a.shape
    asc = not reverse
    segments = [(S, T, asc), (0, S, asc)]

    def body(a_ref, b_ref, h_ref, hp_ref):
        def step(r0, carry, asc):
            rows = pl.ds(r0, SUBLANE)
            h, prev, carry = _chain_step(a_ref[rows, :], b_ref[rows, :], carry, asc)
            h_ref[rows, :] = h
            hp_ref[rows, :] = prev
            return carry
        _chain_loop(segments, step)

    spec = _scan_specs(T)
    return _call(body, name="scan_rev_fwd" if reverse else "scan_fwd_fwd", grid=(D // LANE, B),
                 in_specs=[spec, spec], out_specs=[spec, spec],
                 out_shape=[jax.ShapeDtypeStruct(a.shape, F32)] * 2,
                 compiler_params=_params(("parallel", "parallel")))(a, b)


def _scan_bwd(a, hp, dy, S, reverse):
    B, T, D = a.shape
    asc = reverse
    segments = [(0, S, asc), (S, T, asc)]

    def body(a_ref, hp_ref, dy_ref, da_ref, db_ref):
        def step(r0, carry, asc):
            rows = pl.ds(r0, SUBLANE)
            A, dy = a_ref[rows, :], dy_ref[rows, :]
            _, s_prev, carry = _chain_step(A, A * dy, carry, asc)
            g = dy + s_prev
            db_ref[rows, :] = g
            da_ref[rows, :] = g * hp_ref[rows, :]
            return carry
        _chain_loop(segments, step)

    spec = _scan_specs(T)
    return _call(body, name="scan_rev_bwd" if reverse else "scan_fwd_bwd", grid=(D // LANE, B),
                 in_specs=[spec, spec, spec], out_specs=[spec, spec],
                 out_shape=[jax.ShapeDtypeStruct(a.shape, F32)] * 2,
                 compiler_params=_params(("parallel", "parallel")))(a, hp, dy)


def _scan_op(S, reverse):
    @jax.custom_vjp
    def op(a, b):
        return _scan_fwd(a, b, S, reverse)[0]

    def fwd(a, b):
        h, hp = _scan_fwd(a, b, S, reverse)
        return h, (a, hp)

    def bwd(res, g):
        a, hp = res
        return tuple(_scan_bwd(a, hp, g, S, reverse))

    op.defvjp(fwd, bwd)
    return op


def _band_lo(qi, tq, S):
    span = tq + 2 * WINDOW
    return pl.multiple_of(jnp.clip(qi * tq - WINDOW, 0, S - span), LANE)


def _band_mask(qi, tq, lo, span, transposed, heads=1):
    shape = (span, heads * tq) if transposed else (heads * tq, span)
    assert tq & (tq - 1) == 0
    qpos = qi * tq + jnp.bitwise_and(lax.broadcasted_iota(jnp.int32, shape, 1 if transposed else 0), tq - 1)
    kpos = lo + lax.broadcasted_iota(jnp.int32, shape, 0 if transposed else 1)
    return jnp.abs(kpos - qpos) <= WINDOW


NT = (((1,), (1,)), ((), ()))
LOG2E = 1.4426950408889634
KEY_CHUNK = 1024


def _col_to_row(c):
    return jnp.broadcast_to(c, (c.shape[0], LANE)).T[0:1, :]


def _attn_fwd(q, k, v, sink, gate, S, band, side=None):
    B, T, HD = q.shape
    H = HD // HEAD_DIM
    L = T - S
    tq = L
    n_lq = S // tq
    span = tq + 2 * WINDOW
    c2 = HEAD_DIM ** -0.5 * LOG2E
    has_sink = sink is not None
    kc = min(KEY_CHUNK, S)
    hp = KV_GROUP if band else 1

    def body(*refs):
        if has_sink:
            q_ref, k_ref, v_ref, g_ref, s_ref, z_ref, o_ref, lse_ref = refs
        else:
            q_ref, k_ref, v_ref, g_ref, z_ref, o_ref, lse_ref = refs
        qi = pl.program_id(2)

        def online(qv, sk2, segs):
            def scores(seg):
                s = lax.dot_general(qv, k_ref[seg[0], :], NT, preferred_element_type=F32) * c2
                return s if seg[1] is None else jnp.where(seg[1], s, NEG_INF)

            m = l = acc = None
            s_next = scores(segs[0])
            for j, (rows, _) in enumerate(segs):
                s = s_next
                if j + 1 < len(segs):
                    s_next = scores(segs[j + 1])
                ms = jnp.max(s, axis=-1, keepdims=True)
                if m is None:
                    m = ms if sk2 is None else jnp.maximum(ms, sk2)
                    p = jnp.exp2(s - m)
                    l = jnp.sum(p, axis=-1, keepdims=True)
                    if sk2 is not None:
                        l = l + jnp.exp2(sk2 - m)
                    acc = jnp.dot(p.astype(BF16), v_ref[rows, :], preferred_element_type=F32)
                else:
                    m_new = jnp.maximum(m, ms)
                    alpha = jnp.exp2(m - m_new)
                    p = jnp.exp2(s - m_new)
                    l = alpha * l + jnp.sum(p, axis=-1, keepdims=True)
                    acc = alpha * acc + jnp.dot(p.astype(BF16), v_ref[rows, :], preferred_element_type=F32)
                    m = m_new
            return acc, m, l

        def joint(qv, sk2, segs):
            ss = []
            for rows, mask in segs:
                s = lax.dot_general(qv, k_ref[rows, :], NT, preferred_element_type=F32) * c2
                ss.append(s if mask is None else jnp.where(mask, s, NEG_INF))
            m = functools.reduce(jnp.maximum, [jnp.max(s, axis=-1, keepdims=True) for s in ss])
            if sk2 is not None:
                m = jnp.maximum(m, sk2)
            ps = [jnp.exp2(s - m) for s in ss]
            l = functools.reduce(jnp.add, [jnp.sum(p, axis=-1, keepdims=True) for p in ps])
            if sk2 is not None:
                l = l + jnp.exp2(sk2 - m)
            acc = functools.reduce(jnp.add, [jnp.dot(p.astype(BF16), v_ref[rows, :], preferred_element_type=F32)
                                             for p, (rows, _) in zip(ps, segs)])
            return acc, m, l

        def run(segs, softmax):
            for g in range(hp):
                lanes = slice(g * HEAD_DIM, (g + 1) * HEAD_DIM)
                sk2 = s_ref[g][:, 0:1] * LOG2E if has_sink else None
                acc, m, l = softmax(q_ref[:, lanes], sk2, segs)
                o = acc * (1.0 / l)
                gv = g_ref[:, lanes].astype(F32)
                o_ref[:, lanes] = o
                z_ref[:, lanes] = (o * (gv * jax.nn.sigmoid(gv))).astype(z_ref.dtype)
                lse_ref[g] = _col_to_row(m + jnp.log2(l))

        ctx_rows = pl.ds(S, L)

        @pl.when(qi < n_lq)
        def _():
            if band:
                lo = _band_lo(qi, tq, S)
                run([(ctx_rows, None), (pl.ds(lo, span), _band_mask(qi, tq, lo, span, False))], joint)
            else:
                run([(pl.ds(j * kc, kc), None) for j in range(S // kc)] + [(ctx_rows, None)], online)

        @pl.when(qi >= n_lq)
        def _():
            run([(ctx_rows, None)], joint)

    q_spec = pl.BlockSpec((None, tq, hp * HEAD_DIM), lambda b, h, i: (b, i, h))
    kv_spec = pl.BlockSpec((None, T, HEAD_DIM), lambda b, h, i: (b, 0, h * hp // KV_GROUP))
    in_specs = [q_spec, kv_spec, kv_spec, q_spec]
    args = [q, k, v, gate]
    if has_sink:
        in_specs.append(pl.BlockSpec((hp, 1, LANE), lambda b, h, i: (h, 0, 0)))
        args.append(sink)
    name = "attn_band_fwd" if band else "attn_dense_fwd"
    return _call_with_side(
        body, side, name=name, grid=(B, H // hp, T // tq), in_specs=in_specs, args=args,
        out_specs=[q_spec, q_spec, pl.BlockSpec((None, hp, 1, tq), lambda b, h, i: (b, h, 0, i))],
        out_shape=[jax.ShapeDtypeStruct(q.shape, BF16), jax.ShapeDtypeStruct(q.shape, F32),
                   jax.ShapeDtypeStruct((B, H, 1, T), F32)],
        scratch_shapes=[], dims=("parallel", "parallel", "parallel"))


def _attn_bwd(q, k, v, sink, gate, dz, o, lse, S, band, side=None):
    B, T, HD = q.shape
    H = HD // HEAD_DIM
    KVH = H // KV_GROUP
    L = T - S
    tq = L
    n_lq = S // tq
    span = tq + 2 * WINDOW
    scale = HEAD_DIM ** -0.5
    c2 = scale * LOG2E
    has_sink = sink is not None
    kc = min(KEY_CHUNK, S)
    hp = KV_GROUP if band else 1

    def body(*refs):
        if has_sink:
            (q_ref, k_ref, v_ref, g_ref, dz_ref, o_ref, lse_ref, s_ref,
             dq_ref, dk_ref, dv_ref, dg_ref, ds_ref, *scr) = refs
        else:
            q_ref, k_ref, v_ref, g_ref, dz_ref, o_ref, lse_ref, dq_ref, dk_ref, dv_ref, dg_ref, *scr = refs
        g, qi = pl.program_id(2), pl.program_id(3)

        @pl.when(jnp.logical_and(g == 0, qi == 0))
        def _():
            dk_ref[...] = jnp.zeros_like(dk_ref)
            dv_ref[...] = jnp.zeros_like(dv_ref)
            if not band:
                scr[0][...] = k_ref[...].astype(F32).T.astype(BF16)

        def run(segs):
            lanes = [slice(h * HEAD_DIM, (h + 1) * HEAD_DIM) for h in range(hp)]

            def stack(ref, axis=0):
                return jnp.concatenate([ref[:, ln] for ln in lanes], axis=axis) if hp > 1 else ref[...]

            qv, gv, ov, dzv = stack(q_ref), stack(g_ref).astype(F32), stack(o_ref), stack(dz_ref).astype(F32)
            sg = jax.nn.sigmoid(gv)
            dof = dzv * (gv * sg)
            dgate = dzv * ov * (sg * (1.0 + gv * (1.0 - sg)))
            dov = dof.astype(BF16)
            lse2 = jnp.concatenate([lse_ref[h] for h in range(hp)], axis=1) if hp > 1 else lse_ref[0]
            delta = _col_to_row(jnp.sum(dof * ov, axis=-1, keepdims=True))

            def head(seg):
                rows = pl.ds(seg[0], seg[1])
                s = lax.dot_general(k_ref[rows, :], qv, NT, preferred_element_type=F32) * c2
                if seg[2] is not None:
                    s = jnp.where(seg[2], s, NEG_INF)
                return s, lax.dot_general(v_ref[rows, :], dov, NT, preferred_element_type=F32)

            dq, dqT = None, None
            nxt = head(segs[0])
            for j, (lo, n, _) in enumerate(segs):
                rows = pl.ds(lo, n)
                s, dp = nxt
                if j + 1 < len(segs):
                    nxt = head(segs[j + 1])
                p = jnp.exp2(s - lse2)
                ds = p * (dp - delta)
                dsb = ds.astype(BF16)
                dv_ref[rows, :] += jnp.dot(p.astype(BF16), dov, preferred_element_type=F32)
                dk_ref[rows, :] += jnp.dot(dsb, qv, preferred_element_type=F32) * scale
                if band:
                    part = jnp.dot(ds.T.astype(BF16), k_ref[rows, :], preferred_element_type=F32)
                    dq = part if dq is None else dq + part
                else:
                    part = jnp.dot(scr[0][:, lo:lo + n], dsb, preferred_element_type=F32)
                    dqT = part if dqT is None else dqT + part
            dq = (dq if band else dqT.T) * scale
            for h, ln in enumerate(lanes):
                rows = slice(h * tq, (h + 1) * tq)
                dq_ref[:, ln] = dq[rows].astype(dq_ref.dtype)
                dg_ref[:, ln] = dgate[rows].astype(dg_ref.dtype)
                if has_sink:
                    psk = jnp.exp2(s_ref[h][:, 0:1] * LOG2E - lse2[:, rows])
                    dsk = jnp.broadcast_to(-jnp.sum(psk * delta[:, rows], axis=1, keepdims=True), (1, LANE))

                    @pl.when(qi == 0)
                    def _(h=h, dsk=dsk):
                        ds_ref[h] = dsk

                    @pl.when(qi > 0)
                    def _(h=h, dsk=dsk):
                        ds_ref[h] += dsk

        @pl.when(qi < n_lq)
        def _():
            if band:
                lo = _band_lo(qi, tq, S)
                run([(S, L, None), (lo, span, _band_mask(qi, tq, lo, span, True, hp))])
            else:
                run([(j * kc, kc, None) for j in range(S // kc)] + [(S, L, None)])

        @pl.when(qi >= n_lq)
        def _():
            run([(S, L, None)])

    ng = KV_GROUP // hp
    q_spec = pl.BlockSpec((None, tq, hp * HEAD_DIM), lambda b, kv, g, i: (b, i, kv * ng + g))
    kv_spec = pl.BlockSpec((None, T, HEAD_DIM), lambda b, kv, g, i: (b, 0, kv))
    lse_spec = pl.BlockSpec((None, hp, 1, tq), lambda b, kv, g, i: (b, kv * ng + g, 0, i))
    in_specs = [q_spec, kv_spec, kv_spec, q_spec, q_spec, q_spec, lse_spec]
    out_specs = [q_spec, kv_spec, kv_spec, q_spec]
    out_shape = [jax.ShapeDtypeStruct(q.shape, BF16), jax.ShapeDtypeStruct(k.shape, F32),
                 jax.ShapeDtypeStruct(v.shape, F32), jax.ShapeDtypeStruct(q.shape, gate.dtype)]
    args = [q, k, v, gate, dz, o, lse]
    if has_sink:
        in_specs.append(pl.BlockSpec((hp, 1, LANE), lambda b, kv, g, i: (kv * ng + g, 0, 0)))
        args.append(sink)
        out_specs.append(pl.BlockSpec((None, hp, 1, LANE), lambda b, kv, g, i: (b, kv * ng + g, 0, 0)))
        out_shape.append(jax.ShapeDtypeStruct((B, H, 1, LANE), F32))
    res = _call_with_side(
        body, side, name="attn_band_bwd" if band else "attn_dense_bwd", grid=(B, KVH, ng, T // tq),
        in_specs=in_specs, args=args, out_specs=out_specs, out_shape=out_shape,
        scratch_shapes=[] if band else [pltpu.VMEM((HEAD_DIM, T), BF16)],
        dims=("parallel", "parallel", "arbitrary", "arbitrary"))
    return (res[0], res[1], res[2], res[3], res[4] if has_sink else None, res[-1] if side is not None else None)


def _prep_f(norm):
    def f(q, kv, *rest):
        q, kv = q.astype(F32), kv.astype(F32)
        kw = kv.shape[1] // 2
        k, v = kv[:, :kw], kv[:, kw:]
        if norm:
            qg, kg, cos, sin = rest
        else:
            (cos, sin), qg, kg = rest, None, None

        def heads(x, g):
            outs = []
            for h in range(x.shape[1] // HEAD_DIM):
                xh = x[:, h * HEAD_DIM:(h + 1) * HEAD_DIM]
                if g is not None:
                    xh = xh * lax.rsqrt(jnp.mean(xh * xh, axis=-1, keepdims=True) + EPS) * g
                outs.append(xh * cos + _swap(xh) * sin)
            return jnp.concatenate(outs, axis=1) if len(outs) > 1 else outs[0]

        return heads(q, qg), heads(k, kg), v

    return f


def _attn_branch(S, tm, band, norm, has_sink, carries=False):
    f = _prep_f(norm)
    name = "band" if band else "dense"

    def prep(q, kv, gains, tabs):
        outs = [(q.shape[2], BF16), (kv.shape[2] // 2, BF16), (kv.shape[2] // 2, BF16)]
        return _row_fwd(f, "prep_" + name + "_fwd", [q, kv], [], list(gains), list(tabs), outs, tm, S)

    def unpack(args):
        q, kv, gate = args[:3]
        rest = list(args[3:])
        gains = [rest.pop(0), rest.pop(0)] if norm else []
        sink = rest.pop(0) if has_sink else None
        pack = rest.pop(0) if carries else None
        return q, kv, gate, gains, sink, pack, rest

    def sink_lanes(sink):
        return None if sink is None else jnp.broadcast_to(sink[:, None, None], (sink.shape[0], 1, LANE))

    def run_fwd(args):
        q, kv, gate, gains, sink, pack, tabs = unpack(args)
        qp, kp, vp = prep(q, kv, gains, tabs)
        side = (pack.astype(BF16), True) if carries else None
        res = _attn_fwd(qp, kp, vp, sink_lanes(sink), gate, S, band, side)
        return ((res[0], res[3]) if carries else res[0]), (args, qp, kp, vp, res[1], res[2])

    @jax.custom_vjp
    def op(*args):
        return run_fwd(args)[0]

    def fwd(*args):
        return run_fwd(args)

    def bwd(res, ct):
        args, qp, kp, vp, o, lse = res
        q, kv, gate, gains, sink, pack, tabs = unpack(args)
        dz, side = (ct[0], (ct[1], False)) if carries else (ct, None)
        dqp, dkp, dvp, dgate, dsk, recv = _attn_bwd(qp, kp, vp, sink_lanes(sink), gate, dz, o, lse, S, band, side)
        dxs, _, dgains = _row_bwd(f, "prep_" + name + "_bwd", [q, kv], [], list(gains), list(tabs),
                                  [dqp, dkp, dvp], tm, S)
        out = list(dxs) + [dgate] + list(dgains)
        if has_sink:
            out.append(jnp.sum(dsk[:, :, 0, 0], axis=0))
        if carries:
            out.append(_sum_slots(recv, "sum_grads"))
        return (*out, *[jnp.zeros_like(t) for t in tabs])

    op.defvjp(fwd, bwd)
    return op


def _final_loss(S, tm):
    def run(X, target, g):
        B, T, D = X.shape
        n_lat_tiles = S // tm

        def lossf(x, gg, tgt):
            y = x * lax.rsqrt(jnp.mean(x * x, axis=-1, keepdims=True) + EPS) * gg
            err = y - tgt
            return 0.5 * jnp.sum(jnp.sum(err * err, axis=-1, keepdims=True), axis=0, keepdims=True) / D

        def body(x_ref, t_ref, g_ref, loss_ref, dx_ref, dg_ref):
            b, i = pl.program_id(0), pl.program_id(1)
            tgt = t_ref[...]
            val, vjp = jax.vjp(lambda x, gg: lossf(x, gg, tgt), x_ref[...], g_ref[...])
            dx, dg = vjp(jnp.ones((1, 1), F32))
            lat = (i < n_lat_tiles).astype(F32)
            dx_ref[...] = dx * lat

            @pl.when(i == 0)
            def _():
                loss_ref[...] = jnp.zeros_like(loss_ref)

            @pl.when(jnp.logical_and(b == 0, i == 0))
            def _():
                dg_ref[...] = jnp.zeros_like(dg_ref)

            loss_ref[...] += jnp.broadcast_to(val * lat, loss_ref.shape)
            dg_ref[...] += dg * lat

        x_spec = pl.BlockSpec((None, tm, D), lambda b, i: (b, i, 0))
        t_spec = pl.BlockSpec((None, tm, D), lambda b, i: (b, jnp.minimum(i, n_lat_tiles - 1), 0))
        g_spec = pl.BlockSpec((1, D), lambda b, i: (0, 0))
        loss, dx, dg = _call(
            body, name="final_loss", grid=(B, T // tm), in_specs=[x_spec, t_spec, g_spec],
            out_specs=[pl.BlockSpec((None, 1, LANE), lambda b, i: (b, 0, 0)), x_spec, g_spec],
            out_shape=[jax.ShapeDtypeStruct((B, 1, LANE), F32), jax.ShapeDtypeStruct(X.shape, F32),
                       jax.ShapeDtypeStruct(g.shape, F32)],
            compiler_params=_params(("arbitrary", "arbitrary")))(X, target, g)
        return jnp.sum(loss[:, 0, 0]), dx, dg

    @jax.custom_vjp
    def op(X, target, g):
        return run(X, target, g)[0]

    def fwd(X, target, g):
        loss, dx, dg = run(X, target, g)
        return loss, (dx, dg, target)

    def bwd(res, ct):
        dx, dg, target = res
        return ct * dx, jnp.zeros_like(target), ct * dg

    op.defvjp(fwd, bwd)
    return op


def _prenorm_f(D):
    def f(x, mp, g):
        y = x * lax.rsqrt(jnp.mean(x * x, axis=-1, keepdims=True) + EPS) * g
        return (y * (1.0 + mp[:, D:2 * D]) + mp[:, 0:D],)
    return f


def _resid_f(D):
    def f(x, y, mp):
        return (x + mp[:, 2 * D:3 * D] * y,)
    return f


def _decay_impl(la):
    t = jnp.tanh(la)
    x = -2.0 * t / (1.0 - t)
    r = lax.rsqrt(x)
    return jnp.exp(la), x * r, r


@jax.custom_vjp
def _decay(la):
    a, bc, _ = _decay_impl(la)
    return a, bc


def _decay_fwd(la):
    a, bc, r = _decay_impl(la)
    return (a, bc), (a, r)


def _decay_bwd(res, ct):
    a, r = res
    return (ct[0] * a - ct[1] * (a * a * r),)


_decay.defvjp(_decay_fwd, _decay_bwd)


def _gate2_f(hf, hr, g):
    g = g.astype(F32)
    return ((hf + hr) * (g * jax.nn.sigmoid(g)),)


def _merge_f(D):
    def f(m3, pa, pb, pc):
        m3 = m3.astype(F32)
        return (jax.nn.sigmoid(m3[:, 0:D]) * pa.astype(F32) + jax.nn.sigmoid(m3[:, D:2 * D]) * pb.astype(F32)
                + jax.nn.sigmoid(m3[:, 2 * D:3 * D]) * pc.astype(F32),)
    return f


def _coef_f(D):
    nblk = D // GATE_BLOCK

    def f(v, pv, wm):
        vb = v.astype(BF16)

        def gate(k):
            cols = []
            for j in range(nblk):
                r0 = (k * nblk + j) * GATE_BLOCK
                cols.append(jnp.dot(vb[:, j * GATE_BLOCK:(j + 1) * GATE_BLOCK],
                                    wm[r0:r0 + GATE_BLOCK, :].astype(BF16), preferred_element_type=F32))
            return jnp.concatenate(cols, axis=1)

        outs = []
        for d in range(2):
            r = jax.nn.sigmoid(gate(d) + pv[d:d + 1])
            i = jax.nn.sigmoid(gate(2 + d) + pv[2 + d:3 + d])
            a, bc = _decay(r * pv[4 + d:5 + d])
            outs += [a, bc * (i * v)]
        return tuple(outs)

    return f


def _gate_blocks(w):
    per = GATE_BLOCK // LRU_BLOCK_W
    n = w.shape[1]
    w5 = w.reshape(2, n // per, per, LRU_BLOCK_W, LRU_BLOCK_W)
    dense = jnp.einsum("djiab,ik->djiakb", w5, jnp.eye(per, dtype=w.dtype))
    return dense.reshape(2, (n // per) * GATE_BLOCK, GATE_BLOCK)


def _exchange_shape(x, gather):
    return jax.ShapeDtypeStruct((N_DEV,) + x.shape if gather else x.shape, x.dtype)


EXCHANGE_SEMS = [pltpu.SemaphoreType.DMA((N_DEV - 1,)), pltpu.SemaphoreType.DMA((N_DEV - 1,)),
                 pltpu.SemaphoreType.DMA(())]


def _exchange_ops(x_ref, o_ref, send_sems, recv_sems, local_sem, gather):
    mx, my, mc = lax.axis_index("x"), lax.axis_index("y"), lax.axis_index("c")
    me = 4 * mx + 2 * my + mc

    def src(p):
        return x_ref if gather else x_ref.at[p]

    local = pltpu.make_async_copy(src(me), o_ref.at[me], local_sem)
    sends, recvs = [], []
    for k in range(1, N_DEV):
        px = 1 - mx if k & 4 else mx
        py = 1 - my if k & 2 else my
        pc = 1 - mc if k & 1 else mc
        peer = 4 * px + 2 * py + pc
        sends.append(pltpu.make_async_remote_copy(
            src_ref=src(peer), dst_ref=o_ref.at[me], send_sem=send_sems.at[k - 1],
            recv_sem=recv_sems.at[k - 1], device_id=(px, py, pc), device_id_type=pl.DeviceIdType.MESH))
        recvs.append(pltpu.make_async_remote_copy(
            src_ref=src(peer), dst_ref=o_ref.at[peer], send_sem=send_sems.at[k - 1],
            recv_sem=recv_sems.at[k - 1], device_id=(px, py, pc), device_id_type=pl.DeviceIdType.MESH))

    def start():
        local.start()
        for cp in sends:
            cp.start()

    def finish():
        for cp in recvs:
            cp.wait_recv()
        for cp in sends:
            cp.wait_send()
        local.wait()

    return start, finish


def _exchange(x, name, gather):
    def body(x_ref, o_ref, send_sems, recv_sems, local_sem):
        start, finish = _exchange_ops(x_ref, o_ref, send_sems, recv_sems, local_sem, gather)
        start()
        finish()

    hbm = pl.BlockSpec(memory_space=pltpu.HBM)
    return _call(body, name=name, in_specs=[hbm], out_specs=hbm, out_shape=_exchange_shape(x, gather),
                 scratch_shapes=EXCHANGE_SEMS)(x)


def _call_with_side(body, side, *, name, grid, in_specs, args, out_specs, out_shape, scratch_shapes, dims):
    if side is None:
        return _call(body, name=name, grid=grid, in_specs=in_specs, out_specs=out_specs, out_shape=out_shape,
                     scratch_shapes=scratch_shapes, compiler_params=_params(dims))(*args)
    x, gather = side
    n_in, n_out, n_scr = len(in_specs), len(out_specs), len(scratch_shapes)

    def wrapped(*refs):
        ins, x_ref = refs[:n_in], refs[n_in]
        outs, o_ref = refs[n_in + 1:n_in + 1 + n_out], refs[n_in + 1 + n_out]
        scr = refs[n_in + 2 + n_out:n_in + 2 + n_out + n_scr]
        start, finish = _exchange_ops(x_ref, o_ref, *refs[n_in + 2 + n_out + n_scr:], gather)
        ids = [pl.program_id(a) for a in range(len(grid))]
        first = functools.reduce(jnp.logical_and, [i == 0 for i in ids])
        last = functools.reduce(jnp.logical_and, [i == g - 1 for i, g in zip(ids, grid)])
        pl.when(first)(start)
        body(*ins, *outs, *scr)
        pl.when(last)(finish)

    hbm = pl.BlockSpec(memory_space=pltpu.HBM)
    return _call(wrapped, name=name + "_xchg", grid=grid, in_specs=list(in_specs) + [hbm],
                 out_specs=list(out_specs) + [hbm], out_shape=list(out_shape) + [_exchange_shape(x, gather)],
                 scratch_shapes=list(scratch_shapes) + EXCHANGE_SEMS,
                 compiler_params=_params(("arbitrary",) * len(grid)))(*args, x)


def _adamw(gs, w, m, v, name):
    n, R, C = gs.shape
    tr = _pick(R, 256, 16)

    def body(g_ref, w_ref, m_ref, v_ref, go_ref, d_ref, mo_ref, vo_ref):
        g = g_ref[0].astype(F32)
        for p in range(1, n):
            g = g + g_ref[p].astype(F32)
        m2 = ADAM_B1 * m_ref[...] + (1.0 - ADAM_B1) * g
        v2 = ADAM_B2 * v_ref[...] + (1.0 - ADAM_B2) * (g * g)
        m_hat = m2 / (1.0 - ADAM_B1 ** ADAM_STEP)
        v_hat = v2 / (1.0 - ADAM_B2 ** ADAM_STEP)
        go_ref[...] = g
        d_ref[...] = -ADAM_LR * (m_hat / (jnp.sqrt(v_hat) + ADAM_EPS) + ADAM_WD * w_ref[...])
        mo_ref[...] = m2
        vo_ref[...] = v2

    spec = pl.BlockSpec((tr, C), lambda i: (i, 0))
    return _call(body, name=name, grid=(R // tr,),
                 in_specs=[pl.BlockSpec((n, tr, C), lambda i: (0, i, 0)), spec, spec, spec],
                 out_specs=[spec] * 4, out_shape=[jax.ShapeDtypeStruct((R, C), F32)] * 4,
                 compiler_params=_params(("parallel",)))(gs, w, m, v)


def _sum_slots(gs, name):
    n, R, C = gs.shape
    tr = _pick(R, 256, 16)

    def body(g_ref, o_ref):
        g = g_ref[0].astype(F32)
        for p in range(1, n):
            g = g + g_ref[p].astype(F32)
        o_ref[...] = g

    return _call(body, name=name, grid=(R // tr,), in_specs=[pl.BlockSpec((n, tr, C), lambda i: (0, i, 0))],
                 out_specs=pl.BlockSpec((tr, C), lambda i: (i, 0)), out_shape=jax.ShapeDtypeStruct((R, C), F32),
                 compiler_params=_params(("parallel",)))(gs)


def _gather_op(name):
    @jax.custom_vjp
    def op(pack):
        return _exchange(pack.astype(BF16), name, True)

    def fwd(pack):
        return op(pack), None

    def bwd(_, ct):
        return (_sum_slots(_exchange(ct, name + "_transpose", False), "sum_grads"),)

    op.defvjp(fwd, bwd)
    return op


def _rope_tables(S, L):
    P = HEAD_DIM // 4
    rows = S // GRID_W
    row_id = jnp.repeat(jnp.arange(rows), GRID_W)
    col_id = jnp.tile(jnp.arange(GRID_W), rows)
    inv = ROPE_THETA ** (-jnp.arange(P, dtype=F32) / P)
    ar, ac = row_id[:, None] * inv, col_id[:, None] * inv
    cos = jnp.concatenate([jnp.cos(ar), jnp.cos(ar), jnp.cos(ac), jnp.cos(ac)], axis=1)
    sin = jnp.concatenate([-jnp.sin(ar), jnp.sin(ar), -jnp.sin(ac), jnp.sin(ac)], axis=1)
    cos = jnp.concatenate([cos, jnp.ones((L, HEAD_DIM), F32)], axis=0)
    sin = jnp.concatenate([sin, jnp.zeros((L, HEAD_DIM), F32)], axis=0)
    return cos, sin


SQ_TILES = (((2176, 1024, 1024), (2176, 1024, 1024), (1024, 1024, 1088)),) * DEPTH


def _layer_shard_shapes(D, IN):
    return (("w_in", (D, IN // N_DEV), 1), ("w_mod", (D, 3 * D // N_DEV), 1),
            ("w_branch", (3, D // N_DEV, D), 1), ("w_out", (D // N_DEV, D), 0))


def _pack_layer(w, l, D):
    return jnp.concatenate([w[n][l].reshape(-1) for n in BIG_PACK]).reshape(-1, D)


def _unpack_shards(pack, D, IN):
    flat, out, off = pack.reshape(-1), {}, 0
    for n, shape, _ in _layer_shard_shapes(D, IN):
        sz = shape[0] * shape[1] * (shape[2] if len(shape) > 2 else 1)
        out[n] = flat[off:off + sz].reshape(shape)
        off += sz
    return out


def _unpack_gathered(g, D, IN):
    flat, out, off = g.reshape(N_DEV, -1), {}, 0
    for n, shape, axis in _layer_shard_shapes(D, IN):
        sz = shape[0] * shape[1] * (shape[2] if len(shape) > 2 else 1)
        out[n] = _unshard(flat[:, off:off + sz].reshape((N_DEV,) + shape), axis)
        off += sz
    return out


def _loss_fn(packs, p, x, c, ctx, target):
    B, S, D = x.shape
    L = ctx.shape[1]
    T = S + L
    tm = min(L, 256)
    KVW = D // KV_GROUP
    widths = (D, D, D, 2 * KVW, D, D, 2 * KVW, D, 3 * D)
    part_dtypes = (F32,) + (BF16,) * 8
    IN = sum(widths)
    cos, sin = _rope_tables(S, L)

    X = jnp.concatenate([x, ctx], axis=1)
    sc, scc = jax.nn.silu(c), jax.nn.silu(p["c_ctx"])
    A = jnp.concatenate([scc[None], sc, jnp.zeros((SUBLANE - 1 - B, D), F32)], axis=0)
    gathered = _gather_op("gather_layer0")(packs[0])
    for l in range(DEPTH):
        big = _unpack_gathered(gathered, D, IN)
        mod = _matmul("mm_mod")(A, big["w_mod"]) + p["b_mod"][l]
        modp = jnp.stack([jnp.broadcast_to(mod[0], (B, 3 * D)), mod[1:1 + B]], axis=1)[:, :, None, :]
        (h,) = _rowwise(_prenorm_f(D), "prenorm", [(D, BF16)], tm, S, 1, 1, 1)(X, modp, p["norm_g"][l][None])
        parts = _in_proj(widths, part_dtypes)(h.reshape(B * T, D), big["w_in"])
        uA, gA, qB, kvB, gB, qC, kvC, gC, m3 = [t.reshape(B, T, -1) for t in parts]
        u = _conv_op(S)(uA, p["conv_w"][l], p["conv_b"][l][None])
        nsp = -LRU_C * jax.nn.softplus(-p["lru_lambda"][l])
        pv = jnp.concatenate([p["lru_ba"][l], p["lru_bx"][l], nsp, jnp.zeros((2, D), F32)], axis=0)
        wm = jnp.concatenate([_gate_blocks(p["lru_wa"][l]), _gate_blocks(p["lru_wx"][l])], axis=0)
        wm = wm.reshape(-1, GATE_BLOCK)
        af, bf, ar, br = _rowwise(_coef_f(D), "lru_coef", [(D, F32)] * 4, tm, S, 1, 0, 2)(u, pv, wm)
        hf = _scan_op(S, False)(af, bf)
        hr = _scan_op(S, True)(ar, br)
        (zA,) = _rowwise(_gate2_f, "gate_a", [(D, BF16)], tm, S, 3, 0, 0)(hf, hr, gA)
        zB = _attn_branch(S, tm, True, False, True)(qB, kvB, gB, p["attn_sink"][l], cos, sin)
        qkv_c = (qC, kvC, gC, p["q_norm_g"][l][None], p["k_norm_g"][l][None])
        if l + 1 < DEPTH:
            zC, gathered = _attn_branch(S, tm, False, True, False, True)(*qkv_c, packs[l + 1], cos, sin)
        else:
            zC = _attn_branch(S, tm, False, True, False)(*qkv_c, cos, sin)
        pr = [_matmul("mm_branch", *SQ_TILES[l], out_dtype=BF16)(z.reshape(B * T, D), big["w_branch"][n])
              .reshape(B, T, D) for n, z in enumerate((zA, zB, zC))]
        (mg,) = _rowwise(_merge_f(D), "merge", [(D, BF16)], tm, S, 4, 0, 0)(m3, *pr)
        y = _matmul("mm_out", *SQ_TILES[l])(mg.reshape(B * T, D), big["w_out"]).reshape(B, T, D)
        (X,) = _rowwise(_resid_f(D), "resid", [(D, F32)], tm, S, 2, 1, 0)(X, y, modp)
    return _final_loss(S, tm)(X, target, p["final_g"][None])


def _shard_axis(name):
    return {"w_mod": 2, "w_in": 2, "conv_w": 2, "lru_ba": 2, "lru_bx": 2, "lru_lambda": 2,
            "w_branch": 2, "w_out": 1}.get(name)


def _unshard(g, axis):
    full = jnp.moveaxis(g, 0, axis)
    shape = list(full.shape)
    shape[axis:axis + 2] = [shape[axis] * shape[axis + 1]]
    return full.reshape(shape)


def _reshard(full, axis):
    shape = list(full.shape)
    shape[axis:axis + 1] = [N_DEV, shape[axis] // N_DEV]
    return jnp.moveaxis(full.reshape(shape), axis, 0)


def _pad_to(v, n):
    return jnp.concatenate([v, jnp.zeros((n - v.shape[0],), v.dtype)]) if n > v.shape[0] else v


def _step(x, c, ctx, target, w, m, v):
    D = x.shape[2]
    IN = w["w_in"].shape[2] * N_DEV
    full = {n: w[n] for n in REPLICATED}
    small_local = jnp.concatenate([w[n].reshape(-1) for n in SMALL_SHARDED])
    small_all = _exchange(small_local.reshape(-1, LANE), "gather_small", True).reshape(N_DEV, -1)
    off = 0
    for n in SMALL_SHARDED:
        sz = w[n].size
        full[n] = _unshard(small_all[:, off:off + sz].reshape((N_DEV,) + w[n].shape), _shard_axis(n))
        off += sz
    packs = [_pack_layer(w, l, D) for l in range(DEPTH)]

    loss, (gpacks, gp, gx) = jax.value_and_grad(_loss_fn, argnums=(0, 1, 2))(packs, full, x, c, ctx, target)
    loss = lax.psum(loss, AXES)

    out = {}
    gshards = [_unpack_shards(g, D, IN) for g in gpacks]
    for n in BIG_PACK:
        g = jnp.stack([gs[n] for gs in gshards])
        C = w[n].shape[-1]
        res = _adamw(g.reshape(1, -1, C), w[n].reshape(-1, C), m[n].reshape(-1, C), v[n].reshape(-1, C),
                     "adamw_" + n)
        out[n] = [r.reshape(w[n].shape) for r in res]

    rep = jnp.concatenate([gp[n].reshape(-1) for n in REPLICATED])
    n_rep = rep.shape[0]
    chunk = -(-n_rep // (N_DEV * LANE)) * LANE
    rep = _pad_to(rep, N_DEV * chunk).reshape(N_DEV, chunk)
    shards = jnp.concatenate([_reshard(gp[n], _shard_axis(n)).reshape(N_DEV, -1) for n in SMALL_SHARDED], axis=1)
    n_sh = shards.shape[1]
    recv = _exchange(jnp.concatenate([rep, shards], axis=1).reshape(N_DEV, -1, LANE), "scatter_small", False)
    wl =jnp.concatenate([w[n].reshape(-1) for n in SMALL_SHARDED])
    ml = jnp.concatenate([m[n].reshape(-1) for n in SMALL_SHARDED])
    vl = jnp.concatenate([v[n].reshape(-1) for n in SMALL_SHARDED])
    rows = recv.shape[1]
    rrows = chunk // LANE
    g_sh, d_sh, m_sh, v_sh = _adamw(recv[:, rrows:], wl.reshape(-1, LANE), ml.reshape(-1, LANE),
                                    vl.reshape(-1, LANE), "adamw_small_sharded")
    g_rep8 = _sum_slots(recv[:, :rrows], "sum_replicated")
    g_rep = _exchange(g_rep8, "gather_replicated", True).reshape(-1)
    wr = _pad_to(jnp.concatenate([w[n].reshape(-1) for n in REPLICATED]), N_DEV * chunk)
    mr = _pad_to(jnp.concatenate([m[n].reshape(-1) for n in REPLICATED]), N_DEV * chunk)
    vr = _pad_to(jnp.concatenate([v[n].reshape(-1) for n in REPLICATED]), N_DEV * chunk)
    res_rep = _adamw(g_rep.reshape(1, -1, LANE), wr.reshape(-1, LANE), mr.reshape(-1, LANE), vr.reshape(-1, LANE),
                     "adamw_replicated")
    off = 0
    for n in REPLICATED:
        sz = w[n].size
        out[n] = [r.reshape(-1)[off:off + sz].reshape(w[n].shape) for r in res_rep]
        off += sz
    off = 0
    for n in SMALL_SHARDED:
        sz = w[n].size
        out[n] = [r.reshape(-1)[off:off + sz].reshape(w[n].shape) for r in (g_sh, d_sh, m_sh, v_sh)]
        off += sz
    assert off == n_sh and rows == rrows + n_sh // LANE
    return (loss, gx, *[out[n][0] for n in WEIGHTS], *[out[n][1] for n in WEIGHTS],
            *[out[n][2] for n in WEIGHTS], *[out[n][3] for n in WEIGHTS])


def kernel(x, c, ctx, c_ctx, norm_g, w_mod, b_mod, w_in, conv_w, conv_b, lru_wa, lru_ba, lru_wx, lru_bx, lru_lambda, attn_sink, q_norm_g, k_norm_g, w_branch, w_out, final_g, loss_target, m_c_ctx, m_norm_g, m_w_mod, m_b_mod, m_w_in, m_conv_w, m_conv_b, m_lru_wa, m_lru_ba, m_lru_wx, m_lru_bx, m_lru_lambda, m_attn_sink, m_q_norm_g, m_k_norm_g, m_w_branch, m_w_out, m_final_g, v_c_ctx, v_norm_g, v_w_mod, v_b_mod, v_w_in, v_conv_w, v_conv_b, v_lru_wa, v_lru_ba, v_lru_wx, v_lru_bx, v_lru_lambda, v_attn_sink, v_q_norm_g, v_k_norm_g, v_w_branch, v_w_out, v_final_g):
    w = dict(zip(WEIGHTS, (c_ctx, norm_g, w_mod, b_mod, w_in, conv_w, conv_b, lru_wa, lru_ba, lru_wx, lru_bx,
                           lru_lambda, attn_sink, q_norm_g, k_norm_g, w_branch, w_out, final_g)))
    m = dict(zip(WEIGHTS, (m_c_ctx, m_norm_g, m_w_mod, m_b_mod, m_w_in, m_conv_w, m_conv_b, m_lru_wa, m_lru_ba,
                           m_lru_wx, m_lru_bx, m_lru_lambda, m_attn_sink, m_q_norm_g, m_k_norm_g, m_w_branch,
                           m_w_out, m_final_g)))
    v = dict(zip(WEIGHTS, (v_c_ctx, v_norm_g, v_w_mod, v_b_mod, v_w_in, v_conv_w, v_conv_b, v_lru_wa, v_lru_ba,
                           v_lru_wx, v_lru_bx, v_lru_lambda, v_attn_sink, v_q_norm_g, v_k_norm_g, v_w_branch,
                           v_w_out, v_final_g)))
    return _step(x, c, ctx, loss_target, w, m, v)
```

```python
import functools
import math

import jax
import jax.numpy as jnp
from jax import lax
from jax.experimental import pallas as pl
from jax.experimental.pallas import tpu as pltpu

F32 = jnp.float32
BF16 = jnp.bfloat16

AXES = ("x", "y", "c")
N_DEV = 8
DEPTH = 4
HEAD_DIM = 128
GRID_W = 64
WINDOW = 128
LRU_BLOCK_W = 64
GATE_BLOCK = 256
LRU_C = 8.0
ROPE_THETA = 10000.0
EPS = 1e-6
NEG_INF = -1e30
KV_GROUP = 4
LANE = 128
SUBLANE = 8
VMEM_LIMIT = 56 * 1024 * 1024

ADAM_LR = 0.001
ADAM_B1 = 0.9
ADAM_B2 = 0.999
ADAM_EPS = 1e-08
ADAM_WD = 0.01
ADAM_STEP = 10

WEIGHTS = ("c_ctx", "norm_g", "w_mod", "b_mod", "w_in", "conv_w", "conv_b", "lru_wa", "lru_ba", "lru_wx",
           "lru_bx", "lru_lambda", "attn_sink", "q_norm_g", "k_norm_g", "w_branch", "w_out", "final_g")
BIG_PACK = ("w_in", "w_mod", "w_branch", "w_out")
SMALL_SHARDED = ("conv_w", "lru_ba", "lru_bx", "lru_lambda")
REPLICATED = ("c_ctx", "norm_g", "b_mod", "conv_b", "lru_wa", "lru_wx", "attn_sink", "q_norm_g", "k_norm_g",
              "final_g")


def _call(body, **kw):
    return pl.pallas_call(body, **kw)


def _params(dims=None, vmem=VMEM_LIMIT):
    return pltpu.CompilerParams(dimension_semantics=dims, vmem_limit_bytes=vmem)


def _pick(n, target, mult):
    for t in range(min(n, target), 0, -1):
        if n % t == 0 and t % mult == 0:
            return t
    return n


def _mm(a, b, *, name, ta=False, tb=False, out_dtype=F32, tm=512, tn=1024, tk=1024):
    M, K = (a.shape[1], a.shape[0]) if ta else a.shape
    N = b.shape[0] if tb else b.shape[1]
    assert (b.shape[1] if tb else b.shape[0]) == K
    tm = _pick(M, tm, LANE if ta else 16)
    tn = _pick(N, tn, LANE)
    tk = _pick(K, tk, 16 if ta and not tb else LANE)
    nk = K // tk
    dn = (((0 if ta else 1,), (1 if tb else 0,)), ((), ()))

    def body(a_ref, b_ref, o_ref, *acc):
        r = lax.dot_general(a_ref[...].astype(BF16), b_ref[...].astype(BF16), dn,
                            preferred_element_type=F32)
        if nk == 1:
            o_ref[...] = r.astype(o_ref.dtype)
        else:
            k = pl.program_id(2)

            @pl.when(k == 0)
            def _():
                acc[0][...] = r

            @pl.when(k > 0)
            def _():
                acc[0][...] += r

            @pl.when(k == nk - 1)
            def _():
                o_ref[...] = acc[0][...].astype(o_ref.dtype)

    a_spec = (pl.BlockSpec((tk, tm), lambda i, j, k: (k, i)) if ta
              else pl.BlockSpec((tm, tk), lambda i, j, k: (i, k)))
    b_spec = (pl.BlockSpec((tn, tk), lambda i, j, k: (j, k)) if tb
              else pl.BlockSpec((tk, tn), lambda i, j, k: (k, j)))
    return _call(
        body, name=name, grid=(M // tm, N // tn, nk),
        in_specs=[a_spec, b_spec],
        out_specs=pl.BlockSpec((tm, tn), lambda i, j, k: (i, j)),
        out_shape=jax.ShapeDtypeStruct((M, N), out_dtype),
        scratch_shapes=[pltpu.VMEM((tm, tn), F32)] if nk > 1 else [],
        compiler_params=_params(("parallel", "parallel", "arbitrary")),
    )(a, b)


def _matmul(name, t_fwd=(512, 1024, 1024), t_da=(512, 1024, 1024), t_dw=(1024, 1024, 512), out_dtype=F32):
    def tiles(t):
        return dict(tm=t[0], tn=t[1], tk=t[2])

    @jax.custom_vjp
    def f(a, w):
        return _mm(a, w, name=name + "_fwd", out_dtype=out_dtype, **tiles(t_fwd))

    def fwd(a, w):
        return f(a, w), (a, w)

    def bwd(res, g):
        a, w = res
        da = _mm(g, w, name=name + "_da", tb=True, out_dtype=a.dtype, **tiles(t_da))
        dw = _mm(a, g, name=name + "_dw", ta=True, out_dtype=w.dtype, **tiles(t_dw))
        return da, dw

    f.defvjp(fwd, bwd)
    return f


def _mm_parts_nt(parts, w, col0, init, out_dtype, *, name, tm=1088, tk=512):
    M = parts[0].shape[0]
    D = w.shape[0]
    tm = _pick(M, tm, 16)
    tk = math.gcd(tk, *[a.shape[1] for a in parts])
    starts, n = [], 0
    for a in parts:
        assert a.shape[1] % tk == 0
        starts.append(n)
        n += a.shape[1] // tk
    k0 = col0 // tk
    has_init = init is not None

    def body(*refs):
        a_refs, w_ref = refs[:len(parts)], refs[len(parts)]
        o_ref, acc = refs[-2], refs[-1]
        k = pl.program_id(1)

        @pl.when(k == 0)
        def _():
            acc[...] = refs[len(parts) + 1][...].astype(F32) if has_init else jnp.zeros_like(acc)

        for a_ref, s, a in zip(a_refs, starts, parts):
            @pl.when(jnp.logical_and(k >= s, k < s + a.shape[1] // tk))
            def _(a_ref=a_ref):
                acc[...] += lax.dot_general(a_ref[...].astype(BF16), w_ref[...], NT, preferred_element_type=F32)

        @pl.when(k == n - 1)
        def _():
            o_ref[...] = acc[...].astype(o_ref.dtype)

    in_specs = [pl.BlockSpec((tm, tk), lambda i, k, s=s, c=a.shape[1] // tk: (i, jnp.clip(k - s, 0, c - 1)))
                for a, s in zip(parts, starts)]
    in_specs.append(pl.BlockSpec((D, tk), lambda i, k: (0, k0 + k)))
    o_spec = pl.BlockSpec((tm, D), lambda i, k: (i, 0))
    args = list(parts) + [w]
    if has_init:
        in_specs.append(o_spec)
        args.append(init)
    return _call(body, name=name, grid=(M // tm, n), in_specs=in_specs, out_specs=o_spec,
                 out_shape=jax.ShapeDtypeStruct((M, D), out_dtype),
                 scratch_shapes=[pltpu.VMEM((tm, D), F32)],
                 compiler_params=_params(("parallel", "arbitrary")))(*args)


def _in_proj(widths, dtypes, t_fwd=(2176, 1024, 1024), t_dw=(1024, 1024, 2176), group=5):
    offs = [0]
    for wd in widths:
        offs.append(offs[-1] + wd)

    def cols(w, i):
        return w[:, offs[i]:offs[i + 1]]

    @jax.custom_vjp
    def op(h, w):
        return tuple(_mm(h, cols(w, i), name="mm_in_fwd", out_dtype=dtypes[i], tm=t_fwd[0], tn=t_fwd[1],
                         tk=t_fwd[2]) for i in range(len(widths)))

    def fwd(h, w):
        return op(h, w), (h, w)

    def bwd(res, gs):
        h, w = res
        da = None
        for i0 in range(0, len(widths), group):
            last = i0 + group >= len(widths)
            da = _mm_parts_nt(list(gs[i0:i0 + group]), w, offs[i0], da, h.dtype if last else F32, name="mm_in_da")
        dw = jnp.concatenate([_mm(h, g, name="mm_in_dw", ta=True, out_dtype=w.dtype, tm=t_dw[0], tn=t_dw[1],
                                  tk=t_dw[2]) for g in gs], axis=1)
        return da, dw

    op.defvjp(fwd, bwd)
    return op


def _row_specs(xs, tps, gps, rts, tm, n_lat_tiles):
    x_specs = [pl.BlockSpec((None, tm, x.shape[2]), lambda b, i: (b, i, 0)) for x in xs]
    tp_specs = [pl.BlockSpec((None, None, 1, p.shape[3]),
                             lambda b, i: (b, (i < n_lat_tiles).astype(jnp.int32), 0, 0)) for p in tps]
    gp_specs = [pl.BlockSpec(p.shape, lambda b, i: (0, 0)) for p in gps]
    rt_specs = [pl.BlockSpec((tm, t.shape[1]), lambda b, i: (i, 0)) for t in rts]
    return x_specs, tp_specs, gp_specs, rt_specs


def _row_fwd(f, name, xs, tps, gps, rts, outs, tm, n_lat):
    B, T, _ = xs[0].shape
    n_in = len(xs) + len(tps) + len(gps) + len(rts)

    def body(*refs):
        vals = f(*[r[...] for r in refs[:n_in]])
        for o, v in zip(refs[n_in:], vals):
            o[...] = v.astype(o.dtype)

    x_specs, tp_specs, gp_specs, rt_specs = _row_specs(xs, tps, gps, rts, tm, n_lat // tm)
    res = _call(
        body, name=name, grid=(B, T // tm),
        in_specs=x_specs + tp_specs + gp_specs + rt_specs,
        out_specs=[pl.BlockSpec((None, tm, w), lambda b, i: (b, i, 0)) for w, _ in outs],
        out_shape=[jax.ShapeDtypeStruct((B, T, w), dt) for w, dt in outs],
        compiler_params=_params(("parallel", "parallel")),
    )(*xs, *tps, *gps, *rts)
    return list(res)


def _row_bwd(f, name, xs, tps, gps, rts, douts, tm, n_lat):
    B, T, _ = xs[0].shape
    nx, ntp, ngp, nd = len(xs), len(tps), len(gps), len(douts)
    n_lat_tiles = n_lat // tm
    n_diff = nx + ntp + ngp
    n_in = n_diff + len(rts)

    def body(*refs):
        diff = [r[...] for r in refs[:n_diff]]
        tabs = [r[...] for r in refs[n_diff:n_in]]
        dos = [r[...] for r in refs[n_in:n_in + nd]]
        o_refs = refs[n_in + nd:]
        prim, vjp = jax.vjp(lambda *d: tuple(f(*d, *tabs)), *diff)
        grads = vjp(tuple(d.astype(p.dtype) for d, p in zip(dos, prim)))
        b, i = pl.program_id(0), pl.program_id(1)
        for k in range(nx):
            o_refs[k][...] = grads[k].astype(o_refs[k].dtype)
        first_tp = jnp.logical_or(i == 0, i == n_lat_tiles)
        first_gp = jnp.logical_and(b == 0, i == 0)
        for k in range(nx, n_diff):
            first = first_tp if k < nx + ntp else first_gp

            @pl.when(first)
            def _(k=k):
                o_refs[k][...] = grads[k]

            @pl.when(jnp.logical_not(first))
            def _(k=k):
                o_refs[k][...] += grads[k]

    x_specs, tp_specs, gp_specs, rt_specs = _row_specs(xs, tps, gps, rts, tm, n_lat_tiles)
    d_specs = [pl.BlockSpec((None, tm, d.shape[2]), lambda b, i: (b, i, 0)) for d in douts]
    res = _call(
        body, name=name, grid=(B, T // tm),
        in_specs=x_specs + tp_specs + gp_specs + rt_specs + d_specs,
        out_specs=x_specs + tp_specs + gp_specs,
        out_shape=[jax.ShapeDtypeStruct(a.shape, a.dtype) for a in xs]
        + [jax.ShapeDtypeStruct(a.shape, F32) for a in (*tps, *gps)],
        compiler_params=_params(("arbitrary", "arbitrary")),
    )(*xs, *tps, *gps, *rts, *douts)
    res = list(res)
    return res[:nx], res[nx:nx + ntp], res[nx + ntp:]


def _rowwise(f, name, outs, tm, n_lat, n_x, n_tp, n_gp):
    def split(args):
        return (args[:n_x], args[n_x:n_x + n_tp], args[n_x + n_tp:n_x + n_tp + n_gp],
                args[n_x + n_tp + n_gp:])

    @jax.custom_vjp
    def op(*args):
        xs, tps, gps, rts = split(args)
        return tuple(_row_fwd(f, name + "_fwd", xs, tps, gps, rts, outs, tm, n_lat))

    def fwd(*args):
        return op(*args), args

    def bwd(args, g):
        xs, tps, gps, rts = split(args)
        dxs, dtps, dgps = _row_bwd(f, name + "_bwd", xs, tps, gps, rts, list(g), tm, n_lat)
        return (*dxs, *dtps, *dgps, *[jnp.zeros_like(t) for t in rts])

    op.defvjp(fwd, bwd)
    return op


def _shift_impl(u, k):
    n = u.shape[0]
    r = pltpu.roll(u, k % n, axis=0)
    row = lax.broadcasted_iota(jnp.int32, u.shape, 0)
    valid = (row >= k) if k > 0 else (row < n + k)
    return jnp.where(valid, r, 0.0)


@functools.partial(jax.custom_vjp, nondiff_argnums=(1,))
def _shift(u, k):
    return _shift_impl(u, k)


_shift.defvjp(lambda u, k: (_shift_impl(u, k), None), lambda k, _, g: (_shift_impl(g, -k),))


def _swap_impl(x):
    lane = lax.broadcasted_iota(jnp.int32, x.shape, 1)
    q = HEAD_DIM // 4
    return jnp.where((lane % (2 * q)) < q, pltpu.roll(x, HEAD_DIM - q, axis=1), pltpu.roll(x, q, axis=1))


@jax.custom_vjp
def _swap(x):
    return _swap_impl(x)


_swap.defvjp(lambda x: (_swap_impl(x), None), lambda _, g: (_swap_impl(g),))


def _conv_f(ul, uc, cw, cb):
    def conv(u):
        return (_shift(u, 2) * cw[0:1] + _shift(u, 1) * cw[1:2] + u * cw[2:3] + _shift(u, -1) * cw[3:4] + cb)
    return conv(ul), conv(uc)


def _conv_specs(B, T, D):
    u_spec = pl.BlockSpec((None, T, LANE), lambda j, b: (b, 0, j))
    cw_spec = pl.BlockSpec((4, LANE), lambda j, b: (0, j))
    cb_spec = pl.BlockSpec((1, LANE), lambda j, b: (0, j))
    return u_spec, cw_spec, cb_spec


def _conv_fwd(u, cw, cb, S):
    B, T, D = u.shape

    def body(u_ref, cw_ref, cb_ref, o_ref):
        vl, vc = _conv_f(u_ref[0:S, :], u_ref[S:T, :], cw_ref[...], cb_ref[...])
        o_ref[0:S, :] = vl
        o_ref[S:T, :] = vc

    u_spec, cw_spec, cb_spec = _conv_specs(B, T, D)
    return _call(body, name="conv_fwd", grid=(D // LANE, B), in_specs=[u_spec, cw_spec, cb_spec],
                 out_specs=u_spec, out_shape=jax.ShapeDtypeStruct(u.shape, F32),
                 compiler_params=_params(("parallel", "parallel")))(u, cw, cb)


def _conv_bwd(u, cw, cb, dv, S):
    B, T, D = u.shape

    def body(u_ref, cw_ref, cb_ref, dv_ref, du_ref, dcw_ref, dcb_ref):
        _, vjp = jax.vjp(_conv_f, u_ref[0:S, :], u_ref[S:T, :], cw_ref[...], cb_ref[...])
        dul, duc, dcw, dcb = vjp((dv_ref[0:S, :], dv_ref[S:T, :]))
        du_ref[0:S, :] = dul
        du_ref[S:T, :] = duc
        first = pl.program_id(1) == 0

        @pl.when(first)
        def _():
            dcw_ref[...] = dcw
            dcb_ref[...] = dcb

        @pl.when(jnp.logical_not(first))
        def _():
            dcw_ref[...] += dcw
            dcb_ref[...] += dcb

    u_spec, cw_spec, cb_spec = _conv_specs(B, T, D)
    return _call(body, name="conv_bwd", grid=(D // LANE, B), in_specs=[u_spec, cw_spec, cb_spec, u_spec],
                 out_specs=[u_spec, cw_spec, cb_spec],
                 out_shape=[jax.ShapeDtypeStruct(u.shape, F32), jax.ShapeDtypeStruct(cw.shape, F32),
                            jax.ShapeDtypeStruct(cb.shape, F32)],
                 compiler_params=_params(("parallel", "arbitrary")))(u, cw, cb, dv)


def _conv_op(S):
    @jax.custom_vjp
    def op(u, cw, cb):
        return _conv_fwd(u, cw, cb, S)

    def fwd(u, cw, cb):
        return op(u, cw, cb), (u, cw, cb)

    def bwd(res, g):
        u, cw, cb = res
        return tuple(_conv_bwd(u, cw, cb, g, S))

    op.defvjp(fwd, bwd)
    return op


SCAN_UNROLL = 4


def _group_scan(A, Bv, asc):
    row = lax.broadcasted_iota(jnp.int32, A.shape, 0)
    for s in (1, 2, 4):
        sh = s if asc else SUBLANE - s
        valid = (row >= s) if asc else (row < SUBLANE - s)
        A_sh = pltpu.roll(A, sh, axis=0)
        B_sh = pltpu.roll(Bv, sh, axis=0)
        Bv = jnp.where(valid, A * B_sh, 0.0) + Bv
        A = jnp.where(valid, A * A_sh, A)
    return A, Bv


def _chain_step(A, Bv, carry, asc):
    row = lax.broadcasted_iota(jnp.int32, A.shape, 0)
    A2, B2 = _group_scan(A, Bv, asc)
    h = A2 * carry + B2
    if asc:
        prev = jnp.where(row == 0, carry, pltpu.roll(h, 1, axis=0))
        return h, prev, h[SUBLANE - 1:SUBLANE, :]
    prev = jnp.where(row == SUBLANE - 1, carry, pltpu.roll(h, SUBLANE - 1, axis=0))
    return h, prev, h[0:1, :]


def _chain_loop(segments, step):
    carry = jnp.zeros((1, LANE), F32)
    for lo, hi, asc in segments:
        span = SUBLANE * SCAN_UNROLL
        assert (hi - lo) % span == 0

        def it(t, carry, lo=lo, hi=hi, asc=asc, span=span):
            base = lo + t * span if asc else hi - (t + 1) * span
            order = range(SCAN_UNROLL) if asc else reversed(range(SCAN_UNROLL))
            for j in order:
                carry = step(pl.multiple_of(base + SUBLANE * j, SUBLANE), carry, asc)
            return carry

        carry = lax.fori_loop(0, (hi - lo) // span, it, carry)
    return carry


def _scan_specs(T):
    return pl.BlockSpec((None, T, LANE), lambda j, b: (b, 0, j))


def _scan_fwd(a, b, S, reverse):
    B, T, D = a.shape
    asc = not reverse
    segments = [(S, T, asc), (0, S, asc)]

    def body(a_ref, b_ref, h_ref, hp_ref):
        def step(r0, carry, asc):
            rows = pl.ds(r0, SUBLANE)
            h, prev, carry = _chain_step(a_ref[rows, :], b_ref[rows, :], carry, asc)
            h_ref[rows, :] = h
            hp_ref[rows, :] = prev
            return carry
        _chain_loop(segments, step)

    spec = _scan_specs(T)
    return _call(body, name="scan_rev_fwd" if reverse else "scan_fwd_fwd", grid=(D // LANE, B),
                 in_specs=[spec, spec], out_specs=[spec, spec],
                 out_shape=[jax.ShapeDtypeStruct(a.shape, F32)] * 2,
                 compiler_params=_params(("parallel", "parallel")))(a, b)


def _scan_bwd(a, hp, dy, S, reverse):
    B, T, D = a.shape
    asc = reverse
    segments = [(0, S, asc), (S, T, asc)]

    def body(a_ref, hp_ref, dy_ref, da_ref, db_ref):
        def step(r0, carry, asc):
            rows = pl.ds(r0, SUBLANE)
            A, dy = a_ref[rows, :], dy_ref[rows, :]
            _, s_prev, carry = _chain_step(A, A * dy, carry, asc)
            g = dy + s_prev
            db_ref[rows, :] = g
            da_ref[rows, :] = g * hp_ref[rows, :]
            return carry
        _chain_loop(segments, step)

    spec = _scan_specs(T)
    return _call(body, name="scan_rev_bwd" if reverse else "scan_fwd_bwd", grid=(D // LANE, B),
                 in_specs=[spec, spec, spec], out_specs=[spec, spec],
                 out_shape=[jax.ShapeDtypeStruct(a.shape, F32)] * 2,
                 compiler_params=_params(("parallel", "parallel")))(a, hp, dy)


def _scan_op(S, reverse):
    @jax.custom_vjp
    def op(a, b):
        return _scan_fwd(a, b, S, reverse)[0]

    def fwd(a, b):
        h, hp = _scan_fwd(a, b, S, reverse)
        return h, (a, hp)

    def bwd(res, g):
        a, hp = res
        return tuple(_scan_bwd(a, hp, g, S, reverse))

    op.defvjp(fwd, bwd)
    return op


def _band_lo(qi, tq, S):
    span = tq + 2 * WINDOW
    return pl.multiple_of(jnp.clip(qi * tq - WINDOW, 0, S - span), LANE)


def _band_mask(qi, tq, lo, span, transposed, heads=1):
    shape = (span, heads * tq) if transposed else (heads * tq, span)
    assert tq & (tq - 1) == 0
    qpos = qi * tq + jnp.bitwise_and(lax.broadcasted_iota(jnp.int32, shape, 1 if transposed else 0), tq - 1)
    kpos = lo + lax.broadcasted_iota(jnp.int32, shape, 0 if transposed else 1)
    return jnp.abs(kpos - qpos) <= WINDOW


NT = (((1,), (1,)), ((), ()))
LOG2E = 1.4426950408889634
KEY_CHUNK = 1024


def _col_to_row(c):
    return jnp.broadcast_to(c, (c.shape[0], LANE)).T[0:1, :]


def _attn_fwd(q, k, v, sink, gate, S, band, side=None):
    B, T, HD = q.shape
    H = HD // HEAD_DIM
    L = T - S
    tq = L
    n_lq = S // tq
    span = tq + 2 * WINDOW
    c2 = HEAD_DIM ** -0.5 * LOG2E
    has_sink = sink is not None
    kc = min(KEY_CHUNK, S)
    hp = KV_GROUP if band else 1

    def body(*refs):
        if has_sink:
            q_ref, k_ref, v_ref, g_ref, s_ref, z_ref, o_ref, lse_ref = refs
        else:
            q_ref, k_ref, v_ref, g_ref, z_ref, o_ref, lse_ref = refs
        qi = pl.program_id(2)

        def online(qv, sk2, segs):
            def scores(seg):
                s = lax.dot_general(qv, k_ref[seg[0], :], NT, preferred_element_type=F32) * c2
                return s if seg[1] is None else jnp.where(seg[1], s, NEG_INF)

            m = l = acc = None
            s_next = scores(segs[0])
            for j, (rows, _) in enumerate(segs):
                s = s_next
                if j + 1 < len(segs):
                    s_next = scores(segs[j + 1])
                ms = jnp.max(s, axis=-1, keepdims=True)
                if m is None:
                    m = ms if sk2 is None else jnp.maximum(ms, sk2)
                    p = jnp.exp2(s - m)
                    l = jnp.sum(p, axis=-1, keepdims=True)
                    if sk2 is not None:
                        l = l + jnp.exp2(sk2 - m)
                    acc = jnp.dot(p.astype(BF16), v_ref[rows, :], preferred_element_type=F32)
                else:
                    m_new = jnp.maximum(m, ms)
                    alpha = jnp.exp2(m - m_new)
                    p = jnp.exp2(s - m_new)
                    l = alpha * l + jnp.sum(p, axis=-1, keepdims=True)
                    acc = alpha * acc + jnp.dot(p.astype(BF16), v_ref[rows, :], preferred_element_type=F32)
                    m = m_new
            return acc, m, l

        def joint(qv, sk2, segs):
            ss = []
            for rows, mask in segs:
                s = lax.dot_general(qv, k_ref[rows, :], NT, preferred_element_type=F32) * c2
                ss.append(s if mask is None else jnp.where(mask, s, NEG_INF))
            m = functools.reduce(jnp.maximum, [jnp.max(s, axis=-1, keepdims=True) for s in ss])
            if sk2 is not None:
                m = jnp.maximum(m, sk2)
            ps = [jnp.exp2(s - m) for s in ss]
            l = functools.reduce(jnp.add, [jnp.sum(p, axis=-1, keepdims=True) for p in ps])
            if sk2 is not None:
                l = l + jnp.exp2(sk2 - m)
            acc = functools.reduce(jnp.add, [jnp.dot(p.astype(BF16), v_ref[rows, :], preferred_element_type=F32)
                                             for p, (rows, _) in zip(ps, segs)])
            return acc, m, l

        def run(segs, softmax):
            for g in range(hp):
                lanes = slice(g * HEAD_DIM, (g + 1) * HEAD_DIM)
                sk2 = s_ref[g][:, 0:1] * LOG2E if has_sink else None
                acc, m, l = softmax(q_ref[:, lanes], sk2, segs)
                o = acc * (1.0 / l)
                gv = g_ref[:, lanes].astype(F32)
                o_ref[:, lanes] = o.astype(o_ref.dtype)
                z_ref[:, lanes] = (o * (gv * jax.nn.sigmoid(gv))).astype(z_ref.dtype)
                lse_ref[g] = _col_to_row(m + jnp.log2(l))

        ctx_rows = pl.ds(S, L)

        @pl.when(qi < n_lq)
        def _():
            if band:
                lo = _band_lo(qi, tq, S)
                run([(ctx_rows, None), (pl.ds(lo, span), _band_mask(qi, tq, lo, span, False))], joint)
            else:
                run([(pl.ds(j * kc, kc), None) for j in range(S // kc)] + [(ctx_rows, None)], online)

        @pl.when(qi >= n_lq)
        def _():
            run([(ctx_rows, None)], joint)

    q_spec = pl.BlockSpec((None, tq, hp * HEAD_DIM), lambda b, h, i: (b, i, h))
    kv_spec = pl.BlockSpec((None, T, HEAD_DIM), lambda b, h, i: (b, 0, h * hp // KV_GROUP))
    in_specs = [q_spec, kv_spec, kv_spec, q_spec]
    args = [q, k, v, gate]
    if has_sink:
        in_specs.append(pl.BlockSpec((hp, 1, LANE), lambda b, h, i: (h, 0, 0)))
        args.append(sink)
    name = "attn_band_fwd" if band else "attn_dense_fwd"
    return _call_with_side(
        body, side, name=name, grid=(B, H // hp, T // tq), in_specs=in_specs, args=args,
        out_specs=[q_spec, q_spec, pl.BlockSpec((None, hp, 1, tq), lambda b, h, i: (b, h, 0, i))],
        out_shape=[jax.ShapeDtypeStruct(q.shape, BF16), jax.ShapeDtypeStruct(q.shape, BF16),
                   jax.ShapeDtypeStruct((B, H, 1, T), F32)],
        scratch_shapes=[], dims=("parallel", "parallel", "parallel"))


def _attn_bwd(q, k, v, sink, gate, dz, o, lse, S, band, side=None):
    B, T, HD = q.shape
    H = HD // HEAD_DIM
    KVH = H // KV_GROUP
    L = T - S
    tq = L
    n_lq = S // tq
    span = tq + 2 * WINDOW
    scale = HEAD_DIM ** -0.5
    c2 = scale * LOG2E
    has_sink = sink is not None
    kc = min(KEY_CHUNK, S)
    hp = KV_GROUP if band else 1

    def body(*refs):
        if has_sink:
            (q_ref, k_ref, v_ref, g_ref, dz_ref, o_ref, lse_ref, s_ref,
             dq_ref, dk_ref, dv_ref, dg_ref, ds_ref, *scr) = refs
        else:
            q_ref, k_ref, v_ref, g_ref, dz_ref, o_ref, lse_ref, dq_ref, dk_ref, dv_ref, dg_ref, *scr = refs
        g, qi = pl.program_id(2), pl.program_id(3)

        @pl.when(jnp.logical_and(g == 0, qi == 0))
        def _():
            dk_ref[...] = jnp.zeros_like(dk_ref)
            dv_ref[...] = jnp.zeros_like(dv_ref)
            if not band:
                scr[0][...] = k_ref[...].astype(F32).T.astype(BF16)

        def run(segs):
            lanes = [slice(h * HEAD_DIM, (h + 1) * HEAD_DIM) for h in range(hp)]

            def stack(ref, axis=0):
                return jnp.concatenate([ref[:, ln] for ln in lanes], axis=axis) if hp > 1 else ref[...]

            qv, gv, dzv = stack(q_ref), stack(g_ref).astype(F32), stack(dz_ref).astype(F32)
            ov = stack(o_ref).astype(F32)
            sg = jax.nn.sigmoid(gv)
            dof = dzv * (gv * sg)
            dgate = dzv * ov * (sg * (1.0 + gv * (1.0 - sg)))
            dov = dof.astype(BF16)
            lse2 = jnp.concatenate([lse_ref[h] for h in range(hp)], axis=1) if hp > 1 else lse_ref[0]
            delta = _col_to_row(jnp.sum(dof * ov, axis=-1, keepdims=True))

            def head(seg):
                rows = pl.ds(seg[0], seg[1])
                s = lax.dot_general(k_ref[rows, :], qv, NT, preferred_element_type=F32) * c2
                if seg[2] is not None:
                    s = jnp.where(seg[2], s, NEG_INF)
                return s, lax.dot_general(v_ref[rows, :], dov, NT, preferred_element_type=F32)

            dq, dqT = None, None
            nxt = head(segs[0])
            for j, (lo, n, _) in enumerate(segs):
                rows = pl.ds(lo, n)
                s, dp = nxt
                if j + 1 < len(segs):
                    nxt = head(segs[j + 1])
                p = jnp.exp2(s - lse2)
                ds = p * (dp - delta)
                dsb = ds.astype(BF16)
                dv_ref[rows, :] += jnp.dot(p.astype(BF16), dov, preferred_element_type=F32)
                dk_ref[rows, :] += jnp.dot(dsb, qv, preferred_element_type=F32) * scale
                if band:
                    part = jnp.dot(ds.T.astype(BF16), k_ref[rows, :], preferred_element_type=F32)
                    dq = part if dq is None else dq + part
                else:
                    part = jnp.dot(scr[0][:, lo:lo + n], dsb, preferred_element_type=F32)
                    dqT = part if dqT is None else dqT + part
            dq = (dq if band else dqT.T) * scale
            for h, ln in enumerate(lanes):
                rows = slice(h * tq, (h + 1) * tq)
                dq_ref[:, ln] = dq[rows].astype(dq_ref.dtype)
                dg_ref[:, ln] = dgate[rows].astype(dg_ref.dtype)
                if has_sink:
                    psk = jnp.exp2(s_ref[h][:, 0:1] * LOG2E - lse2[:, rows])
                    dsk = jnp.broadcast_to(-jnp.sum(psk * delta[:, rows], axis=1, keepdims=True), (1, LANE))

                    @pl.when(qi == 0)
                    def _(h=h, dsk=dsk):
                        ds_ref[h] = dsk

                    @pl.when(qi > 0)
                    def _(h=h, dsk=dsk):
                        ds_ref[h] += dsk

        @pl.when(qi < n_lq)
        def _():
            if band:
                lo = _band_lo(qi, tq, S)
                run([(S, L, None), (lo, span, _band_mask(qi, tq, lo, span, True, hp))])
            else:
                run([(j * kc, kc, None) for j in range(S // kc)] + [(S, L, None)])

        @pl.when(qi >= n_lq)
        def _():
            run([(S, L, None)])

    ng = KV_GROUP // hp
    q_spec = pl.BlockSpec((None, tq, hp * HEAD_DIM), lambda b, kv, g, i: (b, i, kv * ng + g))
    kv_spec = pl.BlockSpec((None, T, HEAD_DIM), lambda b, kv, g, i: (b, 0, kv))
    lse_spec = pl.BlockSpec((None, hp, 1, tq), lambda b, kv, g, i: (b, kv * ng + g, 0, i))
    in_specs = [q_spec, kv_spec, kv_spec, q_spec, q_spec, q_spec, lse_spec]
    out_specs = [q_spec, kv_spec, kv_spec, q_spec]
    out_shape = [jax.ShapeDtypeStruct(q.shape, BF16), jax.ShapeDtypeStruct(k.shape, F32),
                 jax.ShapeDtypeStruct(v.shape, F32), jax.ShapeDtypeStruct(q.shape, gate.dtype)]
    args = [q, k, v, gate, dz, o, lse]
    if has_sink:
        in_specs.append(pl.BlockSpec((hp, 1, LANE), lambda b, kv, g, i: (kv * ng + g, 0, 0)))
        args.append(sink)
        out_specs.append(pl.BlockSpec((None, hp, 1, LANE), lambda b, kv, g, i: (b, kv * ng + g, 0, 0)))
        out_shape.append(jax.ShapeDtypeStruct((B, H, 1, LANE), F32))
    res = _call_with_side(
        body, side, name="attn_band_bwd" if band else "attn_dense_bwd", grid=(B, KVH, ng, T // tq),
        in_specs=in_specs, args=args, out_specs=out_specs, out_shape=out_shape,
        scratch_shapes=[] if band else [pltpu.VMEM((HEAD_DIM, T), BF16)],
        dims=("parallel", "parallel", "arbitrary", "arbitrary"))
    return (res[0], res[1], res[2], res[3], res[4] if has_sink else None, res[-1] if side is not None else None)


def _prep_f(norm):
    def f(q, kv, *rest):
        q, kv = q.astype(F32), kv.astype(F32)
        kw = kv.shape[1] // 2
        k, v = kv[:, :kw], kv[:, kw:]
        if norm:
            qg, kg, cos, sin = rest
        else:
            (cos, sin), qg, kg = rest, None, None

        def heads(x, g):
            outs = []
            for h in range(x.shape[1] // HEAD_DIM):
                xh = x[:, h * HEAD_DIM:(h + 1) * HEAD_DIM]
                if g is not None:
                    xh = xh * lax.rsqrt(jnp.mean(xh * xh, axis=-1, keepdims=True) + EPS) * g
                outs.append(xh * cos + _swap(xh) * sin)
            return jnp.concatenate(outs, axis=1) if len(outs) > 1 else outs[0]

        return heads(q, qg), heads(k, kg), v

    return f


def _attn_branch(S, tm, band, norm, has_sink, carries=False):
    f = _prep_f(norm)
    name = "band" if band else "dense"

    def prep(q, kv, gains, tabs):
        outs = [(q.shape[2], BF16), (kv.shape[2] // 2, BF16), (kv.shape[2] // 2, BF16)]
        return _row_fwd(f, "prep_" + name + "_fwd", [q, kv], [], list(gains), list(tabs), outs, tm, S)

    def unpack(args):
        q, kv, gate = args[:3]
        rest = list(args[3:])
        gains = [rest.pop(0), rest.pop(0)] if norm else []
        sink = rest.pop(0) if has_sink else None
        pack = rest.pop(0) if carries else None
        return q, kv, gate, gains, sink, pack, rest

    def sink_lanes(sink):
        return None if sink is None else jnp.broadcast_to(sink[:, None, None], (sink.shape[0], 1, LANE))

    def run_fwd(args):
        q, kv, gate, gains, sink, pack, tabs = unpack(args)
        qp, kp, vp = prep(q, kv, gains, tabs)
        side = (pack.astype(BF16), True) if carries else None
        res = _attn_fwd(qp, kp, vp, sink_lanes(sink), gate, S, band, side)
        return ((res[0], res[3]) if carries else res[0]), (args, qp, kp, vp, res[1], res[2])

    @jax.custom_vjp
    def op(*args):
        return run_fwd(args)[0]

    def fwd(*args):
        return run_fwd(args)

    def bwd(res, ct):
        args, qp, kp, vp, o, lse = res
        q, kv, gate, gains, sink, pack, tabs = unpack(args)
        dz, side = (ct[0], (ct[1], False)) if carries else (ct, None)
        dqp, dkp, dvp, dgate, dsk, recv = _attn_bwd(qp, kp, vp, sink_lanes(sink), gate, dz, o, lse, S, band, side)
        dxs, _, dgains = _row_bwd(f, "prep_" + name + "_bwd", [q, kv], [], list(gains), list(tabs),
                                  [dqp, dkp, dvp], tm, S)
        out = list(dxs) + [dgate] + list(dgains)
        if has_sink:
            out.append(jnp.sum(dsk[:, :, 0, 0], axis=0))
        if carries:
            out.append(_sum_slots(recv, "sum_grads"))
        return (*out, *[jnp.zeros_like(t) for t in tabs])

    op.defvjp(fwd, bwd)
    return op


def _final_loss(S, tm):
    def run(X, target, g):
        B, T, D = X.shape
        n_lat_tiles = S // tm

        def lossf(x, gg, tgt):
            y = x * lax.rsqrt(jnp.mean(x * x, axis=-1, keepdims=True) + EPS) * gg
            err = y - tgt
            return 0.5 * jnp.sum(jnp.sum(err * err, axis=-1, keepdims=True), axis=0, keepdims=True) / D

        def body(x_ref, t_ref, g_ref, loss_ref, dx_ref, dg_ref):
            b, i = pl.program_id(0), pl.program_id(1)
            tgt = t_ref[...]
            val, vjp = jax.vjp(lambda x, gg: lossf(x, gg, tgt), x_ref[...], g_ref[...])
            dx, dg = vjp(jnp.ones((1, 1), F32))
            lat = (i < n_lat_tiles).astype(F32)
            dx_ref[...] = dx * lat

            @pl.when(i == 0)
            def _():
                loss_ref[...] = jnp.zeros_like(loss_ref)

            @pl.when(jnp.logical_and(b == 0, i == 0))
            def _():
                dg_ref[...] = jnp.zeros_like(dg_ref)

            loss_ref[...] += jnp.broadcast_to(val * lat, loss_ref.shape)
            dg_ref[...] += dg * lat

        x_spec = pl.BlockSpec((None, tm, D), lambda b, i: (b, i, 0))
        t_spec = pl.BlockSpec((None, tm, D), lambda b, i: (b, jnp.minimum(i, n_lat_tiles - 1), 0))
        g_spec = pl.BlockSpec((1, D), lambda b, i: (0, 0))
        loss, dx, dg = _call(
            body, name="final_loss", grid=(B, T // tm), in_specs=[x_spec, t_spec, g_spec],
            out_specs=[pl.BlockSpec((None, 1, LANE), lambda b, i: (b, 0, 0)), x_spec, g_spec],
            out_shape=[jax.ShapeDtypeStruct((B, 1, LANE), F32), jax.ShapeDtypeStruct(X.shape, F32),
                       jax.ShapeDtypeStruct(g.shape, F32)],
            compiler_params=_params(("arbitrary", "arbitrary")))(X, target, g)
        return jnp.sum(loss[:, 0, 0]), dx, dg

    @jax.custom_vjp
    def op(X, target, g):
        return run(X, target, g)[0]

    def fwd(X, target, g):
        loss, dx, dg = run(X, target, g)
        return loss, (dx, dg, target)

    def bwd(res, ct):
        dx, dg, target = res
        return ct * dx, jnp.zeros_like(target), ct * dg

    op.defvjp(fwd, bwd)
    return op


def _prenorm_f(D):
    def f(x, mp, g):
        y = x * lax.rsqrt(jnp.mean(x * x, axis=-1, keepdims=True) + EPS) * g
        return (y * (1.0 + mp[:, D:2 * D]) + mp[:, 0:D],)
    return f


def _resid_f(D):
    def f(x, y, mp):
        return (x + mp[:, 2 * D:3 * D] * y.astype(F32),)
    return f


def _decay_impl(la):
    t = jnp.tanh(la)
    x = -2.0 * t / (1.0 - t)
    r = lax.rsqrt(x)
    return jnp.exp(la), x * r, r


@jax.custom_vjp
def _decay(la):
    a, bc, _ = _decay_impl(la)
    return a, bc


def _decay_fwd(la):
    a, bc, r = _decay_impl(la)
    return (a, bc), (a, r)


def _decay_bwd(res, ct):
    a, r = res
    return (ct[0] * a - ct[1] * (a * a * r),)


_decay.defvjp(_decay_fwd, _decay_bwd)


def _gate2_f(hf, hr, g):
    g = g.astype(F32)
    return ((hf + hr) * (g * jax.nn.sigmoid(g)),)


def _merge_f(D):
    def f(m3, pa, pb, pc):
        m3 = m3.astype(F32)
        return (jax.nn.sigmoid(m3[:, 0:D]) * pa.astype(F32) + jax.nn.sigmoid(m3[:, D:2 * D]) * pb.astype(F32)
                + jax.nn.sigmoid(m3[:, 2 * D:3 * D]) * pc.astype(F32),)
    return f


def _coef_f(D):
    nblk = D // GATE_BLOCK

    def f(v, pv, wm):
        vb = v.astype(BF16)

        def gate(k):
            cols = []
            for j in range(nblk):
                r0 = (k * nblk + j) * GATE_BLOCK
                cols.append(jnp.dot(vb[:, j * GATE_BLOCK:(j + 1) * GATE_BLOCK],
                                    wm[r0:r0 + GATE_BLOCK, :].astype(BF16), preferred_element_type=F32))
            return jnp.concatenate(cols, axis=1)

        outs = []
        for d in range(2):
            r = jax.nn.sigmoid(gate(d) + pv[d:d + 1])
            i = jax.nn.sigmoid(gate(2 + d) + pv[2 + d:3 + d])
            a, bc = _decay(r * pv[4 + d:5 + d])
            outs += [a, bc * (i * v)]
        return tuple(outs)

    return f


def _gate_blocks(w):
    per = GATE_BLOCK // LRU_BLOCK_W
    n = w.shape[1]
    w5 = w.reshape(2, n // per, per, LRU_BLOCK_W, LRU_BLOCK_W)
    dense = jnp.einsum("djiab,ik->djiakb", w5, jnp.eye(per, dtype=w.dtype))
    return dense.reshape(2, (n // per) * GATE_BLOCK, GATE_BLOCK)


def _exchange_shape(x, gather):
    return jax.ShapeDtypeStruct((N_DEV,) + x.shape if gather else x.shape, x.dtype)


EXCHANGE_SEMS = [pltpu.SemaphoreType.DMA((N_DEV - 1,)), pltpu.SemaphoreType.DMA((N_DEV - 1,)),
                 pltpu.SemaphoreType.DMA(())]


def _exchange_ops(x_ref, o_ref, send_sems, recv_sems, local_sem, gather):
    mx, my, mc = lax.axis_index("x"), lax.axis_index("y"), lax.axis_index("c")
    me = 4 * mx + 2 * my + mc

    def src(p):
        return x_ref if gather else x_ref.at[p]

    local = pltpu.make_async_copy(src(me), o_ref.at[me], local_sem)
    sends, recvs = [], []
    for k in range(1, N_DEV):
        px = 1 - mx if k & 4 else mx
        py = 1 - my if k & 2 else my
        pc = 1 - mc if k & 1 else mc
        peer = 4 * px + 2 * py + pc
        sends.append(pltpu.make_async_remote_copy(
            src_ref=src(peer), dst_ref=o_ref.at[me], send_sem=send_sems.at[k - 1],
            recv_sem=recv_sems.at[k - 1], device_id=(px, py, pc), device_id_type=pl.DeviceIdType.MESH))
        recvs.append(pltpu.make_async_remote_copy(
            src_ref=src(peer), dst_ref=o_ref.at[peer], send_sem=send_sems.at[k - 1],
            recv_sem=recv_sems.at[k - 1], device_id=(px, py, pc), device_id_type=pl.DeviceIdType.MESH))

    def start():
        local.start()
        for cp in sends:
            cp.start()

    def finish():
        for cp in recvs:
            cp.wait_recv()
        for cp in sends:
            cp.wait_send()
        local.wait()

    return start, finish


def _exchange(x, name, gather):
    def body(x_ref, o_ref, send_sems, recv_sems, local_sem):
        start, finish = _exchange_ops(x_ref, o_ref, send_sems, recv_sems, local_sem, gather)
        start()
        finish()

    hbm = pl.BlockSpec(memory_space=pltpu.HBM)
    return _call(body, name=name, in_specs=[hbm], out_specs=hbm, out_shape=_exchange_shape(x, gather),
                 scratch_shapes=EXCHANGE_SEMS)(x)


def _call_with_side(body, side, *, name, grid, in_specs, args, out_specs, out_shape, scratch_shapes, dims):
    if side is None:
        return _call(body, name=name, grid=grid, in_specs=in_specs, out_specs=out_specs, out_shape=out_shape,
                     scratch_shapes=scratch_shapes, compiler_params=_params(dims))(*args)
    x, gather = side
    n_in, n_out, n_scr = len(in_specs), len(out_specs), len(scratch_shapes)

    def wrapped(*refs):
        ins, x_ref = refs[:n_in], refs[n_in]
        outs, o_ref = refs[n_in + 1:n_in + 1 + n_out], refs[n_in + 1 + n_out]
        scr = refs[n_in + 2 + n_out:n_in + 2 + n_out + n_scr]
        start, finish = _exchange_ops(x_ref, o_ref, *refs[n_in + 2 + n_out + n_scr:], gather)
        ids = [pl.program_id(a) for a in range(len(grid))]
        first = functools.reduce(jnp.logical_and, [i == 0 for i in ids])
        last = functools.reduce(jnp.logical_and, [i == g - 1 for i, g in zip(ids, grid)])
        pl.when(first)(start)
        body(*ins, *outs, *scr)
        pl.when(last)(finish)

    hbm = pl.BlockSpec(memory_space=pltpu.HBM)
    return _call(wrapped, name=name + "_xchg", grid=grid, in_specs=list(in_specs) + [hbm],
                 out_specs=list(out_specs) + [hbm], out_shape=list(out_shape) + [_exchange_shape(x, gather)],
                 scratch_shapes=list(scratch_shapes) + EXCHANGE_SEMS,
                 compiler_params=_params(("arbitrary",) * len(grid)))(*args, x)


def _adamw(gs, w, m, v, name):
    n, R, C = gs.shape
    tr = _pick(R, 256, 16)

    def body(g_ref, w_ref, m_ref, v_ref, go_ref, d_ref, mo_ref, vo_ref):
        g = g_ref[0].astype(F32)
        for p in range(1, n):
            g = g + g_ref[p].astype(F32)
        m2 = ADAM_B1 * m_ref[...] + (1.0 - ADAM_B1) * g
        v2 = ADAM_B2 * v_ref[...] + (1.0 - ADAM_B2) * (g * g)
        m_hat = m2 / (1.0 - ADAM_B1 ** ADAM_STEP)
        v_hat = v2 / (1.0 - ADAM_B2 ** ADAM_STEP)
        go_ref[...] = g
        d_ref[...] = -ADAM_LR * (m_hat / (jnp.sqrt(v_hat) + ADAM_EPS) + ADAM_WD * w_ref[...])
        mo_ref[...] = m2
        vo_ref[...] = v2

    spec = pl.BlockSpec((tr, C), lambda i: (i, 0))
    return _call(body, name=name, grid=(R // tr,),
                 in_specs=[pl.BlockSpec((n, tr, C), lambda i: (0, i, 0)), spec, spec, spec],
                 out_specs=[spec] * 4, out_shape=[jax.ShapeDtypeStruct((R, C), F32)] * 4,
                 compiler_params=_params(("parallel",)))(gs, w, m, v)


def _sum_slots(gs, name):
    n, R, C = gs.shape
    tr = _pick(R, 256, 16)

    def body(g_ref, o_ref):
        g = g_ref[0].astype(F32)
        for p in range(1, n):
            g = g + g_ref[p].astype(F32)
        o_ref[...] = g

    return _call(body, name=name, grid=(R // tr,), in_specs=[pl.BlockSpec((n, tr, C), lambda i: (0, i, 0))],
                 out_specs=pl.BlockSpec((tr, C), lambda i: (i, 0)), out_shape=jax.ShapeDtypeStruct((R, C), F32),
                 compiler_params=_params(("parallel",)))(gs)


def _gather_op(name):
    @jax.custom_vjp
    def op(pack):
        return _exchange(pack.astype(BF16), name, True)

    def fwd(pack):
        return op(pack), None

    def bwd(_, ct):
        return (_sum_slots(_exchange(ct, name + "_transpose", False), "sum_grads"),)

    op.defvjp(fwd, bwd)
    return op


def _rope_tables(S, L):
    P = HEAD_DIM // 4
    rows = S // GRID_W
    row_id = jnp.repeat(jnp.arange(rows), GRID_W)
    col_id = jnp.tile(jnp.arange(GRID_W), rows)
    inv = ROPE_THETA ** (-jnp.arange(P, dtype=F32) / P)
    ar, ac = row_id[:, None] * inv, col_id[:, None] * inv
    cos = jnp.concatenate([jnp.cos(ar), jnp.cos(ar), jnp.cos(ac), jnp.cos(ac)], axis=1)
    sin = jnp.concatenate([-jnp.sin(ar), jnp.sin(ar), -jnp.sin(ac), jnp.sin(ac)], axis=1)
    cos = jnp.concatenate([cos, jnp.ones((L, HEAD_DIM), F32)], axis=0)
    sin = jnp.concatenate([sin, jnp.zeros((L, HEAD_DIM), F32)], axis=0)
    return cos, sin


SQ_TILES = (((2176, 1024, 1024), (2176, 1024, 1024), (1024, 1024, 1088)),) * DEPTH


def _layer_shard_shapes(D, IN):
    return (("w_in", (D, IN // N_DEV), 1), ("w_mod", (D, 3 * D // N_DEV), 1),
            ("w_branch", (3, D // N_DEV, D), 1), ("w_out", (D // N_DEV, D), 0))


def _pack_layer(w, l, D):
    return jnp.concatenate([w[n][l].reshape(-1) for n in BIG_PACK]).reshape(-1, D)


def _unpack_shards(pack, D, IN):
    flat, out, off = pack.reshape(-1), {}, 0
    for n, shape, _ in _layer_shard_shapes(D, IN):
        sz = shape[0] * shape[1] * (shape[2] if len(shape) > 2 else 1)
        out[n] = flat[off:off + sz].reshape(shape)
        off += sz
    return out


def _unpack_gathered(g, D, IN):
    flat, out, off = g.reshape(N_DEV, -1), {}, 0
    for n, shape, axis in _layer_shard_shapes(D, IN):
        sz = shape[0] * shape[1] * (shape[2] if len(shape) > 2 else 1)
        out[n] = _unshard(flat[:, off:off + sz].reshape((N_DEV,) + shape), axis)
        off += sz
    return out


def _loss_fn(packs, p, x, c, ctx, target):
    B, S, D = x.shape
    L = ctx.shape[1]
    T = S + L
    tm = min(L, 256)
    KVW = D // KV_GROUP
    widths = (D, D, D, 2 * KVW, D, D, 2 * KVW, D, 3 * D)
    part_dtypes = (F32,) + (BF16,) * 8
    IN = sum(widths)
    cos, sin = _rope_tables(S, L)

    X = jnp.concatenate([x, ctx], axis=1)
    sc, scc = jax.nn.silu(c), jax.nn.silu(p["c_ctx"])
    A = jnp.concatenate([scc[None], sc, jnp.zeros((SUBLANE - 1 - B, D), F32)], axis=0)
    gathered = _gather_op("gather_layer0")(packs[0])
    for l in range(DEPTH):
        big = _unpack_gathered(gathered, D, IN)
        mod = _matmul("mm_mod")(A, big["w_mod"]) + p["b_mod"][l]
        modp = jnp.stack([jnp.broadcast_to(mod[0], (B, 3 * D)), mod[1:1 + B]], axis=1)[:, :, None, :]
        (h,) = _rowwise(_prenorm_f(D), "prenorm", [(D, BF16)], tm, S, 1, 1, 1)(X, modp, p["norm_g"][l][None])
        parts = _in_proj(widths, part_dtypes)(h.reshape(B * T, D), big["w_in"])
        uA, gA, qB, kvB, gB, qC, kvC, gC, m3 = [t.reshape(B, T, -1) for t in parts]
        u = _conv_op(S)(uA, p["conv_w"][l], p["conv_b"][l][None])
        nsp = -LRU_C * jax.nn.softplus(-p["lru_lambda"][l])
        pv = jnp.concatenate([p["lru_ba"][l], p["lru_bx"][l], nsp, jnp.zeros((2, D), F32)], axis=0)
        wm = jnp.concatenate([_gate_blocks(p["lru_wa"][l]), _gate_blocks(p["lru_wx"][l])], axis=0)
        wm = wm.reshape(-1, GATE_BLOCK)
        af, bf, ar, br = _rowwise(_coef_f(D), "lru_coef", [(D, F32)] * 4, tm, S, 1, 0, 2)(u, pv, wm)
        hf = _scan_op(S, False)(af, bf)
        hr = _scan_op(S, True)(ar, br)
        (zA,) = _rowwise(_gate2_f, "gate_a", [(D, BF16)], tm, S, 3, 0, 0)(hf, hr, gA)
        zB = _attn_branch(S, tm, True, False, True)(qB, kvB, gB, p["attn_sink"][l], cos, sin)
        qkv_c = (qC, kvC, gC, p["q_norm_g"][l][None], p["k_norm_g"][l][None])
        if l + 1 < DEPTH:
            zC, gathered = _attn_branch(S, tm, False, True, False, True)(*qkv_c, packs[l + 1], cos, sin)
        else:
            zC = _attn_branch(S, tm, False, True, False)(*qkv_c, cos, sin)
        pr = [_matmul("mm_branch", *SQ_TILES[l], out_dtype=BF16)(z.reshape(B * T, D), big["w_branch"][n])
              .reshape(B, T, D) for n, z in enumerate((zA, zB, zC))]
        (mg,) = _rowwise(_merge_f(D), "merge", [(D, BF16)], tm, S, 4, 0, 0)(m3, *pr)
        y = _matmul("mm_out", *SQ_TILES[l], out_dtype=BF16)(mg.reshape(B * T, D), big["w_out"]).reshape(B, T, D)
        (X,) = _rowwise(_resid_f(D), "resid", [(D, F32)], tm, S, 2, 1, 0)(X, y, modp)
    return _final_loss(S, tm)(X, target, p["final_g"][None])


def _shard_axis(name):
    return {"w_mod": 2, "w_in": 2, "conv_w": 2, "lru_ba": 2, "lru_bx": 2, "lru_lambda": 2,
            "w_branch": 2, "w_out": 1}.get(name)


def _unshard(g, axis):
    full = jnp.moveaxis(g, 0, axis)
    shape = list(full.shape)
    shape[axis:axis + 2] = [shape[axis] * shape[axis + 1]]
    return full.reshape(shape)


def _reshard(full, axis):
    shape = list(full.shape)
    shape[axis:axis + 1] = [N_DEV, shape[axis] // N_DEV]
    return jnp.moveaxis(full.reshape(shape), axis, 0)


def _pad_to(v, n):
    return jnp.concatenate([v, jnp.zeros((n - v.shape[0],), v.dtype)]) if n > v.shape[0] else v


def _step(x, c, ctx, target, w, m, v):
    D = x.shape[2]
    IN = w["w_in"].shape[2] * N_DEV
    full = {n: w[n] for n in REPLICATED}
    small_local = jnp.concatenate([w[n].reshape(-1) for n in SMALL_SHARDED])
    small_all = _exchange(small_local.reshape(-1, LANE), "gather_small", True).reshape(N_DEV, -1)
    off = 0
    for n in SMALL_SHARDED:
        sz = w[n].size
        full[n] = _unshard(small_all[:, off:off + sz].reshape((N_DEV,) + w[n].shape), _shard_axis(n))
        off += sz
    packs = [_pack_layer(w, l, D) for l in range(DEPTH)]

    loss, (gpacks, gp, gx) = jax.value_and_grad(_loss_fn, argnums=(0, 1, 2))(packs, full, x, c, ctx, target)
    loss = lax.psum(loss, AXES)

    out = {}
    gshards = [_unpack_shards(g, D, IN) for g in gpacks]
    for n in BIG_PACK:
        g = jnp.stack([gs[n] for gs in gshards])
        C = w[n].shape[-1]
        res = _adamw(g.reshape(1, -1, C), w[n].reshape(-1, C), m[n].reshape(-1, C), v[n].reshape(-1, C),
                     "adamw_" + n)
        out[n] = [r.reshape(w[n].shape) for r in res]

    rep = jnp.concatenate([gp[n].reshape(-1) for n in REPLICATED])
    n_rep = rep.shape[0]
    chunk = -(-n_rep // (N_DEV * LANE)) * LANE
    rep = _pad_to(rep, N_DEV * chunk).reshape(N_DEV, chunk)
    shards = jnp.concatenate([_reshard(gp[n], _shard_axis(n)).reshape(N_DEV, -1) for n in SMALL_SHARDED], axis=1)
    n_sh = shards.shape[1]
    recv = _exchange(jnp.concatenate([rep, shards], axis=1).reshape(N_DEV, -1, LANE), "scatter_small", False)
    wl =jnp.concatenate([w[n].reshape(-1) for n in SMALL_SHARDED])
    ml = jnp.concatenate([m[n].reshape(-1) for n in SMALL_SHARDED])
    vl = jnp.concatenate([v[n].reshape(-1) for n in SMALL_SHARDED])
    rows = recv.shape[1]
    rrows = chunk // LANE
    g_sh, d_sh, m_sh, v_sh = _adamw(recv[:, rrows:], wl.reshape(-1, LANE), ml.reshape(-1, LANE),
                                    vl.reshape(-1, LANE), "adamw_small_sharded")
    g_rep8 = _sum_slots(recv[:, :rrows], "sum_replicated")
    g_rep = _exchange(g_rep8, "gather_replicated", True).reshape(-1)
    wr = _pad_to(jnp.concatenate([w[n].reshape(-1) for n in REPLICATED]), N_DEV * chunk)
    mr = _pad_to(jnp.concatenate([m[n].reshape(-1) for n in REPLICATED]), N_DEV * chunk)
    vr = _pad_to(jnp.concatenate([v[n].reshape(-1) for n in REPLICATED]), N_DEV * chunk)
    res_rep = _adamw(g_rep.reshape(1, -1, LANE), wr.reshape(-1, LANE), mr.reshape(-1, LANE), vr.reshape(-1, LANE),
                     "adamw_replicated")
    off = 0
    for n in REPLICATED:
        sz = w[n].size
        out[n] = [r.reshape(-1)[off:off + sz].reshape(w[n].shape) for r in res_rep]
        off += sz
    off = 0
    for n in SMALL_SHARDED:
        sz = w[n].size
        out[n] = [r.reshape(-1)[off:off + sz].reshape(w[n].shape) for r in (g_sh, d_sh, m_sh, v_sh)]
        off += sz
    assert off == n_sh and rows == rrows + n_sh // LANE
    return (loss, gx, *[out[n][0] for n in WEIGHTS], *[out[n][1] for n in WEIGHTS],
            *[out[n][2] for n in WEIGHTS], *[out[n][3] for n in WEIGHTS])


def kernel(x, c, ctx, c_ctx, norm_g, w_mod, b_mod, w_in, conv_w, conv_b, lru_wa, lru_ba, lru_wx, lru_bx, lru_lambda, attn_sink, q_norm_g, k_norm_g, w_branch, w_out, final_g, loss_target, m_c_ctx, m_norm_g, m_w_mod, m_b_mod, m_w_in, m_conv_w, m_conv_b, m_lru_wa, m_lru_ba, m_lru_wx, m_lru_bx, m_lru_lambda, m_attn_sink, m_q_norm_g, m_k_norm_g, m_w_branch, m_w_out, m_final_g, v_c_ctx, v_norm_g, v_w_mod, v_b_mod, v_w_in, v_conv_w, v_conv_b, v_lru_wa, v_lru_ba, v_lru_wx, v_lru_bx, v_lru_lambda, v_attn_sink, v_q_norm_g, v_k_norm_g, v_w_branch, v_w_out, v_final_g):
    w = dict(zip(WEIGHTS, (c_ctx, norm_g, w_mod, b_mod, w_in, conv_w, conv_b, lru_wa, lru_ba, lru_wx, lru_bx,
                           lru_lambda, attn_sink, q_norm_g, k_norm_g, w_branch, w_out, final_g)))
    m = dict(zip(WEIGHTS, (m_c_ctx, m_norm_g, m_w_mod, m_b_mod, m_w_in, m_conv_w, m_conv_b, m_lru_wa, m_lru_ba,
                           m_lru_wx, m_lru_bx, m_lru_lambda, m_attn_sink, m_q_norm_g, m_k_norm_g, m_w_branch,
                           m_w_out, m_final_g)))
    v = dict(zip(WEIGHTS, (v_c_ctx, v_norm_g, v_w_mod, v_b_mod, v_w_in, v_conv_w, v_conv_b, v_lru_wa, v_lru_ba,
                           v_lru_wx, v_lru_bx, v_lru_lambda, v_attn_sink, v_q_norm_g, v_k_norm_g, v_w_branch,
                           v_w_out, v_final_g)))
    return _step(x, c, ctx, loss_target, w, m, v)
```

```python
import functools
import math

import jax
import jax.numpy as jnp
from jax import lax
from jax.experimental import pallas as pl
from jax.experimental.pallas import tpu as pltpu

F32 = jnp.float32
BF16 = jnp.bfloat16

AXES = ("x", "y", "c")
N_DEV = 8
DEPTH = 4
HEAD_DIM = 128
GRID_W = 64
WINDOW = 128
LRU_BLOCK_W = 64
GATE_BLOCK = 256
LRU_C = 8.0
ROPE_THETA = 10000.0
EPS = 1e-6
NEG_INF = -1e30
KV_GROUP = 4
LANE = 128
SUBLANE = 8
VMEM_LIMIT = 56 * 1024 * 1024

ADAM_LR = 0.001
ADAM_B1 = 0.9
ADAM_B2 = 0.999
ADAM_EPS = 1e-08
ADAM_WD = 0.01
ADAM_STEP = 10

WEIGHTS = ("c_ctx", "norm_g", "w_mod", "b_mod", "w_in", "conv_w", "conv_b", "lru_wa", "lru_ba", "lru_wx",
           "lru_bx", "lru_lambda", "attn_sink", "q_norm_g", "k_norm_g", "w_branch", "w_out", "final_g")
BIG_PACK = ("w_in", "w_mod", "w_branch", "w_out")
SMALL_SHARDED = ("conv_w", "lru_ba", "lru_bx", "lru_lambda")
REPLICATED = ("c_ctx", "norm_g", "b_mod", "conv_b", "lru_wa", "lru_wx", "attn_sink", "q_norm_g", "k_norm_g",
              "final_g")


def _call(body, **kw):
    return pl.pallas_call(body, **kw)


def _params(dims=None, vmem=VMEM_LIMIT):
    return pltpu.CompilerParams(dimension_semantics=dims, vmem_limit_bytes=vmem)


def _pick(n, target, mult):
    for t in range(min(n, target), 0, -1):
        if n % t == 0 and t % mult == 0:
            return t
    return n


def _mm(a, b, *, name, ta=False, tb=False, out_dtype=F32, tm=512, tn=1024, tk=1024):
    M, K = (a.shape[1], a.shape[0]) if ta else a.shape
    N = b.shape[0] if tb else b.shape[1]
    assert (b.shape[1] if tb else b.shape[0]) == K
    tm = _pick(M, tm, LANE if ta else 16)
    tn = _pick(N, tn, LANE)
    tk = _pick(K, tk, 16 if ta and not tb else LANE)
    nk = K // tk
    dn = (((0 if ta else 1,), (1 if tb else 0,)), ((), ()))

    def body(a_ref, b_ref, o_ref, *acc):
        r = lax.dot_general(a_ref[...].astype(BF16), b_ref[...].astype(BF16), dn,
                            preferred_element_type=F32)
        if nk == 1:
            o_ref[...] = r.astype(o_ref.dtype)
        else:
            k = pl.program_id(2)

            @pl.when(k == 0)
            def _():
                acc[0][...] = r

            @pl.when(k > 0)
            def _():
                acc[0][...] += r

            @pl.when(k == nk - 1)
            def _():
                o_ref[...] = acc[0][...].astype(o_ref.dtype)

    a_spec = (pl.BlockSpec((tk, tm), lambda i, j, k: (k, i)) if ta
              else pl.BlockSpec((tm, tk), lambda i, j, k: (i, k)))
    b_spec = (pl.BlockSpec((tn, tk), lambda i, j, k: (j, k)) if tb
              else pl.BlockSpec((tk, tn), lambda i, j, k: (k, j)))
    return _call(
        body, name=name, grid=(M // tm, N // tn, nk),
        in_specs=[a_spec, b_spec],
        out_specs=pl.BlockSpec((tm, tn), lambda i, j, k: (i, j)),
        out_shape=jax.ShapeDtypeStruct((M, N), out_dtype),
        scratch_shapes=[pltpu.VMEM((tm, tn), F32)] if nk > 1 else [],
        compiler_params=_params(("parallel", "parallel", "arbitrary")),
    )(a, b)


def _matmul(name, t_fwd=(512, 1024, 1024), t_da=(512, 1024, 1024), t_dw=(1024, 1024, 512), out_dtype=F32):
    def tiles(t):
        return dict(tm=t[0], tn=t[1], tk=t[2])

    @jax.custom_vjp
    def f(a, w):
        return _mm(a, w, name=name + "_fwd", out_dtype=out_dtype, **tiles(t_fwd))

    def fwd(a, w):
        return f(a, w), (a, w)

    def bwd(res, g):
        a, w = res
        da = _mm(g, w, name=name + "_da", tb=True, out_dtype=a.dtype, **tiles(t_da))
        dw = _mm(a, g, name=name + "_dw", ta=True, out_dtype=w.dtype, **tiles(t_dw))
        return da, dw

    f.defvjp(fwd, bwd)
    return f


def _mm_parts_nt(parts, w, col0, init, out_dtype, *, name, tm=1088, tk=512):
    M = parts[0].shape[0]
    D = w.shape[0]
    tm = _pick(M, tm, 16)
    tk = math.gcd(tk, *[a.shape[1] for a in parts])
    starts, n = [], 0
    for a in parts:
        assert a.shape[1] % tk == 0
        starts.append(n)
        n += a.shape[1] // tk
    k0 = col0 // tk
    has_init = init is not None

    def body(*refs):
        a_refs, w_ref = refs[:len(parts)], refs[len(parts)]
        o_ref, acc = refs[-2], refs[-1]
        k = pl.program_id(1)

        @pl.when(k == 0)
        def _():
            acc[...] = refs[len(parts) + 1][...].astype(F32) if has_init else jnp.zeros_like(acc)

        for a_ref, s, a in zip(a_refs, starts, parts):
            @pl.when(jnp.logical_and(k >= s, k < s + a.shape[1] // tk))
            def _(a_ref=a_ref):
                acc[...] += lax.dot_general(a_ref[...].astype(BF16), w_ref[...], NT, preferred_element_type=F32)

        @pl.when(k == n - 1)
        def _():
            o_ref[...] = acc[...].astype(o_ref.dtype)

    in_specs = [pl.BlockSpec((tm, tk), lambda i, k, s=s, c=a.shape[1] // tk: (i, jnp.clip(k - s, 0, c - 1)))
                for a, s in zip(parts, starts)]
    in_specs.append(pl.BlockSpec((D, tk), lambda i, k: (0, k0 + k)))
    o_spec = pl.BlockSpec((tm, D), lambda i, k: (i, 0))
    args = list(parts) + [w]
    if has_init:
        in_specs.append(o_spec)
        args.append(init)
    return _call(body, name=name, grid=(M // tm, n), in_specs=in_specs, out_specs=o_spec,
                 out_shape=jax.ShapeDtypeStruct((M, D), out_dtype),
                 scratch_shapes=[pltpu.VMEM((tm, D), F32)],
                 compiler_params=_params(("parallel", "arbitrary")))(*args)


def _in_proj(widths, dtypes, t_fwd=(2176, 1024, 1024), t_dw=(1024, 1024, 2176), group=5):
    offs = [0]
    for wd in widths:
        offs.append(offs[-1] + wd)

    def cols(w, i):
        return w[:, offs[i]:offs[i + 1]]

    @jax.custom_vjp
    def op(h, w):
        return tuple(_mm(h, cols(w, i), name="mm_in_fwd", out_dtype=dtypes[i], tm=t_fwd[0], tn=t_fwd[1],
                         tk=t_fwd[2]) for i in range(len(widths)))

    def fwd(h, w):
        return op(h, w), (h, w)

    def bwd(res, gs):
        h, w = res
        da = None
        for i0 in range(0, len(widths), group):
            last = i0 + group >= len(widths)
            da = _mm_parts_nt(list(gs[i0:i0 + group]), w, offs[i0], da, h.dtype if last else F32, name="mm_in_da")
        dw = jnp.concatenate([_mm(h, g, name="mm_in_dw", ta=True, out_dtype=w.dtype, tm=t_dw[0], tn=t_dw[1],
                                  tk=t_dw[2]) for g in gs], axis=1)
        return da, dw

    op.defvjp(fwd, bwd)
    return op


def _row_specs(xs, tps, gps, rts, tm, n_lat_tiles):
    x_specs = [pl.BlockSpec((None, tm, x.shape[2]), lambda b, i: (b, i, 0)) for x in xs]
    tp_specs = [pl.BlockSpec((None, None, 1, p.shape[3]),
                             lambda b, i: (b, (i < n_lat_tiles).astype(jnp.int32), 0, 0)) for p in tps]
    gp_specs = [pl.BlockSpec(p.shape, lambda b, i: (0, 0)) for p in gps]
    rt_specs = [pl.BlockSpec((tm, t.shape[1]), lambda b, i: (i, 0)) for t in rts]
    return x_specs, tp_specs, gp_specs, rt_specs


def _row_fwd(f, name, xs, tps, gps, rts, outs, tm, n_lat):
    B, T, _ = xs[0].shape
    n_in = len(xs) + len(tps) + len(gps) + len(rts)

    def body(*refs):
        vals = f(*[r[...] for r in refs[:n_in]])
        for o, v in zip(refs[n_in:], vals):
            o[...] = v.astype(o.dtype)

    x_specs, tp_specs, gp_specs, rt_specs = _row_specs(xs, tps, gps, rts, tm, n_lat // tm)
    res = _call(
        body, name=name, grid=(B, T // tm),
        in_specs=x_specs + tp_specs + gp_specs + rt_specs,
        out_specs=[pl.BlockSpec((None, tm, w), lambda b, i: (b, i, 0)) for w, _ in outs],
        out_shape=[jax.ShapeDtypeStruct((B, T, w), dt) for w, dt in outs],
        compiler_params=_params(("parallel", "parallel")),
    )(*xs, *tps, *gps, *rts)
    return list(res)


def _row_bwd(f, name, xs, tps, gps, rts, douts, tm, n_lat):
    B, T, _ = xs[0].shape
    nx, ntp, ngp, nd = len(xs), len(tps), len(gps), len(douts)
    n_lat_tiles = n_lat // tm
    n_diff = nx + ntp + ngp
    n_in = n_diff + len(rts)

    def body(*refs):
        diff = [r[...] for r in refs[:n_diff]]
        tabs = [r[...] for r in refs[n_diff:n_in]]
        dos = [r[...] for r in refs[n_in:n_in + nd]]
        o_refs = refs[n_in + nd:]
        prim, vjp = jax.vjp(lambda *d: tuple(f(*d, *tabs)), *diff)
        grads = vjp(tuple(d.astype(p.dtype) for d, p in zip(dos, prim)))
        b, i = pl.program_id(0), pl.program_id(1)
        for k in range(nx):
            o_refs[k][...] = grads[k].astype(o_refs[k].dtype)
        first_tp = jnp.logical_or(i == 0, i == n_lat_tiles)
        first_gp = jnp.logical_and(b == 0, i == 0)
        for k in range(nx, n_diff):
            first = first_tp if k < nx + ntp else first_gp

            @pl.when(first)
            def _(k=k):
                o_refs[k][...] = grads[k]

            @pl.when(jnp.logical_not(first))
            def _(k=k):
                o_refs[k][...] += grads[k]

    x_specs, tp_specs, gp_specs, rt_specs = _row_specs(xs, tps, gps, rts, tm, n_lat_tiles)
    d_specs = [pl.BlockSpec((None, tm, d.shape[2]), lambda b, i: (b, i, 0)) for d in douts]
    res = _call(
        body, name=name, grid=(B, T // tm),
        in_specs=x_specs + tp_specs + gp_specs + rt_specs + d_specs,
        out_specs=x_specs + tp_specs + gp_specs,
        out_shape=[jax.ShapeDtypeStruct(a.shape, a.dtype) for a in xs]
        + [jax.ShapeDtypeStruct(a.shape, F32) for a in (*tps, *gps)],
        compiler_params=_params(("arbitrary", "arbitrary")),
    )(*xs, *tps, *gps, *rts, *douts)
    res = list(res)
    return res[:nx], res[nx:nx + ntp], res[nx + ntp:]


def _rowwise(f, name, outs, tm, n_lat, n_x, n_tp, n_gp):
    def split(args):
        return (args[:n_x], args[n_x:n_x + n_tp], args[n_x + n_tp:n_x + n_tp + n_gp],
                args[n_x + n_tp + n_gp:])

    @jax.custom_vjp
    def op(*args):
        xs, tps, gps, rts = split(args)
        return tuple(_row_fwd(f, name + "_fwd", xs, tps, gps, rts, outs, tm, n_lat))

    def fwd(*args):
        return op(*args), args

    def bwd(args, g):
        xs, tps, gps, rts = split(args)
        dxs, dtps, dgps = _row_bwd(f, name + "_bwd", xs, tps, gps, rts, list(g), tm, n_lat)
        return (*dxs, *dtps, *dgps, *[jnp.zeros_like(t) for t in rts])

    op.defvjp(fwd, bwd)
    return op


def _shift_impl(u, k):
    n = u.shape[0]
    r = pltpu.roll(u, k % n, axis=0)
    row = lax.broadcasted_iota(jnp.int32, u.shape, 0)
    valid = (row >= k) if k > 0 else (row < n + k)
    return jnp.where(valid, r, 0.0)


@functools.partial(jax.custom_vjp, nondiff_argnums=(1,))
def _shift(u, k):
    return _shift_impl(u, k)


_shift.defvjp(lambda u, k: (_shift_impl(u, k), None), lambda k, _, g: (_shift_impl(g, -k),))


def _swap_impl(x):
    lane = lax.broadcasted_iota(jnp.int32, x.shape, 1)
    q = HEAD_DIM // 4
    return jnp.where((lane % (2 * q)) < q, pltpu.roll(x, HEAD_DIM - q, axis=1), pltpu.roll(x, q, axis=1))


@jax.custom_vjp
def _swap(x):
    return _swap_impl(x)


_swap.defvjp(lambda x: (_swap_impl(x), None), lambda _, g: (_swap_impl(g),))


def _conv_f(ul, uc, cw, cb):
    def conv(u):
        return (_shift(u, 2) * cw[0:1] + _shift(u, 1) * cw[1:2] + u * cw[2:3] + _shift(u, -1) * cw[3:4] + cb)
    return conv(ul), conv(uc)


def _conv_specs(B, T, D):
    u_spec = pl.BlockSpec((None, T, LANE), lambda j, b: (b, 0, j))
    cw_spec = pl.BlockSpec((4, LANE), lambda j, b: (0, j))
    cb_spec = pl.BlockSpec((1, LANE), lambda j, b: (0, j))
    return u_spec, cw_spec, cb_spec


def _conv_fwd(u, cw, cb, S):
    B, T, D = u.shape

    def body(u_ref, cw_ref, cb_ref, o_ref):
        vl, vc = _conv_f(u_ref[0:S, :], u_ref[S:T, :], cw_ref[...], cb_ref[...])
        o_ref[0:S, :] = vl
        o_ref[S:T, :] = vc

    u_spec, cw_spec, cb_spec = _conv_specs(B, T, D)
    return _call(body, name="conv_fwd", grid=(D // LANE, B), in_specs=[u_spec, cw_spec, cb_spec],
                 out_specs=u_spec, out_shape=jax.ShapeDtypeStruct(u.shape, F32),
                 compiler_params=_params(("parallel", "parallel")))(u, cw, cb)


def _conv_bwd(u, cw, cb, dv, S):
    B, T, D = u.shape

    def body(u_ref, cw_ref, cb_ref, dv_ref, du_ref, dcw_ref, dcb_ref):
        _, vjp = jax.vjp(_conv_f, u_ref[0:S, :], u_ref[S:T, :], cw_ref[...], cb_ref[...])
        dul, duc, dcw, dcb = vjp((dv_ref[0:S, :], dv_ref[S:T, :]))
        du_ref[0:S, :] = dul
        du_ref[S:T, :] = duc
        first = pl.program_id(1) == 0

        @pl.when(first)
        def _():
            dcw_ref[...] = dcw
            dcb_ref[...] = dcb

        @pl.when(jnp.logical_not(first))
        def _():
            dcw_ref[...] += dcw
            dcb_ref[...] += dcb

    u_spec, cw_spec, cb_spec = _conv_specs(B, T, D)
    return _call(body, name="conv_bwd", grid=(D // LANE, B), in_specs=[u_spec, cw_spec, cb_spec, u_spec],
                 out_specs=[u_spec, cw_spec, cb_spec],
                 out_shape=[jax.ShapeDtypeStruct(u.shape, F32), jax.ShapeDtypeStruct(cw.shape, F32),
                            jax.ShapeDtypeStruct(cb.shape, F32)],
                 compiler_params=_params(("parallel", "arbitrary")))(u, cw, cb, dv)


def _conv_op(S):
    @jax.custom_vjp
    def op(u, cw, cb):
        return _conv_fwd(u, cw, cb, S)

    def fwd(u, cw, cb):
        return op(u, cw, cb), (u, cw, cb)

    def bwd(res, g):
        u, cw, cb = res
        return tuple(_conv_bwd(u, cw, cb, g, S))

    op.defvjp(fwd, bwd)
    return op


SCAN_UNROLL = 4


def _group_scan(A, Bv, asc):
    row = lax.broadcasted_iota(jnp.int32, A.shape, 0)
    for s in (1, 2, 4):
        sh = s if asc else SUBLANE - s
        valid = (row >= s) if asc else (row < SUBLANE - s)
        A_sh = pltpu.roll(A, sh, axis=0)
        B_sh = pltpu.roll(Bv, sh, axis=0)
        Bv = jnp.where(valid, A * B_sh, 0.0) + Bv
        A = jnp.where(valid, A * A_sh, A)
    return A, Bv


def _chain_step(A, Bv, carry, asc):
    row = lax.broadcasted_iota(jnp.int32, A.shape, 0)
    A2, B2 = _group_scan(A, Bv, asc)
    h = A2 * carry + B2
    if asc:
        prev = jnp.where(row == 0, carry, pltpu.roll(h, 1, axis=0))
        return h, prev, h[SUBLANE - 1:SUBLANE, :]
    prev = jnp.where(row == SUBLANE - 1, carry, pltpu.roll(h, SUBLANE - 1, axis=0))
    return h, prev, h[0:1, :]


def _chain_loop(segments, step):
    carry = jnp.zeros((1, LANE), F32)
    for lo, hi, asc in segments:
        span = SUBLANE * SCAN_UNROLL
        assert (hi - lo) % span == 0

        def it(t, carry, lo=lo, hi=hi, asc=asc, span=span):
            base = lo + t * span if asc else hi - (t + 1) * span
            order = range(SCAN_UNROLL) if asc else reversed(range(SCAN_UNROLL))
            for j in order:
                carry = step(pl.multiple_of(base + SUBLANE * j, SUBLANE), carry, asc)
            return carry

        carry = lax.fori_loop(0, (hi - lo) // span, it, carry)
    return carry


def _scan_specs(T):
    return pl.BlockSpec((None, T, LANE), lambda j, b: (b, 0, j))


def _scan_fwd(a, b, S, reverse):
    B, T, D = a.shape
    asc = not reverse
    segments = [(S, T, asc), (0, S, asc)]

    def body(a_ref, b_ref, h_ref, hp_ref):
        def step(r0, carry, asc):
            rows = pl.ds(r0, SUBLANE)
            h, prev, carry = _chain_step(a_ref[rows, :], b_ref[rows, :], carry, asc)
            h_ref[rows, :] = h
            hp_ref[rows, :] = prev
            return carry
        _chain_loop(segments, step)

    spec = _scan_specs(T)
    return _call(body, name="scan_rev_fwd" if reverse else "scan_fwd_fwd", grid=(D // LANE, B),
                 in_specs=[spec, spec], out_specs=[spec, spec],
                 out_shape=[jax.ShapeDtypeStruct(a.shape, F32)] * 2,
                 compiler_params=_params(("parallel", "parallel")))(a, b)


def _scan_bwd(a, hp, dy, S, reverse):
    B, T, D = a.shape
    asc = reverse
    segments = [(0, S, asc), (S, T, asc)]

    def body(a_ref, hp_ref, dy_ref, da_ref, db_ref):
        def step(r0, carry, asc):
            rows = pl.ds(r0, SUBLANE)
            A, dy = a_ref[rows, :], dy_ref[rows, :]
            _, s_prev, carry = _chain_step(A, A * dy, carry, asc)
            g = dy + s_prev
            db_ref[rows, :] = g
            da_ref[rows, :] = g * hp_ref[rows, :]
            return carry
        _chain_loop(segments, step)

    spec = _scan_specs(T)
    return _call(body, name="scan_rev_bwd" if reverse else "scan_fwd_bwd", grid=(D // LANE, B),
                 in_specs=[spec, spec, spec], out_specs=[spec, spec],
                 out_shape=[jax.ShapeDtypeStruct(a.shape, F32)] * 2,
                 compiler_params=_params(("parallel", "parallel")))(a, hp, dy)


def _scan_op(S, reverse):
    @jax.custom_vjp
    def op(a, b):
        return _scan_fwd(a, b, S, reverse)[0]

    def fwd(a, b):
        h, hp = _scan_fwd(a, b, S, reverse)
        return h, (a, hp)

    def bwd(res, g):
        a, hp = res
        return tuple(_scan_bwd(a, hp, g, S, reverse))

    op.defvjp(fwd, bwd)
    return op


def _band_lo(qi, tq, S):
    span = tq + 2 * WINDOW
    return pl.multiple_of(jnp.clip(qi * tq - WINDOW, 0, S - span), LANE)


def _band_mask(qi, tq, lo, span, transposed, heads=1):
    shape = (span, heads * tq) if transposed else (heads * tq, span)
    assert tq & (tq - 1) == 0
    qpos = qi * tq + jnp.bitwise_and(lax.broadcasted_iota(jnp.int32, shape, 1 if transposed else 0), tq - 1)
    kpos = lo + lax.broadcasted_iota(jnp.int32, shape, 0 if transposed else 1)
    return jnp.abs(kpos - qpos) <= WINDOW


NT = (((1,), (1,)), ((), ()))
LOG2E = 1.4426950408889634
KEY_CHUNK = 1024


def _col_to_row(c):
    return jnp.broadcast_to(c, (c.shape[0], LANE)).T[0:1, :]


def _attn_fwd(q, k, v, sink, gate, S, band, side=None):
    B, T, HD = q.shape
    H = HD // HEAD_DIM
    L = T - S
    tq = L
    n_lq = S // tq
    span = tq + 2 * WINDOW
    c2 = HEAD_DIM ** -0.5 * LOG2E
    has_sink = sink is not None
    kc = min(KEY_CHUNK, S)
    hp = KV_GROUP if band else 1

    def body(*refs):
        if has_sink:
            q_ref, k_ref, v_ref, g_ref, s_ref, z_ref, o_ref, lse_ref = refs
        else:
            q_ref, k_ref, v_ref, g_ref, z_ref, o_ref, lse_ref = refs
        qi = pl.program_id(2)

        def online(qv, sk2, segs):
            def scores(seg):
                s = lax.dot_general(qv, k_ref[seg[0], :], NT, preferred_element_type=F32) * c2
                return s if seg[1] is None else jnp.where(seg[1], s, NEG_INF)

            m = l = acc = None
            s_next = scores(segs[0])
            for j, (rows, _) in enumerate(segs):
                s = s_next
                if j + 1 < len(segs):
                    s_next = scores(segs[j + 1])
                ms = jnp.max(s, axis=-1, keepdims=True)
                if m is None:
                    m = ms if sk2 is None else jnp.maximum(ms, sk2)
                    p = jnp.exp2(s - m)
                    l = jnp.sum(p, axis=-1, keepdims=True)
                    if sk2 is not None:
                        l = l + jnp.exp2(sk2 - m)
                    acc = jnp.dot(p.astype(BF16), v_ref[rows, :], preferred_element_type=F32)
                else:
                    m_new = jnp.maximum(m, ms)
                    alpha = jnp.exp2(m - m_new)
                    p = jnp.exp2(s - m_new)
                    l = alpha * l + jnp.sum(p, axis=-1, keepdims=True)
                    acc = alpha * acc + jnp.dot(p.astype(BF16), v_ref[rows, :], preferred_element_type=F32)
                    m = m_new
            return acc, m, l

        def joint(qv, sk2, segs):
            ss = []
            for rows, mask in segs:
                s = lax.dot_general(qv, k_ref[rows, :], NT, preferred_element_type=F32) * c2
                ss.append(s if mask is None else jnp.where(mask, s, NEG_INF))
            m = functools.reduce(jnp.maximum, [jnp.max(s, axis=-1, keepdims=True) for s in ss])
            if sk2 is not None:
                m = jnp.maximum(m, sk2)
            ps = [jnp.exp2(s - m) for s in ss]
            l = functools.reduce(jnp.add, [jnp.sum(p, axis=-1, keepdims=True) for p in ps])
            if sk2 is not None:
                l = l + jnp.exp2(sk2 - m)
            acc = functools.reduce(jnp.add, [jnp.dot(p.astype(BF16), v_ref[rows, :], preferred_element_type=F32)
                                             for p, (rows, _) in zip(ps, segs)])
            return acc, m, l

        def run(segs, softmax):
            for g in range(hp):
                lanes = slice(g * HEAD_DIM, (g + 1) * HEAD_DIM)
                sk2 = s_ref[g][:, 0:1] * LOG2E if has_sink else None
                acc, m, l = softmax(q_ref[:, lanes], sk2, segs)
                o = acc * (1.0 / l)
                gv = g_ref[:, lanes].astype(F32)
                o_ref[:, lanes] = o.astype(o_ref.dtype)
                z_ref[:, lanes] = (o * (gv * jax.nn.sigmoid(gv))).astype(z_ref.dtype)
                lse_ref[g] = _col_to_row(m + jnp.log2(l))

        ctx_rows = pl.ds(S, L)

        @pl.when(qi < n_lq)
        def _():
            if band:
                lo = _band_lo(qi, tq, S)
                run([(ctx_rows, None), (pl.ds(lo, span), _band_mask(qi, tq, lo, span, False))], joint)
            else:
                run([(pl.ds(j * kc, kc), None) for j in range(S // kc)] + [(ctx_rows, None)], online)

        @pl.when(qi >= n_lq)
        def _():
            run([(ctx_rows, None)], joint)

    q_spec = pl.BlockSpec((None, tq, hp * HEAD_DIM), lambda b, h, i: (b, i, h))
    kv_spec = pl.BlockSpec((None, T, HEAD_DIM), lambda b, h, i: (b, 0, h * hp // KV_GROUP))
    in_specs = [q_spec, kv_spec, kv_spec, q_spec]
    args = [q, k, v, gate]
    if has_sink:
        in_specs.append(pl.BlockSpec((hp, 1, LANE), lambda b, h, i: (h, 0, 0)))
        args.append(sink)
    name = "attn_band_fwd" if band else "attn_dense_fwd"
    return _call_with_side(
        body, side, name=name, grid=(B, H // hp, T // tq), in_specs=in_specs, args=args,
        out_specs=[q_spec, q_spec, pl.BlockSpec((None, hp, 1, tq), lambda b, h, i: (b, h, 0, i))],
        out_shape=[jax.ShapeDtypeStruct(q.shape, BF16), jax.ShapeDtypeStruct(q.shape, BF16),
                   jax.ShapeDtypeStruct((B, H, 1, T), F32)],
        scratch_shapes=[], dims=("parallel", "parallel", "parallel"))


def _attn_bwd(q, k, v, sink, gate, dz, o, lse, S, band, side=None):
    B, T, HD = q.shape
    H = HD // HEAD_DIM
    KVH = H // KV_GROUP
    L = T - S
    tq = L
    n_lq = S // tq
    span = tq + 2 * WINDOW
    scale = HEAD_DIM ** -0.5
    c2 = scale * LOG2E
    has_sink = sink is not None
    kc = min(KEY_CHUNK, S)
    hp = KV_GROUP if band else 1

    def body(*refs):
        if has_sink:
            (q_ref, k_ref, v_ref, g_ref, dz_ref, o_ref, lse_ref, s_ref,
             dq_ref, dk_ref, dv_ref, dg_ref, ds_ref, *scr) = refs
        else:
            q_ref, k_ref, v_ref, g_ref, dz_ref, o_ref, lse_ref, dq_ref, dk_ref, dv_ref, dg_ref, *scr = refs
        g, qi = pl.program_id(2), pl.program_id(3)

        @pl.when(jnp.logical_and(g == 0, qi == 0))
        def _():
            dk_ref[...] = jnp.zeros_like(dk_ref)
            dv_ref[...] = jnp.zeros_like(dv_ref)
            if not band:
                scr[0][...] = k_ref[...].astype(F32).T.astype(BF16)

        def run(segs):
            lanes = [slice(h * HEAD_DIM, (h + 1) * HEAD_DIM) for h in range(hp)]

            def stack(ref, axis=0):
                return jnp.concatenate([ref[:, ln] for ln in lanes], axis=axis) if hp > 1 else ref[...]

            qv, gv, dzv = stack(q_ref), stack(g_ref).astype(F32), stack(dz_ref).astype(F32)
            ov = stack(o_ref).astype(F32)
            sg = jax.nn.sigmoid(gv)
            dof = dzv * (gv * sg)
            dgate = dzv * ov * (sg * (1.0 + gv * (1.0 - sg)))
            dov = dof.astype(BF16)
            lse2 = jnp.concatenate([lse_ref[h] for h in range(hp)], axis=1) if hp > 1 else lse_ref[0]
            delta = _col_to_row(jnp.sum(dof * ov, axis=-1, keepdims=True))

            def head(seg):
                rows = pl.ds(seg[0], seg[1])
                s = lax.dot_general(k_ref[rows, :], qv, NT, preferred_element_type=F32) * c2
                if seg[2] is not None:
                    s = jnp.where(seg[2], s, NEG_INF)
                return s, lax.dot_general(v_ref[rows, :], dov, NT, preferred_element_type=F32)

            dq, dqT = None, None
            nxt = head(segs[0])
            for j, (lo, n, _) in enumerate(segs):
                rows = pl.ds(lo, n)
                s, dp = nxt
                if j + 1 < len(segs):
                    nxt = head(segs[j + 1])
                p = jnp.exp2(s - lse2)
                ds = p * (dp - delta)
                dsb = ds.astype(BF16)
                dv_ref[rows, :] += jnp.dot(p.astype(BF16), dov, preferred_element_type=F32)
                dk_ref[rows, :] += jnp.dot(dsb, qv, preferred_element_type=F32) * scale
                if band:
                    part = jnp.dot(ds.T.astype(BF16), k_ref[rows, :], preferred_element_type=F32)
                    dq = part if dq is None else dq + part
                else:
                    part = jnp.dot(scr[0][:, lo:lo + n], dsb, preferred_element_type=F32)
                    dqT = part if dqT is None else dqT + part
            dq = (dq if band else dqT.T) * scale
            for h, ln in enumerate(lanes):
                rows = slice(h * tq, (h + 1) * tq)
                dq_ref[:, ln] = dq[rows].astype(dq_ref.dtype)
                dg_ref[:, ln] = dgate[rows].astype(dg_ref.dtype)
                if has_sink:
                    psk = jnp.exp2(s_ref[h][:, 0:1] * LOG2E - lse2[:, rows])
                    dsk = jnp.broadcast_to(-jnp.sum(psk * delta[:, rows], axis=1, keepdims=True), (1, LANE))

                    @pl.when(qi == 0)
                    def _(h=h, dsk=dsk):
                        ds_ref[h] = dsk

                    @pl.when(qi > 0)
                    def _(h=h, dsk=dsk):
                        ds_ref[h] += dsk

        @pl.when(qi < n_lq)
        def _():
            if band:
                lo = _band_lo(qi, tq, S)
                run([(S, L, None), (lo, span, _band_mask(qi, tq, lo, span, True, hp))])
            else:
                run([(j * kc, kc, None) for j in range(S // kc)] + [(S, L, None)])

        @pl.when(qi >= n_lq)
        def _():
            run([(S, L, None)])

    ng = KV_GROUP // hp
    q_spec = pl.BlockSpec((None, tq, hp * HEAD_DIM), lambda b, kv, g, i: (b, i, kv * ng + g))
    kv_spec = pl.BlockSpec((None, T, HEAD_DIM), lambda b, kv, g, i: (b, 0, kv))
    lse_spec = pl.BlockSpec((None, hp, 1, tq), lambda b, kv, g, i: (b, kv * ng + g, 0, i))
    in_specs = [q_spec, kv_spec, kv_spec, q_spec, q_spec, q_spec, lse_spec]
    out_specs = [q_spec, kv_spec, kv_spec, q_spec]
    out_shape = [jax.ShapeDtypeStruct(q.shape, BF16), jax.ShapeDtypeStruct(k.shape, F32),
                 jax.ShapeDtypeStruct(v.shape, F32), jax.ShapeDtypeStruct(q.shape, gate.dtype)]
    args = [q, k, v, gate, dz, o, lse]
    if has_sink:
        in_specs.append(pl.BlockSpec((hp, 1, LANE), lambda b, kv, g, i: (kv * ng + g, 0, 0)))
        args.append(sink)
        out_specs.append(pl.BlockSpec((None, hp, 1, LANE), lambda b, kv, g, i: (b, kv * ng + g, 0, 0)))
        out_shape.append(jax.ShapeDtypeStruct((B, H, 1, LANE), F32))
    res = _call_with_side(
        body, side, name="attn_band_bwd" if band else "attn_dense_bwd", grid=(B, KVH, ng, T // tq),
        in_specs=in_specs, args=args, out_specs=out_specs, out_shape=out_shape,
        scratch_shapes=[] if band else [pltpu.VMEM((HEAD_DIM, T), BF16)],
        dims=("parallel", "parallel", "arbitrary", "arbitrary"))
    return (res[0], res[1], res[2], res[3], res[4] if has_sink else None, res[-1] if side is not None else None)


def _prep_f(norm):
    def f(q, kv, *rest):
        q, kv = q.astype(F32), kv.astype(F32)
        kw = kv.shape[1] // 2
        k, v = kv[:, :kw], kv[:, kw:]
        if norm:
            qg, kg, cos, sin = rest
        else:
            (cos, sin), qg, kg = rest, None, None

        def heads(x, g):
            outs = []
            for h in range(x.shape[1] // HEAD_DIM):
                xh = x[:, h * HEAD_DIM:(h + 1) * HEAD_DIM]
                if g is not None:
                    xh = xh * lax.rsqrt(jnp.mean(xh * xh, axis=-1, keepdims=True) + EPS) * g
                outs.append(xh * cos + _swap(xh) * sin)
            return jnp.concatenate(outs, axis=1) if len(outs) > 1 else outs[0]

        return heads(q, qg), heads(k, kg), v

    return f


def _attn_branch(S, tm, band, norm, has_sink, carries=False):
    f = _prep_f(norm)
    name = "band" if band else "dense"

    def prep(q, kv, gains, tabs):
        outs = [(q.shape[2], BF16), (kv.shape[2] // 2, BF16), (kv.shape[2] // 2, BF16)]
        return _row_fwd(f, "prep_" + name + "_fwd", [q, kv], [], list(gains), list(tabs), outs, tm, S)

    def unpack(args):
        q, kv, gate = args[:3]
        rest = list(args[3:])
        gains = [rest.pop(0), rest.pop(0)] if norm else []
        sink = rest.pop(0) if has_sink else None
        pack = rest.pop(0) if carries else None
        return q, kv, gate, gains, sink, pack, rest

    def sink_lanes(sink):
        return None if sink is None else jnp.broadcast_to(sink[:, None, None], (sink.shape[0], 1, LANE))

    def run_fwd(args):
        q, kv, gate, gains, sink, pack, tabs = unpack(args)
        qp, kp, vp = prep(q, kv, gains, tabs)
        side = (pack.astype(BF16), True) if carries else None
        res = _attn_fwd(qp, kp, vp, sink_lanes(sink), gate, S, band, side)
        return ((res[0], res[3]) if carries else res[0]), (args, qp, kp, vp, res[1], res[2])

    @jax.custom_vjp
    def op(*args):
        return run_fwd(args)[0]

    def fwd(*args):
        return run_fwd(args)

    def bwd(res, ct):
        args, qp, kp, vp, o, lse = res
        q, kv, gate, gains, sink, pack, tabs = unpack(args)
        dz, side = (ct[0], (ct[1], False)) if carries else (ct, None)
        dqp, dkp, dvp, dgate, dsk, recv = _attn_bwd(qp, kp, vp, sink_lanes(sink), gate, dz, o, lse, S, band, side)
        dxs, _, dgains = _row_bwd(f, "prep_" + name + "_bwd", [q, kv], [], list(gains), list(tabs),
                                  [dqp, dkp, dvp], tm, S)
        out = list(dxs) + [dgate] + list(dgains)
        if has_sink:
            out.append(jnp.sum(dsk[:, :, 0, 0], axis=0))
        if carries:
            out.append(_sum_slots(recv, "sum_grads"))
        return (*out, *[jnp.zeros_like(t) for t in tabs])

    op.defvjp(fwd, bwd)
    return op


def _final_loss(S, tm):
    def run(X, target, g):
        B, T, D = X.shape
        n_lat_tiles = S // tm

        def lossf(x, gg, tgt):
            y = x * lax.rsqrt(jnp.mean(x * x, axis=-1, keepdims=True) + EPS) * gg
            err = y - tgt
            return 0.5 * jnp.sum(jnp.sum(err * err, axis=-1, keepdims=True), axis=0, keepdims=True) / D

        def body(x_ref, t_ref, g_ref, loss_ref, dx_ref, dg_ref):
            b, i = pl.program_id(0), pl.program_id(1)
            tgt = t_ref[...]
            val, vjp = jax.vjp(lambda x, gg: lossf(x, gg, tgt), x_ref[...], g_ref[...])
            dx, dg = vjp(jnp.ones((1, 1), F32))
            lat = (i < n_lat_tiles).astype(F32)
            dx_ref[...] = dx * lat

            @pl.when(i == 0)
            def _():
                loss_ref[...] = jnp.zeros_like(loss_ref)

            @pl.when(jnp.logical_and(b == 0, i == 0))
            def _():
                dg_ref[...] = jnp.zeros_like(dg_ref)

            loss_ref[...] += jnp.broadcast_to(val * lat, loss_ref.shape)
            dg_ref[...] += dg * lat

        x_spec = pl.BlockSpec((None, tm, D), lambda b, i: (b, i, 0))
        t_spec = pl.BlockSpec((None, tm, D), lambda b, i: (b, jnp.minimum(i, n_lat_tiles - 1), 0))
        g_spec = pl.BlockSpec((1, D), lambda b, i: (0, 0))
        loss, dx, dg = _call(
            body, name="final_loss", grid=(B, T // tm), in_specs=[x_spec, t_spec, g_spec],
            out_specs=[pl.BlockSpec((None, 1, LANE), lambda b, i: (b, 0, 0)), x_spec, g_spec],
            out_shape=[jax.ShapeDtypeStruct((B, 1, LANE), F32), jax.ShapeDtypeStruct(X.shape, F32),
                       jax.ShapeDtypeStruct(g.shape, F32)],
            compiler_params=_params(("arbitrary", "arbitrary")))(X, target, g)
        return jnp.sum(loss[:, 0, 0]), dx, dg

    @jax.custom_vjp
    def op(X, target, g):
        return run(X, target, g)[0]

    def fwd(X, target, g):
        loss, dx, dg = run(X, target, g)
        return loss, (dx, dg, target)

    def bwd(res, ct):
        dx, dg, target = res
        return ct * dx, jnp.zeros_like(target), ct * dg

    op.defvjp(fwd, bwd)
    return op


def _prenorm_f(D):
    def f(x, mp, g):
        y = x * lax.rsqrt(jnp.mean(x * x, axis=-1, keepdims=True) + EPS) * g
        return (y * (1.0 + mp[:, D:2 * D]) + mp[:, 0:D],)
    return f


def _resid_f(D):
    def f(x, y, mp):
        return (x + mp[:, 2 * D:3 * D] * y.astype(F32),)
    return f


def _decay_impl(la):
    t = jnp.tanh(la)
    x = -2.0 * t / (1.0 - t)
    r = lax.rsqrt(x)
    return jnp.exp(la), x * r, r


@jax.custom_vjp
def _decay(la):
    a, bc, _ = _decay_impl(la)
    return a, bc


def _decay_fwd(la):
    a, bc, r = _decay_impl(la)
    return (a, bc), (a, r)


def _decay_bwd(res, ct):
    a, r = res
    return (ct[0] * a - ct[1] * (a * a * r),)


_decay.defvjp(_decay_fwd, _decay_bwd)


def _gate2_f(hf, hr, g):
    g = g.astype(F32)
    return ((hf + hr) * (g * jax.nn.sigmoid(g)),)


def _merge_f(D):
    def f(m3, pa, pb, pc):
        m3 = m3.astype(F32)
        return (jax.nn.sigmoid(m3[:, 0:D]) * pa.astype(F32) + jax.nn.sigmoid(m3[:, D:2 * D]) * pb.astype(F32)
                + jax.nn.sigmoid(m3[:, 2 * D:3 * D]) * pc.astype(F32),)
    return f


def _coef_f(D):
    nblk = D // GATE_BLOCK

    def f(v, pv, wm):
        vb = v.astype(BF16)

        def gate(k):
            cols = []
            for j in range(nblk):
                r0 = (k * nblk + j) * GATE_BLOCK
                cols.append(jnp.dot(vb[:, j * GATE_BLOCK:(j + 1) * GATE_BLOCK],
                                    wm[r0:r0 + GATE_BLOCK, :].astype(BF16), preferred_element_type=F32))
            return jnp.concatenate(cols, axis=1)

        outs = []
        for d in range(2):
            r = jax.nn.sigmoid(gate(d) + pv[d:d + 1])
            i = jax.nn.sigmoid(gate(2 + d) + pv[2 + d:3 + d])
            a, bc = _decay(r * pv[4 + d:5 + d])
            outs += [a, bc * (i * v)]
        return tuple(outs)

    return f


def _gate_blocks(w):
    per = GATE_BLOCK // LRU_BLOCK_W
    n = w.shape[1]
    w5 = w.reshape(2, n // per, per, LRU_BLOCK_W, LRU_BLOCK_W)
    dense = jnp.einsum("djiab,ik->djiakb", w5, jnp.eye(per, dtype=w.dtype))
    return dense.reshape(2, (n // per) * GATE_BLOCK, GATE_BLOCK)


def _exchange_shape(x, gather):
    return jax.ShapeDtypeStruct((N_DEV,) + x.shape if gather else x.shape, x.dtype)


EXCHANGE_SEMS = [pltpu.SemaphoreType.DMA((N_DEV - 1,)), pltpu.SemaphoreType.DMA((N_DEV - 1,)),
                 pltpu.SemaphoreType.DMA(())]


def _exchange_ops(x_ref, o_ref, send_sems, recv_sems, local_sem, gather):
    mx, my, mc = lax.axis_index("x"), lax.axis_index("y"), lax.axis_index("c")
    me = 4 * mx + 2 * my + mc

    def src(p):
        return x_ref if gather else x_ref.at[p]

    local = pltpu.make_async_copy(src(me), o_ref.at[me], local_sem)
    sends, recvs = [], []
    for k in range(1, N_DEV):
        px = 1 - mx if k & 4 else mx
        py = 1 - my if k & 2 else my
        pc = 1 - mc if k & 1 else mc
        peer = 4 * px + 2 * py + pc
        sends.append(pltpu.make_async_remote_copy(
            src_ref=src(peer), dst_ref=o_ref.at[me], send_sem=send_sems.at[k - 1],
            recv_sem=recv_sems.at[k - 1], device_id=(px, py, pc), device_id_type=pl.DeviceIdType.MESH))
        recvs.append(pltpu.make_async_remote_copy(
            src_ref=src(peer), dst_ref=o_ref.at[peer], send_sem=send_sems.at[k - 1],
            recv_sem=recv_sems.at[k - 1], device_id=(px, py, pc), device_id_type=pl.DeviceIdType.MESH))

    def start():
        local.start()
        for cp in sends:
            cp.start()

    def finish():
        for cp in recvs:
            cp.wait_recv()
        for cp in sends:
            cp.wait_send()
        local.wait()

    return start, finish


def _exchange(x, name, gather):
    def body(x_ref, o_ref, send_sems, recv_sems, local_sem):
        start, finish = _exchange_ops(x_ref, o_ref, send_sems, recv_sems, local_sem, gather)
        start()
        finish()

    hbm = pl.BlockSpec(memory_space=pltpu.HBM)
    return _call(body, name=name, in_specs=[hbm], out_specs=hbm, out_shape=_exchange_shape(x, gather),
                 scratch_shapes=EXCHANGE_SEMS)(x)


def _call_with_side(body, side, *, name, grid, in_specs, args, out_specs, out_shape, scratch_shapes, dims):
    if side is None:
        return _call(body, name=name, grid=grid, in_specs=in_specs, out_specs=out_specs, out_shape=out_shape,
                     scratch_shapes=scratch_shapes, compiler_params=_params(dims))(*args)
    x, gather = side
    n_in, n_out, n_scr = len(in_specs), len(out_specs), len(scratch_shapes)

    def wrapped(*refs):
        ins, x_ref = refs[:n_in], refs[n_in]
        outs, o_ref = refs[n_in + 1:n_in + 1 + n_out], refs[n_in + 1 + n_out]
        scr = refs[n_in + 2 + n_out:n_in + 2 + n_out + n_scr]
        start, finish = _exchange_ops(x_ref, o_ref, *refs[n_in + 2 + n_out + n_scr:], gather)
        ids = [pl.program_id(a) for a in range(len(grid))]
        first = functools.reduce(jnp.logical_and, [i == 0 for i in ids])
        last = functools.reduce(jnp.logical_and, [i == g - 1 for i, g in zip(ids, grid)])
        pl.when(first)(start)
        body(*ins, *outs, *scr)
        pl.when(last)(finish)

    hbm = pl.BlockSpec(memory_space=pltpu.HBM)
    return _call(wrapped, name=name + "_xchg", grid=grid, in_specs=list(in_specs) + [hbm],
                 out_specs=list(out_specs) + [hbm], out_shape=list(out_shape) + [_exchange_shape(x, gather)],
                 scratch_shapes=list(scratch_shapes) + EXCHANGE_SEMS,
                 compiler_params=_params(("arbitrary",) * len(grid)))(*args, x)


def _adamw(gs, w, m, v, name):
    n, R, C = gs.shape
    tr = _pick(R, 256, 16)

    def body(g_ref, w_ref, m_ref, v_ref, go_ref, d_ref, mo_ref, vo_ref):
        g = g_ref[0].astype(F32)
        for p in range(1, n):
            g = g + g_ref[p].astype(F32)
        m2 = ADAM_B1 * m_ref[...] + (1.0 - ADAM_B1) * g
        v2 = ADAM_B2 * v_ref[...] + (1.0 - ADAM_B2) * (g * g)
        m_hat = m2 / (1.0 - ADAM_B1 ** ADAM_STEP)
        v_hat = v2 / (1.0 - ADAM_B2 ** ADAM_STEP)
        go_ref[...] = g
        d_ref[...] = -ADAM_LR * (m_hat / (jnp.sqrt(v_hat) + ADAM_EPS) + ADAM_WD * w_ref[...])
        mo_ref[...] = m2
        vo_ref[...] = v2

    spec = pl.BlockSpec((tr, C), lambda i: (i, 0))
    return _call(body, name=name, grid=(R // tr,),
                 in_specs=[pl.BlockSpec((n, tr, C), lambda i: (0, i, 0)), spec, spec, spec],
                 out_specs=[spec] * 4, out_shape=[jax.ShapeDtypeStruct((R, C), F32)] * 4,
                 compiler_params=_params(("parallel",)))(gs, w, m, v)


def _sum_slots(gs, name):
    n, R, C = gs.shape
    tr = _pick(R, 256, 16)

    def body(g_ref, o_ref):
        g = g_ref[0].astype(F32)
        for p in range(1, n):
            g = g + g_ref[p].astype(F32)
        o_ref[...] = g

    return _call(body, name=name, grid=(R // tr,), in_specs=[pl.BlockSpec((n, tr, C), lambda i: (0, i, 0))],
                 out_specs=pl.BlockSpec((tr, C), lambda i: (i, 0)), out_shape=jax.ShapeDtypeStruct((R, C), F32),
                 compiler_params=_params(("parallel",)))(gs)


def _gather_op(name):
    @jax.custom_vjp
    def op(pack):
        return _exchange(pack.astype(BF16), name, True)

    def fwd(pack):
        return op(pack), None

    def bwd(_, ct):
        return (_sum_slots(_exchange(ct, name + "_transpose", False), "sum_grads"),)

    op.defvjp(fwd, bwd)
    return op


def _rope_tables(S, L):
    P = HEAD_DIM // 4
    rows = S // GRID_W
    row_id = jnp.repeat(jnp.arange(rows), GRID_W)
    col_id = jnp.tile(jnp.arange(GRID_W), rows)
    inv = ROPE_THETA ** (-jnp.arange(P, dtype=F32) / P)
    ar, ac = row_id[:, None] * inv, col_id[:, None] * inv
    cos = jnp.concatenate([jnp.cos(ar), jnp.cos(ar), jnp.cos(ac), jnp.cos(ac)], axis=1)
    sin = jnp.concatenate([-jnp.sin(ar), jnp.sin(ar), -jnp.sin(ac), jnp.sin(ac)], axis=1)
    cos = jnp.concatenate([cos, jnp.ones((L, HEAD_DIM), F32)], axis=0)
    sin = jnp.concatenate([sin, jnp.zeros((L, HEAD_DIM), F32)], axis=0)
    return cos, sin


SQ_TILES = (((2176, 1024, 1024), (2176, 1024, 1024), (1024, 1024, 1088)),) * DEPTH


def _layer_shard_shapes(D, IN):
    return (("w_in", (D, IN // N_DEV), 1), ("w_mod", (D, 3 * D // N_DEV), 1),
            ("w_branch", (3, D // N_DEV, D), 1), ("w_out", (D // N_DEV, D), 0))


def _pack_layer(w, l, D):
    return jnp.concatenate([w[n][l].reshape(-1) for n in BIG_PACK]).reshape(-1, D)


def _unpack_shards(pack, D, IN):
    flat, out, off = pack.reshape(-1), {}, 0
    for n, shape, _ in _layer_shard_shapes(D, IN):
        sz = shape[0] * shape[1] * (shape[2] if len(shape) > 2 else 1)
        out[n] = flat[off:off + sz].reshape(shape)
        off += sz
    return out


def _unpack_gathered(g, D, IN):
    flat, out, off = g.reshape(N_DEV, -1), {}, 0
    for n, shape, axis in _layer_shard_shapes(D, IN):
        sz = shape[0] * shape[1] * (shape[2] if len(shape) > 2 else 1)
        out[n] = _unshard(flat[:, off:off + sz].reshape((N_DEV,) + shape), axis)
        off += sz
    return out


def _loss_fn(packs, p, x, c, ctx, target):
    B, S, D = x.shape
    L = ctx.shape[1]
    T = S + L
    tm = min(L, 256)
    KVW = D // KV_GROUP
    widths = (D, D, D, 2 * KVW, D, D, 2 * KVW, D, 3 * D)
    part_dtypes = (F32,) + (BF16,) * 8
    IN = sum(widths)
    cos, sin = _rope_tables(S, L)

    X = jnp.concatenate([x, ctx], axis=1)
    sc, scc = jax.nn.silu(c), jax.nn.silu(p["c_ctx"])
    A = jnp.concatenate([scc[None], sc, jnp.zeros((SUBLANE - 1 - B, D), F32)], axis=0)
    gathered = _gather_op("gather_layer0")(packs[0])
    for l in range(DEPTH):
        big = _unpack_gathered(gathered, D, IN)
        mod = _matmul("mm_mod")(A, big["w_mod"]) + p["b_mod"][l]
        modp = jnp.stack([jnp.broadcast_to(mod[0], (B, 3 * D)), mod[1:1 + B]], axis=1)[:, :, None, :]
        (h,) = _rowwise(_prenorm_f(D), "prenorm", [(D, BF16)], tm, S, 1, 1, 1)(X, modp, p["norm_g"][l][None])
        parts = _in_proj(widths, part_dtypes)(h.reshape(B * T, D), big["w_in"])
        uA, gA, qB, kvB, gB, qC, kvC, gC, m3 = [t.reshape(B, T, -1) for t in parts]
        u = _conv_op(S)(uA, p["conv_w"][l], p["conv_b"][l][None])
        nsp = -LRU_C * jax.nn.softplus(-p["lru_lambda"][l])
        pv = jnp.concatenate([p["lru_ba"][l], p["lru_bx"][l], nsp, jnp.zeros((2, D), F32)], axis=0)
        wm = jnp.concatenate([_gate_blocks(p["lru_wa"][l]), _gate_blocks(p["lru_wx"][l])], axis=0)
        wm = wm.reshape(-1, GATE_BLOCK)
        af, bf, ar, br = _rowwise(_coef_f(D), "lru_coef", [(D, F32)] * 4, tm // 2, S, 1, 0, 2)(u, pv, wm)
        hf = _scan_op(S, False)(af, bf)
        hr = _scan_op(S, True)(ar, br)
        (zA,) = _rowwise(_gate2_f, "gate_a", [(D, BF16)], tm, S, 3, 0, 0)(hf, hr, gA)
        zB = _attn_branch(S, tm, True, False, True)(qB, kvB, gB, p["attn_sink"][l], cos, sin)
        qkv_c = (qC, kvC, gC, p["q_norm_g"][l][None], p["k_norm_g"][l][None])
        if l + 1 < DEPTH:
            zC, gathered = _attn_branch(S, tm, False, True, False, True)(*qkv_c, packs[l + 1], cos, sin)
        else:
            zC = _attn_branch(S, tm, False, True, False)(*qkv_c, cos, sin)
        pr = [_matmul("mm_branch", *SQ_TILES[l], out_dtype=BF16)(z.reshape(B * T, D), big["w_branch"][n])
              .reshape(B, T, D) for n, z in enumerate((zA, zB, zC))]
        (mg,) = _rowwise(_merge_f(D), "merge", [(D, BF16)], tm, S, 4, 0, 0)(m3, *pr)
        y = _matmul("mm_out", *SQ_TILES[l], out_dtype=BF16)(mg.reshape(B * T, D), big["w_out"]).reshape(B, T, D)
        (X,) = _rowwise(_resid_f(D), "resid", [(D, F32)], tm, S, 2, 1, 0)(X, y, modp)
    return _final_loss(S, tm)(X, target, p["final_g"][None])


def _shard_axis(name):
    return {"w_mod": 2, "w_in": 2, "conv_w": 2, "lru_ba": 2, "lru_bx": 2, "lru_lambda": 2,
            "w_branch": 2, "w_out": 1}.get(name)


def _unshard(g, axis):
    full = jnp.moveaxis(g, 0, axis)
    shape = list(full.shape)
    shape[axis:axis + 2] = [shape[axis] * shape[axis + 1]]
    return full.reshape(shape)


def _reshard(full, axis):
    shape = list(full.shape)
    shape[axis:axis + 1] = [N_DEV, shape[axis] // N_DEV]
    return jnp.moveaxis(full.reshape(shape), axis, 0)


def _pad_to(v, n):
    return jnp.concatenate([v, jnp.zeros((n - v.shape[0],), v.dtype)]) if n > v.shape[0] else v


def _step(x, c, ctx, target, w, m, v):
    D = x.shape[2]
    IN = w["w_in"].shape[2] * N_DEV
    full = {n: w[n] for n in REPLICATED}
    small_local = jnp.concatenate([w[n].reshape(-1) for n in SMALL_SHARDED])
    small_all = _exchange(small_local.reshape(-1, LANE), "gather_small", True).reshape(N_DEV, -1)
    off = 0
    for n in SMALL_SHARDED:
        sz = w[n].size
        full[n] = _unshard(small_all[:, off:off + sz].reshape((N_DEV,) + w[n].shape), _shard_axis(n))
        off += sz
    packs = [_pack_layer(w, l, D) for l in range(DEPTH)]

    loss, (gpacks, gp, gx) = jax.value_and_grad(_loss_fn, argnums=(0, 1, 2))(packs, full, x, c, ctx, target)
    loss = lax.psum(loss, AXES)

    out = {}
    gshards = [_unpack_shards(g, D, IN) for g in gpacks]
    for n in BIG_PACK:
        g = jnp.stack([gs[n] for gs in gshards])
        C = w[n].shape[-1]
        res = _adamw(g.reshape(1, -1, C), w[n].reshape(-1, C), m[n].reshape(-1, C), v[n].reshape(-1, C),
                     "adamw_" + n)
        out[n] = [r.reshape(w[n].shape) for r in res]

    rep = jnp.concatenate([gp[n].reshape(-1) for n in REPLICATED])
    n_rep = rep.shape[0]
    chunk = -(-n_rep // (N_DEV * LANE)) * LANE
    rep = _pad_to(rep, N_DEV * chunk).reshape(N_DEV, chunk)
    shards = jnp.concatenate([_reshard(gp[n], _shard_axis(n)).reshape(N_DEV, -1) for n in SMALL_SHARDED], axis=1)
    n_sh = shards.shape[1]
    recv = _exchange(jnp.concatenate([rep, shards], axis=1).reshape(N_DEV, -1, LANE), "scatter_small", False)
    wl =jnp.concatenate([w[n].reshape(-1) for n in SMALL_SHARDED])
    ml = jnp.concatenate([m[n].reshape(-1) for n in SMALL_SHARDED])
    vl = jnp.concatenate([v[n].reshape(-1) for n in SMALL_SHARDED])
    rows = recv.shape[1]
    rrows = chunk // LANE
    g_sh, d_sh, m_sh, v_sh = _adamw(recv[:, rrows:], wl.reshape(-1, LANE), ml.reshape(-1, LANE),
                                    vl.reshape(-1, LANE), "adamw_small_sharded")
    g_rep8 = _sum_slots(recv[:, :rrows], "sum_replicated")
    g_rep = _exchange(g_rep8, "gather_replicated", True).reshape(-1)
    wr = _pad_to(jnp.concatenate([w[n].reshape(-1) for n in REPLICATED]), N_DEV * chunk)
    mr = _pad_to(jnp.concatenate([m[n].reshape(-1) for n in REPLICATED]), N_DEV * chunk)
    vr = _pad_to(jnp.concatenate([v[n].reshape(-1) for n in REPLICATED]), N_DEV * chunk)
    res_rep = _adamw(g_rep.reshape(1, -1, LANE), wr.reshape(-1, LANE), mr.reshape(-1, LANE), vr.reshape(-1, LANE),
                     "adamw_replicated")
    off = 0
    for n in REPLICATED:
        sz = w[n].size
        out[n] = [r.reshape(-1)[off:off + sz].reshape(w[n].shape) for r in res_rep]
        off += sz
    off = 0
    for n in SMALL_SHARDED:
        sz = w[n].size
        out[n] = [r.reshape(-1)[off:off + sz].reshape(w[n].shape) for r in (g_sh, d_sh, m_sh, v_sh)]
        off += sz
    assert off == n_sh and rows == rrows + n_sh // LANE
    return (loss, gx, *[out[n][0] for n in WEIGHTS], *[out[n][1] for n in WEIGHTS],
            *[out[n][2] for n in WEIGHTS], *[out[n][3] for n in WEIGHTS])


def kernel(x, c, ctx, c_ctx, norm_g, w_mod, b_mod, w_in, conv_w, conv_b, lru_wa, lru_ba, lru_wx, lru_bx, lru_lambda, attn_sink, q_norm_g, k_norm_g, w_branch, w_out, final_g, loss_target, m_c_ctx, m_norm_g, m_w_mod, m_b_mod, m_w_in, m_conv_w, m_conv_b, m_lru_wa, m_lru_ba, m_lru_wx, m_lru_bx, m_lru_lambda, m_attn_sink, m_q_norm_g, m_k_norm_g, m_w_branch, m_w_out, m_final_g, v_c_ctx, v_norm_g, v_w_mod, v_b_mod, v_w_in, v_conv_w, v_conv_b, v_lru_wa, v_lru_ba, v_lru_wx, v_lru_bx, v_lru_lambda, v_attn_sink, v_q_norm_g, v_k_norm_g, v_w_branch, v_w_out, v_final_g):
    w = dict(zip(WEIGHTS, (c_ctx, norm_g, w_mod, b_mod, w_in, conv_w, conv_b, lru_wa, lru_ba, lru_wx, lru_bx,
                           lru_lambda, attn_sink, q_norm_g, k_norm_g, w_branch, w_out, final_g)))
    m = dict(zip(WEIGHTS, (m_c_ctx, m_norm_g, m_w_mod, m_b_mod, m_w_in, m_conv_w, m_conv_b, m_lru_wa, m_lru_ba,
                           m_lru_wx, m_lru_bx, m_lru_lambda, m_attn_sink, m_q_norm_g, m_k_norm_g, m_w_branch,
                           m_w_out, m_final_g)))
    v = dict(zip(WEIGHTS, (v_c_ctx, v_norm_g, v_w_mod, v_b_mod, v_w_in, v_conv_w, v_conv_b, v_lru_wa, v_lru_ba,
                           v_lru_wx, v_lru_bx, v_lru_lambda, v_attn_sink, v_q_norm_g, v_k_norm_g, v_w_branch,
                           v_w_out, v_final_g)))
    return _step(x, c, ctx, loss_target, w, m, v)
```
